```python
import jax, jax.numpy as jnp
from jax import lax
import numpy as np

D_MODEL = 1024
BATCH = 16
SEQ = 256
DEPTH = 1
DEC_BATCH = 2
DEC_SEQ = 4096
PAST_LEN = 256

GRID_W = 64
D_MIX = D_MODEL
HEAD_DIM = 64
N_HEADS = (D_MIX // 2) // HEAD_DIM
N_KV_HEADS = 2
ATTN_W = N_HEADS * HEAD_DIM
KV_W = N_KV_HEADS * HEAD_DIM
FOURIER_W = D_MIX - ATTN_W
N_FOURIER_GROUPS = 4
FOURIER_GROUP_W = FOURIER_W // N_FOURIER_GROUPS
D_IN = ATTN_W + 2 * KV_W + FOURIER_W
N_EXPERTS = 32
TOP_K = 4
D_EXPERT = D_MODEL
SWIGLU_ALPHA = 1.702
SWIGLU_LIMIT = 7.0
ROPE_THETA = 10000.0
ROT_PAIRS = HEAD_DIM // 4
Q_BLOCK = 128
EPS = 1e-6

kernel_name = "hybrid_gqa_fnet_moe_diffusion_step"


def rmsnorm(x, g):
    xf = x.astype(jnp.float32)
    y = xf * lax.rsqrt(jnp.mean(xf * xf, axis=-1, keepdims=True) + EPS)
    return (y * g.astype(jnp.float32)).astype(x.dtype)


def rope_2d(x):
    T = x.shape[1]
    rows = T // GRID_W
    row = jnp.repeat(jnp.arange(rows), GRID_W).astype(jnp.float32)
    col = jnp.tile(jnp.arange(GRID_W), rows).astype(jnp.float32)
    inv = ROPE_THETA ** (-jnp.arange(ROT_PAIRS, dtype=jnp.float32) / ROT_PAIRS)
    ang = jnp.stack([row[:, None] * inv, col[:, None] * inv], axis=1)
    cos = jnp.cos(ang)[None, :, None]
    sin = jnp.sin(ang)[None, :, None]
    xr = x.astype(jnp.float32).reshape(x.shape[:3] + (2, 2, ROT_PAIRS))
    x1, x2 = xr[..., 0, :], xr[..., 1, :]
    out = jnp.stack([x1 * cos - x2 * sin, x2 * cos + x1 * sin], axis=-2)
    return out.reshape(x.shape).astype(x.dtype)


def attention(q, k, v):
    B, T, H, Dh = q.shape
    G = H // N_KV_HEADS
    nb = T // Q_BLOCK
    qb = q.reshape(B, nb, Q_BLOCK, N_KV_HEADS, G, Dh).transpose(1, 0, 2, 3, 4, 5)
    kf = k.astype(jnp.float32)
    vf = v.astype(jnp.float32)
    scale = HEAD_DIM ** -0.5

    def block(qblk):
        s = jnp.einsum('bqkgd,bskd->bkgqs', qblk.astype(jnp.float32), kf) * scale
        p = jax.nn.softmax(s, axis=-1)
        return jnp.einsum('bkgqs,bskd->bqkgd', p, vf)

    o = lax.map(block, qb)
    return o.transpose(1, 0, 2, 3, 4, 5).reshape(B, T, H * Dh).astype(q.dtype)


def fourier_mix(f, w_f):
    B, T, _ = f.shape
    fg = f.reshape(B, T, N_FOURIER_GROUPS, FOURIER_GROUP_W).astype(jnp.float32)
    spec = jnp.fft.fft2(fg, axes=(1, 3), norm="ortho").real
    out = jnp.einsum('btgc,gcd->btgd', spec, w_f.astype(jnp.float32))
    return out.reshape(B, T, FOURIER_W).astype(f.dtype)


def moe(h, w_router, b_router, w_gu, b_gu, w_dn, b_dn):
    B, T, D = h.shape
    ht = h.reshape(B * T, D)
    logits = ht.astype(jnp.float32) @ w_router.astype(jnp.float32) + b_router.astype(jnp.float32)
    top_val, top_idx = lax.top_k(logits, TOP_K)
    top_w = jax.nn.softmax(top_val, axis=-1)
    gates = jnp.sum(jax.nn.one_hot(top_idx, N_EXPERTS, dtype=jnp.float32) * top_w[..., None], axis=1)

    def expert(acc, xs):
        wgu, bgu, wdn, bdn, g = xs
        hg = ht @ wgu + bgu
        x_glu = jnp.minimum(hg[:, ::2], SWIGLU_LIMIT)
        x_lin = jnp.clip(hg[:, 1::2], -SWIGLU_LIMIT, SWIGLU_LIMIT)
        y = ((x_lin + 1) * (x_glu * jax.nn.sigmoid(SWIGLU_ALPHA * x_glu))) @ wdn + bdn
        return acc + g[:, None].astype(ht.dtype) * y, None

    acc, _ = lax.scan(expert, jnp.zeros_like(ht), (w_gu, b_gu, w_dn, b_dn, gates.T))
    return acc.reshape(B, T, D)


def layer(x, mod, ctx_k, ctx_v, latent, norm1_g, w_in, q_norm_g, k_norm_g, w_f, attn_out_g, fourier_out_g,
          w_out, norm2_g, w_router, b_router, w_gu, b_gu, w_dn, b_dn):
    B, T, _ = x.shape
    shift1, scale1, gate1, shift2, scale2, gate2 = jnp.split(mod.astype(x.dtype), 6, axis=-1)
    h = rmsnorm(x, norm1_g) * (1 + scale1) + shift1
    proj = h @ w_in
    q = proj[..., :ATTN_W].reshape(B, T, N_HEADS, HEAD_DIM)
    k = proj[..., ATTN_W:ATTN_W + KV_W].reshape(B, T, N_KV_HEADS, HEAD_DIM)
    v = proj[..., ATTN_W + KV_W:ATTN_W + 2 * KV_W].reshape(B, T, N_KV_HEADS, HEAD_DIM)
    f = proj[..., ATTN_W + 2 * KV_W:]
    q = rmsnorm(q, q_norm_g)
    k = rmsnorm(k, k_norm_g)
    if latent:
        keys = jnp.concatenate([ctx_k.astype(x.dtype), rope_2d(k)], axis=1)
        vals = jnp.concatenate([ctx_v.astype(x.dtype), v], axis=1)
        q = rope_2d(q)
    else:
        keys, vals = k, v
    attn = attention(q, keys, vals)
    four = fourier_mix(f, w_f)
    mixed = jnp.concatenate([rmsnorm(attn, attn_out_g), rmsnorm(four, fourier_out_g)], axis=-1) @ w_out
    x = x + gate1 * mixed
    h2 = rmsnorm(x, norm2_g) * (1 + scale2) + shift2
    x = x + gate2 * moe(h2, w_router, b_router, w_gu, b_gu, w_dn, b_dn)
    return x, k, v


def setup_inputs(seed: int = 0) -> dict:
    key = jax.random.key(seed)
    ks = jax.random.split(key, 24)
    f32 = jnp.float32
    nrm = lambda k, shape, s: jax.random.normal(k, shape, f32) * s
    return {
        "x_prompt": nrm(ks[0], (BATCH, SEQ, D_MODEL), 1.0),
        "x_sample": nrm(ks[1], (DEC_BATCH, DEC_SEQ, D_MODEL), 1.0),
        "cache_k": nrm(ks[2], (DEC_BATCH, DEPTH, PAST_LEN, N_KV_HEADS, HEAD_DIM), 1.0),
        "cache_v": nrm(ks[3], (DEC_BATCH, DEPTH, PAST_LEN, N_KV_HEADS, HEAD_DIM), 1.0),
        "c": nrm(ks[4], (DEC_BATCH, D_MODEL), 1.0),
        "c_ctx": nrm(ks[5], (D_MODEL,), 1.0),
        "norm1_g": 1.0 + nrm(ks[6], (DEPTH, D_MODEL), 0.02),
        "w_mod": nrm(ks[7], (DEPTH, D_MODEL, 6 * D_MODEL), 0.5 * D_MODEL ** -0.5),
        "b_mod": nrm(ks[8], (DEPTH, 6 * D_MODEL), 0.02),
        "w_in": nrm(ks[9], (DEPTH, D_MODEL, D_IN), D_MODEL ** -0.5),
        "q_norm_g": 1.0 + nrm(ks[10], (DEPTH, HEAD_DIM), 0.02),
        "k_norm_g": 1.0 + nrm(ks[11], (DEPTH, HEAD_DIM), 0.02),
        "w_fourier": nrm(ks[12], (DEPTH, N_FOURIER_GROUPS, FOURIER_GROUP_W, FOURIER_GROUP_W), FOURIER_GROUP_W ** -0.5),
        "attn_out_g": 1.0 + nrm(ks[13], (DEPTH, ATTN_W), 0.02),
        "fourier_out_g": 1.0 + nrm(ks[14], (DEPTH, FOURIER_W), 0.02),
        "w_out": nrm(ks[15], (DEPTH, D_MIX, D_MODEL), D_MIX ** -0.5),
        "norm2_g": 1.0 + nrm(ks[16], (DEPTH, D_MODEL), 0.02),
        "w_router": nrm(ks[17], (DEPTH, D_MODEL, N_EXPERTS), D_MODEL ** -0.5),
        "b_router": nrm(ks[18], (DEPTH, N_EXPERTS), 0.01),
        "w_gate_up": nrm(ks[19], (DEPTH, N_EXPERTS, D_MODEL, 2 * D_EXPERT), D_MODEL ** -0.5),
        "b_gate_up": nrm(ks[20], (DEPTH, N_EXPERTS, 2 * D_EXPERT), 0.02),
        "w_down": nrm(ks[21], (DEPTH, N_EXPERTS, D_EXPERT, D_MODEL), D_EXPERT ** -0.5),
        "b_down": nrm(ks[22], (DEPTH, N_EXPERTS, D_MODEL), 0.02),
        "final_g": 1.0 + nrm(ks[23], (D_MODEL,), 0.02),
    }


def reference(x_prompt, x_sample, cache_k, cache_v, c, c_ctx, norm1_g, w_mod, b_mod, w_in, q_norm_g, k_norm_g,
              w_fourier, attn_out_g, fourier_out_g, w_out, norm2_g, w_router, b_router, w_gate_up, b_gate_up,
              w_down, b_down, final_g):
    xc = x_prompt
    xl = x_sample
    new_k_list = []
    new_v_list = []
    for l in range(DEPTH):
        params = (norm1_g[l], w_in[l], q_norm_g[l], k_norm_g[l], w_fourier[l], attn_out_g[l], fourier_out_g[l],
                  w_out[l], norm2_g[l], w_router[l], b_router[l], w_gate_up[l], b_gate_up[l], w_down[l], b_down[l])
        mod_ctx = (jax.nn.silu(c_ctx) @ w_mod[l] + b_mod[l])[None, None, :]
        mod_lat = (jax.nn.silu(c) @ w_mod[l] + b_mod[l])[:, None, :]
        xc, k_ctx, v_ctx = layer(xc, mod_ctx, None, None, False, *params)
        new_k_list.append(k_ctx)
        new_v_list.append(v_ctx)
        xl, _, _ = layer(xl, mod_lat, cache_k[:, l], cache_v[:, l], True, *params)
    y_prompt = rmsnorm(xc, final_g)
    y_sample = rmsnorm(xl, final_g)
    new_k = jnp.stack(new_k_list, axis=1)
    new_v = jnp.stack(new_v_list, axis=1)
    return (y_prompt, y_sample, new_k, new_v)
```

```python
import functools
import math

import numpy as np
import jax
import jax.numpy as jnp
from jax import lax
from jax.experimental import pallas as pl
from jax.experimental.pallas import tpu as pltpu

F32 = jnp.float32
BF16 = jnp.bfloat16
I32 = jnp.int32

D_MODEL = 1024
HEAD_DIM = 64
N_HEADS = 8
N_KV_HEADS = 2
HEADS_PER_KV = N_HEADS // N_KV_HEADS
ATTN_W = N_HEADS * HEAD_DIM
KV_W = N_KV_HEADS * HEAD_DIM
FOURIER_W = 512
FOURIER_GROUP_W = 128
D_IN = ATTN_W + 2 * KV_W + FOURIER_W
N_EXPERTS = 32
TOP_K = 4
D_EXPERT = 1024
SWIGLU_ALPHA = 1.702
SWIGLU_LIMIT = 7.0
ROPE_THETA = 10000.0
ROT_PAIRS = HEAD_DIM // 4
GRID_W = 64
EPS = 1e-6

LANES = 128
TM = 512
TQ = 256
TG = 256
FFT_N1 = 4
VMEM_LIMIT = 56 * 1024 * 1024


def _cparams(sem, vmem=VMEM_LIMIT):
    return pltpu.CompilerParams(dimension_semantics=sem, vmem_limit_bytes=vmem)


def _rope_tables(t_lat):
    pos = np.arange(t_lat)
    row = (pos // GRID_W).astype(np.float64)
    col = (pos % GRID_W).astype(np.float64)
    inv = ROPE_THETA ** (-np.arange(ROT_PAIRS, dtype=np.float64) / ROT_PAIRS)
    lane = np.arange(LANES)
    d = lane % HEAD_DIM
    axis = d // (2 * ROT_PAIRS)
    second = (d // ROT_PAIRS) % 2
    p = d % ROT_PAIRS
    ang = np.where(axis[None, :] == 0, row[:, None], col[:, None]) * inv[p][None, :]
    cos = np.cos(ang)
    sin = np.sin(ang) * np.where(second == 0, -1.0, 1.0)[None, :]
    cos = np.concatenate([np.ones((TM, LANES)), cos], axis=0)
    sin = np.concatenate([np.zeros((TM, LANES)), sin], axis=0)
    return jnp.asarray(cos, F32), jnp.asarray(sin, F32)


def _dft_tables(n1, n2):
    c = np.arange(FOURIER_GROUP_W)
    ang = 2.0 * np.pi * np.outer(c, c) / FOURIER_GROUP_W
    z = np.zeros_like(ang)
    cbd = np.block([[np.cos(ang), z], [z, np.cos(ang)]])
    sbd = np.block([[np.sin(ang), z], [z, np.sin(ang)]])
    cd = np.concatenate([cbd, -sbd], axis=1)
    u = np.arange(n2)
    ang2 = 2.0 * np.pi * np.outer(u, u) / n2
    ct, st = np.cos(ang2), np.sin(ang2)
    nt = max(n1 - 1, 1)
    tw_ang = 2.0 * np.pi * np.outer(np.arange(1, nt + 1), u) / (n1 * n2)
    twc = np.repeat(np.cos(tw_ang)[:, :, None], LANES, axis=2)
    tws = np.repeat(np.sin(tw_ang)[:, :, None], LANES, axis=2)
    return (jnp.asarray(cd, BF16), jnp.asarray(ct, BF16), jnp.asarray(st, BF16),
            jnp.asarray(twc, F32), jnp.asarray(tws, F32))


def _head_mean_matrix():
    h = np.arange(ATTN_W) // HEAD_DIM
    return jnp.asarray((h[:, None] == h[None, :]) / HEAD_DIM, BF16)


def _deinterleave_matrix():
    p = np.zeros((2 * LANES, 2 * LANES))
    m = np.arange(LANES)
    p[2 * m, m] = 1.0
    p[2 * m + 1, LANES + m] = 1.0
    return jnp.asarray(p, BF16)


def _mod_row(i, nc, tpb):
    return jnp.where(i < nc, 0, 1 + (i - nc) // tpb)


def _rms(x):
    return x * lax.rsqrt(jnp.mean(x * x, axis=-1, keepdims=True) + EPS)


def _pack_cols(cols, lane):
    out = jnp.zeros(lane.shape, cols[0].dtype)
    for k, c in enumerate(cols):
        out = jnp.where(lane == k, c, out)
    return out


def _mod_kernel(c_ref, w_ref, b_ref, o_ref):
    c = c_ref[...]
    s = c * (1.0 / (1.0 + jnp.exp(-c)))
    o_ref[...] = jnp.dot(s.astype(BF16), w_ref[...].astype(BF16), preferred_element_type=F32) + b_ref[...]


def _modulation(c_rows, w_mod, b_mod):
    n = w_mod.shape[1] // D_MODEL
    return pl.pallas_call(
        _mod_kernel,
        grid=(n,),
        in_specs=[pl.BlockSpec((8, D_MODEL), lambda j: (0, 0)),
                  pl.BlockSpec((D_MODEL, D_MODEL), lambda j: (0, j)),
                  pl.BlockSpec((1, D_MODEL), lambda j: (0, j))],
        out_specs=pl.BlockSpec((8, D_MODEL), lambda j: (0, j)),
        out_shape=jax.ShapeDtypeStruct((8, w_mod.shape[1]), F32),
        compiler_params=_cparams(("parallel",)),
        name="modulation",
    )(c_rows, w_mod, b_mod.reshape(1, -1))


def _pre_kernel(xc_ref, xl_ref, sh_ref, sc_ref, g1_ref, win_ref, qg_ref, kg_ref, bd_ref, cos_ref, sin_ref,
                q_ref, k_ref, v_ref, f_ref, kn_ref, vn_ref, *, nc, tpb):
    i = pl.program_id(0)
    r = _mod_row(i, nc, tpb)
    x = jnp.where(i < nc, xc_ref[...], xl_ref[...])
    h = _rms(x) * g1_ref[...]
    h = h * (1.0 + sc_ref[pl.ds(r, 1), :]) + sh_ref[pl.ds(r, 1), :]
    proj = jnp.dot(h.astype(BF16), win_ref[...], preferred_element_type=F32)
    q = proj[:, :ATTN_W]
    k = proj[:, ATTN_W:ATTN_W + KV_W]
    v = proj[:, ATTN_W + KV_W:ATTN_W + 2 * KV_W]
    f_ref[...] = proj[:, ATTN_W + 2 * KV_W:].astype(BF16)
    vn_ref[...] = v
    v_ref[...] = v.astype(BF16)
    q_ms = jnp.dot((q * q).astype(BF16), bd_ref[...], preferred_element_type=F32)
    k_ms = jnp.dot((k * k).astype(BF16), bd_ref[:KV_W, :KV_W], preferred_element_type=F32)
    qn = q * lax.rsqrt(q_ms + EPS) * qg_ref[...]
    kn = k * lax.rsqrt(k_ms + EPS) * kg_ref[...]
    kn_ref[...] = kn
    cos = cos_ref[...]
    sin = sin_ref[...]
    lane = lax.broadcasted_iota(I32, cos.shape, 1)
    first = (lane & ROT_PAIRS) == 0

    def rope(c):
        partner = jnp.where(first, pltpu.roll(c, LANES - ROT_PAIRS, 1), pltpu.roll(c, ROT_PAIRS, 1))
        return c * cos + partner * sin

    k_ref[...] = rope(kn).astype(BF16)
    scale = HEAD_DIM ** -0.5
    for j in range(ATTN_W // LANES):
        q_ref[:, j * LANES:(j + 1) * LANES] = (rope(qn[:, j * LANES:(j + 1) * LANES]) * scale).astype(BF16)


def _pre(xc, xl, mod, norm1_g, w_in, q_g, k_g, bd, cos_t, sin_t, *, nc, nl, tpb):
    n = (nc + nl) * TM
    row = lambda i: (i, 0)
    const = lambda i: (0, 0)
    tab = lambda i: (jnp.where(i < nc, 0, 1 + (i - nc) % tpb), 0)
    return pl.pallas_call(
        functools.partial(_pre_kernel, nc=nc, tpb=tpb),
        grid=(nc + nl,),
        in_specs=[pl.BlockSpec((TM, D_MODEL), lambda i: (jnp.minimum(i, nc - 1), 0)),
                  pl.BlockSpec((TM, D_MODEL), lambda i: (jnp.maximum(i - nc, 0), 0)),
                  pl.BlockSpec((8, D_MODEL), lambda i: (0, 0)),
                  pl.BlockSpec((8, D_MODEL), lambda i: (0, 1)),
                  pl.BlockSpec((1, D_MODEL), const),
                  pl.BlockSpec((D_MODEL, D_IN), const),
                  pl.BlockSpec((1, ATTN_W), const),
                  pl.BlockSpec((1, KV_W), const),
                  pl.BlockSpec((ATTN_W, ATTN_W), const),
                  pl.BlockSpec((TM, LANES), tab),
                  pl.BlockSpec((TM, LANES), tab)],
        out_specs=[pl.BlockSpec((TM, ATTN_W), row), pl.BlockSpec((TM, KV_W), row), pl.BlockSpec((TM, KV_W), row),
                   pl.BlockSpec((TM, FOURIER_W), row), pl.BlockSpec((TM, KV_W), row), pl.BlockSpec((TM, KV_W), row)],
        out_shape=[jax.ShapeDtypeStruct((n, ATTN_W), BF16), jax.ShapeDtypeStruct((n, KV_W), BF16),
                   jax.ShapeDtypeStruct((n, KV_W), BF16), jax.ShapeDtypeStruct((n, FOURIER_W), BF16),
                   jax.ShapeDtypeStruct((n, KV_W), F32), jax.ShapeDtypeStruct((n, KV_W), F32)],
        compiler_params=_cparams(("parallel",)),
        name="pre_proj",
    )(xc, xl, mod, mod, norm1_g, w_in, q_g, k_g, bd, cos_t, sin_t)


def _attn_kernel(q_ref, k_ref, v_ref, o_ref):
    for h in range(N_HEADS):
        j = h // HEADS_PER_KV
        qh = q_ref[:, h * HEAD_DIM:(h + 1) * HEAD_DIM]
        kh = k_ref[:, j * HEAD_DIM:(j + 1) * HEAD_DIM]
        vh = v_ref[:, j * HEAD_DIM:(j + 1) * HEAD_DIM]
        s = lax.dot_general(qh, kh, (((1,), (1,)), ((), ())), preferred_element_type=F32)
        p = jnp.exp(s - jnp.max(s, axis=-1, keepdims=True))
        l = jnp.sum(p, axis=-1, keepdims=True)
        o = jnp.dot(p.astype(BF16), vh, preferred_element_type=F32)
        o_ref[:, h * HEAD_DIM:(h + 1) * HEAD_DIM] = o / l


def _attention(q, k, v, *, batch, t, s, q_row0, kv_batched):
    qpb = t // TQ
    q0 = q_row0 // TQ
    if kv_batched:
        kv_spec = pl.BlockSpec((None, s, KV_W), lambda b, i: (b, 0, 0))
    else:
        kv_spec = pl.BlockSpec((s, KV_W), lambda b, i: (b, 0))
    return pl.pallas_call(
        _attn_kernel,
        grid=(batch, qpb),
        in_specs=[pl.BlockSpec((TQ, ATTN_W), lambda b, i: (q0 + b * qpb + i, 0)), kv_spec, kv_spec],
        out_specs=pl.BlockSpec((TQ, ATTN_W), lambda b, i: (b * qpb + i, 0)),
        out_shape=jax.ShapeDtypeStruct((batch * t, ATTN_W), F32),
        compiler_params=_cparams(("parallel", "parallel")),
        name="attention",
    )(q, k, v)


def _fourier_kernel(x_ref, cd_ref, ct_ref, st_ref, twc_ref, tws_ref, wf_ref, o_ref, *, n1, n2):
    pw = 2 * FOURIER_GROUP_W
    scale = 1.0 / math.sqrt(n1 * n2 * FOURIER_GROUP_W)
    for gp in range(FOURIER_W // pw):
        z = jnp.dot(x_ref[:, gp * pw:(gp + 1) * pw], cd_ref[...], preferred_element_type=F32)
        zr = [z[t1 * n2:(t1 + 1) * n2, :pw] for t1 in range(n1)]
        zi = [z[t1 * n2:(t1 + 1) * n2, pw:] for t1 in range(n1)]
        if n1 == 1:
            a = [(zr[0], zi[0])]
        else:
            a = [(zr[0] + zr[1] + zr[2] + zr[3], zi[0] + zi[1] + zi[2] + zi[3]),
                 (zr[0] + zi[1] - zr[2] - zi[3], zi[0] - zr[1] - zi[2] + zr[3]),
                 (zr[0] - zr[1] + zr[2] - zr[3], zi[0] - zi[1] + zi[2] - zi[3]),
                 (zr[0] - zi[1] - zr[2] + zi[3], zi[0] + zr[1] - zi[2] - zr[3])]
        for u1 in range(n1):
            ar, ai = a[u1]
            if u1 > 0:
                c = jnp.concatenate([twc_ref[u1 - 1]] * 2, axis=1)
                s = jnp.concatenate([tws_ref[u1 - 1]] * 2, axis=1)
                ar, ai = ar * c + ai * s, ai * c - ar * s
            y = (jnp.dot(ct_ref[...], ar.astype(BF16), preferred_element_type=F32)
                 + jnp.dot(st_ref[...], ai.astype(BF16), preferred_element_type=F32)) * scale
            y = jnp.dot(y.astype(BF16), wf_ref[gp], preferred_element_type=F32)
            o_ref[:, u1 * FOURIER_W + gp * pw:u1 * FOURIER_W + (gp + 1) * pw] = y


def _fourier(f, wf_pairs, *, batch, t, row0, n1):
    n2 = t // n1
    cd, ct, st, twc, tws = _dft_tables(n1, n2)
    nt = twc.shape[0]
    b0 = row0 // t
    c2 = lambda b: (0, 0)
    c3 = lambda b: (0, 0, 0)
    out = pl.pallas_call(
        functools.partial(_fourier_kernel, n1=n1, n2=n2),
        grid=(batch,),
        in_specs=[pl.BlockSpec((t, FOURIER_W), lambda b: (b0 + b, 0)),
                  pl.BlockSpec(cd.shape, c2), pl.BlockSpec((n2, n2), c2), pl.BlockSpec((n2, n2), c2),
                  pl.BlockSpec((nt, n2, LANES), c3), pl.BlockSpec((nt, n2, LANES), c3),
                  pl.BlockSpec(wf_pairs.shape, c3)],
        out_specs=pl.BlockSpec((None, n2, n1 * FOURIER_W), lambda b: (b, 0, 0)),
        out_shape=jax.ShapeDtypeStruct((batch, n2, n1 * FOURIER_W), F32),
        compiler_params=_cparams(("parallel",)),
        name="fourier",
    )(f, cd, ct, st, twc, tws, wf_pairs)
    return out.reshape(batch * t, FOURIER_W)


def _mix_kernel(xc_ref, xl_ref, ac_ref, al_ref, fc_ref, fl_ref, g1_ref, sh2_ref, sc2_ref, ag_ref, fg_ref,
                wout_ref, n2g_ref, x1_ref, h2_ref, *, nc, tpb):
    i = pl.program_id(0)
    r = _mod_row(i, nc, tpb)
    is_ctx = i < nc
    x = jnp.where(is_ctx, xc_ref[...], xl_ref[...])
    a = _rms(jnp.where(is_ctx, ac_ref[...], al_ref[...])) * ag_ref[...]
    f = _rms(jnp.where(is_ctx, fc_ref[...], fl_ref[...])) * fg_ref[...]
    mixed = (jnp.dot(a.astype(BF16), wout_ref[:ATTN_W, :], preferred_element_type=F32)
             + jnp.dot(f.astype(BF16), wout_ref[ATTN_W:, :], preferred_element_type=F32))
    x1 = x + g1_ref[pl.ds(r, 1), :] * mixed
    x1_ref[...] = x1
    h2 = _rms(x1) * n2g_ref[...]
    h2_ref[...] = h2 * (1.0 + sc2_ref[pl.ds(r, 1), :]) + sh2_ref[pl.ds(r, 1), :]


def _mix(xc, xl, ac, al, fc, fl, mod, attn_g, four_g, w_out, norm2_g, *, nc, nl, tpb):
    n = (nc + nl) * TM
    lo = lambda i: (jnp.minimum(i, nc - 1), 0)
    hi = lambda i: (jnp.maximum(i - nc, 0), 0)
    const = lambda i: (0, 0)
    row = lambda i: (i, 0)
    return pl.pallas_call(
        functools.partial(_mix_kernel, nc=nc, tpb=tpb),
        grid=(nc + nl,),
        in_specs=[pl.BlockSpec((TM, D_MODEL), lo), pl.BlockSpec((TM, D_MODEL), hi),
                  pl.BlockSpec((TM, ATTN_W), lo), pl.BlockSpec((TM, ATTN_W), hi),
                  pl.BlockSpec((TM, FOURIER_W), lo), pl.BlockSpec((TM, FOURIER_W), hi),
                  pl.BlockSpec((8, D_MODEL), lambda i: (0, 2)),
                  pl.BlockSpec((8, D_MODEL), lambda i: (0, 3)),
                  pl.BlockSpec((8, D_MODEL), lambda i: (0, 4)),
                  pl.BlockSpec((1, ATTN_W), const), pl.BlockSpec((1, FOURIER_W), const),
                  pl.BlockSpec((D_MODEL, D_MODEL), const), pl.BlockSpec((1, D_MODEL), const)],
        out_specs=[pl.BlockSpec((TM, D_MODEL), row), pl.BlockSpec((TM, D_MODEL), row)],
        out_shape=[jax.ShapeDtypeStruct((n, D_MODEL), F32), jax.ShapeDtypeStruct((n, D_MODEL), F32)],
        compiler_params=_cparams(("parallel",)),
        name="mix_proj",
    )(xc, xl, ac, al, fc, fl, mod, mod, mod, attn_g, four_g, w_out, norm2_g)


def _router_kernel(h2_ref, wr_ref, br_ref, idx_ref, w_ref, cnt_ref):
    i = pl.program_id(0)
    logits = jnp.dot(h2_ref[...], wr_ref[...], preferred_element_type=F32,
                     precision=lax.Precision.HIGHEST) + br_ref[...]
    lane = lax.broadcasted_iota(I32, logits.shape, 1)
    l = logits
    vals, idxs = [], []
    hits = jnp.zeros(logits.shape, F32)
    for _ in range(TOP_K):
        m = jnp.max(l, axis=-1, keepdims=True)
        ix = jnp.min(jnp.where(l == m, lane, LANES), axis=-1, keepdims=True)
        sel = lane == ix
        hits = hits + sel.astype(F32)
        l = jnp.where(sel, -jnp.inf, l)
        vals.append(m)
        idxs.append(ix)
    es = [jnp.exp(v - vals[0]) for v in vals]
    den = es[0] + es[1] + es[2] + es[3]
    idx_ref[...] = _pack_cols(idxs, lane)
    w_ref[...] = _pack_cols([e / den for e in es], lane)

    @pl.when(i == 0)
    def _():
        cnt_ref[...] = jnp.zeros(cnt_ref.shape, F32)

    cnt_ref[...] += jnp.sum(hits, axis=0, keepdims=True)


def _router(h2, wr, br):
    n = h2.shape[0]
    row = lambda i: (i, 0)
    const = lambda i: (0, 0)
    return pl.pallas_call(
        _router_kernel,
        grid=(n // TM,),
        in_specs=[pl.BlockSpec((TM, D_MODEL), row), pl.BlockSpec((D_MODEL, LANES), const),
                  pl.BlockSpec((1, LANES), const)],
        out_specs=[pl.BlockSpec((TM, LANES), row), pl.BlockSpec((TM, LANES), row), pl.BlockSpec((8, LANES), const)],
        out_shape=[jax.ShapeDtypeStruct((n, LANES), I32), jax.ShapeDtypeStruct((n, LANES), F32),
                   jax.ShapeDtypeStruct((8, LANES), F32)],
        compiler_params=_cparams(("arbitrary",)),
        name="router",
    )(h2, wr, br)


def _pos_kernel(idx_ref, gs_ref, pos_ref, carry_ref):
    i = pl.program_id(0)

    @pl.when(i == 0)
    def _():
        carry_ref[...] = jnp.zeros(carry_ref.shape, F32)

    idx = idx_ref[...]
    lane = lax.broadcasted_iota(I32, idx.shape, 1)
    sels = [lane == idx[:, k:k + 1] for k in range(TOP_K)]
    hits = jnp.zeros(idx.shape, F32)
    for s in sels:
        hits = hits + s.astype(F32)
    r = lax.broadcasted_iota(I32, (TM, TM), 0)
    c = lax.broadcasted_iota(I32, (TM, TM), 1)
    before = (c < r).astype(BF16)
    rank = jnp.dot(before, hits.astype(BF16), preferred_element_type=F32) + carry_ref[0:1, :] + gs_ref[0:1, :]
    cols = [jnp.sum(jnp.where(s, rank, 0.0), axis=-1, keepdims=True) for s in sels]
    pos_ref[...] = _pack_cols(cols, lane).astype(I32)
    carry_ref[...] += jnp.sum(hits, axis=0, keepdims=True)


def _positions(idx, gs_rows):
    n = idx.shape[0]
    return pl.pallas_call(
        _pos_kernel,
        grid=(n // TM,),
        in_specs=[pl.BlockSpec((TM, LANES), lambda i: (i, 0)), pl.BlockSpec((8, LANES), lambda i: (0, 0))],
        out_specs=pl.BlockSpec((TM, LANES), lambda i: (i, 0)),
        out_shape=jax.ShapeDtypeStruct((n, LANES), I32),
        scratch_shapes=[pltpu.VMEM((8, LANES), F32)],
        compiler_params=_cparams(("arbitrary",)),
        name="positions",
    )(idx, gs_rows)


def _row_copy_wait(src, dst, sem):
    pltpu.make_async_copy(src, dst, sem).wait()


def _dispatch_kernel(pos_ref, h2_ref, xs_in_ref, xs_ref, sem):
    del xs_in_ref

    def issue(t, carry):
        for k in range(TOP_K):
            p = pos_ref[TOP_K * t + k]
            pltpu.make_async_copy(h2_ref.at[pl.ds(t, 1), :], xs_ref.at[pl.ds(p, 1), :], sem).start()
        return carry

    lax.fori_loop(0, TM, issue, 0)
    for k in range(TOP_K):
        _row_copy_wait(h2_ref, xs_ref.at[pl.ds(0, TM), :], sem)


def _dispatch(pos_flat, h2, xs_zero):
    n = h2.shape[0]
    return pl.pallas_call(
        _dispatch_kernel,
        grid=(n // TM,),
        in_specs=[pl.BlockSpec((TOP_K * TM,), lambda i: (i,), memory_space=pltpu.SMEM),
                  pl.BlockSpec((TM, D_MODEL), lambda i: (i, 0)),
                  pl.BlockSpec(memory_space=pl.ANY)],
        out_specs=pl.BlockSpec(memory_space=pl.ANY),
        out_shape=jax.ShapeDtypeStruct(xs_zero.shape, F32),
        scratch_shapes=[pltpu.SemaphoreType.DMA(())],
        input_output_aliases={2: 0},
        compiler_params=_cparams(("arbitrary",)),
        name="dispatch",
    )(pos_flat, h2, xs_zero)


def _ffn_kernel(te_ref, nv_ref, x_ref, wgu_ref, bg_ref, bl_ref, wdn_ref, bdn_ref, perm_ref, o_ref,
                wg_s, wl_s, wd_s):
    i = pl.program_id(0)
    e = te_ref[i]
    prev = te_ref[jnp.maximum(i - 1, 0)]
    valid = i < nv_ref[0]
    fresh = jnp.logical_or(i == 0, e != prev)

    @pl.when(jnp.logical_and(valid, fresh))
    def _():
        for j in range(2 * D_EXPERT // (2 * LANES)):
            blk = wgu_ref[:, j * 2 * LANES:(j + 1) * 2 * LANES].astype(BF16)
            d = jnp.dot(blk, perm_ref[...], preferred_element_type=F32)
            wg_s[:, j * LANES:(j + 1) * LANES] = d[:, :LANES].astype(BF16)
            wl_s[:, j * LANES:(j + 1) * LANES] = d[:, LANES:].astype(BF16)
        wd_s[...] = wdn_ref[...].astype(BF16)

    @pl.when(valid)
    def _():
        x = x_ref[...].astype(BF16)
        g = jnp.dot(x, wg_s[...], preferred_element_type=F32) + bg_ref[...]
        l = jnp.dot(x, wl_s[...], preferred_element_type=F32) + bl_ref[...]
        g = jnp.minimum(g, SWIGLU_LIMIT)
        l = jnp.clip(l, -SWIGLU_LIMIT, SWIGLU_LIMIT)
        a = (l + 1.0) * (g * (1.0 / (1.0 + jnp.exp(-SWIGLU_ALPHA * g))))
        o_ref[...] = jnp.dot(a.astype(BF16), wd_s[...], preferred_element_type=F32) + bdn_ref[...]

    @pl.when(jnp.logical_not(valid))
    def _():
        o_ref[...] = jnp.zeros(o_ref.shape, F32)


def _expert_ffn(tile_expert, n_valid, xs, w_gu, b_g, b_l, w_dn, b_dn, perm):
    p = xs.shape[0]
    nt = p // TG
    tile = lambda i, te, nv: (jnp.minimum(i, nv[0] - 1), 0)
    ex3 = lambda i, te, nv: (te[i], 0, 0)
    grid_spec = pltpu.PrefetchScalarGridSpec(
        num_scalar_prefetch=2,
        grid=(nt,),
        in_specs=[pl.BlockSpec((TG, D_MODEL), tile),
                  pl.BlockSpec((None, D_MODEL, 2 * D_EXPERT), ex3),
                  pl.BlockSpec((None, 1, D_EXPERT), ex3),
                  pl.BlockSpec((None, 1, D_EXPERT), ex3),
                  pl.BlockSpec((None, D_EXPERT, D_MODEL), ex3),
                  pl.BlockSpec((None, 1, D_MODEL), ex3),
                  pl.BlockSpec((2 * LANES, 2 * LANES), lambda i, te, nv: (0, 0))],
        out_specs=pl.BlockSpec((TG, D_MODEL), lambda i, te, nv: (i, 0)),
        scratch_shapes=[pltpu.VMEM((D_MODEL, D_EXPERT), BF16), pltpu.VMEM((D_MODEL, D_EXPERT), BF16),
                        pltpu.VMEM((D_EXPERT, D_MODEL), BF16)],
    )
    return pl.pallas_call(
        _ffn_kernel,
        grid_spec=grid_spec,
        out_shape=jax.ShapeDtypeStruct((p, D_MODEL), F32),
        compiler_params=_cparams(("arbitrary",)),
        name="expert_ffn",
    )(tile_expert, n_valid, xs, w_gu, b_g, b_l, w_dn, b_dn, perm)


def _combine_kernel(pos_ref, w_ref, x1_ref, g2_ref, fg_ref, ys_ref, o_ref, buf, sem, *, tile0, nc, tpb):
    i = pl.program_id(0) + tile0
    r = _mod_row(i, nc, tpb)

    def issue(t, carry):
        for k in range(TOP_K):
            p = pos_ref[TOP_K * t + k]
            pltpu.make_async_copy(ys_ref.at[pl.ds(p, 1), :], buf.at[k, pl.ds(t, 1), :], sem).start()
        return carry

    lax.fori_loop(0, TM, issue, 0)
    for k in range(TOP_K):
        _row_copy_wait(ys_ref.at[pl.ds(0, TM), :], buf.at[k], sem)
    w = w_ref[...]
    moe = w[:, 0:1] * buf[0]
    for k in range(1, TOP_K):
        moe = moe + w[:, k:k + 1] * buf[k]
    y = x1_ref[...] + g2_ref[pl.ds(r, 1), :] * moe
    o_ref[...] = _rms(y) * fg_ref[...]


def _combine(pos_flat, w, x1, mod, final_g, ys, *, tile0, ntiles, nc, tpb):
    return pl.pallas_call(
        functools.partial(_combine_kernel, tile0=tile0, nc=nc, tpb=tpb),
        grid=(ntiles,),
        in_specs=[pl.BlockSpec((TOP_K * TM,), lambda i: (i + tile0,), memory_space=pltpu.SMEM),
                  pl.BlockSpec((TM, LANES), lambda i: (i + tile0, 0)),
                  pl.BlockSpec((TM, D_MODEL), lambda i: (i + tile0, 0)),
                  pl.BlockSpec((8, D_MODEL), lambda i: (0, 5)),
                  pl.BlockSpec((1, D_MODEL), lambda i: (0, 0)),
                  pl.BlockSpec(memory_space=pl.ANY)],
        out_specs=pl.BlockSpec((TM, D_MODEL), lambda i: (i, 0)),
        out_shape=jax.ShapeDtypeStruct((ntiles * TM, D_MODEL), F32),
        scratch_shapes=[pltpu.VMEM((TOP_K, TM, D_MODEL), F32), pltpu.SemaphoreType.DMA(())],
        compiler_params=_cparams(("arbitrary",)),
        name="combine",
    )(pos_flat, w, x1, mod, final_g, ys)


def kernel(x_prompt, x_sample, cache_k, cache_v, c, c_ctx, norm1_g, w_mod, b_mod, w_in, q_norm_g, k_norm_g,
           w_fourier, attn_out_g, fourier_out_g, w_out, norm2_g, w_router, b_router, w_gate_up, b_gate_up,
           w_down, b_down, final_g):
    bc, tc, _ = x_prompt.shape
    bl, tl, _ = x_sample.shape
    depth = w_in.shape[0]
    past = cache_k.shape[2]
    assert depth == 1 and tc % TQ == 0 and tl % TM == 0 and (bc * tc) % TM == 0
    n_ctx, n_lat = bc * tc, bl * tl
    n_tok = n_ctx + n_lat
    nc, nl, tpb = n_ctx // TM, n_lat // TM, tl // TM

    xc = x_prompt.reshape(n_ctx, D_MODEL)
    xl = x_sample.reshape(n_lat, D_MODEL)

    c_rows = jnp.zeros((8, D_MODEL), F32).at[0].set(c_ctx).at[1:1 + bl].set(c)
    mod = _modulation(c_rows, w_mod[0], b_mod[0])

    cos_t, sin_t = _rope_tables(tl)
    q, k, v, f, k_new, v_new = _pre(
        xc, xl, mod, norm1_g, w_in[0].astype(BF16), jnp.tile(q_norm_g, (1, N_HEADS)),
        jnp.tile(k_norm_g, (1, N_KV_HEADS)), _head_mean_matrix(), cos_t, sin_t, nc=nc, nl=nl, tpb=tpb)

    attn_c = _attention(q, k, v, batch=bc, t=tc, s=tc, q_row0=0, kv_batched=False)
    keys = jnp.concatenate([cache_k[:, 0].reshape(bl, past, KV_W).astype(BF16),
                            k[n_ctx:].reshape(bl, tl, KV_W)], axis=1)
    vals = jnp.concatenate([cache_v[:, 0].reshape(bl, past, KV_W).astype(BF16),
                            v[n_ctx:].reshape(bl, tl, KV_W)], axis=1)
    attn_l = _attention(q, keys, vals, batch=bl, t=tl, s=past + tl, q_row0=n_ctx, kv_batched=True)

    wf = w_fourier[0].astype(BF16)
    zero = jnp.zeros_like(wf[0])
    wf_pairs = jnp.stack([jnp.block([[wf[0], zero], [zero, wf[1]]]), jnp.block([[wf[2], zero], [zero, wf[3]]])])
    four_c = _fourier(f, wf_pairs, batch=bc, t=tc, row0=0, n1=1)
    four_l = _fourier(f, wf_pairs, batch=bl, t=tl, row0=n_ctx, n1=FFT_N1)

    x1, h2 = _mix(xc, xl, attn_c, attn_l, four_c, four_l, mod, attn_out_g, fourier_out_g,
                  w_out[0].astype(BF16), norm2_g, nc=nc, nl=nl, tpb=tpb)

    wr = jnp.zeros((D_MODEL, LANES), F32).at[:, :N_EXPERTS].set(w_router[0])
    br = jnp.full((1, LANES), -1e30, F32).at[0, :N_EXPERTS].set(b_router[0])
    idx, gate_w, counts = _router(h2, wr, br)
    cnt = counts[0].astype(I32)
    padded = ((cnt + TG - 1) // TG) * TG
    ends = jnp.cumsum(padded)
    starts = ends - padded
    n_rows = n_tok * TOP_K + N_EXPERTS * TG
    n_tiles = n_rows // TG
    tile_expert = jnp.minimum(jnp.searchsorted(ends // TG, jnp.arange(n_tiles, dtype=I32), side="right"),
                              N_EXPERTS - 1).astype(I32)
    n_valid = (ends[N_EXPERTS - 1] // TG).astype(I32).reshape(1)
    gs_rows = jnp.broadcast_to(starts.astype(F32)[None, :], (8, LANES))
    pos = _positions(idx, gs_rows)
    pos_flat = pos[:, :TOP_K].reshape(-1)

    xs = _dispatch(pos_flat, h2, jnp.zeros((n_rows, D_MODEL), F32))
    b_gu = b_gate_up[0].reshape(N_EXPERTS, 1, D_EXPERT, 2)
    ys = _expert_ffn(tile_expert, n_valid, xs, w_gate_up[0], b_gu[..., 0], b_gu[..., 1], w_down[0],
                     b_down[0].reshape(N_EXPERTS, 1, D_MODEL), _deinterleave_matrix())

    y_c = _combine(pos_flat, gate_w, x1, mod, final_g.reshape(1, -1), ys, tile0=0, ntiles=nc, nc=nc, tpb=tpb)
    y_l = _combine(pos_flat, gate_w, x1, mod, final_g.reshape(1, -1), ys, tile0=nc, ntiles=nl, nc=nc, tpb=tpb)

    new_k = k_new[:n_ctx].reshape(bc, 1, tc, N_KV_HEADS, HEAD_DIM)
    new_v = v_new[:n_ctx].reshape(bc, 1, tc, N_KV_HEADS, HEAD_DIM)
    return (y_c.reshape(bc, tc, D_MODEL), y_l.reshape(bl, tl, D_MODEL), new_k, new_v)
```

```python
import functools
import math

import numpy as np
import jax
import jax.numpy as jnp
from jax import lax
from jax.experimental import pallas as pl
from jax.experimental.pallas import tpu as pltpu

F32 = jnp.float32
BF16 = jnp.bfloat16
I32 = jnp.int32

D_MODEL = 1024
HEAD_DIM = 64
N_HEADS = 8
N_KV_HEADS = 2
HEADS_PER_KV = N_HEADS // N_KV_HEADS
ATTN_W = N_HEADS * HEAD_DIM
KV_W = N_KV_HEADS * HEAD_DIM
FOURIER_W = 512
FOURIER_GROUP_W = 128
D_IN = ATTN_W + 2 * KV_W + FOURIER_W
N_EXPERTS = 32
TOP_K = 4
D_EXPERT = 1024
SWIGLU_ALPHA = 1.702
SWIGLU_LIMIT = 7.0
ROPE_THETA = 10000.0
ROT_PAIRS = HEAD_DIM // 4
GRID_W = 64
EPS = 1e-6

LANES = 128
TM = 512
TQ = 256
TG = 256
FFT_N1 = 4
VMEM_LIMIT = 56 * 1024 * 1024


def _cparams(sem, vmem=VMEM_LIMIT):
    return pltpu.CompilerParams(dimension_semantics=sem, vmem_limit_bytes=vmem)


def _rope_tables(t_lat):
    pos = np.arange(t_lat)
    row = (pos // GRID_W).astype(np.float64)
    col = (pos % GRID_W).astype(np.float64)
    inv = ROPE_THETA ** (-np.arange(ROT_PAIRS, dtype=np.float64) / ROT_PAIRS)
    lane = np.arange(LANES)
    d = lane % HEAD_DIM
    axis = d // (2 * ROT_PAIRS)
    second = (d // ROT_PAIRS) % 2
    p = d % ROT_PAIRS
    ang = np.where(axis[None, :] == 0, row[:, None], col[:, None]) * inv[p][None, :]
    cos = np.cos(ang)
    sin = np.sin(ang) * np.where(second == 0, -1.0, 1.0)[None, :]
    cos = np.concatenate([np.ones((TM, LANES)), cos], axis=0)
    sin = np.concatenate([np.zeros((TM, LANES)), sin], axis=0)
    return jnp.asarray(cos, F32), jnp.asarray(sin, F32)


def _dft_tables(n1, n2):
    c = np.arange(FOURIER_GROUP_W)
    ang = 2.0 * np.pi * np.outer(c, c) / FOURIER_GROUP_W
    z = np.zeros_like(ang)
    cbd = np.block([[np.cos(ang), z], [z, np.cos(ang)]])
    sbd = np.block([[np.sin(ang), z], [z, np.sin(ang)]])
    cd = np.concatenate([cbd, -sbd], axis=1)
    u = np.arange(n2)
    ang2 = 2.0 * np.pi * np.outer(u, u) / n2
    ct, st = np.cos(ang2), np.sin(ang2)
    nt = max(n1 - 1, 1)
    tw_ang = 2.0 * np.pi * np.outer(np.arange(1, nt + 1), u) / (n1 * n2)
    twc = np.repeat(np.cos(tw_ang)[:, :, None], LANES, axis=2)
    tws = np.repeat(np.sin(tw_ang)[:, :, None], LANES, axis=2)
    return (jnp.asarray(cd, F32), jnp.asarray(ct, F32), jnp.asarray(st, F32),
            jnp.asarray(twc, F32), jnp.asarray(tws, F32))


def _head_mean_matrix():
    h = np.arange(ATTN_W) // HEAD_DIM
    return jnp.asarray((h[:, None] == h[None, :]) / HEAD_DIM, BF16)


def _deinterleave_matrix():
    p = np.zeros((2 * LANES, 2 * LANES))
    m = np.arange(LANES)
    p[2 * m, m] = 1.0
    p[2 * m + 1, LANES + m] = 1.0
    return jnp.asarray(p, BF16)


def _mod_row(i, nc, tpb):
    return jnp.where(i < nc, 0, 1 + (i - nc) // tpb)


def _rms(x):
    return x * lax.rsqrt(jnp.mean(x * x, axis=-1, keepdims=True) + EPS)


def _pack_cols(cols, lane):
    out = jnp.zeros(lane.shape, cols[0].dtype)
    for k, c in enumerate(cols):
        out = jnp.where(lane == k, c, out)
    return out


def _mod_kernel(c_ref, w_ref, b_ref, o_ref):
    c = c_ref[...]
    s = c * (1.0 / (1.0 + jnp.exp(-c)))
    o_ref[...] = jnp.dot(s.astype(BF16), w_ref[...].astype(BF16), preferred_element_type=F32) + b_ref[...]


def _modulation(c_rows, w_mod, b_mod):
    n = w_mod.shape[1] // D_MODEL
    return pl.pallas_call(
        _mod_kernel,
        grid=(n,),
        in_specs=[pl.BlockSpec((8, D_MODEL), lambda j: (0, 0)),
                  pl.BlockSpec((D_MODEL, D_MODEL), lambda j: (0, j)),
                  pl.BlockSpec((1, D_MODEL), lambda j: (0, j))],
        out_specs=pl.BlockSpec((8, D_MODEL), lambda j: (0, j)),
        out_shape=jax.ShapeDtypeStruct((8, w_mod.shape[1]), F32),
        compiler_params=_cparams(("parallel",)),
        name="modulation",
    )(c_rows, w_mod, b_mod.reshape(1, -1))


def _pre_kernel(xc_ref, xl_ref, sh_ref, sc_ref, g1_ref, win_ref, qg_ref, kg_ref, bd_ref, cos_ref, sin_ref,
                q_ref, k_ref, v_ref, f_ref, kn_ref, vn_ref, *, nc, tpb):
    i = pl.program_id(0)
    r = _mod_row(i, nc, tpb)
    x = jnp.where(i < nc, xc_ref[...], xl_ref[...])
    h = _rms(x) * g1_ref[...]
    h = h * (1.0 + sc_ref[pl.ds(r, 1), :]) + sh_ref[pl.ds(r, 1), :]
    proj = jnp.dot(h.astype(BF16), win_ref[...], preferred_element_type=F32)
    q = proj[:, :ATTN_W]
    k = proj[:, ATTN_W:ATTN_W + KV_W]
    v = proj[:, ATTN_W + KV_W:ATTN_W + 2 * KV_W]
    f_ref[...] = proj[:, ATTN_W + 2 * KV_W:].astype(BF16)
    vn_ref[...] = v
    v_ref[...] = v.astype(BF16)
    q_ms = jnp.dot((q * q).astype(BF16), bd_ref[...], preferred_element_type=F32)
    k_ms = jnp.dot((k * k).astype(BF16), bd_ref[:KV_W, :KV_W], preferred_element_type=F32)
    qn = q * lax.rsqrt(q_ms + EPS) * qg_ref[...]
    kn = k * lax.rsqrt(k_ms + EPS) * kg_ref[...]
    kn_ref[...] = kn
    cos = cos_ref[...]
    sin = sin_ref[...]
    lane = lax.broadcasted_iota(I32, cos.shape, 1)
    first = (lane & ROT_PAIRS) == 0

    def rope(c):
        partner = jnp.where(first, pltpu.roll(c, LANES - ROT_PAIRS, 1), pltpu.roll(c, ROT_PAIRS, 1))
        return c * cos + partner * sin

    k_ref[...] = rope(kn).astype(BF16)
    scale = HEAD_DIM ** -0.5
    for j in range(ATTN_W // LANES):
        q_ref[:, j * LANES:(j + 1) * LANES] = (rope(qn[:, j * LANES:(j + 1) * LANES]) * scale).astype(BF16)


def _pre(xc, xl, mod, norm1_g, w_in, q_g, k_g, bd, cos_t, sin_t, *, nc, nl, tpb):
    n = (nc + nl) * TM
    row = lambda i: (i, 0)
    const = lambda i: (0, 0)
    tab = lambda i: (jnp.where(i < nc, 0, 1 + (i - nc) % tpb), 0)
    return pl.pallas_call(
        functools.partial(_pre_kernel, nc=nc, tpb=tpb),
        grid=(nc + nl,),
        in_specs=[pl.BlockSpec((TM, D_MODEL), lambda i: (jnp.minimum(i, nc - 1), 0)),
                  pl.BlockSpec((TM, D_MODEL), lambda i: (jnp.maximum(i - nc, 0), 0)),
                  pl.BlockSpec((8, D_MODEL), lambda i: (0, 0)),
                  pl.BlockSpec((8, D_MODEL), lambda i: (0, 1)),
                  pl.BlockSpec((1, D_MODEL), const),
                  pl.BlockSpec((D_MODEL, D_IN), const),
                  pl.BlockSpec((1, ATTN_W), const),
                  pl.BlockSpec((1, KV_W), const),
                  pl.BlockSpec((ATTN_W, ATTN_W), const),
                  pl.BlockSpec((TM, LANES), tab),
                  pl.BlockSpec((TM, LANES), tab)],
        out_specs=[pl.BlockSpec((TM, ATTN_W), row), pl.BlockSpec((TM, KV_W), row), pl.BlockSpec((TM, KV_W), row),
                   pl.BlockSpec((TM, FOURIER_W), row), pl.BlockSpec((TM, KV_W), row), pl.BlockSpec((TM, KV_W), row)],
        out_shape=[jax.ShapeDtypeStruct((n, ATTN_W), BF16), jax.ShapeDtypeStruct((n, KV_W), BF16),
                   jax.ShapeDtypeStruct((n, KV_W), BF16), jax.ShapeDtypeStruct((n, FOURIER_W), BF16),
                   jax.ShapeDtypeStruct((n, KV_W), F32), jax.ShapeDtypeStruct((n, KV_W), F32)],
        compiler_params=_cparams(("parallel",)),
        name="pre_proj",
    )(xc, xl, mod, mod, norm1_g, w_in, q_g, k_g, bd, cos_t, sin_t)


def _attn_kernel(q_ref, k_ref, v_ref, o_ref):
    for h in range(N_HEADS):
        j = h // HEADS_PER_KV
        qh = q_ref[:, h * HEAD_DIM:(h + 1) * HEAD_DIM]
        kh = k_ref[:, j * HEAD_DIM:(j + 1) * HEAD_DIM]
        vh = v_ref[:, j * HEAD_DIM:(j + 1) * HEAD_DIM]
        s = lax.dot_general(qh, kh, (((1,), (1,)), ((), ())), preferred_element_type=F32)
        p = jnp.exp(s - jnp.max(s, axis=-1, keepdims=True))
        l = jnp.sum(p, axis=-1, keepdims=True)
        o = jnp.dot(p.astype(BF16), vh, preferred_element_type=F32)
        o_ref[:, h * HEAD_DIM:(h + 1) * HEAD_DIM] = o / l


def _attention(q, k, v, *, batch, t, s, q_row0, kv_batched):
    qpb = t // TQ
    q0 = q_row0 // TQ
    if kv_batched:
        kv_spec = pl.BlockSpec((None, s, KV_W), lambda b, i: (b, 0, 0))
    else:
        kv_spec = pl.BlockSpec((s, KV_W), lambda b, i: (b, 0))
    return pl.pallas_call(
        _attn_kernel,
        grid=(batch, qpb),
        in_specs=[pl.BlockSpec((TQ, ATTN_W), lambda b, i: (q0 + b * qpb + i, 0)), kv_spec, kv_spec],
        out_specs=pl.BlockSpec((TQ, ATTN_W), lambda b, i: (b * qpb + i, 0)),
        out_shape=jax.ShapeDtypeStruct((batch * t, ATTN_W), F32),
        compiler_params=_cparams(("parallel", "parallel")),
        name="attention",
    )(q, k, v)


def _fourier_kernel(x_ref, cd_ref, ct_ref, st_ref, twc_ref, tws_ref, wf_ref, o_ref, *, n1, n2):
    pw = 2 * FOURIER_GROUP_W
    scale = 1.0 / math.sqrt(n1 * n2 * FOURIER_GROUP_W)
    for gp in range(FOURIER_W // pw):
        z = jnp.dot(x_ref[:, gp * pw:(gp + 1) * pw], cd_ref[...], preferred_element_type=F32)
        zr = [z[t1 * n2:(t1 + 1) * n2, :pw] for t1 in range(n1)]
        zi = [z[t1 * n2:(t1 + 1) * n2, pw:] for t1 in range(n1)]
        if n1 == 1:
            a = [(zr[0], zi[0])]
        else:
            a = [(zr[0] + zr[1] + zr[2] + zr[3], zi[0] + zi[1] + zi[2] + zi[3]),
                 (zr[0] + zi[1] - zr[2] - zi[3], zi[0] - zr[1] - zi[2] + zr[3]),
                 (zr[0] - zr[1] + zr[2] - zr[3], zi[0] - zi[1] + zi[2] - zi[3]),
                 (zr[0] - zi[1] - zr[2] + zi[3], zi[0] + zr[1] - zi[2] - zr[3])]
        for u1 in range(n1):
            ar, ai = a[u1]
            if u1 > 0:
                c = jnp.concatenate([twc_ref[u1 - 1]] * 2, axis=1)
                s = jnp.concatenate([tws_ref[u1 - 1]] * 2, axis=1)
                ar, ai = ar * c + ai * s, ai * c - ar * s
            y = (jnp.dot(ct_ref[...], ar.astype(BF16), preferred_element_type=F32)
                 + jnp.dot(st_ref[...], ai.astype(BF16), preferred_element_type=F32)) * scale
            y = jnp.dot(y.astype(BF16), wf_ref[gp], preferred_element_type=F32)
            rows = pl.ds(u1, n2, stride=n1) if n1 > 1 else slice(None)
            for half in range(2):
                o_ref[2 * gp + half, rows, :] = y[:, half * FOURIER_GROUP_W:(half + 1) * FOURIER_GROUP_W]


def _fourier(f, wf_pairs, *, batch, t, row0, n1):
    n2 = t // n1
    cd, ct, st, twc, tws = _dft_tables(n1, n2)
    cd, ct, st = cd.astype(BF16), ct.astype(BF16), st.astype(BF16)
    nt = twc.shape[0]
    b0 = row0 // t
    c2 = lambda b: (0, 0)
    c3 = lambda b: (0, 0, 0)
    return pl.pallas_call(
        functools.partial(_fourier_kernel, n1=n1, n2=n2),
        grid=(batch,),
        in_specs=[pl.BlockSpec((t, FOURIER_W), lambda b: (b0 + b, 0)),
                  pl.BlockSpec(cd.shape, c2), pl.BlockSpec((n2, n2), c2), pl.BlockSpec((n2, n2), c2),
                  pl.BlockSpec((nt, n2, LANES), c3), pl.BlockSpec((nt, n2, LANES), c3),
                  pl.BlockSpec(wf_pairs.shape, c3)],
        out_specs=pl.BlockSpec((FOURIER_W // FOURIER_GROUP_W, t, FOURIER_GROUP_W), lambda b: (0, b, 0)),
        out_shape=jax.ShapeDtypeStruct((FOURIER_W // FOURIER_GROUP_W, batch * t, FOURIER_GROUP_W), F32),
        compiler_params=_cparams(("parallel",)),
        name="fourier",
    )(f, cd, ct, st, twc, tws, wf_pairs)


def _mix_kernel(xc_ref, xl_ref, ac_ref, al_ref, fc_ref, fl_ref, g1_ref, sh2_ref, sc2_ref, ag_ref, fg_ref,
                wout_ref, n2g_ref, x1_ref, h2_ref, *, nc, tpb):
    i = pl.program_id(0)
    r = _mod_row(i, nc, tpb)
    is_ctx = i < nc
    x = jnp.where(is_ctx, xc_ref[...], xl_ref[...])
    a = _rms(jnp.where(is_ctx, ac_ref[...], al_ref[...])) * ag_ref[...]
    n_groups = FOURIER_W // FOURIER_GROUP_W
    f = jnp.concatenate([jnp.where(is_ctx, fc_ref[g], fl_ref[g]) for g in range(n_groups)], axis=1)
    f = _rms(f) * fg_ref[...]
    mixed = (jnp.dot(a.astype(BF16), wout_ref[:ATTN_W, :], preferred_element_type=F32)
             + jnp.dot(f.astype(BF16), wout_ref[ATTN_W:, :], preferred_element_type=F32))
    x1 = x + g1_ref[pl.ds(r, 1), :] * mixed
    x1_ref[...] = x1
    h2 = _rms(x1) * n2g_ref[...]
    h2_ref[...] = h2 * (1.0 + sc2_ref[pl.ds(r, 1), :]) + sh2_ref[pl.ds(r, 1), :]


def _mix(xc, xl, ac, al, fc, fl, mod, attn_g, four_g, w_out, norm2_g, *, nc, nl, tpb):
    n = (nc + nl) * TM
    lo = lambda i: (jnp.minimum(i, nc - 1), 0)
    hi = lambda i: (jnp.maximum(i - nc, 0), 0)
    const = lambda i: (0, 0)
    row = lambda i: (i, 0)
    return pl.pallas_call(
        functools.partial(_mix_kernel, nc=nc, tpb=tpb),
        grid=(nc + nl,),
        in_specs=[pl.BlockSpec((TM, D_MODEL), lo), pl.BlockSpec((TM, D_MODEL), hi),
                  pl.BlockSpec((TM, ATTN_W), lo), pl.BlockSpec((TM, ATTN_W), hi),
                  pl.BlockSpec((FOURIER_W // FOURIER_GROUP_W, TM, FOURIER_GROUP_W),
                               lambda i: (0, jnp.minimum(i, nc - 1), 0)),
                  pl.BlockSpec((FOURIER_W // FOURIER_GROUP_W, TM, FOURIER_GROUP_W),
                               lambda i: (0, jnp.maximum(i - nc, 0), 0)),
                  pl.BlockSpec((8, D_MODEL), lambda i: (0, 2)),
                  pl.BlockSpec((8, D_MODEL), lambda i: (0, 3)),
                  pl.BlockSpec((8, D_MODEL), lambda i: (0, 4)),
                  pl.BlockSpec((1, ATTN_W), const), pl.BlockSpec((1, FOURIER_W), const),
                  pl.BlockSpec((D_MODEL, D_MODEL), const), pl.BlockSpec((1, D_MODEL), const)],
        out_specs=[pl.BlockSpec((TM, D_MODEL), row), pl.BlockSpec((TM, D_MODEL), row)],
        out_shape=[jax.ShapeDtypeStruct((n, D_MODEL), F32), jax.ShapeDtypeStruct((n, D_MODEL), F32)],
        compiler_params=_cparams(("parallel",)),
        name="mix_proj",
    )(xc, xl, ac, al, fc, fl, mod, mod, mod, attn_g, four_g, w_out, norm2_g)


def _router_kernel(h2_ref, whi_ref, wlo_ref, br_ref, idx_ref, w_ref, cnt_ref):
    i = pl.program_id(0)
    h = h2_ref[...]
    h_hi = h.astype(BF16)
    h_lo = (h - h_hi.astype(F32)).astype(BF16)
    logits = (jnp.dot(h_hi, whi_ref[...], preferred_element_type=F32)
              + jnp.dot(h_lo, whi_ref[...], preferred_element_type=F32)
              + jnp.dot(h_hi, wlo_ref[...], preferred_element_type=F32)) + br_ref[...]
    lane = lax.broadcasted_iota(I32, logits.shape, 1)
    l = logits
    vals, idxs = [], []
    hits = jnp.zeros(logits.shape, F32)
    for _ in range(TOP_K):
        m = jnp.max(l, axis=-1, keepdims=True)
        ix = jnp.min(jnp.where(l == m, lane, LANES), axis=-1, keepdims=True)
        sel = lane == ix
        hits = hits + sel.astype(F32)
        l = jnp.where(sel, -jnp.inf, l)
        vals.append(m)
        idxs.append(ix)
    es = [jnp.exp(v - vals[0]) for v in vals]
    den = es[0] + es[1] + es[2] + es[3]
    idx_ref[...] = _pack_cols(idxs, lane)
    w_ref[...] = _pack_cols([e / den for e in es], lane)

    @pl.when(i == 0)
    def _():
        cnt_ref[...] = jnp.zeros(cnt_ref.shape, F32)

    cnt_ref[...] += jnp.sum(hits, axis=0, keepdims=True)


def _router(h2, w_hi, w_lo, br):
    n = h2.shape[0]
    row = lambda i: (i, 0)
    const = lambda i: (0, 0)
    return pl.pallas_call(
        _router_kernel,
        grid=(n // TM,),
        in_specs=[pl.BlockSpec((TM, D_MODEL), row), pl.BlockSpec((D_MODEL, LANES), const),
                  pl.BlockSpec((D_MODEL, LANES), const), pl.BlockSpec((1, LANES), const)],
        out_specs=[pl.BlockSpec((TM, LANES), row), pl.BlockSpec((TM, LANES), row), pl.BlockSpec((8, LANES), const)],
        out_shape=[jax.ShapeDtypeStruct((n, LANES), I32), jax.ShapeDtypeStruct((n, LANES), F32),
                   jax.ShapeDtypeStruct((8, LANES), F32)],
        compiler_params=_cparams(("arbitrary",)),
        name="router",
    )(h2, w_hi, w_lo, br)


def _pos_kernel(idx_ref, gs_ref, pos_ref, carry_ref):
    i = pl.program_id(0)

    @pl.when(i == 0)
    def _():
        carry_ref[...] = jnp.zeros(carry_ref.shape, F32)

    idx = idx_ref[...]
    lane = lax.broadcasted_iota(I32, idx.shape, 1)
    sels = [lane == idx[:, k:k + 1] for k in range(TOP_K)]
    hits = jnp.zeros(idx.shape, F32)
    for s in sels:
        hits = hits + s.astype(F32)
    r = lax.broadcasted_iota(I32, (TM, TM), 0)
    c = lax.broadcasted_iota(I32, (TM, TM), 1)
    before = (c < r).astype(BF16)
    rank = jnp.dot(before, hits.astype(BF16), preferred_element_type=F32) + carry_ref[0:1, :] + gs_ref[0:1, :]
    cols = [jnp.sum(jnp.where(s, rank, 0.0), axis=-1, keepdims=True) for s in sels]
    pos_ref[...] = _pack_cols(cols, lane).T[:8, :].astype(I32)
    carry_ref[...] += jnp.sum(hits, axis=0, keepdims=True)


def _positions(idx, gs_rows):
    n = idx.shape[0]
    return pl.pallas_call(
        _pos_kernel,
        grid=(n // TM,),
        in_specs=[pl.BlockSpec((TM, LANES), lambda i: (i, 0)), pl.BlockSpec((8, LANES), lambda i: (0, 0))],
        out_specs=pl.BlockSpec((8, TM), lambda i: (0, i)),
        out_shape=jax.ShapeDtypeStruct((8, n), I32),
        scratch_shapes=[pltpu.VMEM((8, LANES), F32)],
        compiler_params=_cparams(("arbitrary",)),
        name="positions",
    )(idx, gs_rows)


def _row_copy_wait(src, dst, sem):
    pltpu.make_async_copy(src, dst, sem).wait()


def _dispatch_kernel(ends_ref, pos_ref, h2_ref, xs_ref, zero_ref, sem):
    i = pl.program_id(0)

    @pl.when(i == 0)
    def _():
        zero_ref[...] = jnp.zeros(zero_ref.shape, F32)

        def last_tile(e):
            lo = ends_ref[e - 1] if e > 0 else 0
            start = pl.multiple_of(jnp.maximum(ends_ref[e] - TG, 0), TG)
            return ends_ref[e] > lo, pltpu.make_async_copy(zero_ref, xs_ref.at[pl.ds(start, TG), :], sem)

        for e in range(N_EXPERTS):
            nonempty, cp = last_tile(e)
            pl.when(nonempty)(cp.start)
        for e in range(N_EXPERTS):
            nonempty, cp = last_tile(e)
            pl.when(nonempty)(cp.wait)

        def tail_tile(j):
            return pltpu.make_async_copy(zero_ref, xs_ref.at[pl.ds(pl.multiple_of(j * TG, TG), TG), :], sem)

        first_unused = ends_ref[N_EXPERTS - 1] // TG
        n_tiles = xs_ref.shape[0] // TG
        lax.fori_loop(first_unused, n_tiles, lambda j, c: (tail_tile(j).start(), c)[1], 0)
        lax.fori_loop(first_unused, n_tiles, lambda j, c: (tail_tile(j).wait(), c)[1], 0)

    def issue(t, carry):
        for k in range(TOP_K):
            p = pos_ref[k, t]
            pltpu.make_async_copy(h2_ref.at[pl.ds(t, 1), :], xs_ref.at[pl.ds(p, 1), :], sem).start()
        return carry

    lax.fori_loop(0, TM, issue, 0)
    for k in range(TOP_K):
        _row_copy_wait(h2_ref, xs_ref.at[pl.ds(0, TM), :], sem)


def _dispatch(ends, pos, h2, n_rows):
    n = h2.shape[0]
    grid_spec = pltpu.PrefetchScalarGridSpec(
        num_scalar_prefetch=1,
        grid=(n // TM,),
        in_specs=[pl.BlockSpec((8, TM), lambda i, ends: (0, i), memory_space=pltpu.SMEM),
                  pl.BlockSpec((TM, D_MODEL), lambda i, ends: (i, 0))],
        out_specs=pl.BlockSpec(memory_space=pl.ANY),
        scratch_shapes=[pltpu.VMEM((TG, D_MODEL), F32), pltpu.SemaphoreType.DMA(())],
    )
    return pl.pallas_call(
        _dispatch_kernel,
        grid_spec=grid_spec,
        out_shape=jax.ShapeDtypeStruct((n_rows, D_MODEL), F32),
        compiler_params=_cparams(("arbitrary",)),
        name="dispatch",
    )(ends, pos, h2)


def _ffn_kernel(te_ref, nv_ref, x_ref, wgu_ref, bg_ref, bl_ref, wdn_ref, bdn_ref, perm_ref, o_ref,
                wg_s, wl_s, wd_s):
    i = pl.program_id(0)
    e = te_ref[i]
    prev = te_ref[jnp.maximum(i - 1, 0)]
    valid = i < nv_ref[0]
    fresh = jnp.logical_or(i == 0, e != prev)

    @pl.when(jnp.logical_and(valid, fresh))
    def _():
        for j in range(2 * D_EXPERT // (2 * LANES)):
            blk = wgu_ref[:, j * 2 * LANES:(j + 1) * 2 * LANES].astype(BF16)
            d = jnp.dot(blk, perm_ref[...], preferred_element_type=F32)
            wg_s[:, j * LANES:(j + 1) * LANES] = d[:, :LANES].astype(BF16)
            wl_s[:, j * LANES:(j + 1) * LANES] = d[:, LANES:].astype(BF16)
        wd_s[...] = wdn_ref[...].astype(BF16)

    @pl.when(valid)
    def _():
        x = x_ref[...].astype(BF16)
        g = jnp.dot(x, wg_s[...], preferred_element_type=F32) + bg_ref[...]
        l = jnp.dot(x, wl_s[...], preferred_element_type=F32) + bl_ref[...]
        g = jnp.minimum(g, SWIGLU_LIMIT)
        l = jnp.clip(l, -SWIGLU_LIMIT, SWIGLU_LIMIT)
        a = (l + 1.0) * (g * (1.0 / (1.0 + jnp.exp(-SWIGLU_ALPHA * g))))
        o_ref[...] = jnp.dot(a.astype(BF16), wd_s[...], preferred_element_type=F32) + bdn_ref[...]

    @pl.when(jnp.logical_not(valid))
    def _():
        o_ref[...] = jnp.zeros(o_ref.shape, F32)


def _expert_ffn(tile_expert, n_valid, xs, w_gu, b_g, b_l, w_dn, b_dn, perm):
    p = xs.shape[0]
    nt = p // TG
    tile = lambda i, te, nv: (jnp.minimum(i, nv[0] - 1), 0)
    ex3 = lambda i, te, nv: (te[i], 0, 0)
    grid_spec = pltpu.PrefetchScalarGridSpec(
        num_scalar_prefetch=2,
        grid=(nt,),
        in_specs=[pl.BlockSpec((TG, D_MODEL), tile),
                  pl.BlockSpec((None, D_MODEL, 2 * D_EXPERT), ex3),
                  pl.BlockSpec((None, 1, D_EXPERT), ex3),
                  pl.BlockSpec((None, 1, D_EXPERT), ex3),
                  pl.BlockSpec((None, D_EXPERT, D_MODEL), ex3),
                  pl.BlockSpec((None, 1, D_MODEL), ex3),
                  pl.BlockSpec((2 * LANES, 2 * LANES), lambda i, te, nv: (0, 0))],
        out_specs=pl.BlockSpec((TG, D_MODEL), lambda i, te, nv: (i, 0)),
        scratch_shapes=[pltpu.VMEM((D_MODEL, D_EXPERT), BF16), pltpu.VMEM((D_MODEL, D_EXPERT), BF16),
                        pltpu.VMEM((D_EXPERT, D_MODEL), BF16)],
    )
    return pl.pallas_call(
        _ffn_kernel,
        grid_spec=grid_spec,
        out_shape=jax.ShapeDtypeStruct((p, D_MODEL), F32),
        compiler_params=_cparams(("arbitrary",)),
        name="expert_ffn",
    )(tile_expert, n_valid, xs, w_gu, b_g, b_l, w_dn, b_dn, perm)


def _combine_kernel(pos_ref, w_ref, x1_ref, g2_ref, fg_ref, ys_ref, o_ref, buf, sem, *, tile0, nc, tpb):
    i = pl.program_id(0) + tile0
    r = _mod_row(i, nc, tpb)

    def issue(t, carry):
        for k in range(TOP_K):
            p = pos_ref[k, t]
            pltpu.make_async_copy(ys_ref.at[pl.ds(p, 1), :], buf.at[k, pl.ds(t, 1), :], sem).start()
        return carry

    lax.fori_loop(0, TM, issue, 0)
    for k in range(TOP_K):
        _row_copy_wait(ys_ref.at[pl.ds(0, TM), :], buf.at[k], sem)
    w = w_ref[...]
    moe = w[:, 0:1] * buf[0]
    for k in range(1, TOP_K):
        moe = moe + w[:, k:k + 1] * buf[k]
    y = x1_ref[...] + g2_ref[pl.ds(r, 1), :] * moe
    o_ref[...] = _rms(y) * fg_ref[...]


def _combine(pos, w, x1, mod, final_g, ys, *, tile0, ntiles, nc, tpb):
    return pl.pallas_call(
        functools.partial(_combine_kernel, tile0=tile0, nc=nc, tpb=tpb),
        grid=(ntiles,),
        in_specs=[pl.BlockSpec((8, TM), lambda i: (0, i + tile0), memory_space=pltpu.SMEM),
                  pl.BlockSpec((TM, LANES), lambda i: (i + tile0, 0)),
                  pl.BlockSpec((TM, D_MODEL), lambda i: (i + tile0, 0)),
                  pl.BlockSpec((8, D_MODEL), lambda i: (0, 5)),
                  pl.BlockSpec((1, D_MODEL), lambda i: (0, 0)),
                  pl.BlockSpec(memory_space=pl.ANY)],
        out_specs=pl.BlockSpec((TM, D_MODEL), lambda i: (i, 0)),
        out_shape=jax.ShapeDtypeStruct((ntiles * TM, D_MODEL), F32),
        scratch_shapes=[pltpu.VMEM((TOP_K, TM, D_MODEL), F32), pltpu.SemaphoreType.DMA(())],
        compiler_params=_cparams(("arbitrary",)),
        name="combine",
    )(pos, w, x1, mod, final_g, ys)


def kernel(x_prompt, x_sample, cache_k, cache_v, c, c_ctx, norm1_g, w_mod, b_mod, w_in, q_norm_g, k_norm_g,
           w_fourier, attn_out_g, fourier_out_g, w_out, norm2_g, w_router, b_router, w_gate_up, b_gate_up,
           w_down, b_down, final_g):
    bc, tc, _ = x_prompt.shape
    bl, tl, _ = x_sample.shape
    depth = w_in.shape[0]
    past = cache_k.shape[2]
    assert depth == 1 and tc % TQ == 0 and tl % TM == 0 and (bc * tc) % TM == 0
    n_ctx, n_lat = bc * tc, bl * tl
    n_tok = n_ctx + n_lat
    nc, nl, tpb = n_ctx // TM, n_lat // TM, tl // TM

    xc = x_prompt.reshape(n_ctx, D_MODEL)
    xl = x_sample.reshape(n_lat, D_MODEL)

    c_rows = jnp.concatenate([c_ctx[None, :], c, jnp.zeros((8 - 1 - bl, D_MODEL), F32)], axis=0)
    mod = _modulation(c_rows, w_mod[0], b_mod[0])

    cos_t, sin_t = _rope_tables(tl)
    q, k, v, f, k_new, v_new = _pre(
        xc, xl, mod, norm1_g, w_in[0].astype(BF16), jnp.tile(q_norm_g, (1, N_HEADS)),
        jnp.tile(k_norm_g, (1, N_KV_HEADS)), _head_mean_matrix(), cos_t, sin_t, nc=nc, nl=nl, tpb=tpb)

    attn_c = _attention(q, k, v, batch=bc, t=tc, s=tc, q_row0=0, kv_batched=False)
    keys = jnp.concatenate([cache_k[:, 0].reshape(bl, past, KV_W).astype(BF16),
                            k[n_ctx:].reshape(bl, tl, KV_W)], axis=1)
    vals = jnp.concatenate([cache_v[:, 0].reshape(bl, past, KV_W).astype(BF16),
                            v[n_ctx:].reshape(bl, tl, KV_W)], axis=1)
    attn_l = _attention(q, keys, vals, batch=bl, t=tl, s=past + tl, q_row0=n_ctx, kv_batched=True)

    wf = w_fourier[0].astype(BF16)
    zero = jnp.zeros_like(wf[0])
    wf_pairs = jnp.stack([jnp.block([[wf[0], zero], [zero, wf[1]]]), jnp.block([[wf[2], zero], [zero, wf[3]]])])
    four_c = _fourier(f, wf_pairs, batch=bc, t=tc, row0=0, n1=1)
    four_l = _fourier(f, wf_pairs, batch=bl, t=tl, row0=n_ctx, n1=FFT_N1)

    x1, h2 = _mix(xc, xl, attn_c, attn_l, four_c, four_l, mod, attn_out_g, fourier_out_g,
                  w_out[0].astype(BF16), norm2_g, nc=nc, nl=nl, tpb=tpb)

    wr = jnp.pad(w_router[0], ((0, 0), (0, LANES - N_EXPERTS)))
    wr_hi = wr.astype(BF16)
    wr_lo = (wr - wr_hi.astype(F32)).astype(BF16)
    br = jnp.pad(b_router[0], (0, LANES - N_EXPERTS), constant_values=-1e30).reshape(1, LANES)
    idx, gate_w, counts = _router(h2, wr_hi, wr_lo, br)
    cnt = counts[0, :N_EXPERTS].astype(I32)
    padded = ((cnt + TG - 1) // TG) * TG
    ends = jnp.cumsum(padded)
    starts = ends - padded
    n_rows = n_tok * TOP_K + N_EXPERTS * TG
    n_tiles = n_rows // TG
    tile_ids = jnp.arange(n_tiles, dtype=I32)
    tile_expert = jnp.minimum(jnp.sum((ends[None, :] // TG <= tile_ids[:, None]).astype(I32), axis=1),
                              N_EXPERTS - 1)
    n_valid = (ends[N_EXPERTS - 1] // TG).reshape(1)
    gs_rows = jnp.broadcast_to(jnp.pad(starts.astype(F32), (0, LANES - N_EXPERTS))[None, :], (8, LANES))
    pos = _positions(idx, gs_rows)

    xs = _dispatch(ends, pos, h2, n_rows)
    b_gu = b_gate_up[0].reshape(N_EXPERTS, 1, D_EXPERT, 2)
    ys = _expert_ffn(tile_expert, n_valid, xs, w_gate_up[0], b_gu[..., 0], b_gu[..., 1], w_down[0],
                     b_down[0].reshape(N_EXPERTS, 1, D_MODEL), _deinterleave_matrix())

    y_c = _combine(pos, gate_w, x1, mod, final_g.reshape(1, -1), ys, tile0=0, ntiles=nc, nc=nc, tpb=tpb)
    y_l = _combine(pos, gate_w, x1, mod, final_g.reshape(1, -1), ys, tile0=nc, ntiles=nl, nc=nc, tpb=tpb)

    new_k = k_new[:n_ctx].reshape(bc, 1, tc, N_KV_HEADS, HEAD_DIM)
    new_v = v_new[:n_ctx].reshape(bc, 1, tc, N_KV_HEADS, HEAD_DIM)
    return (y_c.reshape(bc, tc, D_MODEL), y_l.reshape(bl, tl, D_MODEL), new_k, new_v)
```

```python
import functools
import math

import numpy as np
import jax
import jax.numpy as jnp
from jax import lax
from jax.experimental import pallas as pl
from jax.experimental.pallas import tpu as pltpu

F32 = jnp.float32
BF16 = jnp.bfloat16
I32 = jnp.int32

D_MODEL = 1024
HEAD_DIM = 64
N_HEADS = 8
N_KV_HEADS = 2
HEADS_PER_KV = N_HEADS // N_KV_HEADS
ATTN_W = N_HEADS * HEAD_DIM
KV_W = N_KV_HEADS * HEAD_DIM
FOURIER_W = 512
FOURIER_GROUP_W = 128
D_IN = ATTN_W + 2 * KV_W + FOURIER_W
N_EXPERTS = 32
TOP_K = 4
D_EXPERT = 1024
SWIGLU_ALPHA = 1.702
SWIGLU_LIMIT = 7.0
ROPE_THETA = 10000.0
ROT_PAIRS = HEAD_DIM // 4
GRID_W = 64
EPS = 1e-6

LANES = 128
TM = 512
TQ = 256
TG = 256
FFT_N1 = 4
VMEM_LIMIT = 56 * 1024 * 1024


def _cparams(sem, vmem=VMEM_LIMIT):
    return pltpu.CompilerParams(dimension_semantics=sem, vmem_limit_bytes=vmem)


def _rope_tables(t_lat):
    pos = np.arange(t_lat)
    row = (pos // GRID_W).astype(np.float64)
    col = (pos % GRID_W).astype(np.float64)
    inv = ROPE_THETA ** (-np.arange(ROT_PAIRS, dtype=np.float64) / ROT_PAIRS)
    lane = np.arange(LANES)
    d = lane % HEAD_DIM
    axis = d // (2 * ROT_PAIRS)
    second = (d // ROT_PAIRS) % 2
    p = d % ROT_PAIRS
    ang = np.where(axis[None, :] == 0, row[:, None], col[:, None]) * inv[p][None, :]
    cos = np.cos(ang)
    sin = np.sin(ang) * np.where(second == 0, -1.0, 1.0)[None, :]
    cos = np.concatenate([np.ones((TM, LANES)), cos], axis=0)
    sin = np.concatenate([np.zeros((TM, LANES)), sin], axis=0)
    return jnp.asarray(cos, F32), jnp.asarray(sin, F32)


def _dft_tables(n1, n2):
    c = np.arange(FOURIER_GROUP_W)
    ang = 2.0 * np.pi * np.outer(c, c) / FOURIER_GROUP_W
    z = np.zeros_like(ang)
    cbd = np.block([[np.cos(ang), z], [z, np.cos(ang)]])
    sbd = np.block([[np.sin(ang), z], [z, np.sin(ang)]])
    cd = np.concatenate([cbd, -sbd], axis=1)
    u = np.arange(n2)
    ang2 = 2.0 * np.pi * np.outer(u, u) / n2
    ct, st = np.cos(ang2), np.sin(ang2)
    nt = max(n1 - 1, 1)
    tw_ang = 2.0 * np.pi * np.outer(np.arange(1, nt + 1), u) / (n1 * n2)
    twc = np.repeat(np.cos(tw_ang)[:, :, None], LANES, axis=2)
    tws = np.repeat(np.sin(tw_ang)[:, :, None], LANES, axis=2)
    return (jnp.asarray(cd, F32), jnp.asarray(ct, F32), jnp.asarray(st, F32),
            jnp.asarray(twc, F32), jnp.asarray(tws, F32))


def _head_mean_matrix():
    h = np.arange(ATTN_W) // HEAD_DIM
    return jnp.asarray((h[:, None] == h[None, :]) / HEAD_DIM, BF16)


def _deinterleave_matrix():
    p = np.zeros((2 * LANES, 2 * LANES))
    m = np.arange(LANES)
    p[2 * m, m] = 1.0
    p[2 * m + 1, LANES + m] = 1.0
    return jnp.asarray(p, BF16)


def _mod_row(i, nc, tpb):
    return jnp.where(i < nc, 0, 1 + (i - nc) // tpb)


def _rms(x):
    return x * lax.rsqrt(jnp.mean(x * x, axis=-1, keepdims=True) + EPS)


def _pack_cols(cols, lane):
    out = jnp.zeros(lane.shape, cols[0].dtype)
    for k, c in enumerate(cols):
        out = jnp.where(lane == k, c, out)
    return out


def _mod_kernel(c_ref, w_ref, b_ref, o_ref):
    c = c_ref[...]
    s = c * (1.0 / (1.0 + jnp.exp(-c)))
    o_ref[...] = jnp.dot(s.astype(BF16), w_ref[...].astype(BF16), preferred_element_type=F32) + b_ref[...]


def _modulation(c_rows, w_mod, b_mod):
    n = w_mod.shape[1] // D_MODEL
    return pl.pallas_call(
        _mod_kernel,
        grid=(n,),
        in_specs=[pl.BlockSpec((8, D_MODEL), lambda j: (0, 0)),
                  pl.BlockSpec((D_MODEL, D_MODEL), lambda j: (0, j)),
                  pl.BlockSpec((1, D_MODEL), lambda j: (0, j))],
        out_specs=pl.BlockSpec((8, D_MODEL), lambda j: (0, j)),
        out_shape=jax.ShapeDtypeStruct((8, w_mod.shape[1]), F32),
        compiler_params=_cparams(("parallel",)),
        name="modulation",
    )(c_rows, w_mod, b_mod.reshape(1, -1))


def _pre_kernel(xc_ref, xl_ref, sh_ref, sc_ref, g1_ref, win_ref, qg_ref, kg_ref, bd_ref, cos_ref, sin_ref,
                q_ref, k_ref, v_ref, f_ref, kn_ref, vn_ref, *, nc, tpb):
    i = pl.program_id(0)
    r = _mod_row(i, nc, tpb)
    x = jnp.where(i < nc, xc_ref[...], xl_ref[...])
    h = _rms(x) * g1_ref[...]
    h = h * (1.0 + sc_ref[pl.ds(r, 1), :]) + sh_ref[pl.ds(r, 1), :]
    proj = jnp.dot(h.astype(BF16), win_ref[...], preferred_element_type=F32)
    q = proj[:, :ATTN_W]
    k = proj[:, ATTN_W:ATTN_W + KV_W]
    v = proj[:, ATTN_W + KV_W:ATTN_W + 2 * KV_W]
    f_ref[...] = proj[:, ATTN_W + 2 * KV_W:].astype(BF16)
    vn_ref[...] = v
    v_ref[...] = v.astype(BF16)
    q_ms = jnp.dot((q * q).astype(BF16), bd_ref[...], preferred_element_type=F32)
    k_ms = jnp.dot((k * k).astype(BF16), bd_ref[:KV_W, :KV_W], preferred_element_type=F32)
    qn = q * lax.rsqrt(q_ms + EPS) * qg_ref[...]
    kn = k * lax.rsqrt(k_ms + EPS) * kg_ref[...]
    kn_ref[...] = kn
    cos = cos_ref[...]
    sin = sin_ref[...]
    lane = lax.broadcasted_iota(I32, cos.shape, 1)
    first = (lane & ROT_PAIRS) == 0

    def rope(c):
        partner = jnp.where(first, pltpu.roll(c, LANES - ROT_PAIRS, 1), pltpu.roll(c, ROT_PAIRS, 1))
        return c * cos + partner * sin

    k_ref[...] = rope(kn).astype(BF16)
    scale = HEAD_DIM ** -0.5 * math.log2(math.e)
    q_rot = jnp.concatenate([rope(qn[:, j * LANES:(j + 1) * LANES]) for j in range(ATTN_W // LANES)], axis=1)
    q_ref[...] = (q_rot * scale).T.astype(BF16)


def _pre(xc, xl, mod, norm1_g, w_in, q_g, k_g, bd, cos_t, sin_t, *, nc, nl, tpb):
    n = (nc + nl) * TM
    row = lambda i: (i, 0)
    const = lambda i: (0, 0)
    tab = lambda i: (jnp.where(i < nc, 0, 1 + (i - nc) % tpb), 0)
    return pl.pallas_call(
        functools.partial(_pre_kernel, nc=nc, tpb=tpb),
        grid=(nc + nl,),
        in_specs=[pl.BlockSpec((TM, D_MODEL), lambda i: (jnp.minimum(i, nc - 1), 0)),
                  pl.BlockSpec((TM, D_MODEL), lambda i: (jnp.maximum(i - nc, 0), 0)),
                  pl.BlockSpec((8, D_MODEL), lambda i: (0, 0)),
                  pl.BlockSpec((8, D_MODEL), lambda i: (0, 1)),
                  pl.BlockSpec((1, D_MODEL), const),
                  pl.BlockSpec((D_MODEL, D_IN), const),
                  pl.BlockSpec((1, ATTN_W), const),
                  pl.BlockSpec((1, KV_W), const),
                  pl.BlockSpec((ATTN_W, ATTN_W), const),
                  pl.BlockSpec((TM, LANES), tab),
                  pl.BlockSpec((TM, LANES), tab)],
        out_specs=[pl.BlockSpec((ATTN_W, TM), lambda i: (0, i)), pl.BlockSpec((TM, KV_W), row),
                   pl.BlockSpec((TM, KV_W), row),
                   pl.BlockSpec((TM, FOURIER_W), row), pl.BlockSpec((TM, KV_W), row), pl.BlockSpec((TM, KV_W), row)],
        out_shape=[jax.ShapeDtypeStruct((ATTN_W, n), BF16), jax.ShapeDtypeStruct((n, KV_W), BF16),
                   jax.ShapeDtypeStruct((n, KV_W), BF16), jax.ShapeDtypeStruct((n, FOURIER_W), BF16),
                   jax.ShapeDtypeStruct((n, KV_W), F32), jax.ShapeDtypeStruct((n, KV_W), F32)],
        compiler_params=_cparams(("parallel",)),
        name="pre_proj",
    )(xc, xl, mod, mod, norm1_g, w_in, q_g, k_g, bd, cos_t, sin_t)


V_ROWS = 80


def _attn_kernel(qt_ref, k_ref, vt_ref, o_ref, ot_ref):
    def head(h, carry):
        j = h // HEADS_PER_KV
        qt = qt_ref[pl.ds(pl.multiple_of(h * HEAD_DIM, HEAD_DIM), HEAD_DIM), :]
        st = jnp.dot(k_ref[j], qt, preferred_element_type=F32)
        pt = jnp.exp2(st - jnp.max(st, axis=0, keepdims=True)).astype(BF16)
        ot = jnp.dot(vt_ref[j], pt, preferred_element_type=F32)
        ot_ref[pl.ds(pl.multiple_of(h * HEAD_DIM, HEAD_DIM), HEAD_DIM), :] = (
            ot[:HEAD_DIM] * (1.0 / ot[HEAD_DIM:HEAD_DIM + 1]))
        return carry

    lax.fori_loop(0, N_HEADS, head, 0)
    o_ref[...] = ot_ref[...].T


def _attention(qt, k_heads, vt_ext, *, batch, t, s, q_row0):
    qpb = t // TQ
    q0 = q_row0 // TQ
    return pl.pallas_call(
        _attn_kernel,
        grid=(batch, qpb),
        in_specs=[pl.BlockSpec((ATTN_W, TQ), lambda b, i: (0, q0 + b * qpb + i)),
                  pl.BlockSpec((None, N_KV_HEADS, s, HEAD_DIM), lambda b, i: (b, 0, 0, 0)),
                  pl.BlockSpec((None, N_KV_HEADS, V_ROWS, s), lambda b, i: (b, 0, 0, 0))],
        out_specs=pl.BlockSpec((TQ, ATTN_W), lambda b, i: (b * qpb + i, 0)),
        out_shape=jax.ShapeDtypeStruct((batch * t, ATTN_W), F32),
        scratch_shapes=[pltpu.VMEM((ATTN_W, TQ), F32)],
        compiler_params=_cparams(("parallel", "parallel")),
        name="attention",
    )(qt, k_heads, vt_ext)


def _split_kv_heads(keys, vals):
    b, s, _ = keys.shape
    k_heads = keys.reshape(b, s, N_KV_HEADS, HEAD_DIM).transpose(0, 2, 1, 3)
    vt = vals.reshape(b, s, N_KV_HEADS, HEAD_DIM).transpose(0, 2, 3, 1)
    extra = jnp.concatenate([jnp.ones((b, N_KV_HEADS, 1, s), BF16),
                             jnp.zeros((b, N_KV_HEADS, V_ROWS - HEAD_DIM - 1, s), BF16)], axis=2)
    return k_heads, jnp.concatenate([vt, extra], axis=2)


def _fourier_kernel(x_ref, cd_ref, ct_ref, st_ref, twc_ref, tws_ref, wf_ref, o_ref, *, n1, n2):
    pw = 2 * FOURIER_GROUP_W
    scale = 1.0 / math.sqrt(n1 * n2 * FOURIER_GROUP_W)
    for gp in range(FOURIER_W // pw):
        z = jnp.dot(x_ref[:, gp * pw:(gp + 1) * pw], cd_ref[...], preferred_element_type=F32)
        zr = [z[t1 * n2:(t1 + 1) * n2, :pw] for t1 in range(n1)]
        zi = [z[t1 * n2:(t1 + 1) * n2, pw:] for t1 in range(n1)]
        if n1 == 1:
            a = [(zr[0], zi[0])]
        else:
            a = [(zr[0] + zr[1] + zr[2] + zr[3], zi[0] + zi[1] + zi[2] + zi[3]),
                 (zr[0] + zi[1] - zr[2] - zi[3], zi[0] - zr[1] - zi[2] + zr[3]),
                 (zr[0] - zr[1] + zr[2] - zr[3], zi[0] - zi[1] + zi[2] - zi[3]),
                 (zr[0] - zi[1] - zr[2] + zi[3], zi[0] + zr[1] - zi[2] - zr[3])]
        for u1 in range(n1):
            ar, ai = a[u1]
            if u1 > 0:
                c = jnp.concatenate([twc_ref[u1 - 1]] * 2, axis=1)
                s = jnp.concatenate([tws_ref[u1 - 1]] * 2, axis=1)
                ar, ai = ar * c + ai * s, ai * c - ar * s
            y = (jnp.dot(ct_ref[...], ar.astype(BF16), preferred_element_type=F32)
                 + jnp.dot(st_ref[...], ai.astype(BF16), preferred_element_type=F32)) * scale
            y = jnp.dot(y.astype(BF16), wf_ref[gp], preferred_element_type=F32)
            rows = pl.ds(u1, n2, stride=n1) if n1 > 1 else slice(None)
            for half in range(2):
                o_ref[2 * gp + half, rows, :] = y[:, half * FOURIER_GROUP_W:(half + 1) * FOURIER_GROUP_W]


def _fourier(f, wf_pairs, *, batch, t, row0, n1):
    n2 = t // n1
    cd, ct, st, twc, tws = _dft_tables(n1, n2)
    cd, ct, st = cd.astype(BF16), ct.astype(BF16), st.astype(BF16)
    nt = twc.shape[0]
    b0 = row0 // t
    c2 = lambda b: (0, 0)
    c3 = lambda b: (0, 0, 0)
    return pl.pallas_call(
        functools.partial(_fourier_kernel, n1=n1, n2=n2),
        grid=(batch,),
        in_specs=[pl.BlockSpec((t, FOURIER_W), lambda b: (b0 + b, 0)),
                  pl.BlockSpec(cd.shape, c2), pl.BlockSpec((n2, n2), c2), pl.BlockSpec((n2, n2), c2),
                  pl.BlockSpec((nt, n2, LANES), c3), pl.BlockSpec((nt, n2, LANES), c3),
                  pl.BlockSpec(wf_pairs.shape, c3)],
        out_specs=pl.BlockSpec((FOURIER_W // FOURIER_GROUP_W, t, FOURIER_GROUP_W), lambda b: (0, b, 0)),
        out_shape=jax.ShapeDtypeStruct((FOURIER_W // FOURIER_GROUP_W, batch * t, FOURIER_GROUP_W), F32),
        compiler_params=_cparams(("parallel",)),
        name="fourier",
    )(f, cd, ct, st, twc, tws, wf_pairs)


def _mix_kernel(xc_ref, xl_ref, ac_ref, al_ref, fc_ref, fl_ref, g1_ref, sh2_ref, sc2_ref, ag_ref, fg_ref,
                wout_ref, n2g_ref, x1_ref, h2_ref, *, nc, tpb):
    i = pl.program_id(0)
    r = _mod_row(i, nc, tpb)
    is_ctx = i < nc
    x = jnp.where(is_ctx, xc_ref[...], xl_ref[...])
    a = _rms(jnp.where(is_ctx, ac_ref[...], al_ref[...])) * ag_ref[...]
    n_groups = FOURIER_W // FOURIER_GROUP_W
    f = jnp.concatenate([jnp.where(is_ctx, fc_ref[g], fl_ref[g]) for g in range(n_groups)], axis=1)
    f = _rms(f) * fg_ref[...]
    mixed = (jnp.dot(a.astype(BF16), wout_ref[:ATTN_W, :], preferred_element_type=F32)
             + jnp.dot(f.astype(BF16), wout_ref[ATTN_W:, :], preferred_element_type=F32))
    x1 = x + g1_ref[pl.ds(r, 1), :] * mixed
    x1_ref[...] = x1
    h2 = _rms(x1) * n2g_ref[...]
    h2_ref[...] = h2 * (1.0 + sc2_ref[pl.ds(r, 1), :]) + sh2_ref[pl.ds(r, 1), :]


def _mix(xc, xl, ac, al, fc, fl, mod, attn_g, four_g, w_out, norm2_g, *, nc, nl, tpb):
    n = (nc + nl) * TM
    lo = lambda i: (jnp.minimum(i, nc - 1), 0)
    hi = lambda i: (jnp.maximum(i - nc, 0), 0)
    const = lambda i: (0, 0)
    row = lambda i: (i, 0)
    return pl.pallas_call(
        functools.partial(_mix_kernel, nc=nc, tpb=tpb),
        grid=(nc + nl,),
        in_specs=[pl.BlockSpec((TM, D_MODEL), lo), pl.BlockSpec((TM, D_MODEL), hi),
                  pl.BlockSpec((TM, ATTN_W), lo), pl.BlockSpec((TM, ATTN_W), hi),
                  pl.BlockSpec((FOURIER_W // FOURIER_GROUP_W, TM, FOURIER_GROUP_W),
                               lambda i: (0, jnp.minimum(i, nc - 1), 0)),
                  pl.BlockSpec((FOURIER_W // FOURIER_GROUP_W, TM, FOURIER_GROUP_W),
                               lambda i: (0, jnp.maximum(i - nc, 0), 0)),
                  pl.BlockSpec((8, D_MODEL), lambda i: (0, 2)),
                  pl.BlockSpec((8, D_MODEL), lambda i: (0, 3)),
                  pl.BlockSpec((8, D_MODEL), lambda i: (0, 4)),
                  pl.BlockSpec((1, ATTN_W), const), pl.BlockSpec((1, FOURIER_W), const),
                  pl.BlockSpec((D_MODEL, D_MODEL), const), pl.BlockSpec((1, D_MODEL), const)],
        out_specs=[pl.BlockSpec((TM, D_MODEL), row), pl.BlockSpec((TM, D_MODEL), row)],
        out_shape=[jax.ShapeDtypeStruct((n, D_MODEL), F32), jax.ShapeDtypeStruct((n, D_MODEL), F32)],
        compiler_params=_cparams(("parallel",)),
        name="mix_proj",
    )(xc, xl, ac, al, fc, fl, mod, mod, mod, attn_g, four_g, w_out, norm2_g)


def _router_kernel(h2_ref, whi_ref, wlo_ref, br_ref, idx_ref, w_ref, cnt_ref):
    i = pl.program_id(0)
    h = h2_ref[...]
    h_hi = h.astype(BF16)
    h_lo = (h - h_hi.astype(F32)).astype(BF16)
    logits = (jnp.dot(h_hi, whi_ref[...], preferred_element_type=F32)
              + jnp.dot(h_lo, whi_ref[...], preferred_element_type=F32)
              + jnp.dot(h_hi, wlo_ref[...], preferred_element_type=F32)) + br_ref[...]
    lane = lax.broadcasted_iota(I32, logits.shape, 1)
    l = logits
    vals, idxs = [], []
    hits = jnp.zeros(logits.shape, F32)
    for _ in range(TOP_K):
        m = jnp.max(l, axis=-1, keepdims=True)
        ix = jnp.min(jnp.where(l == m, lane, LANES), axis=-1, keepdims=True)
        sel = lane == ix
        hits = hits + sel.astype(F32)
        l = jnp.where(sel, -jnp.inf, l)
        vals.append(m)
        idxs.append(ix)
    es = [jnp.exp(v - vals[0]) for v in vals]
    den = es[0] + es[1] + es[2] + es[3]
    idx_ref[...] = _pack_cols(idxs, lane)
    w_ref[...] = _pack_cols([e / den for e in es], lane)

    @pl.when(i == 0)
    def _():
        cnt_ref[...] = jnp.zeros(cnt_ref.shape, F32)

    cnt_ref[...] += jnp.sum(hits, axis=0, keepdims=True)


def _router(h2, w_hi, w_lo, br):
    n = h2.shape[0]
    row = lambda i: (i, 0)
    const = lambda i: (0, 0)
    return pl.pallas_call(
        _router_kernel,
        grid=(n // TM,),
        in_specs=[pl.BlockSpec((TM, D_MODEL), row), pl.BlockSpec((D_MODEL, LANES), const),
                  pl.BlockSpec((D_MODEL, LANES), const), pl.BlockSpec((1, LANES), const)],
        out_specs=[pl.BlockSpec((TM, LANES), row), pl.BlockSpec((TM, LANES), row), pl.BlockSpec((8, LANES), const)],
        out_shape=[jax.ShapeDtypeStruct((n, LANES), I32), jax.ShapeDtypeStruct((n, LANES), F32),
                   jax.ShapeDtypeStruct((8, LANES), F32)],
        compiler_params=_cparams(("arbitrary",)),
        name="router",
    )(h2, w_hi, w_lo, br)


def _pos_kernel(idx_ref, gs_ref, pos_ref, carry_ref):
    i = pl.program_id(0)

    @pl.when(i == 0)
    def _():
        carry_ref[...] = jnp.zeros(carry_ref.shape, F32)

    idx = idx_ref[...]
    lane = lax.broadcasted_iota(I32, idx.shape, 1)
    sels = [lane == idx[:, k:k + 1] for k in range(TOP_K)]
    hits = jnp.zeros(idx.shape, F32)
    for s in sels:
        hits = hits + s.astype(F32)
    r = lax.broadcasted_iota(I32, (TM, TM), 0)
    c = lax.broadcasted_iota(I32, (TM, TM), 1)
    before = (c < r).astype(BF16)
    rank = jnp.dot(before, hits.astype(BF16), preferred_element_type=F32) + carry_ref[0:1, :] + gs_ref[0:1, :]
    cols = [jnp.sum(jnp.where(s, rank, 0.0), axis=-1, keepdims=True) for s in sels]
    pos_ref[...] = _pack_cols(cols, lane).T[:8, :].astype(I32)
    carry_ref[...] += jnp.sum(hits, axis=0, keepdims=True)


def _positions(idx, gs_rows):
    n = idx.shape[0]
    return pl.pallas_call(
        _pos_kernel,
        grid=(n // TM,),
        in_specs=[pl.BlockSpec((TM, LANES), lambda i: (i, 0)), pl.BlockSpec((8, LANES), lambda i: (0, 0))],
        out_specs=pl.BlockSpec((8, TM), lambda i: (0, i)),
        out_shape=jax.ShapeDtypeStruct((8, n), I32),
        scratch_shapes=[pltpu.VMEM((8, LANES), F32)],
        compiler_params=_cparams(("arbitrary",)),
        name="positions",
    )(idx, gs_rows)


def _row_copy_wait(src, dst, sem):
    pltpu.make_async_copy(src, dst, sem).wait()


def _dispatch_kernel(ends_ref, pos_ref, h2_ref, xs_ref, zero_ref, sem):
    i = pl.program_id(0)

    @pl.when(i == 0)
    def _():
        zero_ref[...] = jnp.zeros(zero_ref.shape, F32)

        def last_tile(e):
            lo = ends_ref[e - 1] if e > 0 else 0
            start = pl.multiple_of(jnp.maximum(ends_ref[e] - TG, 0), TG)
            return ends_ref[e] > lo, pltpu.make_async_copy(zero_ref, xs_ref.at[pl.ds(start, TG), :], sem)

        for e in range(N_EXPERTS):
            nonempty, cp = last_tile(e)
            pl.when(nonempty)(cp.start)
        for e in range(N_EXPERTS):
            nonempty, cp = last_tile(e)
            pl.when(nonempty)(cp.wait)

        def tail_tile(j):
            return pltpu.make_async_copy(zero_ref, xs_ref.at[pl.ds(pl.multiple_of(j * TG, TG), TG), :], sem)

        first_unused = ends_ref[N_EXPERTS - 1] // TG
        n_tiles = xs_ref.shape[0] // TG
        lax.fori_loop(first_unused, n_tiles, lambda j, c: (tail_tile(j).start(), c)[1], 0)
        lax.fori_loop(first_unused, n_tiles, lambda j, c: (tail_tile(j).wait(), c)[1], 0)

    def issue(t, carry):
        for k in range(TOP_K):
            p = pos_ref[k, t]
            pltpu.make_async_copy(h2_ref.at[pl.ds(t, 1), :], xs_ref.at[pl.ds(p, 1), :], sem).start()
        return carry

    lax.fori_loop(0, TM, issue, 0)
    for k in range(TOP_K):
        _row_copy_wait(h2_ref, xs_ref.at[pl.ds(0, TM), :], sem)


def _dispatch(ends, pos, h2, n_rows):
    n = h2.shape[0]
    grid_spec = pltpu.PrefetchScalarGridSpec(
        num_scalar_prefetch=1,
        grid=(n // TM,),
        in_specs=[pl.BlockSpec((8, TM), lambda i, ends: (0, i), memory_space=pltpu.SMEM),
                  pl.BlockSpec((TM, D_MODEL), lambda i, ends: (i, 0))],
        out_specs=pl.BlockSpec(memory_space=pl.ANY),
        scratch_shapes=[pltpu.VMEM((TG, D_MODEL), F32), pltpu.SemaphoreType.DMA(())],
    )
    return pl.pallas_call(
        _dispatch_kernel,
        grid_spec=grid_spec,
        out_shape=jax.ShapeDtypeStruct((n_rows, D_MODEL), F32),
        compiler_params=_cparams(("arbitrary",)),
        name="dispatch",
    )(ends, pos, h2)


def _ffn_kernel(te_ref, nv_ref, x_ref, wgu_ref, bg_ref, bl_ref, wdn_ref, bdn_ref, perm_ref, o_ref,
                wg_s, wl_s, wd_s):
    i = pl.program_id(0)
    e = te_ref[i]
    prev = te_ref[jnp.maximum(i - 1, 0)]
    valid = i < nv_ref[0]
    fresh = jnp.logical_or(i == 0, e != prev)

    @pl.when(jnp.logical_and(valid, fresh))
    def _():
        for j in range(2 * D_EXPERT // (2 * LANES)):
            blk = wgu_ref[:, j * 2 * LANES:(j + 1) * 2 * LANES].astype(BF16)
            d = jnp.dot(blk, perm_ref[...], preferred_element_type=F32)
            wg_s[:, j * LANES:(j + 1) * LANES] = d[:, :LANES].astype(BF16)
            wl_s[:, j * LANES:(j + 1) * LANES] = d[:, LANES:].astype(BF16)
        wd_s[...] = wdn_ref[...].astype(BF16)

    @pl.when(valid)
    def _():
        x = x_ref[...].astype(BF16)
        g = jnp.dot(x, wg_s[...], preferred_element_type=F32) + bg_ref[...]
        l = jnp.dot(x, wl_s[...], preferred_element_type=F32) + bl_ref[...]
        g = jnp.minimum(g, SWIGLU_LIMIT)
        l = jnp.clip(l, -SWIGLU_LIMIT, SWIGLU_LIMIT)
        a = (l + 1.0) * (g * (1.0 / (1.0 + jnp.exp(-SWIGLU_ALPHA * g))))
        o_ref[...] = jnp.dot(a.astype(BF16), wd_s[...], preferred_element_type=F32) + bdn_ref[...]

    @pl.when(jnp.logical_not(valid))
    def _():
        o_ref[...] = jnp.zeros(o_ref.shape, F32)


def _expert_ffn(tile_expert, n_valid, xs, w_gu, b_g, b_l, w_dn, b_dn, perm):
    p = xs.shape[0]
    nt = p // TG
    tile = lambda i, te, nv: (jnp.minimum(i, nv[0] - 1), 0)
    ex3 = lambda i, te, nv: (te[i], 0, 0)
    grid_spec = pltpu.PrefetchScalarGridSpec(
        num_scalar_prefetch=2,
        grid=(nt,),
        in_specs=[pl.BlockSpec((TG, D_MODEL), tile),
                  pl.BlockSpec((None, D_MODEL, 2 * D_EXPERT), ex3),
                  pl.BlockSpec((None, 1, D_EXPERT), ex3),
                  pl.BlockSpec((None, 1, D_EXPERT), ex3),
                  pl.BlockSpec((None, D_EXPERT, D_MODEL), ex3),
                  pl.BlockSpec((None, 1, D_MODEL), ex3),
                  pl.BlockSpec((2 * LANES, 2 * LANES), lambda i, te, nv: (0, 0))],
        out_specs=pl.BlockSpec((TG, D_MODEL), lambda i, te, nv: (i, 0)),
        scratch_shapes=[pltpu.VMEM((D_MODEL, D_EXPERT), BF16), pltpu.VMEM((D_MODEL, D_EXPERT), BF16),
                        pltpu.VMEM((D_EXPERT, D_MODEL), BF16)],
    )
    return pl.pallas_call(
        _ffn_kernel,
        grid_spec=grid_spec,
        out_shape=jax.ShapeDtypeStruct((p, D_MODEL), F32),
        compiler_params=_cparams(("arbitrary",)),
        name="expert_ffn",
    )(tile_expert, n_valid, xs, w_gu, b_g, b_l, w_dn, b_dn, perm)


def _combine_kernel(pos_ref, w_ref, x1_ref, g2_ref, fg_ref, ys_ref, o_ref, buf, sem, *, tile0, nc, tpb):
    i = pl.program_id(0) + tile0
    r = _mod_row(i, nc, tpb)

    def issue(t, carry):
        for k in range(TOP_K):
            p = pos_ref[k, t]
            pltpu.make_async_copy(ys_ref.at[pl.ds(p, 1), :], buf.at[k, pl.ds(t, 1), :], sem).start()
        return carry

    lax.fori_loop(0, TM, issue, 0)
    for k in range(TOP_K):
        _row_copy_wait(ys_ref.at[pl.ds(0, TM), :], buf.at[k], sem)
    w = w_ref[...]
    moe = w[:, 0:1] * buf[0]
    for k in range(1, TOP_K):
        moe = moe + w[:, k:k + 1] * buf[k]
    y = x1_ref[...] + g2_ref[pl.ds(r, 1), :] * moe
    o_ref[...] = _rms(y) * fg_ref[...]


def _combine(pos, w, x1, mod, final_g, ys, *, tile0, ntiles, nc, tpb):
    return pl.pallas_call(
        functools.partial(_combine_kernel, tile0=tile0, nc=nc, tpb=tpb),
        grid=(ntiles,),
        in_specs=[pl.BlockSpec((8, TM), lambda i: (0, i + tile0), memory_space=pltpu.SMEM),
                  pl.BlockSpec((TM, LANES), lambda i: (i + tile0, 0)),
                  pl.BlockSpec((TM, D_MODEL), lambda i: (i + tile0, 0)),
                  pl.BlockSpec((8, D_MODEL), lambda i: (0, 5)),
                  pl.BlockSpec((1, D_MODEL), lambda i: (0, 0)),
                  pl.BlockSpec(memory_space=pl.ANY)],
        out_specs=pl.BlockSpec((TM, D_MODEL), lambda i: (i, 0)),
        out_shape=jax.ShapeDtypeStruct((ntiles * TM, D_MODEL), F32),
        scratch_shapes=[pltpu.VMEM((TOP_K, TM, D_MODEL), F32), pltpu.SemaphoreType.DMA(())],
        compiler_params=_cparams(("arbitrary",)),
        name="combine",
    )(pos, w, x1, mod, final_g, ys)


def kernel(x_prompt, x_sample, cache_k, cache_v, c, c_ctx, norm1_g, w_mod, b_mod, w_in, q_norm_g, k_norm_g,
           w_fourier, attn_out_g, fourier_out_g, w_out, norm2_g, w_router, b_router, w_gate_up, b_gate_up,
           w_down, b_down, final_g):
    bc, tc, _ = x_prompt.shape
    bl, tl, _ = x_sample.shape
    depth = w_in.shape[0]
    past = cache_k.shape[2]
    assert depth == 1 and tc % TQ == 0 and tl % TM == 0 and (bc * tc) % TM == 0
    n_ctx, n_lat = bc * tc, bl * tl
    n_tok = n_ctx + n_lat
    nc, nl, tpb = n_ctx // TM, n_lat // TM, tl // TM

    xc = x_prompt.reshape(n_ctx, D_MODEL)
    xl = x_sample.reshape(n_lat, D_MODEL)

    c_rows = jnp.concatenate([c_ctx[None, :], c, jnp.zeros((8 - 1 - bl, D_MODEL), F32)], axis=0)
    mod = _modulation(c_rows, w_mod[0], b_mod[0])

    cos_t, sin_t = _rope_tables(tl)
    q, k, v, f, k_new, v_new = _pre(
        xc, xl, mod, norm1_g, w_in[0].astype(BF16), jnp.tile(q_norm_g, (1, N_HEADS)),
        jnp.tile(k_norm_g, (1, N_KV_HEADS)), _head_mean_matrix(), cos_t, sin_t, nc=nc, nl=nl, tpb=tpb)

    kc, vtc = _split_kv_heads(k[:n_ctx].reshape(bc, tc, KV_W), v[:n_ctx].reshape(bc, tc, KV_W))
    attn_c = _attention(q, kc, vtc, batch=bc, t=tc, s=tc, q_row0=0)
    keys = jnp.concatenate([cache_k[:, 0].reshape(bl, past, KV_W).astype(BF16),
                            k[n_ctx:].reshape(bl, tl, KV_W)], axis=1)
    vals = jnp.concatenate([cache_v[:, 0].reshape(bl, past, KV_W).astype(BF16),
                            v[n_ctx:].reshape(bl, tl, KV_W)], axis=1)
    kl, vtl = _split_kv_heads(keys, vals)
    attn_l = _attention(q, kl, vtl, batch=bl, t=tl, s=past + tl, q_row0=n_ctx)

    wf = w_fourier[0].astype(BF16)
    zero = jnp.zeros_like(wf[0])
    wf_pairs = jnp.stack([jnp.block([[wf[0], zero], [zero, wf[1]]]), jnp.block([[wf[2], zero], [zero, wf[3]]])])
    four_c = _fourier(f, wf_pairs, batch=bc, t=tc, row0=0, n1=1)
    four_l = _fourier(f, wf_pairs, batch=bl, t=tl, row0=n_ctx, n1=FFT_N1)

    x1, h2 = _mix(xc, xl, attn_c, attn_l, four_c, four_l, mod, attn_out_g, fourier_out_g,
                  w_out[0].astype(BF16), norm2_g, nc=nc, nl=nl, tpb=tpb)

    wr = jnp.pad(w_router[0], ((0, 0), (0, LANES - N_EXPERTS)))
    wr_hi = wr.astype(BF16)
    wr_lo = (wr - wr_hi.astype(F32)).astype(BF16)
    br = jnp.pad(b_router[0], (0, LANES - N_EXPERTS), constant_values=-1e30).reshape(1, LANES)
    idx, gate_w, counts = _router(h2, wr_hi, wr_lo, br)
    cnt = counts[0, :N_EXPERTS].astype(I32)
    padded = ((cnt + TG - 1) // TG) * TG
    ends = jnp.cumsum(padded)
    starts = ends - padded
    n_rows = n_tok * TOP_K + N_EXPERTS * TG
    n_tiles = n_rows // TG
    tile_ids = jnp.arange(n_tiles, dtype=I32)
    tile_expert = jnp.minimum(jnp.sum((ends[None, :] // TG <= tile_ids[:, None]).astype(I32), axis=1),
                              N_EXPERTS - 1)
    n_valid = (ends[N_EXPERTS - 1] // TG).reshape(1)
    gs_rows = jnp.broadcast_to(jnp.pad(starts.astype(F32), (0, LANES - N_EXPERTS))[None, :], (8, LANES))
    pos = _positions(idx, gs_rows)

    xs = _dispatch(ends, pos, h2, n_rows)
    b_gu = b_gate_up[0].reshape(N_EXPERTS, 1, D_EXPERT, 2)
    ys = _expert_ffn(tile_expert, n_valid, xs, w_gate_up[0], b_gu[..., 0], b_gu[..., 1], w_down[0],
                     b_down[0].reshape(N_EXPERTS, 1, D_MODEL), _deinterleave_matrix())

    y_c = _combine(pos, gate_w, x1, mod, final_g.reshape(1, -1), ys, tile0=0, ntiles=nc, nc=nc, tpb=tpb)
    y_l = _combine(pos, gate_w, x1, mod, final_g.reshape(1, -1), ys, tile0=nc, ntiles=nl, nc=nc, tpb=tpb)

    new_k = k_new[:n_ctx].reshape(bc, 1, tc, N_KV_HEADS, HEAD_DIM)
    new_v = v_new[:n_ctx].reshape(bc, 1, tc, N_KV_HEADS, HEAD_DIM)
    return (y_c.reshape(bc, tc, D_MODEL), y_l.reshape(bl, tl, D_MODEL), new_k, new_v)
```

```python
import functools
import math

import numpy as np
import jax
import jax.numpy as jnp
from jax import lax
from jax.experimental import pallas as pl
from jax.experimental.pallas import tpu as pltpu

F32 = jnp.float32
BF16 = jnp.bfloat16
I32 = jnp.int32

D_MODEL = 1024
HEAD_DIM = 64
N_HEADS = 8
N_KV_HEADS = 2
HEADS_PER_KV = N_HEADS // N_KV_HEADS
ATTN_W = N_HEADS * HEAD_DIM
KV_W = N_KV_HEADS * HEAD_DIM
FOURIER_W = 512
FOURIER_GROUP_W = 128
D_IN = ATTN_W + 2 * KV_W + FOURIER_W
N_EXPERTS = 32
TOP_K = 4
D_EXPERT = 1024
SWIGLU_ALPHA = 1.702
SWIGLU_LIMIT = 7.0
ROPE_THETA = 10000.0
ROT_PAIRS = HEAD_DIM // 4
GRID_W = 64
EPS = 1e-6

LANES = 128
TM = 512
TQ = 512
TG = 256
FFT_N1 = 4
VMEM_LIMIT = 56 * 1024 * 1024


def _cparams(sem, vmem=VMEM_LIMIT):
    return pltpu.CompilerParams(dimension_semantics=sem, vmem_limit_bytes=vmem)


def _rope_tables(t_lat):
    pos = np.arange(t_lat)
    row = (pos // GRID_W).astype(np.float64)
    col = (pos % GRID_W).astype(np.float64)
    inv = ROPE_THETA ** (-np.arange(ROT_PAIRS, dtype=np.float64) / ROT_PAIRS)
    lane = np.arange(LANES)
    d = lane % HEAD_DIM
    axis = d // (2 * ROT_PAIRS)
    second = (d // ROT_PAIRS) % 2
    p = d % ROT_PAIRS
    ang = np.where(axis[None, :] == 0, row[:, None], col[:, None]) * inv[p][None, :]
    cos = np.cos(ang)
    sin = np.sin(ang) * np.where(second == 0, -1.0, 1.0)[None, :]
    cos = np.concatenate([np.ones((TM, LANES)), cos], axis=0)
    sin = np.concatenate([np.zeros((TM, LANES)), sin], axis=0)
    return jnp.asarray(cos, F32), jnp.asarray(sin, F32)


def _dft_tables(n1, n2):
    c = np.arange(FOURIER_GROUP_W)
    ang = 2.0 * np.pi * np.outer(c, c) / FOURIER_GROUP_W
    z = np.zeros_like(ang)
    cbd = np.block([[np.cos(ang), z], [z, np.cos(ang)]])
    sbd = np.block([[np.sin(ang), z], [z, np.sin(ang)]])
    cd = np.concatenate([cbd, -sbd], axis=1)
    u = np.arange(n2)
    ang2 = 2.0 * np.pi * np.outer(u, u) / n2
    ct, st = np.cos(ang2), np.sin(ang2)
    nt = max(n1 - 1, 1)
    tw_ang = 2.0 * np.pi * np.outer(np.arange(1, nt + 1), u) / (n1 * n2)
    twc = np.repeat(np.cos(tw_ang)[:, :, None], LANES, axis=2)
    tws = np.repeat(np.sin(tw_ang)[:, :, None], LANES, axis=2)
    return (jnp.asarray(cd, F32), jnp.asarray(ct, F32), jnp.asarray(st, F32),
            jnp.asarray(twc, F32), jnp.asarray(tws, F32))


def _head_mean_matrix():
    h = np.arange(ATTN_W) // HEAD_DIM
    return jnp.asarray((h[:, None] == h[None, :]) / HEAD_DIM, BF16)


def _deinterleave_matrix():
    p = np.zeros((2 * LANES, 2 * LANES))
    m = np.arange(LANES)
    p[2 * m, m] = 1.0
    p[2 * m + 1, LANES + m] = 1.0
    return jnp.asarray(p, BF16)


def _mod_row(i, nc, tpb):
    return jnp.where(i < nc, 0, 1 + (i - nc) // tpb)


def _rms(x):
    return x * lax.rsqrt(jnp.mean(x * x, axis=-1, keepdims=True) + EPS)


def _pack_cols(cols, lane):
    out = jnp.zeros(lane.shape, cols[0].dtype)
    for k, c in enumerate(cols):
        out = jnp.where(lane == k, c, out)
    return out


def _mod_kernel(c_ref, w_ref, b_ref, o_ref):
    c = c_ref[...]
    s = c * (1.0 / (1.0 + jnp.exp(-c)))
    o_ref[...] = jnp.dot(s.astype(BF16), w_ref[...].astype(BF16), preferred_element_type=F32) + b_ref[...]


def _modulation(c_rows, w_mod, b_mod):
    n = w_mod.shape[1] // D_MODEL
    return pl.pallas_call(
        _mod_kernel,
        grid=(n,),
        in_specs=[pl.BlockSpec((8, D_MODEL), lambda j: (0, 0)),
                  pl.BlockSpec((D_MODEL, D_MODEL), lambda j: (0, j)),
                  pl.BlockSpec((1, D_MODEL), lambda j: (0, j))],
        out_specs=pl.BlockSpec((8, D_MODEL), lambda j: (0, j)),
        out_shape=jax.ShapeDtypeStruct((8, w_mod.shape[1]), F32),
        compiler_params=_cparams(("parallel",)),
        name="modulation",
    )(c_rows, w_mod, b_mod.reshape(1, -1))


def _pre_kernel(xc_ref, xl_ref, sh_ref, sc_ref, g1_ref, win_ref, qg_ref, kg_ref, bd_ref, cos_ref, sin_ref,
                q_ref, k_ref, v_ref, f_ref, kn_ref, vn_ref, *, nc, tpb):
    i = pl.program_id(0)
    r = _mod_row(i, nc, tpb)
    x = jnp.where(i < nc, xc_ref[...], xl_ref[...])
    h = _rms(x) * g1_ref[...]
    h = h * (1.0 + sc_ref[pl.ds(r, 1), :]) + sh_ref[pl.ds(r, 1), :]
    proj = jnp.dot(h.astype(BF16), win_ref[...], preferred_element_type=F32)
    q = proj[:, :ATTN_W]
    k = proj[:, ATTN_W:ATTN_W + KV_W]
    v = proj[:, ATTN_W + KV_W:ATTN_W + 2 * KV_W]
    f_ref[...] = proj[:, ATTN_W + 2 * KV_W:].astype(BF16)
    vn_ref[...] = v
    v_ref[...] = v.astype(BF16)
    q_ms = jnp.dot((q * q).astype(BF16), bd_ref[...], preferred_element_type=F32)
    k_ms = jnp.dot((k * k).astype(BF16), bd_ref[:KV_W, :KV_W], preferred_element_type=F32)
    qn = q * lax.rsqrt(q_ms + EPS) * qg_ref[...]
    kn = k * lax.rsqrt(k_ms + EPS) * kg_ref[...]
    kn_ref[...] = kn
    cos = cos_ref[...]
    sin = sin_ref[...]
    lane = lax.broadcasted_iota(I32, cos.shape, 1)
    first = (lane & ROT_PAIRS) == 0

    def rope(c):
        partner = jnp.where(first, pltpu.roll(c, LANES - ROT_PAIRS, 1), pltpu.roll(c, ROT_PAIRS, 1))
        return c * cos + partner * sin

    k_ref[...] = rope(kn).astype(BF16)
    scale = HEAD_DIM ** -0.5 * math.log2(math.e)
    q_rot = jnp.concatenate([rope(qn[:, j * LANES:(j + 1) * LANES]) for j in range(ATTN_W // LANES)], axis=1)
    q_ref[...] = (q_rot * scale).T.astype(BF16)


def _pre(xc, xl, mod, norm1_g, w_in, q_g, k_g, bd, cos_t, sin_t, *, nc, nl, tpb):
    n = (nc + nl) * TM
    row = lambda i: (i, 0)
    const = lambda i: (0, 0)
    tab = lambda i: (jnp.where(i < nc, 0, 1 + (i - nc) % tpb), 0)
    return pl.pallas_call(
        functools.partial(_pre_kernel, nc=nc, tpb=tpb),
        grid=(nc + nl,),
        in_specs=[pl.BlockSpec((TM, D_MODEL), lambda i: (jnp.minimum(i, nc - 1), 0)),
                  pl.BlockSpec((TM, D_MODEL), lambda i: (jnp.maximum(i - nc, 0), 0)),
                  pl.BlockSpec((8, D_MODEL), lambda i: (0, 0)),
                  pl.BlockSpec((8, D_MODEL), lambda i: (0, 1)),
                  pl.BlockSpec((1, D_MODEL), const),
                  pl.BlockSpec((D_MODEL, D_IN), const),
                  pl.BlockSpec((1, ATTN_W), const),
                  pl.BlockSpec((1, KV_W), const),
                  pl.BlockSpec((ATTN_W, ATTN_W), const),
                  pl.BlockSpec((TM, LANES), tab),
                  pl.BlockSpec((TM, LANES), tab)],
        out_specs=[pl.BlockSpec((ATTN_W, TM), lambda i: (0, i)), pl.BlockSpec((TM, KV_W), row),
                   pl.BlockSpec((TM, KV_W), row),
                   pl.BlockSpec((TM, FOURIER_W), row), pl.BlockSpec((TM, KV_W), row), pl.BlockSpec((TM, KV_W), row)],
        out_shape=[jax.ShapeDtypeStruct((ATTN_W, n), BF16), jax.ShapeDtypeStruct((n, KV_W), BF16),
                   jax.ShapeDtypeStruct((n, KV_W), BF16), jax.ShapeDtypeStruct((n, FOURIER_W), BF16),
                   jax.ShapeDtypeStruct((n, KV_W), F32), jax.ShapeDtypeStruct((n, KV_W), F32)],
        compiler_params=_cparams(("parallel",)),
        name="pre_proj",
    )(xc, xl, mod, mod, norm1_g, w_in, q_g, k_g, bd, cos_t, sin_t)


V_ROWS = 80


def _attn_kernel(qt_ref, k_ref, vt_ref, o_ref, ot_ref, st_ref, pt_ref):
    def scores(h):
        qt = qt_ref[h * HEAD_DIM:(h + 1) * HEAD_DIM, :]
        st_ref[h % 2] = jnp.dot(k_ref[h // HEADS_PER_KV], qt, preferred_element_type=F32)

    scores(0)
    for h in range(N_HEADS):
        if h + 1 < N_HEADS:
            scores(h + 1)
        st = st_ref[h % 2]
        pt_ref[h % 2] = jnp.exp2(st - jnp.max(st, axis=0, keepdims=True)).astype(BF16)
        ot = jnp.dot(vt_ref[h // HEADS_PER_KV], pt_ref[h % 2], preferred_element_type=F32)
        ot_ref[h * HEAD_DIM:(h + 1) * HEAD_DIM, :] = ot[:HEAD_DIM] * (1.0 / ot[HEAD_DIM:HEAD_DIM + 1])
    o_ref[...] = ot_ref[...].T


def _attention(qt, k_heads, vt_ext, *, batch, t, s, q_row0):
    tq = min(TQ, t)
    qpb = t // tq
    q0 = q_row0 // tq
    return pl.pallas_call(
        _attn_kernel,
        grid=(batch, qpb),
        in_specs=[pl.BlockSpec((ATTN_W, tq), lambda b, i: (0, q0 + b * qpb + i)),
                  pl.BlockSpec((None, N_KV_HEADS, s, HEAD_DIM), lambda b, i: (b, 0, 0, 0)),
                  pl.BlockSpec((None, N_KV_HEADS, V_ROWS, s), lambda b, i: (b, 0, 0, 0))],
        out_specs=pl.BlockSpec((tq, ATTN_W), lambda b, i: (b * qpb + i, 0)),
        out_shape=jax.ShapeDtypeStruct((batch * t, ATTN_W), F32),
        scratch_shapes=[pltpu.VMEM((ATTN_W, tq), F32), pltpu.VMEM((2, s, tq), F32), pltpu.VMEM((2, s, tq), BF16)],
        compiler_params=_cparams(("parallel", "parallel")),
        name="attention",
    )(qt, k_heads, vt_ext)


def _split_kv_heads(keys, vals):
    b, s, _ = keys.shape
    k_heads = keys.reshape(b, s, N_KV_HEADS, HEAD_DIM).transpose(0, 2, 1, 3)
    vt = vals.reshape(b, s, N_KV_HEADS, HEAD_DIM).transpose(0, 2, 3, 1)
    extra = jnp.concatenate([jnp.ones((b, N_KV_HEADS, 1, s), BF16),
                             jnp.zeros((b, N_KV_HEADS, V_ROWS - HEAD_DIM - 1, s), BF16)], axis=2)
    return k_heads, jnp.concatenate([vt, extra], axis=2)


def _fourier_kernel(x_ref, cd_ref, ct_ref, st_ref, twc_ref, tws_ref, wf_ref, o_ref, *, n1, n2):
    pw = 2 * FOURIER_GROUP_W
    scale = 1.0 / math.sqrt(n1 * n2 * FOURIER_GROUP_W)
    for gp in range(FOURIER_W // pw):
        z = jnp.dot(x_ref[:, gp * pw:(gp + 1) * pw], cd_ref[...], preferred_element_type=F32)
        zr = [z[t1 * n2:(t1 + 1) * n2, :pw] for t1 in range(n1)]
        zi = [z[t1 * n2:(t1 + 1) * n2, pw:] for t1 in range(n1)]
        if n1 == 1:
            a = [(zr[0], zi[0])]
        else:
            a = [(zr[0] + zr[1] + zr[2] + zr[3], zi[0] + zi[1] + zi[2] + zi[3]),
                 (zr[0] + zi[1] - zr[2] - zi[3], zi[0] - zr[1] - zi[2] + zr[3]),
                 (zr[0] - zr[1] + zr[2] - zr[3], zi[0] - zi[1] + zi[2] - zi[3]),
                 (zr[0] - zi[1] - zr[2] + zi[3], zi[0] + zr[1] - zi[2] - zr[3])]
        for u1 in range(n1):
            ar, ai = a[u1]
            if u1 > 0:
                c = jnp.concatenate([twc_ref[u1 - 1]] * 2, axis=1)
                s = jnp.concatenate([tws_ref[u1 - 1]] * 2, axis=1)
                ar, ai = ar * c + ai * s, ai * c - ar * s
            y = (jnp.dot(ct_ref[...], ar.astype(BF16), preferred_element_type=F32)
                 + jnp.dot(st_ref[...], ai.astype(BF16), preferred_element_type=F32)) * scale
            y = jnp.dot(y.astype(BF16), wf_ref[gp], preferred_element_type=F32)
            rows = pl.ds(u1, n2, stride=n1) if n1 > 1 else slice(None)
            for half in range(2):
                o_ref[2 * gp + half, rows, :] = y[:, half * FOURIER_GROUP_W:(half + 1) * FOURIER_GROUP_W]


def _fourier(f, wf_pairs, *, batch, t, row0, n1):
    n2 = t // n1
    cd, ct, st, twc, tws = _dft_tables(n1, n2)
    cd, ct, st = cd.astype(BF16), ct.astype(BF16), st.astype(BF16)
    nt = twc.shape[0]
    b0 = row0 // t
    c2 = lambda b: (0, 0)
    c3 = lambda b: (0, 0, 0)
    return pl.pallas_call(
        functools.partial(_fourier_kernel, n1=n1, n2=n2),
        grid=(batch,),
        in_specs=[pl.BlockSpec((t, FOURIER_W), lambda b: (b0 + b, 0)),
                  pl.BlockSpec(cd.shape, c2), pl.BlockSpec((n2, n2), c2), pl.BlockSpec((n2, n2), c2),
                  pl.BlockSpec((nt, n2, LANES), c3), pl.BlockSpec((nt, n2, LANES), c3),
                  pl.BlockSpec(wf_pairs.shape, c3)],
        out_specs=pl.BlockSpec((FOURIER_W // FOURIER_GROUP_W, t, FOURIER_GROUP_W), lambda b: (0, b, 0)),
        out_shape=jax.ShapeDtypeStruct((FOURIER_W // FOURIER_GROUP_W, batch * t, FOURIER_GROUP_W), F32),
        compiler_params=_cparams(("parallel",)),
        name="fourier",
    )(f, cd, ct, st, twc, tws, wf_pairs)


def _mix_kernel(xc_ref, xl_ref, ac_ref, al_ref, fc_ref, fl_ref, g1_ref, sh2_ref, sc2_ref, ag_ref, fg_ref,
                wout_ref, n2g_ref, x1_ref, h2_ref, *, nc, tpb):
    i = pl.program_id(0)
    r = _mod_row(i, nc, tpb)
    is_ctx = i < nc
    x = jnp.where(is_ctx, xc_ref[...], xl_ref[...])
    a = _rms(jnp.where(is_ctx, ac_ref[...], al_ref[...])) * ag_ref[...]
    n_groups = FOURIER_W // FOURIER_GROUP_W
    f = jnp.concatenate([jnp.where(is_ctx, fc_ref[g], fl_ref[g]) for g in range(n_groups)], axis=1)
    f = _rms(f) * fg_ref[...]
    mixed = (jnp.dot(a.astype(BF16), wout_ref[:ATTN_W, :], preferred_element_type=F32)
             + jnp.dot(f.astype(BF16), wout_ref[ATTN_W:, :], preferred_element_type=F32))
    x1 = x + g1_ref[pl.ds(r, 1), :] * mixed
    x1_ref[...] = x1
    h2 = _rms(x1) * n2g_ref[...]
    h2_ref[...] = h2 * (1.0 + sc2_ref[pl.ds(r, 1), :]) + sh2_ref[pl.ds(r, 1), :]


def _mix(xc, xl, ac, al, fc, fl, mod, attn_g, four_g, w_out, norm2_g, *, nc, nl, tpb):
    n = (nc + nl) * TM
    lo = lambda i: (jnp.minimum(i, nc - 1), 0)
    hi = lambda i: (jnp.maximum(i - nc, 0), 0)
    const = lambda i: (0, 0)
    row = lambda i: (i, 0)
    return pl.pallas_call(
        functools.partial(_mix_kernel, nc=nc, tpb=tpb),
        grid=(nc + nl,),
        in_specs=[pl.BlockSpec((TM, D_MODEL), lo), pl.BlockSpec((TM, D_MODEL), hi),
                  pl.BlockSpec((TM, ATTN_W), lo), pl.BlockSpec((TM, ATTN_W), hi),
                  pl.BlockSpec((FOURIER_W // FOURIER_GROUP_W, TM, FOURIER_GROUP_W),
                               lambda i: (0, jnp.minimum(i, nc - 1), 0)),
                  pl.BlockSpec((FOURIER_W // FOURIER_GROUP_W, TM, FOURIER_GROUP_W),
                               lambda i: (0, jnp.maximum(i - nc, 0), 0)),
                  pl.BlockSpec((8, D_MODEL), lambda i: (0, 2)),
                  pl.BlockSpec((8, D_MODEL), lambda i: (0, 3)),
                  pl.BlockSpec((8, D_MODEL), lambda i: (0, 4)),
                  pl.BlockSpec((1, ATTN_W), const), pl.BlockSpec((1, FOURIER_W), const),
                  pl.BlockSpec((D_MODEL, D_MODEL), const), pl.BlockSpec((1, D_MODEL), const)],
        out_specs=[pl.BlockSpec((TM, D_MODEL), row), pl.BlockSpec((TM, D_MODEL), row)],
        out_shape=[jax.ShapeDtypeStruct((n, D_MODEL), F32), jax.ShapeDtypeStruct((n, D_MODEL), F32)],
        compiler_params=_cparams(("parallel",)),
        name="mix_proj",
    )(xc, xl, ac, al, fc, fl, mod, mod, mod, attn_g, four_g, w_out, norm2_g)


def _router_kernel(h2_ref, whi_ref, wlo_ref, br_ref, idx_ref, w_ref, cnt_ref):
    i = pl.program_id(0)
    h = h2_ref[...]
    h_hi = h.astype(BF16)
    h_lo = (h - h_hi.astype(F32)).astype(BF16)
    logits = (jnp.dot(h_hi, whi_ref[...], preferred_element_type=F32)
              + jnp.dot(h_lo, whi_ref[...], preferred_element_type=F32)
              + jnp.dot(h_hi, wlo_ref[...], preferred_element_type=F32)) + br_ref[...]
    lane = lax.broadcasted_iota(I32, logits.shape, 1)
    l = logits
    vals, idxs = [], []
    hits = jnp.zeros(logits.shape, F32)
    for _ in range(TOP_K):
        m = jnp.max(l, axis=-1, keepdims=True)
        ix = jnp.min(jnp.where(l == m, lane, LANES), axis=-1, keepdims=True)
        sel = lane == ix
        hits = hits + sel.astype(F32)
        l = jnp.where(sel, -jnp.inf, l)
        vals.append(m)
        idxs.append(ix)
    es = [jnp.exp(v - vals[0]) for v in vals]
    den = es[0] + es[1] + es[2] + es[3]
    idx_ref[...] = _pack_cols(idxs, lane)
    w_ref[...] = _pack_cols([e / den for e in es], lane)

    @pl.when(i == 0)
    def _():
        cnt_ref[...] = jnp.zeros(cnt_ref.shape, F32)

    cnt_ref[...] += jnp.sum(hits, axis=0, keepdims=True)


def _router(h2, w_hi, w_lo, br):
    n = h2.shape[0]
    row = lambda i: (i, 0)
    const = lambda i: (0, 0)
    return pl.pallas_call(
        _router_kernel,
        grid=(n // TM,),
        in_specs=[pl.BlockSpec((TM, D_MODEL), row), pl.BlockSpec((D_MODEL, LANES), const),
                  pl.BlockSpec((D_MODEL, LANES), const), pl.BlockSpec((1, LANES), const)],
        out_specs=[pl.BlockSpec((TM, LANES), row), pl.BlockSpec((TM, LANES), row), pl.BlockSpec((8, LANES), const)],
        out_shape=[jax.ShapeDtypeStruct((n, LANES), I32), jax.ShapeDtypeStruct((n, LANES), F32),
                   jax.ShapeDtypeStruct((8, LANES), F32)],
        compiler_params=_cparams(("arbitrary",)),
        name="router",
    )(h2, w_hi, w_lo, br)


def _pos_kernel(idx_ref, gs_ref, pos_ref, carry_ref):
    i = pl.program_id(0)

    @pl.when(i == 0)
    def _():
        carry_ref[...] = jnp.zeros(carry_ref.shape, F32)

    idx = idx_ref[...]
    lane = lax.broadcasted_iota(I32, idx.shape, 1)
    sels = [lane == idx[:, k:k + 1] for k in range(TOP_K)]
    hits = jnp.zeros(idx.shape, F32)
    for s in sels:
        hits = hits + s.astype(F32)
    r = lax.broadcasted_iota(I32, (TM, TM), 0)
    c = lax.broadcasted_iota(I32, (TM, TM), 1)
    before = (c < r).astype(BF16)
    rank = jnp.dot(before, hits.astype(BF16), preferred_element_type=F32) + carry_ref[0:1, :] + gs_ref[0:1, :]
    cols = [jnp.sum(jnp.where(s, rank, 0.0), axis=-1, keepdims=True) for s in sels]
    pos_ref[...] = _pack_cols(cols, lane).T[:8, :].astype(I32)
    carry_ref[...] += jnp.sum(hits, axis=0, keepdims=True)


def _positions(idx, gs_rows):
    n = idx.shape[0]
    return pl.pallas_call(
        _pos_kernel,
        grid=(n // TM,),
        in_specs=[pl.BlockSpec((TM, LANES), lambda i: (i, 0)), pl.BlockSpec((8, LANES), lambda i: (0, 0))],
        out_specs=pl.BlockSpec((8, TM), lambda i: (0, i)),
        out_shape=jax.ShapeDtypeStruct((8, n), I32),
        scratch_shapes=[pltpu.VMEM((8, LANES), F32)],
        compiler_params=_cparams(("arbitrary",)),
        name="positions",
    )(idx, gs_rows)


def _row_copy_wait(src, dst, sem):
    pltpu.make_async_copy(src, dst, sem).wait()


def _dispatch_kernel(ends_ref, pos_ref, h2_ref, xs_ref, zero_ref, sem):
    i = pl.program_id(0)

    @pl.when(i == 0)
    def _():
        zero_ref[...] = jnp.zeros(zero_ref.shape, F32)

        def last_tile(e):
            lo = ends_ref[e - 1] if e > 0 else 0
            start = pl.multiple_of(jnp.maximum(ends_ref[e] - TG, 0), TG)
            return ends_ref[e] > lo, pltpu.make_async_copy(zero_ref, xs_ref.at[pl.ds(start, TG), :], sem)

        for e in range(N_EXPERTS):
            nonempty, cp = last_tile(e)
            pl.when(nonempty)(cp.start)
        for e in range(N_EXPERTS):
            nonempty, cp = last_tile(e)
            pl.when(nonempty)(cp.wait)

        def tail_tile(j):
            return pltpu.make_async_copy(zero_ref, xs_ref.at[pl.ds(pl.multiple_of(j * TG, TG), TG), :], sem)

        first_unused = ends_ref[N_EXPERTS - 1] // TG
        n_tiles = xs_ref.shape[0] // TG
        lax.fori_loop(first_unused, n_tiles, lambda j, c: (tail_tile(j).start(), c)[1], 0)
        lax.fori_loop(first_unused, n_tiles, lambda j, c: (tail_tile(j).wait(), c)[1], 0)

    def issue(t, carry):
        for k in range(TOP_K):
            p = pos_ref[k, t]
            pltpu.make_async_copy(h2_ref.at[pl.ds(t, 1), :], xs_ref.at[pl.ds(p, 1), :], sem).start()
        return carry

    lax.fori_loop(0, TM, issue, 0)
    for k in range(TOP_K):
        _row_copy_wait(h2_ref, xs_ref.at[pl.ds(0, TM), :], sem)


def _dispatch(ends, pos, h2, n_rows):
    n = h2.shape[0]
    grid_spec = pltpu.PrefetchScalarGridSpec(
        num_scalar_prefetch=1,
        grid=(n // TM,),
        in_specs=[pl.BlockSpec((8, TM), lambda i, ends: (0, i), memory_space=pltpu.SMEM),
                  pl.BlockSpec((TM, D_MODEL), lambda i, ends: (i, 0))],
        out_specs=pl.BlockSpec(memory_space=pl.ANY),
        scratch_shapes=[pltpu.VMEM((TG, D_MODEL), F32), pltpu.SemaphoreType.DMA(())],
    )
    return pl.pallas_call(
        _dispatch_kernel,
        grid_spec=grid_spec,
        out_shape=jax.ShapeDtypeStruct((n_rows, D_MODEL), F32),
        compiler_params=_cparams(("arbitrary",)),
        name="dispatch",
    )(ends, pos, h2)


def _ffn_kernel(te_ref, nv_ref, x_ref, wgu_ref, bg_ref, bl_ref, wdn_ref, bdn_ref, perm_ref, o_ref,
                wg_s, wl_s, wd_s):
    i = pl.program_id(0)
    e = te_ref[i]
    prev = te_ref[jnp.maximum(i - 1, 0)]
    valid = i < nv_ref[0]
    fresh = jnp.logical_or(i == 0, e != prev)

    @pl.when(jnp.logical_and(valid, fresh))
    def _():
        for j in range(2 * D_EXPERT // (2 * LANES)):
            blk = wgu_ref[:, j * 2 * LANES:(j + 1) * 2 * LANES].astype(BF16)
            d = jnp.dot(blk, perm_ref[...], preferred_element_type=F32)
            wg_s[:, j * LANES:(j + 1) * LANES] = d[:, :LANES].astype(BF16)
            wl_s[:, j * LANES:(j + 1) * LANES] = d[:, LANES:].astype(BF16)
        wd_s[...] = wdn_ref[...].astype(BF16)

    @pl.when(valid)
    def _():
        x = x_ref[...].astype(BF16)
        g = jnp.dot(x, wg_s[...], preferred_element_type=F32) + bg_ref[...]
        l = jnp.dot(x, wl_s[...], preferred_element_type=F32) + bl_ref[...]
        g = jnp.minimum(g, SWIGLU_LIMIT)
        l = jnp.clip(l, -SWIGLU_LIMIT, SWIGLU_LIMIT)
        a = (l + 1.0) * (g * (1.0 / (1.0 + jnp.exp(-SWIGLU_ALPHA * g))))
        o_ref[...] = jnp.dot(a.astype(BF16), wd_s[...], preferred_element_type=F32) + bdn_ref[...]

    @pl.when(jnp.logical_not(valid))
    def _():
        o_ref[...] = jnp.zeros(o_ref.shape, F32)


def _expert_ffn(tile_expert, n_valid, xs, w_gu, b_g, b_l, w_dn, b_dn, perm):
    p = xs.shape[0]
    nt = p // TG
    tile = lambda i, te, nv: (jnp.minimum(i, nv[0] - 1), 0)
    ex3 = lambda i, te, nv: (te[i], 0, 0)
    grid_spec = pltpu.PrefetchScalarGridSpec(
        num_scalar_prefetch=2,
        grid=(nt,),
        in_specs=[pl.BlockSpec((TG, D_MODEL), tile),
                  pl.BlockSpec((None, D_MODEL, 2 * D_EXPERT), ex3),
                  pl.BlockSpec((None, 1, D_EXPERT), ex3),
                  pl.BlockSpec((None, 1, D_EXPERT), ex3),
                  pl.BlockSpec((None, D_EXPERT, D_MODEL), ex3),
                  pl.BlockSpec((None, 1, D_MODEL), ex3),
                  pl.BlockSpec((2 * LANES, 2 * LANES), lambda i, te, nv: (0, 0))],
        out_specs=pl.BlockSpec((TG, D_MODEL), lambda i, te, nv: (i, 0)),
        scratch_shapes=[pltpu.VMEM((D_MODEL, D_EXPERT), BF16), pltpu.VMEM((D_MODEL, D_EXPERT), BF16),
                        pltpu.VMEM((D_EXPERT, D_MODEL), BF16)],
    )
    return pl.pallas_call(
        _ffn_kernel,
        grid_spec=grid_spec,
        out_shape=jax.ShapeDtypeStruct((p, D_MODEL), F32),
        compiler_params=_cparams(("arbitrary",)),
        name="expert_ffn",
    )(tile_expert, n_valid, xs, w_gu, b_g, b_l, w_dn, b_dn, perm)


def _combine_kernel(pos_ref, w_ref, x1_ref, g2_ref, fg_ref, ys_ref, o_ref, buf, sem, *, tile0, nc, tpb):
    i = pl.program_id(0) + tile0
    r = _mod_row(i, nc, tpb)

    def issue(t, carry):
        for k in range(TOP_K):
            p = pos_ref[k, t]
            pltpu.make_async_copy(ys_ref.at[pl.ds(p, 1), :], buf.at[k, pl.ds(t, 1), :], sem).start()
        return carry

    lax.fori_loop(0, TM, issue, 0)
    for k in range(TOP_K):
        _row_copy_wait(ys_ref.at[pl.ds(0, TM), :], buf.at[k], sem)
    w = w_ref[...]
    moe = w[:, 0:1] * buf[0]
    for k in range(1, TOP_K):
        moe = moe + w[:, k:k + 1] * buf[k]
    y = x1_ref[...] + g2_ref[pl.ds(r, 1), :] * moe
    o_ref[...] = _rms(y) * fg_ref[...]


def _combine(pos, w, x1, mod, final_g, ys, *, tile0, ntiles, nc, tpb):
    return pl.pallas_call(
        functools.partial(_combine_kernel, tile0=tile0, nc=nc, tpb=tpb),
        grid=(ntiles,),
        in_specs=[pl.BlockSpec((8, TM), lambda i: (0, i + tile0), memory_space=pltpu.SMEM),
                  pl.BlockSpec((TM, LANES), lambda i: (i + tile0, 0)),
                  pl.BlockSpec((TM, D_MODEL), lambda i: (i + tile0, 0)),
                  pl.BlockSpec((8, D_MODEL), lambda i: (0, 5)),
                  pl.BlockSpec((1, D_MODEL), lambda i: (0, 0)),
                  pl.BlockSpec(memory_space=pl.ANY)],
        out_specs=pl.BlockSpec((TM, D_MODEL), lambda i: (i, 0)),
        out_shape=jax.ShapeDtypeStruct((ntiles * TM, D_MODEL), F32),
        scratch_shapes=[pltpu.VMEM((TOP_K, TM, D_MODEL), F32), pltpu.SemaphoreType.DMA(())],
        compiler_params=_cparams(("arbitrary",)),
        name="combine",
    )(pos, w, x1, mod, final_g, ys)


def kernel(x_prompt, x_sample, cache_k, cache_v, c, c_ctx, norm1_g, w_mod, b_mod, w_in, q_norm_g, k_norm_g,
           w_fourier, attn_out_g, fourier_out_g, w_out, norm2_g, w_router, b_router, w_gate_up, b_gate_up,
           w_down, b_down, final_g):
    bc, tc, _ = x_prompt.shape
    bl, tl, _ = x_sample.shape
    depth = w_in.shape[0]
    past = cache_k.shape[2]
    assert depth == 1 and tl % TQ == 0 and tl % TM == 0 and (bc * tc) % TM == 0
    n_ctx, n_lat = bc * tc, bl * tl
    n_tok = n_ctx + n_lat
    nc, nl, tpb = n_ctx // TM, n_lat // TM, tl // TM

    xc = x_prompt.reshape(n_ctx, D_MODEL)
    xl = x_sample.reshape(n_lat, D_MODEL)

    c_rows = jnp.concatenate([c_ctx[None, :], c, jnp.zeros((8 - 1 - bl, D_MODEL), F32)], axis=0)
    mod = _modulation(c_rows, w_mod[0], b_mod[0])

    cos_t, sin_t = _rope_tables(tl)
    q, k, v, f, k_new, v_new = _pre(
        xc, xl, mod, norm1_g, w_in[0].astype(BF16), jnp.tile(q_norm_g, (1, N_HEADS)),
        jnp.tile(k_norm_g, (1, N_KV_HEADS)), _head_mean_matrix(), cos_t, sin_t, nc=nc, nl=nl, tpb=tpb)

    kc, vtc = _split_kv_heads(k[:n_ctx].reshape(bc, tc, KV_W), v[:n_ctx].reshape(bc, tc, KV_W))
    attn_c = _attention(q, kc, vtc, batch=bc, t=tc, s=tc, q_row0=0)
    keys = jnp.concatenate([cache_k[:, 0].reshape(bl, past, KV_W).astype(BF16),
                            k[n_ctx:].reshape(bl, tl, KV_W)], axis=1)
    vals = jnp.concatenate([cache_v[:, 0].reshape(bl, past, KV_W).astype(BF16),
                            v[n_ctx:].reshape(bl, tl, KV_W)], axis=1)
    kl, vtl = _split_kv_heads(keys, vals)
    attn_l = _attention(q, kl, vtl, batch=bl, t=tl, s=past + tl, q_row0=n_ctx)

    wf = w_fourier[0].astype(BF16)
    zero = jnp.zeros_like(wf[0])
    wf_pairs = jnp.stack([jnp.block([[wf[0], zero], [zero, wf[1]]]), jnp.block([[wf[2], zero], [zero, wf[3]]])])
    four_c = _fourier(f, wf_pairs, batch=bc, t=tc, row0=0, n1=1)
    four_l = _fourier(f, wf_pairs, batch=bl, t=tl, row0=n_ctx, n1=FFT_N1)

    x1, h2 = _mix(xc, xl, attn_c, attn_l, four_c, four_l, mod, attn_out_g, fourier_out_g,
                  w_out[0].astype(BF16), norm2_g, nc=nc, nl=nl, tpb=tpb)

    wr = jnp.pad(w_router[0], ((0, 0), (0, LANES - N_EXPERTS)))
    wr_hi = wr.astype(BF16)
    wr_lo = (wr - wr_hi.astype(F32)).astype(BF16)
    br = jnp.pad(b_router[0], (0, LANES - N_EXPERTS), constant_values=-1e30).reshape(1, LANES)
    idx, gate_w, counts = _router(h2, wr_hi, wr_lo, br)
    cnt = counts[0, :N_EXPERTS].astype(I32)
    padded = ((cnt + TG - 1) // TG) * TG
    ends = jnp.cumsum(padded)
    starts = ends - padded
    n_rows = n_tok * TOP_K + N_EXPERTS * TG
    n_tiles = n_rows // TG
    tile_ids = jnp.arange(n_tiles, dtype=I32)
    tile_expert = jnp.minimum(jnp.sum((ends[None, :] // TG <= tile_ids[:, None]).astype(I32), axis=1),
                              N_EXPERTS - 1)
    n_valid = (ends[N_EXPERTS - 1] // TG).reshape(1)
    gs_rows = jnp.broadcast_to(jnp.pad(starts.astype(F32), (0, LANES - N_EXPERTS))[None, :], (8, LANES))
    pos = _positions(idx, gs_rows)

    xs = _dispatch(ends, pos, h2, n_rows)
    b_gu = b_gate_up[0].reshape(N_EXPERTS, 1, D_EXPERT, 2)
    ys = _expert_ffn(tile_expert, n_valid, xs, w_gate_up[0], b_gu[..., 0], b_gu[..., 1], w_down[0],
                     b_down[0].reshape(N_EXPERTS, 1, D_MODEL), _deinterleave_matrix())

    y_c = _combine(pos, gate_w, x1, mod, final_g.reshape(1, -1), ys, tile0=0, ntiles=nc, nc=nc, tpb=tpb)
    y_l = _combine(pos, gate_w, x1, mod, final_g.reshape(1, -1), ys, tile0=nc, ntiles=nl, nc=nc, tpb=tpb)

    new_k = k_new[:n_ctx].reshape(bc, 1, tc, N_KV_HEADS, HEAD_DIM)
    new_v = v_new[:n_ctx].reshape(bc, 1, tc, N_KV_HEADS, HEAD_DIM)
    return (y_c.reshape(bc, tc, D_MODEL), y_l.reshape(bl, tl, D_MODEL), new_k, new_v)
```

```python
import functools
import math

import numpy as np
import jax
import jax.numpy as jnp
from jax import lax
from jax.experimental import pallas as pl
from jax.experimental.pallas import tpu as pltpu

F32 = jnp.float32
BF16 = jnp.bfloat16
I32 = jnp.int32

D_MODEL = 1024
HEAD_DIM = 64
N_HEADS = 8
N_KV_HEADS = 2
HEADS_PER_KV = N_HEADS // N_KV_HEADS
ATTN_W = N_HEADS * HEAD_DIM
KV_W = N_KV_HEADS * HEAD_DIM
FOURIER_W = 512
FOURIER_GROUP_W = 128
D_IN = ATTN_W + 2 * KV_W + FOURIER_W
N_EXPERTS = 32
TOP_K = 4
D_EXPERT = 1024
SWIGLU_ALPHA = 1.702
SWIGLU_LIMIT = 7.0
ROPE_THETA = 10000.0
ROT_PAIRS = HEAD_DIM // 4
GRID_W = 64
EPS = 1e-6

LANES = 128
TM = 512
TQ = 512
TG = 256
FFT_N1 = 4
TILE_ROWS = D_MODEL // LANES
VMEM_LIMIT = 56 * 1024 * 1024


def _cparams(sem, vmem=VMEM_LIMIT):
    return pltpu.CompilerParams(dimension_semantics=sem, vmem_limit_bytes=vmem)


def _rope_tables(t_lat):
    pos = np.arange(t_lat)
    row = (pos // GRID_W).astype(np.float64)
    col = (pos % GRID_W).astype(np.float64)
    inv = ROPE_THETA ** (-np.arange(ROT_PAIRS, dtype=np.float64) / ROT_PAIRS)
    lane = np.arange(LANES)
    d = lane % HEAD_DIM
    axis = d // (2 * ROT_PAIRS)
    second = (d // ROT_PAIRS) % 2
    p = d % ROT_PAIRS
    ang = np.where(axis[None, :] == 0, row[:, None], col[:, None]) * inv[p][None, :]
    cos = np.cos(ang)
    sin = np.sin(ang) * np.where(second == 0, -1.0, 1.0)[None, :]
    cos = np.concatenate([np.ones((TM, LANES)), cos], axis=0)
    sin = np.concatenate([np.zeros((TM, LANES)), sin], axis=0)
    return jnp.asarray(cos, F32), jnp.asarray(sin, F32)


def _dft_tables(n1, n2):
    c = np.arange(FOURIER_GROUP_W)
    ang = 2.0 * np.pi * np.outer(c, c) / FOURIER_GROUP_W
    z = np.zeros_like(ang)
    cbd = np.block([[np.cos(ang), z], [z, np.cos(ang)]])
    sbd = np.block([[np.sin(ang), z], [z, np.sin(ang)]])
    cd = np.concatenate([cbd, -sbd], axis=1)
    u = np.arange(n2)
    ang2 = 2.0 * np.pi * np.outer(u, u) / n2
    ct, st = np.cos(ang2), np.sin(ang2)
    nt = max(n1 - 1, 1)
    tw_ang = 2.0 * np.pi * np.outer(np.arange(1, nt + 1), u) / (n1 * n2)
    twc = np.repeat(np.cos(tw_ang)[:, :, None], LANES, axis=2)
    tws = np.repeat(np.sin(tw_ang)[:, :, None], LANES, axis=2)
    return (jnp.asarray(cd, F32), jnp.asarray(ct, F32), jnp.asarray(st, F32),
            jnp.asarray(twc, F32), jnp.asarray(tws, F32))


def _head_mean_matrix():
    h = np.arange(ATTN_W) // HEAD_DIM
    return jnp.asarray((h[:, None] == h[None, :]) / HEAD_DIM, BF16)


def _deinterleave_matrix():
    p = np.zeros((2 * LANES, 2 * LANES))
    m = np.arange(LANES)
    p[2 * m, m] = 1.0
    p[2 * m + 1, LANES + m] = 1.0
    return jnp.asarray(p, BF16)


def _mod_row(i, nc, tpb):
    return jnp.where(i < nc, 0, 1 + (i - nc) // tpb)


def _rms(x):
    return x * lax.rsqrt(jnp.mean(x * x, axis=-1, keepdims=True) + EPS)


def _pack_cols(cols, lane):
    out = jnp.zeros(lane.shape, cols[0].dtype)
    for k, c in enumerate(cols):
        out = jnp.where(lane == k, c, out)
    return out


def _to_tiles(ref, x):
    r = x.shape[0]
    for s in range(TILE_ROWS):
        ref[pl.ds(s, r, stride=TILE_ROWS), :] = x[:, s * LANES:(s + 1) * LANES]


def _from_tiles(ref):
    r = ref.shape[0] // TILE_ROWS
    return jnp.concatenate([ref[pl.ds(s, r, stride=TILE_ROWS), :] for s in range(TILE_ROWS)], axis=1)


def _mod_kernel(c_ref, w_ref, b_ref, o_ref):
    c = c_ref[...]
    s = c * (1.0 / (1.0 + jnp.exp(-c)))
    o_ref[...] = jnp.dot(s.astype(BF16), w_ref[...].astype(BF16), preferred_element_type=F32) + b_ref[...]


def _modulation(c_rows, w_mod, b_mod):
    n = w_mod.shape[1] // D_MODEL
    return pl.pallas_call(
        _mod_kernel,
        grid=(n,),
        in_specs=[pl.BlockSpec((8, D_MODEL), lambda j: (0, 0)),
                  pl.BlockSpec((D_MODEL, D_MODEL), lambda j: (0, j)),
                  pl.BlockSpec((1, D_MODEL), lambda j: (0, j))],
        out_specs=pl.BlockSpec((8, D_MODEL), lambda j: (0, j)),
        out_shape=jax.ShapeDtypeStruct((8, w_mod.shape[1]), F32),
        compiler_params=_cparams(("parallel",)),
        name="modulation",
    )(c_rows, w_mod, b_mod.reshape(1, -1))


def _pre_kernel(xc_ref, xl_ref, sh_ref, sc_ref, g1_ref, win_ref, qg_ref, kg_ref, bd_ref, cos_ref, sin_ref,
                q_ref, k_ref, v_ref, f_ref, kn_ref, vn_ref, *, nc, tpb):
    i = pl.program_id(0)
    r = _mod_row(i, nc, tpb)
    x = jnp.where(i < nc, xc_ref[...], xl_ref[...])
    h = _rms(x) * g1_ref[...]
    h = h * (1.0 + sc_ref[pl.ds(r, 1), :]) + sh_ref[pl.ds(r, 1), :]
    proj = jnp.dot(h.astype(BF16), win_ref[...], preferred_element_type=F32)
    q = proj[:, :ATTN_W]
    k = proj[:, ATTN_W:ATTN_W + KV_W]
    v = proj[:, ATTN_W + KV_W:ATTN_W + 2 * KV_W]
    f_ref[...] = proj[:, ATTN_W + 2 * KV_W:].astype(BF16)
    vn_ref[...] = v
    v_ref[...] = v.astype(BF16)
    q_ms = jnp.dot((q * q).astype(BF16), bd_ref[...], preferred_element_type=F32)
    k_ms = jnp.dot((k * k).astype(BF16), bd_ref[:KV_W, :KV_W], preferred_element_type=F32)
    qn = q * lax.rsqrt(q_ms + EPS) * qg_ref[...]
    kn = k * lax.rsqrt(k_ms + EPS) * kg_ref[...]
    kn_ref[...] = kn
    cos = cos_ref[...]
    sin = sin_ref[...]
    lane = lax.broadcasted_iota(I32, cos.shape, 1)
    first = (lane & ROT_PAIRS) == 0

    def rope(c):
        partner = jnp.where(first, pltpu.roll(c, LANES - ROT_PAIRS, 1), pltpu.roll(c, ROT_PAIRS, 1))
        return c * cos + partner * sin

    k_ref[...] = rope(kn).astype(BF16)
    scale = HEAD_DIM ** -0.5 * math.log2(math.e)
    q_rot = jnp.concatenate([rope(qn[:, j * LANES:(j + 1) * LANES]) for j in range(ATTN_W // LANES)], axis=1)
    q_ref[...] = (q_rot * scale).T.astype(BF16)


def _pre(xc, xl, mod, norm1_g, w_in, q_g, k_g, bd, cos_t, sin_t, *, nc, nl, tpb):
    n = (nc + nl) * TM
    row = lambda i: (i, 0)
    const = lambda i: (0, 0)
    tab = lambda i: (jnp.where(i < nc, 0, 1 + (i - nc) % tpb), 0)
    return pl.pallas_call(
        functools.partial(_pre_kernel, nc=nc, tpb=tpb),
        grid=(nc + nl,),
        in_specs=[pl.BlockSpec((TM, D_MODEL), lambda i: (jnp.minimum(i, nc - 1), 0)),
                  pl.BlockSpec((TM, D_MODEL), lambda i: (jnp.maximum(i - nc, 0), 0)),
                  pl.BlockSpec((8, D_MODEL), lambda i: (0, 0)),
                  pl.BlockSpec((8, D_MODEL), lambda i: (0, 1)),
                  pl.BlockSpec((1, D_MODEL), const),
                  pl.BlockSpec((D_MODEL, D_IN), const),
                  pl.BlockSpec((1, ATTN_W), const),
                  pl.BlockSpec((1, KV_W), const),
                  pl.BlockSpec((ATTN_W, ATTN_W), const),
                  pl.BlockSpec((TM, LANES), tab),
                  pl.BlockSpec((TM, LANES), tab)],
        out_specs=[pl.BlockSpec((ATTN_W, TM), lambda i: (0, i)), pl.BlockSpec((TM, KV_W), row),
                   pl.BlockSpec((TM, KV_W), row),
                   pl.BlockSpec((TM, FOURIER_W), row), pl.BlockSpec((TM, KV_W), row), pl.BlockSpec((TM, KV_W), row)],
        out_shape=[jax.ShapeDtypeStruct((ATTN_W, n), BF16), jax.ShapeDtypeStruct((n, KV_W), BF16),
                   jax.ShapeDtypeStruct((n, KV_W), BF16), jax.ShapeDtypeStruct((n, FOURIER_W), BF16),
                   jax.ShapeDtypeStruct((n, KV_W), F32), jax.ShapeDtypeStruct((n, KV_W), F32)],
        compiler_params=_cparams(("parallel",)),
        name="pre_proj",
    )(xc, xl, mod, mod, norm1_g, w_in, q_g, k_g, bd, cos_t, sin_t)


V_ROWS = 80


def _attn_kernel(qt_ref, k_ref, vt_ref, o_ref, ot_ref, st_ref, pt_ref):
    def scores(h):
        qt = qt_ref[h * HEAD_DIM:(h + 1) * HEAD_DIM, :]
        st_ref[h % 2] = jnp.dot(k_ref[h // HEADS_PER_KV], qt, preferred_element_type=F32)

    scores(0)
    for h in range(N_HEADS):
        if h + 1 < N_HEADS:
            scores(h + 1)
        st = st_ref[h % 2]
        pt_ref[h % 2] = jnp.exp2(st - jnp.max(st, axis=0, keepdims=True)).astype(BF16)
        ot = jnp.dot(vt_ref[h // HEADS_PER_KV], pt_ref[h % 2], preferred_element_type=F32)
        ot_ref[h * HEAD_DIM:(h + 1) * HEAD_DIM, :] = ot[:HEAD_DIM] * (1.0 / ot[HEAD_DIM:HEAD_DIM + 1])
    o_ref[...] = ot_ref[...].T


def _attention(qt, k_heads, vt_ext, *, batch, t, s, q_row0):
    tq = min(TQ, t)
    qpb = t // tq
    q0 = q_row0 // tq
    return pl.pallas_call(
        _attn_kernel,
        grid=(batch, qpb),
        in_specs=[pl.BlockSpec((ATTN_W, tq), lambda b, i: (0, q0 + b * qpb + i)),
                  pl.BlockSpec((None, N_KV_HEADS, s, HEAD_DIM), lambda b, i: (b, 0, 0, 0)),
                  pl.BlockSpec((None, N_KV_HEADS, V_ROWS, s), lambda b, i: (b, 0, 0, 0))],
        out_specs=pl.BlockSpec((tq, ATTN_W), lambda b, i: (b * qpb + i, 0)),
        out_shape=jax.ShapeDtypeStruct((batch * t, ATTN_W), F32),
        scratch_shapes=[pltpu.VMEM((ATTN_W, tq), F32), pltpu.VMEM((2, s, tq), F32), pltpu.VMEM((2, s, tq), BF16)],
        compiler_params=_cparams(("parallel", "parallel")),
        name="attention",
    )(qt, k_heads, vt_ext)


def _split_kv_heads(keys, vals):
    b, s, _ = keys.shape
    k_heads = keys.reshape(b, s, N_KV_HEADS, HEAD_DIM).transpose(0, 2, 1, 3)
    vt = vals.reshape(b, s, N_KV_HEADS, HEAD_DIM).transpose(0, 2, 3, 1)
    extra = jnp.concatenate([jnp.ones((b, N_KV_HEADS, 1, s), BF16),
                             jnp.zeros((b, N_KV_HEADS, V_ROWS - HEAD_DIM - 1, s), BF16)], axis=2)
    return k_heads, jnp.concatenate([vt, extra], axis=2)


def _fourier_kernel(x_ref, cd_ref, ct_ref, st_ref, twc_ref, tws_ref, wf_ref, o_ref, *, n1, n2):
    pw = 2 * FOURIER_GROUP_W
    scale = 1.0 / math.sqrt(n1 * n2 * FOURIER_GROUP_W)
    for gp in range(FOURIER_W // pw):
        z = jnp.dot(x_ref[:, gp * pw:(gp + 1) * pw], cd_ref[...], preferred_element_type=F32)
        zr = [z[t1 * n2:(t1 + 1) * n2, :pw] for t1 in range(n1)]
        zi = [z[t1 * n2:(t1 + 1) * n2, pw:] for t1 in range(n1)]
        if n1 == 1:
            a = [(zr[0], zi[0])]
        else:
            a = [(zr[0] + zr[1] + zr[2] + zr[3], zi[0] + zi[1] + zi[2] + zi[3]),
                 (zr[0] + zi[1] - zr[2] - zi[3], zi[0] - zr[1] - zi[2] + zr[3]),
                 (zr[0] - zr[1] + zr[2] - zr[3], zi[0] - zi[1] + zi[2] - zi[3]),
                 (zr[0] - zi[1] - zr[2] + zi[3], zi[0] + zr[1] - zi[2] - zr[3])]
        for u1 in range(n1):
            ar, ai = a[u1]
            if u1 > 0:
                c = jnp.concatenate([twc_ref[u1 - 1]] * 2, axis=1)
                s = jnp.concatenate([tws_ref[u1 - 1]] * 2, axis=1)
                ar, ai = ar * c + ai * s, ai * c - ar * s
            y = (jnp.dot(ct_ref[...], ar.astype(BF16), preferred_element_type=F32)
                 + jnp.dot(st_ref[...], ai.astype(BF16), preferred_element_type=F32)) * scale
            y = jnp.dot(y.astype(BF16), wf_ref[gp], preferred_element_type=F32)
            rows = pl.ds(u1, n2, stride=n1) if n1 > 1 else slice(None)
            for half in range(2):
                o_ref[2 * gp + half, rows, :] = y[:, half * FOURIER_GROUP_W:(half + 1) * FOURIER_GROUP_W]


def _fourier(f, wf_pairs, *, batch, t, row0, n1):
    n2 = t // n1
    cd, ct, st, twc, tws = _dft_tables(n1, n2)
    cd, ct, st = cd.astype(BF16), ct.astype(BF16), st.astype(BF16)
    nt = twc.shape[0]
    b0 = row0 // t
    c2 = lambda b: (0, 0)
    c3 = lambda b: (0, 0, 0)
    return pl.pallas_call(
        functools.partial(_fourier_kernel, n1=n1, n2=n2),
        grid=(batch,),
        in_specs=[pl.BlockSpec((t, FOURIER_W), lambda b: (b0 + b, 0)),
                  pl.BlockSpec(cd.shape, c2), pl.BlockSpec((n2, n2), c2), pl.BlockSpec((n2, n2), c2),
                  pl.BlockSpec((nt, n2, LANES), c3), pl.BlockSpec((nt, n2, LANES), c3),
                  pl.BlockSpec(wf_pairs.shape, c3)],
        out_specs=pl.BlockSpec((FOURIER_W // FOURIER_GROUP_W, t, FOURIER_GROUP_W), lambda b: (0, b, 0)),
        out_shape=jax.ShapeDtypeStruct((FOURIER_W // FOURIER_GROUP_W, batch * t, FOURIER_GROUP_W), F32),
        compiler_params=_cparams(("parallel",)),
        name="fourier",
    )(f, cd, ct, st, twc, tws, wf_pairs)


def _mix_kernel(xc_ref, xl_ref, ac_ref, al_ref, fc_ref, fl_ref, g1_ref, sh2_ref, sc2_ref, ag_ref, fg_ref,
                wout_ref, n2g_ref, x1_ref, h2_ref, *, nc, tpb):
    i = pl.program_id(0)
    r = _mod_row(i, nc, tpb)
    is_ctx = i < nc
    x = jnp.where(is_ctx, xc_ref[...], xl_ref[...])
    a = _rms(jnp.where(is_ctx, ac_ref[...], al_ref[...])) * ag_ref[...]
    n_groups = FOURIER_W // FOURIER_GROUP_W
    f = jnp.concatenate([jnp.where(is_ctx, fc_ref[g], fl_ref[g]) for g in range(n_groups)], axis=1)
    f = _rms(f) * fg_ref[...]
    mixed = (jnp.dot(a.astype(BF16), wout_ref[:ATTN_W, :], preferred_element_type=F32)
             + jnp.dot(f.astype(BF16), wout_ref[ATTN_W:, :], preferred_element_type=F32))
    x1 = x + g1_ref[pl.ds(r, 1), :] * mixed
    x1_ref[...] = x1
    h2 = _rms(x1) * n2g_ref[...]
    _to_tiles(h2_ref, h2 * (1.0 + sc2_ref[pl.ds(r, 1), :]) + sh2_ref[pl.ds(r, 1), :])


def _mix(xc, xl, ac, al, fc, fl, mod, attn_g, four_g, w_out, norm2_g, *, nc, nl, tpb):
    n = (nc + nl) * TM
    lo = lambda i: (jnp.minimum(i, nc - 1), 0)
    hi = lambda i: (jnp.maximum(i - nc, 0), 0)
    const = lambda i: (0, 0)
    row = lambda i: (i, 0)
    return pl.pallas_call(
        functools.partial(_mix_kernel, nc=nc, tpb=tpb),
        grid=(nc + nl,),
        in_specs=[pl.BlockSpec((TM, D_MODEL), lo), pl.BlockSpec((TM, D_MODEL), hi),
                  pl.BlockSpec((TM, ATTN_W), lo), pl.BlockSpec((TM, ATTN_W), hi),
                  pl.BlockSpec((FOURIER_W // FOURIER_GROUP_W, TM, FOURIER_GROUP_W),
                               lambda i: (0, jnp.minimum(i, nc - 1), 0)),
                  pl.BlockSpec((FOURIER_W // FOURIER_GROUP_W, TM, FOURIER_GROUP_W),
                               lambda i: (0, jnp.maximum(i - nc, 0), 0)),
                  pl.BlockSpec((8, D_MODEL), lambda i: (0, 2)),
                  pl.BlockSpec((8, D_MODEL), lambda i: (0, 3)),
                  pl.BlockSpec((8, D_MODEL), lambda i: (0, 4)),
                  pl.BlockSpec((1, ATTN_W), const), pl.BlockSpec((1, FOURIER_W), const),
                  pl.BlockSpec((D_MODEL, D_MODEL), const), pl.BlockSpec((1, D_MODEL), const)],
        out_specs=[pl.BlockSpec((TM, D_MODEL), row), pl.BlockSpec((TM * TILE_ROWS, LANES), row)],
        out_shape=[jax.ShapeDtypeStruct((n, D_MODEL), F32), jax.ShapeDtypeStruct((n * TILE_ROWS, LANES), F32)],
        compiler_params=_cparams(("parallel",)),
        name="mix_proj",
    )(xc, xl, ac, al, fc, fl, mod, mod, mod, attn_g, four_g, w_out, norm2_g)


def _router_kernel(h2_ref, whi_ref, wlo_ref, br_ref, idx_ref, w_ref, cnt_ref):
    i = pl.program_id(0)
    h = _from_tiles(h2_ref)
    h_hi = h.astype(BF16)
    h_lo = (h - h_hi.astype(F32)).astype(BF16)
    logits = (jnp.dot(h_hi, whi_ref[...], preferred_element_type=F32)
              + jnp.dot(h_lo, whi_ref[...], preferred_element_type=F32)
              + jnp.dot(h_hi, wlo_ref[...], preferred_element_type=F32)) + br_ref[...]
    lane = lax.broadcasted_iota(I32, logits.shape, 1)
    l = logits
    vals, idxs = [], []
    hits = jnp.zeros(logits.shape, F32)
    for _ in range(TOP_K):
        m = jnp.max(l, axis=-1, keepdims=True)
        ix = jnp.min(jnp.where(l == m, lane, LANES), axis=-1, keepdims=True)
        sel = lane == ix
        hits = hits + sel.astype(F32)
        l = jnp.where(sel, -jnp.inf, l)
        vals.append(m)
        idxs.append(ix)
    es = [jnp.exp(v - vals[0]) for v in vals]
    den = es[0] + es[1] + es[2] + es[3]
    idx_ref[...] = _pack_cols(idxs, lane)
    w_ref[...] = _pack_cols([e / den for e in es], lane)

    @pl.when(i == 0)
    def _():
        cnt_ref[...] = jnp.zeros(cnt_ref.shape, F32)

    cnt_ref[...] += jnp.sum(hits, axis=0, keepdims=True)


def _router(h2t, w_hi, w_lo, br):
    n = h2t.shape[0] // TILE_ROWS
    row = lambda i: (i, 0)
    const = lambda i: (0, 0)
    return pl.pallas_call(
        _router_kernel,
        grid=(n // TM,),
        in_specs=[pl.BlockSpec((TM * TILE_ROWS, LANES), row), pl.BlockSpec((D_MODEL, LANES), const),
                  pl.BlockSpec((D_MODEL, LANES), const), pl.BlockSpec((1, LANES), const)],
        out_specs=[pl.BlockSpec((TM, LANES), row), pl.BlockSpec((TM, LANES), row), pl.BlockSpec((8, LANES), const)],
        out_shape=[jax.ShapeDtypeStruct((n, LANES), I32), jax.ShapeDtypeStruct((n, LANES), F32),
                   jax.ShapeDtypeStruct((8, LANES), F32)],
        compiler_params=_cparams(("arbitrary",)),
        name="router",
    )(h2t, w_hi, w_lo, br)


def _pos_kernel(idx_ref, gs_ref, pos_ref, carry_ref):
    i = pl.program_id(0)

    @pl.when(i == 0)
    def _():
        carry_ref[...] = jnp.zeros(carry_ref.shape, F32)

    idx = idx_ref[...]
    lane = lax.broadcasted_iota(I32, idx.shape, 1)
    sels = [lane == idx[:, k:k + 1] for k in range(TOP_K)]
    hits = jnp.zeros(idx.shape, F32)
    for s in sels:
        hits = hits + s.astype(F32)
    r = lax.broadcasted_iota(I32, (TM, TM), 0)
    c = lax.broadcasted_iota(I32, (TM, TM), 1)
    before = (c < r).astype(BF16)
    rank = jnp.dot(before, hits.astype(BF16), preferred_element_type=F32) + carry_ref[0:1, :] + gs_ref[0:1, :]
    cols = [jnp.sum(jnp.where(s, rank, 0.0), axis=-1, keepdims=True) for s in sels]
    pos_ref[...] = _pack_cols(cols, lane).T[:8, :].astype(I32)
    carry_ref[...] += jnp.sum(hits, axis=0, keepdims=True)


def _positions(idx, gs_rows):
    n = idx.shape[0]
    return pl.pallas_call(
        _pos_kernel,
        grid=(n // TM,),
        in_specs=[pl.BlockSpec((TM, LANES), lambda i: (i, 0)), pl.BlockSpec((8, LANES), lambda i: (0, 0))],
        out_specs=pl.BlockSpec((8, TM), lambda i: (0, i)),
        out_shape=jax.ShapeDtypeStruct((8, n), I32),
        scratch_shapes=[pltpu.VMEM((8, LANES), F32)],
        compiler_params=_cparams(("arbitrary",)),
        name="positions",
    )(idx, gs_rows)


def _sources_kernel(cnt_ref, starts_ref, ends_ref, pos_ref, src_ref, *, n_tok):
    i = pl.program_id(0)
    pad_base = n_tok * TOP_K

    def pad(p, carry):
        src_ref[p] = pad_base + (p & (TG - 1))
        return carry

    @pl.when(i == 0)
    def _():
        for e in range(N_EXPERTS):
            lax.fori_loop(starts_ref[e] + cnt_ref[e], ends_ref[e], pad, 0)
        lax.fori_loop(ends_ref[N_EXPERTS - 1], src_ref.shape[0], pad, 0)

    def place(t, carry):
        code = (i * TM + t) * TOP_K
        for k in range(TOP_K):
            src_ref[pos_ref[k, t]] = code + k
        return carry

    lax.fori_loop(0, TM, place, 0, unroll=8)


def _sources(cnt, starts, ends, pos, n_rows):
    n_tok = pos.shape[1]
    grid_spec = pltpu.PrefetchScalarGridSpec(
        num_scalar_prefetch=3,
        grid=(n_tok // TM,),
        in_specs=[pl.BlockSpec((8, TM), lambda i, *_: (0, i), memory_space=pltpu.SMEM)],
        out_specs=pl.BlockSpec(memory_space=pltpu.SMEM),
    )
    return pl.pallas_call(
        functools.partial(_sources_kernel, n_tok=n_tok),
        grid_spec=grid_spec,
        out_shape=jax.ShapeDtypeStruct((n_rows,), I32),
        compiler_params=_cparams(("arbitrary",)),
        name="sources",
    )(cnt, starts, ends, pos)


def _moe_kernel(te_ref, nv_ref, par_ref, nxt_ref, cur_ref, nxtsrc_ref, bg_ref, bl_ref, bdn_ref, perm_ref,
                h2t_ref, wgu_hbm, wdn_hbm, yt_ref,
                xflat, yflat, wgu_buf, wdn_buf, wg_s, wl_s, wd_s, gsem, ssem, wsem, *, n_tok):
    i = pl.program_id(0)
    e = te_ref[i]
    valid = i < nv_ref[0]
    fresh = jnp.logical_or(i == 0, e != te_ref[jnp.maximum(i - 1, 0)])
    slot = i % 2
    wslot = par_ref[i]
    pad_base = n_tok * TOP_K
    tile_rows = TG * TILE_ROWS

    def weight_copies(expert, s):
        return (pltpu.make_async_copy(wgu_hbm.at[expert], wgu_buf.at[s], wsem.at[s]),
                pltpu.make_async_copy(wdn_hbm.at[expert], wdn_buf.at[s], wsem.at[s]))

    def gather_row(code, r, s):
        tok = jnp.where(code >= pad_base, 0, code >> 2)
        return pltpu.make_async_copy(h2t_ref.at[pl.ds(pl.multiple_of(tok * TILE_ROWS, TILE_ROWS), TILE_ROWS), :],
                                     xflat.at[s, pl.ds(r * TILE_ROWS, TILE_ROWS), :], gsem.at[s])

    def gather_all(s):
        return pltpu.make_async_copy(h2t_ref.at[pl.ds(0, tile_rows), :], xflat.at[s], gsem.at[s])

    def scatter_row(code, r):
        dst = jnp.where(code >= pad_base, code, (code & 3) * n_tok + (code >> 2))
        return pltpu.make_async_copy(yflat.at[pl.ds(r * TILE_ROWS, TILE_ROWS), :],
                                     yt_ref.at[pl.ds(pl.multiple_of(dst * TILE_ROWS, TILE_ROWS), TILE_ROWS), :], ssem)

    def scatter_all():
        return pltpu.make_async_copy(yflat, yt_ref.at[pl.ds(pad_base * TILE_ROWS, tile_rows), :], ssem)

    @pl.when(i == 0)
    def _():
        for cp in weight_copies(e, wslot):
            cp.start()
        yflat[...] = jnp.zeros(yflat.shape, F32)
        scatter_all().start()
        for r in range(TG):
            gather_row(cur_ref[r], r, 0).start()

    @pl.when(jnp.logical_and(valid, fresh))
    def _():
        for cp in weight_copies(e, wslot):
            cp.wait()
        for j in range(2 * D_EXPERT // (2 * LANES)):
            blk = wgu_buf[wslot, :, j * 2 * LANES:(j + 1) * 2 * LANES].astype(BF16)
            d = jnp.dot(blk, perm_ref[...], preferred_element_type=F32)
            wg_s[:, j * LANES:(j + 1) * LANES] = d[:, :LANES].astype(BF16)
            wl_s[:, j * LANES:(j + 1) * LANES] = d[:, LANES:].astype(BF16)
        wd_s[...] = wdn_buf[wslot].astype(BF16)

        @pl.when(nxt_ref[i] >= 0)
        def _():
            for cp in weight_copies(nxt_ref[i], 1 - wslot):
                cp.start()

    @pl.when(valid)
    def _():
        gather_all(slot).wait()
        x = _from_tiles(xflat.at[slot]).astype(BF16)
        for r in range(TG):
            gather_row(nxtsrc_ref[r], r, 1 - slot).start()
        g = jnp.dot(x, wg_s[...], preferred_element_type=F32) + bg_ref[...]
        l = jnp.dot(x, wl_s[...], preferred_element_type=F32) + bl_ref[...]
        g = jnp.minimum(g, SWIGLU_LIMIT)
        l = jnp.clip(l, -SWIGLU_LIMIT, SWIGLU_LIMIT)
        a = (l + 1.0) * (g * (1.0 / (1.0 + jnp.exp(-SWIGLU_ALPHA * g))))
        y = jnp.dot(a.astype(BF16), wd_s[...], preferred_element_type=F32) + bdn_ref[...]
        scatter_all().wait()
        _to_tiles(yflat, y)
        for r in range(TG):
            scatter_row(cur_ref[r], r).start()

    @pl.when(i == nv_ref[0] - 1)
    def _():
        scatter_all().wait()
        gather_all(1 - slot).wait()


def _moe_ffn(tile_expert, n_valid, parity, next_expert, src, h2t, w_gu, b_g, b_l, w_dn, b_dn, perm, *, n_tok):
    n_tiles = src.shape[0] // TG
    ex3 = lambda i, te, *_: (te[i], 0, 0)
    grid_spec = pltpu.PrefetchScalarGridSpec(
        num_scalar_prefetch=4,
        grid=(n_tiles,),
        in_specs=[pl.BlockSpec((TG,), lambda i, *_: (i,), memory_space=pltpu.SMEM),
                  pl.BlockSpec((TG,), lambda i, *_: (jnp.minimum(i + 1, n_tiles - 1),), memory_space=pltpu.SMEM),
                  pl.BlockSpec((None, 1, D_EXPERT), ex3),
                  pl.BlockSpec((None, 1, D_EXPERT), ex3),
                  pl.BlockSpec((None, 1, D_MODEL), ex3),
                  pl.BlockSpec((2 * LANES, 2 * LANES), lambda i, *_: (0, 0)),
                  pl.BlockSpec(memory_space=pl.ANY),
                  pl.BlockSpec(memory_space=pl.ANY),
                  pl.BlockSpec(memory_space=pl.ANY)],
        out_specs=pl.BlockSpec(memory_space=pl.ANY),
        scratch_shapes=[pltpu.VMEM((2, TG * TILE_ROWS, LANES), F32), pltpu.VMEM((TG * TILE_ROWS, LANES), F32),
                        pltpu.VMEM((2, D_MODEL, 2 * D_EXPERT), F32), pltpu.VMEM((2, D_EXPERT, D_MODEL), F32),
                        pltpu.VMEM((D_MODEL, D_EXPERT), BF16), pltpu.VMEM((D_MODEL, D_EXPERT), BF16),
                        pltpu.VMEM((D_EXPERT, D_MODEL), BF16),
                        pltpu.SemaphoreType.DMA((2,)), pltpu.SemaphoreType.DMA(()), pltpu.SemaphoreType.DMA((2,))],
    )
    return pl.pallas_call(
        functools.partial(_moe_kernel, n_tok=n_tok),
        grid_spec=grid_spec,
        out_shape=jax.ShapeDtypeStruct(((n_tok * TOP_K + TG) * TILE_ROWS, LANES), F32),
        compiler_params=_cparams(("arbitrary",)),
        name="moe_ffn",
    )(tile_expert, n_valid, parity, next_expert, src, src, b_g, b_l, b_dn, perm, h2t, w_gu, w_dn)


def _combine_kernel(w_ref, x1_ref, g2_ref, fg_ref, y0_ref, y1_ref, y2_ref, y3_ref, o_ref, *, tile0, nc, tpb):
    i = pl.program_id(0) + tile0
    r = _mod_row(i, nc, tpb)
    w = w_ref[...]
    moe = w[:, 0:1] * _from_tiles(y0_ref)
    for k, y_ref in ((1, y1_ref), (2, y2_ref), (3, y3_ref)):
        moe = moe + w[:, k:k + 1] * _from_tiles(y_ref)
    y = x1_ref[...] + g2_ref[pl.ds(r, 1), :] * moe
    o_ref[...] = _rms(y) * fg_ref[...]


def _combine(w, x1, mod, final_g, yt, *, tile0, ntiles, nc, tpb, n_tok):
    slot_tiles = n_tok // TM
    slot_spec = lambda k: pl.BlockSpec((TM * TILE_ROWS, LANES), lambda i: (k * slot_tiles + tile0 + i, 0))
    return pl.pallas_call(
        functools.partial(_combine_kernel, tile0=tile0, nc=nc, tpb=tpb),
        grid=(ntiles,),
        in_specs=[pl.BlockSpec((TM, LANES), lambda i: (i + tile0, 0)),
                  pl.BlockSpec((TM, D_MODEL), lambda i: (i + tile0, 0)),
                  pl.BlockSpec((8, D_MODEL), lambda i: (0, 5)),
                  pl.BlockSpec((1, D_MODEL), lambda i: (0, 0))] + [slot_spec(k) for k in range(TOP_K)],
        out_specs=pl.BlockSpec((TM, D_MODEL), lambda i: (i, 0)),
        out_shape=jax.ShapeDtypeStruct((ntiles * TM, D_MODEL), F32),
        compiler_params=_cparams(("parallel",)),
        name="combine",
    )(w, x1, mod, final_g, yt, yt, yt, yt)


def kernel(x_prompt, x_sample, cache_k, cache_v, c, c_ctx, norm1_g, w_mod, b_mod, w_in, q_norm_g, k_norm_g,
           w_fourier, attn_out_g, fourier_out_g, w_out, norm2_g, w_router, b_router, w_gate_up, b_gate_up,
           w_down, b_down, final_g):
    bc, tc, _ = x_prompt.shape
    bl, tl, _ = x_sample.shape
    depth = w_in.shape[0]
    past = cache_k.shape[2]
    assert depth == 1 and tl % TQ == 0 and tl % TM == 0 and (bc * tc) % TM == 0
    n_ctx, n_lat = bc * tc, bl * tl
    n_tok = n_ctx + n_lat
    nc, nl, tpb = n_ctx // TM, n_lat // TM, tl // TM

    xc = x_prompt.reshape(n_ctx, D_MODEL)
    xl = x_sample.reshape(n_lat, D_MODEL)

    c_rows = jnp.concatenate([c_ctx[None, :], c, jnp.zeros((8 - 1 - bl, D_MODEL), F32)], axis=0)
    mod = _modulation(c_rows, w_mod[0], b_mod[0])

    cos_t, sin_t = _rope_tables(tl)
    q, k, v, f, k_new, v_new = _pre(
        xc, xl, mod, norm1_g, w_in[0].astype(BF16), jnp.tile(q_norm_g, (1, N_HEADS)),
        jnp.tile(k_norm_g, (1, N_KV_HEADS)), _head_mean_matrix(), cos_t, sin_t, nc=nc, nl=nl, tpb=tpb)

    kc, vtc = _split_kv_heads(k[:n_ctx].reshape(bc, tc, KV_W), v[:n_ctx].reshape(bc, tc, KV_W))
    attn_c = _attention(q, kc, vtc, batch=bc, t=tc, s=tc, q_row0=0)
    keys = jnp.concatenate([cache_k[:, 0].reshape(bl, past, KV_W).astype(BF16),
                            k[n_ctx:].reshape(bl, tl, KV_W)], axis=1)
    vals = jnp.concatenate([cache_v[:, 0].reshape(bl, past, KV_W).astype(BF16),
                            v[n_ctx:].reshape(bl, tl, KV_W)], axis=1)
    kl, vtl = _split_kv_heads(keys, vals)
    attn_l = _attention(q, kl, vtl, batch=bl, t=tl, s=past + tl, q_row0=n_ctx)

    wf = w_fourier[0].astype(BF16)
    zero = jnp.zeros_like(wf[0])
    wf_pairs = jnp.stack([jnp.block([[wf[0], zero], [zero, wf[1]]]), jnp.block([[wf[2], zero], [zero, wf[3]]])])
    four_c = _fourier(f, wf_pairs, batch=bc, t=tc, row0=0, n1=1)
    four_l = _fourier(f, wf_pairs, batch=bl, t=tl, row0=n_ctx, n1=FFT_N1)

    x1, h2t = _mix(xc, xl, attn_c, attn_l, four_c, four_l, mod, attn_out_g, fourier_out_g,
                  w_out[0].astype(BF16), norm2_g, nc=nc, nl=nl, tpb=tpb)

    wr = jnp.pad(w_router[0], ((0, 0), (0, LANES - N_EXPERTS)))
    wr_hi = wr.astype(BF16)
    wr_lo = (wr - wr_hi.astype(F32)).astype(BF16)
    br = jnp.pad(b_router[0], (0, LANES - N_EXPERTS), constant_values=-1e30).reshape(1, LANES)
    idx, gate_w, counts = _router(h2t, wr_hi, wr_lo, br)
    cnt = counts[0, :N_EXPERTS].astype(I32)
    padded = ((cnt + TG - 1) // TG) * TG
    ends = jnp.cumsum(padded)
    starts = ends - padded
    n_rows = n_tok * TOP_K + N_EXPERTS * TG
    n_tiles = n_rows // TG
    tile_ids = jnp.arange(n_tiles, dtype=I32)
    tile_expert = jnp.minimum(jnp.sum((ends[None, :] // TG <= tile_ids[:, None]).astype(I32), axis=1),
                              N_EXPERTS - 1)
    n_valid = (ends[N_EXPERTS - 1] // TG).reshape(1)
    gs_rows = jnp.broadcast_to(jnp.pad(starts.astype(F32), (0, LANES - N_EXPERTS))[None, :], (8, LANES))
    pos = _positions(idx, gs_rows)
    src = _sources(cnt, starts, ends, pos, n_rows)
    fresh_tile = jnp.concatenate([jnp.ones((1,), I32), (tile_expert[1:] != tile_expert[:-1]).astype(I32)])
    parity = (jnp.cumsum(fresh_tile) - 1) % 2
    experts = jnp.arange(N_EXPERTS, dtype=I32)
    later = jnp.logical_and(padded[None, :] > 0, experts[None, :] > experts[:, None])
    next_of = jnp.min(jnp.where(later, experts[None, :], N_EXPERTS), axis=1)
    next_of = jnp.where(next_of == N_EXPERTS, -1, next_of)
    next_expert = jnp.sum(jnp.where(tile_expert[:, None] == experts[None, :], next_of[None, :], 0), axis=1)

    b_gu = b_gate_up[0].reshape(N_EXPERTS, 1, D_EXPERT, 2)
    yt = _moe_ffn(tile_expert, n_valid, parity.astype(I32), next_expert.astype(I32), src, h2t, w_gate_up[0],
                  b_gu[..., 0], b_gu[..., 1], w_down[0], b_down[0].reshape(N_EXPERTS, 1, D_MODEL),
                  _deinterleave_matrix(), n_tok=n_tok)

    y_c = _combine(gate_w, x1, mod, final_g.reshape(1, -1), yt, tile0=0, ntiles=nc, nc=nc, tpb=tpb, n_tok=n_tok)
    y_l = _combine(gate_w, x1, mod, final_g.reshape(1, -1), yt, tile0=nc, ntiles=nl, nc=nc, tpb=tpb, n_tok=n_tok)

    new_k = k_new[:n_ctx].reshape(bc, 1, tc, N_KV_HEADS, HEAD_DIM)
    new_v = v_new[:n_ctx].reshape(bc, 1, tc, N_KV_HEADS, HEAD_DIM)
    return (y_c.reshape(bc, tc, D_MODEL), y_l.reshape(bl, tl, D_MODEL), new_k, new_v)
```

```python
import functools
import math

import numpy as np
import jax
import jax.numpy as jnp
from jax import lax
from jax.experimental import pallas as pl
from jax.experimental.pallas import tpu as pltpu

F32 = jnp.float32
BF16 = jnp.bfloat16
I32 = jnp.int32

D_MODEL = 1024
HEAD_DIM = 64
N_HEADS = 8
N_KV_HEADS = 2
HEADS_PER_KV = N_HEADS // N_KV_HEADS
ATTN_W = N_HEADS * HEAD_DIM
KV_W = N_KV_HEADS * HEAD_DIM
FOURIER_W = 512
FOURIER_GROUP_W = 128
D_IN = ATTN_W + 2 * KV_W + FOURIER_W
N_EXPERTS = 32
TOP_K = 4
D_EXPERT = 1024
SWIGLU_ALPHA = 1.702
SWIGLU_LIMIT = 7.0
ROPE_THETA = 10000.0
ROT_PAIRS = HEAD_DIM // 4
GRID_W = 64
EPS = 1e-6

LANES = 128
TM = 512
TQ = 512
TG = 256
FFT_N1 = 4
TILE_ROWS = D_MODEL // LANES
VMEM_LIMIT = 56 * 1024 * 1024


def _cparams(sem, vmem=VMEM_LIMIT):
    return pltpu.CompilerParams(dimension_semantics=sem, vmem_limit_bytes=vmem)


def _rope_tables(t_lat):
    pos = np.arange(t_lat)
    row = (pos // GRID_W).astype(np.float64)
    col = (pos % GRID_W).astype(np.float64)
    inv = ROPE_THETA ** (-np.arange(ROT_PAIRS, dtype=np.float64) / ROT_PAIRS)
    lane = np.arange(LANES)
    d = lane % HEAD_DIM
    axis = d // (2 * ROT_PAIRS)
    second = (d // ROT_PAIRS) % 2
    p = d % ROT_PAIRS
    ang = np.where(axis[None, :] == 0, row[:, None], col[:, None]) * inv[p][None, :]
    cos = np.cos(ang)
    sin = np.sin(ang) * np.where(second == 0, -1.0, 1.0)[None, :]
    cos = np.concatenate([np.ones((TM, LANES)), cos], axis=0)
    sin = np.concatenate([np.zeros((TM, LANES)), sin], axis=0)
    return jnp.asarray(cos, F32), jnp.asarray(sin, F32)


def _dft_tables(n1, n2):
    c = np.arange(FOURIER_GROUP_W)
    ang = 2.0 * np.pi * np.outer(c, c) / FOURIER_GROUP_W
    z = np.zeros_like(ang)
    cbd = np.block([[np.cos(ang), z], [z, np.cos(ang)]])
    sbd = np.block([[np.sin(ang), z], [z, np.sin(ang)]])
    cd = np.concatenate([cbd, -sbd], axis=1)
    u = np.arange(n2)
    ang2 = 2.0 * np.pi * np.outer(u, u) / n2
    ct, st = np.cos(ang2), np.sin(ang2)
    nt = max(n1 - 1, 1)
    tw_ang = 2.0 * np.pi * np.outer(np.arange(1, nt + 1), u) / (n1 * n2)
    twc = np.repeat(np.cos(tw_ang)[:, :, None], LANES, axis=2)
    tws = np.repeat(np.sin(tw_ang)[:, :, None], LANES, axis=2)
    return (jnp.asarray(cd, F32), jnp.asarray(ct, F32), jnp.asarray(st, F32),
            jnp.asarray(twc, F32), jnp.asarray(tws, F32))


def _head_mean_matrix():
    h = np.arange(ATTN_W) // HEAD_DIM
    return jnp.asarray((h[:, None] == h[None, :]) / HEAD_DIM, BF16)


def _deinterleave_matrix():
    p = np.zeros((2 * LANES, 2 * LANES))
    m = np.arange(LANES)
    p[2 * m, m] = 1.0
    p[2 * m + 1, LANES + m] = 1.0
    return jnp.asarray(p, BF16)


def _mod_row(i, nc, tpb):
    return jnp.where(i < nc, 0, 1 + (i - nc) // tpb)


def _rms(x):
    return x * lax.rsqrt(jnp.mean(x * x, axis=-1, keepdims=True) + EPS)


def _pack_cols(cols, lane):
    out = jnp.zeros(lane.shape, cols[0].dtype)
    for k, c in enumerate(cols):
        out = jnp.where(lane == k, c, out)
    return out


def _to_tiles(ref, x):
    r = x.shape[0]
    for s in range(TILE_ROWS):
        ref[pl.ds(s, r, stride=TILE_ROWS), :] = x[:, s * LANES:(s + 1) * LANES]


def _from_tiles(ref):
    r = ref.shape[0] // TILE_ROWS
    return jnp.concatenate([ref[pl.ds(s, r, stride=TILE_ROWS), :] for s in range(TILE_ROWS)], axis=1)


def _mod_kernel(c_ref, w_ref, b_ref, o_ref):
    c = c_ref[...]
    s = c * (1.0 / (1.0 + jnp.exp(-c)))
    o_ref[...] = jnp.dot(s.astype(BF16), w_ref[...].astype(BF16), preferred_element_type=F32) + b_ref[...]


def _modulation(c_rows, w_mod, b_mod):
    n = w_mod.shape[1] // D_MODEL
    return pl.pallas_call(
        _mod_kernel,
        grid=(n,),
        in_specs=[pl.BlockSpec((8, D_MODEL), lambda j: (0, 0)),
                  pl.BlockSpec((D_MODEL, D_MODEL), lambda j: (0, j)),
                  pl.BlockSpec((1, D_MODEL), lambda j: (0, j))],
        out_specs=pl.BlockSpec((8, D_MODEL), lambda j: (0, j)),
        out_shape=jax.ShapeDtypeStruct((8, w_mod.shape[1]), F32),
        compiler_params=_cparams(("parallel",)),
        name="modulation",
    )(c_rows, w_mod, b_mod.reshape(1, -1))


def _pre_kernel(xc_ref, xl_ref, sh_ref, sc_ref, g1_ref, win_ref, qg_ref, kg_ref, bd_ref, cos_ref, sin_ref,
                q_ref, k_ref, v_ref, f_ref, kn_ref, vn_ref, *, nc, tpb):
    i = pl.program_id(0)
    r = _mod_row(i, nc, tpb)
    x = jnp.where(i < nc, xc_ref[...], xl_ref[...])
    h = _rms(x) * g1_ref[...]
    h = h * (1.0 + sc_ref[pl.ds(r, 1), :]) + sh_ref[pl.ds(r, 1), :]
    proj = jnp.dot(h.astype(BF16), win_ref[...], preferred_element_type=F32)
    q = proj[:, :ATTN_W]
    k = proj[:, ATTN_W:ATTN_W + KV_W]
    v = proj[:, ATTN_W + KV_W:ATTN_W + 2 * KV_W]
    f_ref[...] = proj[:, ATTN_W + 2 * KV_W:].astype(BF16)
    vn_ref[...] = v
    v_ref[...] = v.astype(BF16)
    q_ms = jnp.dot((q * q).astype(BF16), bd_ref[...], preferred_element_type=F32)
    k_ms = jnp.dot((k * k).astype(BF16), bd_ref[:KV_W, :KV_W], preferred_element_type=F32)
    qn = q * lax.rsqrt(q_ms + EPS) * qg_ref[...]
    kn = k * lax.rsqrt(k_ms + EPS) * kg_ref[...]
    kn_ref[...] = kn
    cos = cos_ref[...]
    sin = sin_ref[...]
    lane = lax.broadcasted_iota(I32, cos.shape, 1)
    first = (lane & ROT_PAIRS) == 0

    def rope(c):
        partner = jnp.where(first, pltpu.roll(c, LANES - ROT_PAIRS, 1), pltpu.roll(c, ROT_PAIRS, 1))
        return c * cos + partner * sin

    k_ref[...] = rope(kn).astype(BF16)
    scale = HEAD_DIM ** -0.5 * math.log2(math.e)
    q_rot = jnp.concatenate([rope(qn[:, j * LANES:(j + 1) * LANES]) for j in range(ATTN_W // LANES)], axis=1)
    q_ref[...] = (q_rot * scale).T.astype(BF16)


def _pre(xc, xl, mod, norm1_g, w_in, q_g, k_g, bd, cos_t, sin_t, *, nc, nl, tpb):
    n = (nc + nl) * TM
    row = lambda i: (i, 0)
    const = lambda i: (0, 0)
    tab = lambda i: (jnp.where(i < nc, 0, 1 + (i - nc) % tpb), 0)
    return pl.pallas_call(
        functools.partial(_pre_kernel, nc=nc, tpb=tpb),
        grid=(nc + nl,),
        in_specs=[pl.BlockSpec((TM, D_MODEL), lambda i: (jnp.minimum(i, nc - 1), 0)),
                  pl.BlockSpec((TM, D_MODEL), lambda i: (jnp.maximum(i - nc, 0), 0)),
                  pl.BlockSpec((8, D_MODEL), lambda i: (0, 0)),
                  pl.BlockSpec((8, D_MODEL), lambda i: (0, 1)),
                  pl.BlockSpec((1, D_MODEL), const),
                  pl.BlockSpec((D_MODEL, D_IN), const),
                  pl.BlockSpec((1, ATTN_W), const),
                  pl.BlockSpec((1, KV_W), const),
                  pl.BlockSpec((ATTN_W, ATTN_W), const),
                  pl.BlockSpec((TM, LANES), tab),
                  pl.BlockSpec((TM, LANES), tab)],
        out_specs=[pl.BlockSpec((ATTN_W, TM), lambda i: (0, i)), pl.BlockSpec((TM, KV_W), row),
                   pl.BlockSpec((TM, KV_W), row),
                   pl.BlockSpec((TM, FOURIER_W), row), pl.BlockSpec((TM, KV_W), row), pl.BlockSpec((TM, KV_W), row)],
        out_shape=[jax.ShapeDtypeStruct((ATTN_W, n), BF16), jax.ShapeDtypeStruct((n, KV_W), BF16),
                   jax.ShapeDtypeStruct((n, KV_W), BF16), jax.ShapeDtypeStruct((n, FOURIER_W), BF16),
                   jax.ShapeDtypeStruct((n, KV_W), F32), jax.ShapeDtypeStruct((n, KV_W), F32)],
        compiler_params=_cparams(("parallel",)),
        name="pre_proj",
    )(xc, xl, mod, mod, norm1_g, w_in, q_g, k_g, bd, cos_t, sin_t)


V_ROWS = 80


def _attn_kernel(qt_ref, k_ref, vt_ref, o_ref, ot_ref, st_ref, pt_ref):
    def scores(h):
        qt = qt_ref[h * HEAD_DIM:(h + 1) * HEAD_DIM, :]
        st_ref[h % 2] = jnp.dot(k_ref[h // HEADS_PER_KV], qt, preferred_element_type=F32)

    scores(0)
    for h in range(N_HEADS):
        if h + 1 < N_HEADS:
            scores(h + 1)
        st = st_ref[h % 2]
        pt_ref[h % 2] = jnp.exp2(st - jnp.max(st, axis=0, keepdims=True)).astype(BF16)
        ot = jnp.dot(vt_ref[h // HEADS_PER_KV], pt_ref[h % 2], preferred_element_type=F32)
        ot_ref[h * HEAD_DIM:(h + 1) * HEAD_DIM, :] = ot[:HEAD_DIM] * (1.0 / ot[HEAD_DIM:HEAD_DIM + 1])
    o_ref[...] = ot_ref[...].T


def _attention(qt, k_heads, vt_ext, *, batch, t, s, q_row0):
    tq = min(TQ, t)
    qpb = t // tq
    q0 = q_row0 // tq
    return pl.pallas_call(
        _attn_kernel,
        grid=(batch, qpb),
        in_specs=[pl.BlockSpec((ATTN_W, tq), lambda b, i: (0, q0 + b * qpb + i)),
                  pl.BlockSpec((None, N_KV_HEADS, s, HEAD_DIM), lambda b, i: (b, 0, 0, 0)),
                  pl.BlockSpec((None, N_KV_HEADS, V_ROWS, s), lambda b, i: (b, 0, 0, 0))],
        out_specs=pl.BlockSpec((tq, ATTN_W), lambda b, i: (b * qpb + i, 0)),
        out_shape=jax.ShapeDtypeStruct((batch * t, ATTN_W), F32),
        scratch_shapes=[pltpu.VMEM((ATTN_W, tq), F32), pltpu.VMEM((2, s, tq), F32), pltpu.VMEM((2, s, tq), BF16)],
        compiler_params=_cparams(("parallel", "parallel")),
        name="attention",
    )(qt, k_heads, vt_ext)


def _split_kv_heads(keys, vals):
    b, s, _ = keys.shape
    k_heads = keys.reshape(b, s, N_KV_HEADS, HEAD_DIM).transpose(0, 2, 1, 3)
    vt = vals.reshape(b, s, N_KV_HEADS, HEAD_DIM).transpose(0, 2, 3, 1)
    extra = jnp.concatenate([jnp.ones((b, N_KV_HEADS, 1, s), BF16),
                             jnp.zeros((b, N_KV_HEADS, V_ROWS - HEAD_DIM - 1, s), BF16)], axis=2)
    return k_heads, jnp.concatenate([vt, extra], axis=2)


def _fourier_kernel(x_ref, cd_ref, ct_ref, st_ref, twc_ref, tws_ref, wf_ref, o_ref, *, n1, n2):
    pw = 2 * FOURIER_GROUP_W
    scale = 1.0 / math.sqrt(n1 * n2 * FOURIER_GROUP_W)
    for gp in range(FOURIER_W // pw):
        z = jnp.dot(x_ref[:, gp * pw:(gp + 1) * pw], cd_ref[...], preferred_element_type=F32)
        zr = [z[t1 * n2:(t1 + 1) * n2, :pw] for t1 in range(n1)]
        zi = [z[t1 * n2:(t1 + 1) * n2, pw:] for t1 in range(n1)]
        if n1 == 1:
            a = [(zr[0], zi[0])]
        else:
            a = [(zr[0] + zr[1] + zr[2] + zr[3], zi[0] + zi[1] + zi[2] + zi[3]),
                 (zr[0] + zi[1] - zr[2] - zi[3], zi[0] - zr[1] - zi[2] + zr[3]),
                 (zr[0] - zr[1] + zr[2] - zr[3], zi[0] - zi[1] + zi[2] - zi[3]),
                 (zr[0] - zi[1] - zr[2] + zi[3], zi[0] + zr[1] - zi[2] - zr[3])]
        for u1 in range(n1):
            ar, ai = a[u1]
            if u1 > 0:
                c = jnp.concatenate([twc_ref[u1 - 1]] * 2, axis=1)
                s = jnp.concatenate([tws_ref[u1 - 1]] * 2, axis=1)
                ar, ai = ar * c + ai * s, ai * c - ar * s
            y = (jnp.dot(ct_ref[...], ar.astype(BF16), preferred_element_type=F32)
                 + jnp.dot(st_ref[...], ai.astype(BF16), preferred_element_type=F32)) * scale
            y = jnp.dot(y.astype(BF16), wf_ref[gp], preferred_element_type=F32)
            rows = pl.ds(u1, n2, stride=n1) if n1 > 1 else slice(None)
            for half in range(2):
                o_ref[2 * gp + half, rows, :] = y[:, half * FOURIER_GROUP_W:(half + 1) * FOURIER_GROUP_W]


def _fourier(f, wf_pairs, *, batch, t, row0, n1):
    n2 = t // n1
    cd, ct, st, twc, tws = _dft_tables(n1, n2)
    cd, ct, st = cd.astype(BF16), ct.astype(BF16), st.astype(BF16)
    nt = twc.shape[0]
    b0 = row0 // t
    c2 = lambda b: (0, 0)
    c3 = lambda b: (0, 0, 0)
    return pl.pallas_call(
        functools.partial(_fourier_kernel, n1=n1, n2=n2),
        grid=(batch,),
        in_specs=[pl.BlockSpec((t, FOURIER_W), lambda b: (b0 + b, 0)),
                  pl.BlockSpec(cd.shape, c2), pl.BlockSpec((n2, n2), c2), pl.BlockSpec((n2, n2), c2),
                  pl.BlockSpec((nt, n2, LANES), c3), pl.BlockSpec((nt, n2, LANES), c3),
                  pl.BlockSpec(wf_pairs.shape, c3)],
        out_specs=pl.BlockSpec((FOURIER_W // FOURIER_GROUP_W, t, FOURIER_GROUP_W), lambda b: (0, b, 0)),
        out_shape=jax.ShapeDtypeStruct((FOURIER_W // FOURIER_GROUP_W, batch * t, FOURIER_GROUP_W), F32),
        compiler_params=_cparams(("parallel",)),
        name="fourier",
    )(f, cd, ct, st, twc, tws, wf_pairs)


def _mix_kernel(xc_ref, xl_ref, ac_ref, al_ref, fc_ref, fl_ref, g1_ref, sh2_ref, sc2_ref, ag_ref, fg_ref,
                wout_ref, n2g_ref, x1_ref, h2_ref, *, nc, tpb):
    i = pl.program_id(0)
    r = _mod_row(i, nc, tpb)
    is_ctx = i < nc
    x = jnp.where(is_ctx, xc_ref[...], xl_ref[...])
    a = _rms(jnp.where(is_ctx, ac_ref[...], al_ref[...])) * ag_ref[...]
    n_groups = FOURIER_W // FOURIER_GROUP_W
    f = jnp.concatenate([jnp.where(is_ctx, fc_ref[g], fl_ref[g]) for g in range(n_groups)], axis=1)
    f = _rms(f) * fg_ref[...]
    mixed = (jnp.dot(a.astype(BF16), wout_ref[:ATTN_W, :], preferred_element_type=F32)
             + jnp.dot(f.astype(BF16), wout_ref[ATTN_W:, :], preferred_element_type=F32))
    x1 = x + g1_ref[pl.ds(r, 1), :] * mixed
    x1_ref[...] = x1
    h2 = _rms(x1) * n2g_ref[...]
    _to_tiles(h2_ref, h2 * (1.0 + sc2_ref[pl.ds(r, 1), :]) + sh2_ref[pl.ds(r, 1), :])


def _mix(xc, xl, ac, al, fc, fl, mod, attn_g, four_g, w_out, norm2_g, *, nc, nl, tpb):
    n = (nc + nl) * TM
    lo = lambda i: (jnp.minimum(i, nc - 1), 0)
    hi = lambda i: (jnp.maximum(i - nc, 0), 0)
    const = lambda i: (0, 0)
    row = lambda i: (i, 0)
    return pl.pallas_call(
        functools.partial(_mix_kernel, nc=nc, tpb=tpb),
        grid=(nc + nl,),
        in_specs=[pl.BlockSpec((TM, D_MODEL), lo), pl.BlockSpec((TM, D_MODEL), hi),
                  pl.BlockSpec((TM, ATTN_W), lo), pl.BlockSpec((TM, ATTN_W), hi),
                  pl.BlockSpec((FOURIER_W // FOURIER_GROUP_W, TM, FOURIER_GROUP_W),
                               lambda i: (0, jnp.minimum(i, nc - 1), 0)),
                  pl.BlockSpec((FOURIER_W // FOURIER_GROUP_W, TM, FOURIER_GROUP_W),
                               lambda i: (0, jnp.maximum(i - nc, 0), 0)),
                  pl.BlockSpec((8, D_MODEL), lambda i: (0, 2)),
                  pl.BlockSpec((8, D_MODEL), lambda i: (0, 3)),
                  pl.BlockSpec((8, D_MODEL), lambda i: (0, 4)),
                  pl.BlockSpec((1, ATTN_W), const), pl.BlockSpec((1, FOURIER_W), const),
                  pl.BlockSpec((D_MODEL, D_MODEL), const), pl.BlockSpec((1, D_MODEL), const)],
        out_specs=[pl.BlockSpec((TM, D_MODEL), row), pl.BlockSpec((TM * TILE_ROWS, LANES), row)],
        out_shape=[jax.ShapeDtypeStruct((n, D_MODEL), F32), jax.ShapeDtypeStruct((n * TILE_ROWS, LANES), F32)],
        compiler_params=_cparams(("parallel",)),
        name="mix_proj",
    )(xc, xl, ac, al, fc, fl, mod, mod, mod, attn_g, four_g, w_out, norm2_g)


def _router_kernel(h2_ref, whi_ref, wlo_ref, br_ref, idx_ref, w_ref, cnt_ref):
    i = pl.program_id(0)
    h = _from_tiles(h2_ref)
    h_hi = h.astype(BF16)
    h_lo = (h - h_hi.astype(F32)).astype(BF16)
    logits = (jnp.dot(h_hi, whi_ref[...], preferred_element_type=F32)
              + jnp.dot(h_lo, whi_ref[...], preferred_element_type=F32)
              + jnp.dot(h_hi, wlo_ref[...], preferred_element_type=F32)) + br_ref[...]
    lane = lax.broadcasted_iota(I32, logits.shape, 1)
    l = logits
    vals, idxs = [], []
    hits = jnp.zeros(logits.shape, F32)
    for _ in range(TOP_K):
        m = jnp.max(l, axis=-1, keepdims=True)
        ix = jnp.min(jnp.where(l == m, lane, LANES), axis=-1, keepdims=True)
        sel = lane == ix
        hits = hits + sel.astype(F32)
        l = jnp.where(sel, -jnp.inf, l)
        vals.append(m)
        idxs.append(ix)
    es = [jnp.exp(v - vals[0]) for v in vals]
    den = es[0] + es[1] + es[2] + es[3]
    idx_ref[...] = _pack_cols(idxs, lane)
    w_ref[...] = _pack_cols([e / den for e in es], lane)

    @pl.when(i == 0)
    def _():
        cnt_ref[...] = jnp.zeros(cnt_ref.shape, F32)

    cnt_ref[...] += jnp.sum(hits, axis=0, keepdims=True)


def _router(h2t, w_hi, w_lo, br):
    n = h2t.shape[0] // TILE_ROWS
    row = lambda i: (i, 0)
    const = lambda i: (0, 0)
    return pl.pallas_call(
        _router_kernel,
        grid=(n // TM,),
        in_specs=[pl.BlockSpec((TM * TILE_ROWS, LANES), row), pl.BlockSpec((D_MODEL, LANES), const),
                  pl.BlockSpec((D_MODEL, LANES), const), pl.BlockSpec((1, LANES), const)],
        out_specs=[pl.BlockSpec((TM, LANES), row), pl.BlockSpec((TM, LANES), row), pl.BlockSpec((8, LANES), const)],
        out_shape=[jax.ShapeDtypeStruct((n, LANES), I32), jax.ShapeDtypeStruct((n, LANES), F32),
                   jax.ShapeDtypeStruct((8, LANES), F32)],
        compiler_params=_cparams(("arbitrary",)),
        name="router",
    )(h2t, w_hi, w_lo, br)


def _pos_kernel(idx_ref, gs_ref, pos_ref, carry_ref):
    i = pl.program_id(0)

    @pl.when(i == 0)
    def _():
        carry_ref[...] = jnp.zeros(carry_ref.shape, F32)

    idx = idx_ref[...]
    lane = lax.broadcasted_iota(I32, idx.shape, 1)
    sels = [lane == idx[:, k:k + 1] for k in range(TOP_K)]
    hits = jnp.zeros(idx.shape, F32)
    for s in sels:
        hits = hits + s.astype(F32)
    r = lax.broadcasted_iota(I32, (TM, TM), 0)
    c = lax.broadcasted_iota(I32, (TM, TM), 1)
    before = (c < r).astype(BF16)
    rank = jnp.dot(before, hits.astype(BF16), preferred_element_type=F32) + carry_ref[0:1, :] + gs_ref[0:1, :]
    cols = [jnp.sum(jnp.where(s, rank, 0.0), axis=-1, keepdims=True) for s in sels]
    pos_ref[...] = _pack_cols(cols, lane).T[:8, :].astype(I32)
    carry_ref[...] += jnp.sum(hits, axis=0, keepdims=True)


def _positions(idx, gs_rows):
    n = idx.shape[0]
    return pl.pallas_call(
        _pos_kernel,
        grid=(n // TM,),
        in_specs=[pl.BlockSpec((TM, LANES), lambda i: (i, 0)), pl.BlockSpec((8, LANES), lambda i: (0, 0))],
        out_specs=pl.BlockSpec((8, TM), lambda i: (0, i)),
        out_shape=jax.ShapeDtypeStruct((8, n), I32),
        scratch_shapes=[pltpu.VMEM((8, LANES), F32)],
        compiler_params=_cparams(("arbitrary",)),
        name="positions",
    )(idx, gs_rows)


def _sources_kernel(cnt_ref, starts_ref, ends_ref, pos_ref, src_ref, *, n_tok):
    i = pl.program_id(0)
    pad_base = n_tok * TOP_K

    def pad(p, carry):
        src_ref[p] = pad_base + (p & (TG - 1))
        return carry

    @pl.when(i == 0)
    def _():
        for e in range(N_EXPERTS):
            lax.fori_loop(starts_ref[e] + cnt_ref[e], ends_ref[e], pad, 0)
        lax.fori_loop(ends_ref[N_EXPERTS - 1], src_ref.shape[0], pad, 0)

    batch = 8

    def place(tb, carry):
        t0 = tb * batch
        rows = [[pos_ref[k, t0 + u] for k in range(TOP_K)] for u in range(batch)]
        for u in range(batch):
            code = (i * TM + t0 + u) * TOP_K
            for k in range(TOP_K):
                src_ref[rows[u][k]] = code + k
        return carry

    lax.fori_loop(0, TM // batch, place, 0)


def _sources(cnt, starts, ends, pos, n_rows):
    n_tok = pos.shape[1]
    grid_spec = pltpu.PrefetchScalarGridSpec(
        num_scalar_prefetch=3,
        grid=(n_tok // TM,),
        in_specs=[pl.BlockSpec((8, TM), lambda i, *_: (0, i), memory_space=pltpu.SMEM)],
        out_specs=pl.BlockSpec(memory_space=pltpu.SMEM),
    )
    return pl.pallas_call(
        functools.partial(_sources_kernel, n_tok=n_tok),
        grid_spec=grid_spec,
        out_shape=jax.ShapeDtypeStruct((n_rows,), I32),
        compiler_params=_cparams(("arbitrary",)),
        name="sources",
    )(cnt, starts, ends, pos)


GL_CHUNKS = D_EXPERT // LANES
DN_CHUNKS = D_MODEL // (2 * LANES)


def _moe_kernel(te_ref, nv_ref, par_ref, nxt_ref, prev_ref, cur_ref, nxtsrc_ref, bgl_ref, bdn_ref, perm_ref,
                h2t_ref, wgu_hbm, wdn_hbm, yt_ref,
                xflat, yflat, xb_s, a_s, wgu_buf, wdn_buf, wgl_s, wd_s, gsem, ssem, wsem, *, n_tok):
    i = pl.program_id(0)
    e = te_ref[i]
    n_valid = nv_ref[0]
    valid = i < n_valid
    fresh = jnp.logical_or(i == 0, e != te_ref[jnp.maximum(i - 1, 0)])
    slot = i % 2
    wslot = par_ref[i]
    pad_base = n_tok * TOP_K
    tile_rows = TG * TILE_ROWS

    def weight_copies(expert, s):
        return (pltpu.make_async_copy(wgu_hbm.at[expert], wgu_buf.at[s], wsem.at[s]),
                pltpu.make_async_copy(wdn_hbm.at[expert], wdn_buf.at[s], wsem.at[s]))

    def tile_of(r):
        start = r * TILE_ROWS
        return pl.ds(start if isinstance(start, int) else pl.multiple_of(start, TILE_ROWS), TILE_ROWS)

    def gather_row(code, r, s):
        tok = jnp.where(code >= pad_base, 0, code >> 2)
        return pltpu.make_async_copy(
            h2t_ref.at[pl.ds(pl.multiple_of(tok * TILE_ROWS, TILE_ROWS), TILE_ROWS), :],
            xflat.at[s, tile_of(r), :], gsem.at[s])

    def gather_all(s):
        return pltpu.make_async_copy(h2t_ref.at[pl.ds(0, tile_rows), :], xflat.at[s], gsem.at[s])

    def scatter_row(code, r, s):
        dst = jnp.where(code >= pad_base, code, (code & 3) * n_tok + (code >> 2))
        return pltpu.make_async_copy(
            yflat.at[s, tile_of(r), :],
            yt_ref.at[pl.ds(pl.multiple_of(dst * TILE_ROWS, TILE_ROWS), TILE_ROWS), :], ssem.at[s])

    def scatter_all(s):
        return pltpu.make_async_copy(
            yflat.at[s], yt_ref.at[pl.ds(pl.multiple_of((pad_base + s * TG) * TILE_ROWS, TILE_ROWS), tile_rows), :],
            ssem.at[s])

    @pl.when(i == 0)
    def _():
        for cp in weight_copies(e, wslot):
            cp.start()
        yflat[...] = jnp.zeros(yflat.shape, F32)
        scatter_all(0).start()
        scatter_all(1).start()
        for r in range(TG):
            gather_row(cur_ref[r], r, 0).start()

    @pl.when(jnp.logical_and(valid, fresh))
    def _():
        for cp in weight_copies(e, wslot):
            cp.wait()
        for j in range(GL_CHUNKS):
            blk = wgu_buf[wslot, :, j * 2 * LANES:(j + 1) * 2 * LANES].astype(BF16)
            wgl_s[j] = jnp.dot(blk, perm_ref[...], preferred_element_type=F32).astype(BF16)
        for c in range(DN_CHUNKS):
            wd_s[c] = wdn_buf[wslot, :, c * 2 * LANES:(c + 1) * 2 * LANES].astype(BF16)

        @pl.when(nxt_ref[i] >= 0)
        def _():
            for cp in weight_copies(nxt_ref[i], 1 - wslot):
                cp.start()

    @pl.when(valid)
    def _():
        gather_all(slot).wait()
        scatter_all(slot).wait()
        xb_s[...] = _from_tiles(xflat.at[slot]).astype(BF16)
        gathers_per_chunk = TG // GL_CHUNKS
        scatters_per_chunk = TG // DN_CHUNKS

        def gate_linear(j, carry):
            h = jnp.dot(xb_s[...], wgl_s[j], preferred_element_type=F32) + bgl_ref[j]
            g = jnp.minimum(h[:, :LANES], SWIGLU_LIMIT)
            l = jnp.clip(h[:, LANES:], -SWIGLU_LIMIT, SWIGLU_LIMIT)
            a_s[j] = ((l + 1.0) * (g * (1.0 / (1.0 + jnp.exp(-SWIGLU_ALPHA * g))))).astype(BF16)
            for rr in range(gathers_per_chunk):
                r = j * gathers_per_chunk + rr
                gather_row(nxtsrc_ref[r], r, 1 - slot).start()
            return carry

        lax.fori_loop(0, GL_CHUNKS, gate_linear, 0)

        def down(c, carry):
            a = jnp.concatenate([a_s[j] for j in range(GL_CHUNKS)], axis=1)
            y = jnp.dot(a, wd_s[c], preferred_element_type=F32) + bdn_ref[c]
            yflat[slot, pl.ds(2 * c, TG, stride=TILE_ROWS), :] = y[:, :LANES]
            yflat[slot, pl.ds(2 * c + 1, TG, stride=TILE_ROWS), :] = y[:, LANES:]

            @pl.when(i > 0)
            def _():
                for rr in range(scatters_per_chunk):
                    r = c * scatters_per_chunk + rr
                    scatter_row(prev_ref[r], r, 1 - slot).start()

            return carry

        lax.fori_loop(0, DN_CHUNKS, down, 0)

    @pl.when(i == n_valid - 1)
    def _():
        scatter_all(1 - slot).wait()
        for r in range(TG):
            scatter_row(cur_ref[r], r, slot).start()
        scatter_all(slot).wait()
        gather_all(1 - slot).wait()


def _moe_ffn(tile_expert, n_valid, parity, next_expert, src, h2t, w_gu, b_gl, w_dn, b_dn, perm, *, n_tok):
    n_tiles = src.shape[0] // TG
    ex4 = lambda i, te, *_: (te[i], 0, 0, 0)
    smem_tile = lambda fn: pl.BlockSpec((TG,), fn, memory_space=pltpu.SMEM)
    grid_spec = pltpu.PrefetchScalarGridSpec(
        num_scalar_prefetch=4,
        grid=(n_tiles,),
        in_specs=[smem_tile(lambda i, *_: (jnp.maximum(i - 1, 0),)),
                  smem_tile(lambda i, *_: (i,)),
                  smem_tile(lambda i, *_: (jnp.minimum(i + 1, n_tiles - 1),)),
                  pl.BlockSpec((None, GL_CHUNKS, 1, 2 * LANES), ex4),
                  pl.BlockSpec((None, DN_CHUNKS, 1, 2 * LANES), ex4),
                  pl.BlockSpec((2 * LANES, 2 * LANES), lambda i, *_: (0, 0)),
                  pl.BlockSpec(memory_space=pl.ANY),
                  pl.BlockSpec(memory_space=pl.ANY),
                  pl.BlockSpec(memory_space=pl.ANY)],
        out_specs=pl.BlockSpec(memory_space=pl.ANY),
        scratch_shapes=[pltpu.VMEM((2, TG * TILE_ROWS, LANES), F32), pltpu.VMEM((2, TG * TILE_ROWS, LANES), F32),
                        pltpu.VMEM((TG, D_MODEL), BF16), pltpu.VMEM((GL_CHUNKS, TG, LANES), BF16),
                        pltpu.VMEM((2, D_MODEL, 2 * D_EXPERT), F32), pltpu.VMEM((2, D_EXPERT, D_MODEL), F32),
                        pltpu.VMEM((GL_CHUNKS, D_MODEL, 2 * LANES), BF16),
                        pltpu.VMEM((DN_CHUNKS, D_EXPERT, 2 * LANES), BF16),
                        pltpu.SemaphoreType.DMA((2,)), pltpu.SemaphoreType.DMA((2,)), pltpu.SemaphoreType.DMA((2,))],
    )
    return pl.pallas_call(
        functools.partial(_moe_kernel, n_tok=n_tok),
        grid_spec=grid_spec,
        out_shape=jax.ShapeDtypeStruct(((n_tok * TOP_K + 2 * TG) * TILE_ROWS, LANES), F32),
        compiler_params=_cparams(("arbitrary",)),
        name="moe_ffn",
    )(tile_expert, n_valid, parity, next_expert, src, src, src, b_gl, b_dn, perm, h2t, w_gu, w_dn)


def _combine_kernel(w_ref, x1_ref, g2_ref, fg_ref, y0_ref, y1_ref, y2_ref, y3_ref, o_ref, *, tile0, nc, tpb):
    i = pl.program_id(0) + tile0
    r = _mod_row(i, nc, tpb)
    w = w_ref[...]
    moe = w[:, 0:1] * _from_tiles(y0_ref)
    for k, y_ref in ((1, y1_ref), (2, y2_ref), (3, y3_ref)):
        moe = moe + w[:, k:k + 1] * _from_tiles(y_ref)
    y = x1_ref[...] + g2_ref[pl.ds(r, 1), :] * moe
    o_ref[...] = _rms(y) * fg_ref[...]


def _combine(w, x1, mod, final_g, yt, *, tile0, ntiles, nc, tpb, n_tok):
    slot_tiles = n_tok // TM
    slot_spec = lambda k: pl.BlockSpec((TM * TILE_ROWS, LANES), lambda i: (k * slot_tiles + tile0 + i, 0))
    return pl.pallas_call(
        functools.partial(_combine_kernel, tile0=tile0, nc=nc, tpb=tpb),
        grid=(ntiles,),
        in_specs=[pl.BlockSpec((TM, LANES), lambda i: (i + tile0, 0)),
                  pl.BlockSpec((TM, D_MODEL), lambda i: (i + tile0, 0)),
                  pl.BlockSpec((8, D_MODEL), lambda i: (0, 5)),
                  pl.BlockSpec((1, D_MODEL), lambda i: (0, 0))] + [slot_spec(k) for k in range(TOP_K)],
        out_specs=pl.BlockSpec((TM, D_MODEL), lambda i: (i, 0)),
        out_shape=jax.ShapeDtypeStruct((ntiles * TM, D_MODEL), F32),
        compiler_params=_cparams(("parallel",)),
        name="combine",
    )(w, x1, mod, final_g, yt, yt, yt, yt)


def kernel(x_prompt, x_sample, cache_k, cache_v, c, c_ctx, norm1_g, w_mod, b_mod, w_in, q_norm_g, k_norm_g,
           w_fourier, attn_out_g, fourier_out_g, w_out, norm2_g, w_router, b_router, w_gate_up, b_gate_up,
           w_down, b_down, final_g):
    bc, tc, _ = x_prompt.shape
    bl, tl, _ = x_sample.shape
    depth = w_in.shape[0]
    past = cache_k.shape[2]
    assert depth == 1 and tl % TQ == 0 and tl % TM == 0 and (bc * tc) % TM == 0
    n_ctx, n_lat = bc * tc, bl * tl
    n_tok = n_ctx + n_lat
    nc, nl, tpb = n_ctx // TM, n_lat // TM, tl // TM

    xc = x_prompt.reshape(n_ctx, D_MODEL)
    xl = x_sample.reshape(n_lat, D_MODEL)

    c_rows = jnp.concatenate([c_ctx[None, :], c, jnp.zeros((8 - 1 - bl, D_MODEL), F32)], axis=0)
    mod = _modulation(c_rows, w_mod[0], b_mod[0])

    cos_t, sin_t = _rope_tables(tl)
    q, k, v, f, k_new, v_new = _pre(
        xc, xl, mod, norm1_g, w_in[0].astype(BF16), jnp.tile(q_norm_g, (1, N_HEADS)),
        jnp.tile(k_norm_g, (1, N_KV_HEADS)), _head_mean_matrix(), cos_t, sin_t, nc=nc, nl=nl, tpb=tpb)

    kc, vtc = _split_kv_heads(k[:n_ctx].reshape(bc, tc, KV_W), v[:n_ctx].reshape(bc, tc, KV_W))
    attn_c = _attention(q, kc, vtc, batch=bc, t=tc, s=tc, q_row0=0)
    keys = jnp.concatenate([cache_k[:, 0].reshape(bl, past, KV_W).astype(BF16),
                            k[n_ctx:].reshape(bl, tl, KV_W)], axis=1)
    vals = jnp.concatenate([cache_v[:, 0].reshape(bl, past, KV_W).astype(BF16),
                            v[n_ctx:].reshape(bl, tl, KV_W)], axis=1)
    kl, vtl = _split_kv_heads(keys, vals)
    attn_l = _attention(q, kl, vtl, batch=bl, t=tl, s=past + tl, q_row0=n_ctx)

    wf = w_fourier[0].astype(BF16)
    zero = jnp.zeros_like(wf[0])
    wf_pairs = jnp.stack([jnp.block([[wf[0], zero], [zero, wf[1]]]), jnp.block([[wf[2], zero], [zero, wf[3]]])])
    four_c = _fourier(f, wf_pairs, batch=bc, t=tc, row0=0, n1=1)
    four_l = _fourier(f, wf_pairs, batch=bl, t=tl, row0=n_ctx, n1=FFT_N1)

    x1, h2t = _mix(xc, xl, attn_c, attn_l, four_c, four_l, mod, attn_out_g, fourier_out_g,
                  w_out[0].astype(BF16), norm2_g, nc=nc, nl=nl, tpb=tpb)

    wr = jnp.pad(w_router[0], ((0, 0), (0, LANES - N_EXPERTS)))
    wr_hi = wr.astype(BF16)
    wr_lo = (wr - wr_hi.astype(F32)).astype(BF16)
    br = jnp.pad(b_router[0], (0, LANES - N_EXPERTS), constant_values=-1e30).reshape(1, LANES)
    idx, gate_w, counts = _router(h2t, wr_hi, wr_lo, br)
    cnt = counts[0, :N_EXPERTS].astype(I32)
    padded = ((cnt + TG - 1) // TG) * TG
    ends = jnp.cumsum(padded)
    starts = ends - padded
    n_rows = n_tok * TOP_K + N_EXPERTS * TG
    n_tiles = n_rows // TG
    tile_ids = jnp.arange(n_tiles, dtype=I32)
    tile_expert = jnp.minimum(jnp.sum((ends[None, :] // TG <= tile_ids[:, None]).astype(I32), axis=1),
                              N_EXPERTS - 1)
    n_valid = (ends[N_EXPERTS - 1] // TG).reshape(1)
    gs_rows = jnp.broadcast_to(jnp.pad(starts.astype(F32), (0, LANES - N_EXPERTS))[None, :], (8, LANES))
    pos = _positions(idx, gs_rows)
    src = _sources(cnt, starts, ends, pos, n_rows)
    fresh_tile = jnp.concatenate([jnp.ones((1,), I32), (tile_expert[1:] != tile_expert[:-1]).astype(I32)])
    parity = (jnp.cumsum(fresh_tile) - 1) % 2
    experts = jnp.arange(N_EXPERTS, dtype=I32)
    later = jnp.logical_and(padded[None, :] > 0, experts[None, :] > experts[:, None])
    next_of = jnp.min(jnp.where(later, experts[None, :], N_EXPERTS), axis=1)
    next_of = jnp.where(next_of == N_EXPERTS, -1, next_of)
    next_expert = jnp.sum(jnp.where(tile_expert[:, None] == experts[None, :], next_of[None, :], 0), axis=1)

    b_gu = b_gate_up[0].reshape(N_EXPERTS, GL_CHUNKS, LANES, 2)
    b_gl = jnp.concatenate([b_gu[..., 0], b_gu[..., 1]], axis=-1).reshape(N_EXPERTS, GL_CHUNKS, 1, 2 * LANES)
    yt = _moe_ffn(tile_expert, n_valid, parity.astype(I32), next_expert.astype(I32), src, h2t, w_gate_up[0],
                  b_gl, w_down[0], b_down[0].reshape(N_EXPERTS, DN_CHUNKS, 1, 2 * LANES),
                  _deinterleave_matrix(), n_tok=n_tok)

    y_c = _combine(gate_w, x1, mod, final_g.reshape(1, -1), yt, tile0=0, ntiles=nc, nc=nc, tpb=tpb, n_tok=n_tok)
    y_l = _combine(gate_w, x1, mod, final_g.reshape(1, -1), yt, tile0=nc, ntiles=nl, nc=nc, tpb=tpb, n_tok=n_tok)

    new_k = k_new[:n_ctx].reshape(bc, 1, tc, N_KV_HEADS, HEAD_DIM)
    new_v = v_new[:n_ctx].reshape(bc, 1, tc, N_KV_HEADS, HEAD_DIM)
    return (y_c.reshape(bc, tc, D_MODEL), y_l.reshape(bl, tl, D_MODEL), new_k, new_v)
```

```python
import functools
import math

import numpy as np
import jax
import jax.numpy as jnp
from jax import lax
from jax.experimental import pallas as pl
from jax.experimental.pallas import tpu as pltpu

F32 = jnp.float32
BF16 = jnp.bfloat16
I32 = jnp.int32

D_MODEL = 1024
HEAD_DIM = 64
N_HEADS = 8
N_KV_HEADS = 2
HEADS_PER_KV = N_HEADS // N_KV_HEADS
ATTN_W = N_HEADS * HEAD_DIM
KV_W = N_KV_HEADS * HEAD_DIM
FOURIER_W = 512
FOURIER_GROUP_W = 128
D_IN = ATTN_W + 2 * KV_W + FOURIER_W
N_EXPERTS = 32
TOP_K = 4
D_EXPERT = 1024
SWIGLU_ALPHA = 1.702
SWIGLU_LIMIT = 7.0
ROPE_THETA = 10000.0
ROT_PAIRS = HEAD_DIM // 4
GRID_W = 64
EPS = 1e-6

LANES = 128
TM = 512
TQ = 512
TG = 256
FFT_N1 = 4
TILE_ROWS = D_MODEL // LANES
VMEM_LIMIT = 56 * 1024 * 1024


def _cparams(sem, vmem=VMEM_LIMIT):
    return pltpu.CompilerParams(dimension_semantics=sem, vmem_limit_bytes=vmem)


def _rope_tables(t_lat):
    pos = np.arange(t_lat)
    row = (pos // GRID_W).astype(np.float64)
    col = (pos % GRID_W).astype(np.float64)
    inv = ROPE_THETA ** (-np.arange(ROT_PAIRS, dtype=np.float64) / ROT_PAIRS)
    lane = np.arange(LANES)
    d = lane % HEAD_DIM
    axis = d // (2 * ROT_PAIRS)
    second = (d // ROT_PAIRS) % 2
    p = d % ROT_PAIRS
    ang = np.where(axis[None, :] == 0, row[:, None], col[:, None]) * inv[p][None, :]
    cos = np.cos(ang)
    sin = np.sin(ang) * np.where(second == 0, -1.0, 1.0)[None, :]
    cos = np.concatenate([np.ones((TM, LANES)), cos], axis=0)
    sin = np.concatenate([np.zeros((TM, LANES)), sin], axis=0)
    return jnp.asarray(cos, F32), jnp.asarray(sin, F32)


def _dft_tables(n1, n2):
    c = np.arange(FOURIER_GROUP_W)
    ang = 2.0 * np.pi * np.outer(c, c) / FOURIER_GROUP_W
    z = np.zeros_like(ang)
    cbd = np.block([[np.cos(ang), z], [z, np.cos(ang)]])
    sbd = np.block([[np.sin(ang), z], [z, np.sin(ang)]])
    cd = np.concatenate([cbd, -sbd], axis=1)
    u = np.arange(n2)
    ang2 = 2.0 * np.pi * np.outer(u, u) / n2
    ct, st = np.cos(ang2), np.sin(ang2)
    nt = max(n1 - 1, 1)
    tw_ang = 2.0 * np.pi * np.outer(np.arange(1, nt + 1), u) / (n1 * n2)
    twc = np.repeat(np.cos(tw_ang)[:, :, None], LANES, axis=2)
    tws = np.repeat(np.sin(tw_ang)[:, :, None], LANES, axis=2)
    return (jnp.asarray(cd, F32), jnp.asarray(ct, F32), jnp.asarray(st, F32),
            jnp.asarray(twc, F32), jnp.asarray(tws, F32))


def _head_mean_matrix():
    h = np.arange(ATTN_W) // HEAD_DIM
    return jnp.asarray((h[:, None] == h[None, :]) / HEAD_DIM, BF16)


def _deinterleave_matrix():
    p = np.zeros((2 * LANES, 2 * LANES))
    m = np.arange(LANES)
    p[2 * m, m] = 1.0
    p[2 * m + 1, LANES + m] = 1.0
    return jnp.asarray(p, BF16)


def _mod_row(i, nc, tpb):
    return jnp.where(i < nc, 0, 1 + (i - nc) // tpb)


def _rms(x):
    return x * lax.rsqrt(jnp.mean(x * x, axis=-1, keepdims=True) + EPS)


def _pack_cols(cols, lane):
    out = jnp.zeros(lane.shape, cols[0].dtype)
    for k, c in enumerate(cols):
        out = jnp.where(lane == k, c, out)
    return out


def _to_tiles(ref, x):
    r = x.shape[0]
    for s in range(TILE_ROWS):
        ref[pl.ds(s, r, stride=TILE_ROWS), :] = x[:, s * LANES:(s + 1) * LANES]


def _from_tiles(ref):
    r = ref.shape[0] // TILE_ROWS
    return jnp.concatenate([ref[pl.ds(s, r, stride=TILE_ROWS), :] for s in range(TILE_ROWS)], axis=1)


def _mod_kernel(c_ref, w_ref, b_ref, o_ref):
    c = c_ref[...]
    s = c * (1.0 / (1.0 + jnp.exp(-c)))
    o_ref[...] = jnp.dot(s.astype(BF16), w_ref[...].astype(BF16), preferred_element_type=F32) + b_ref[...]


def _modulation(c_rows, w_mod, b_mod):
    n = w_mod.shape[1] // D_MODEL
    return pl.pallas_call(
        _mod_kernel,
        grid=(n,),
        in_specs=[pl.BlockSpec((8, D_MODEL), lambda j: (0, 0)),
                  pl.BlockSpec((D_MODEL, D_MODEL), lambda j: (0, j)),
                  pl.BlockSpec((1, D_MODEL), lambda j: (0, j))],
        out_specs=pl.BlockSpec((8, D_MODEL), lambda j: (0, j)),
        out_shape=jax.ShapeDtypeStruct((8, w_mod.shape[1]), F32),
        compiler_params=_cparams(("parallel",)),
        name="modulation",
    )(c_rows, w_mod, b_mod.reshape(1, -1))


def _pre_kernel(xc_ref, xl_ref, sh_ref, sc_ref, g1_ref, win_ref, qg_ref, kg_ref, bd_ref, cos_ref, sin_ref,
                q_ref, k_ref, v_ref, f_ref, kn_ref, vn_ref, *, nc, tpb):
    i = pl.program_id(0)
    r = _mod_row(i, nc, tpb)
    x = jnp.where(i < nc, xc_ref[...], xl_ref[...])
    h = _rms(x) * g1_ref[...]
    h = h * (1.0 + sc_ref[pl.ds(r, 1), :]) + sh_ref[pl.ds(r, 1), :]
    proj = jnp.dot(h.astype(BF16), win_ref[...], preferred_element_type=F32)
    q = proj[:, :ATTN_W]
    k = proj[:, ATTN_W:ATTN_W + KV_W]
    v = proj[:, ATTN_W + KV_W:ATTN_W + 2 * KV_W]
    f_ref[...] = proj[:, ATTN_W + 2 * KV_W:].astype(BF16)
    vn_ref[...] = v
    v_ref[...] = v.astype(BF16)
    q_ms = jnp.dot((q * q).astype(BF16), bd_ref[...], preferred_element_type=F32)
    k_ms = jnp.dot((k * k).astype(BF16), bd_ref[:KV_W, :KV_W], preferred_element_type=F32)
    qn = q * lax.rsqrt(q_ms + EPS) * qg_ref[...]
    kn = k * lax.rsqrt(k_ms + EPS) * kg_ref[...]
    kn_ref[...] = kn
    cos = cos_ref[...]
    sin = sin_ref[...]
    lane = lax.broadcasted_iota(I32, cos.shape, 1)
    first = (lane & ROT_PAIRS) == 0

    def rope(c):
        partner = jnp.where(first, pltpu.roll(c, LANES - ROT_PAIRS, 1), pltpu.roll(c, ROT_PAIRS, 1))
        return c * cos + partner * sin

    k_ref[...] = rope(kn).astype(BF16)
    scale = HEAD_DIM ** -0.5 * math.log2(math.e)
    q_rot = jnp.concatenate([rope(qn[:, j * LANES:(j + 1) * LANES]) for j in range(ATTN_W // LANES)], axis=1)
    q_ref[...] = (q_rot * scale).T.astype(BF16)


def _pre(xc, xl, mod, norm1_g, w_in, q_g, k_g, bd, cos_t, sin_t, *, nc, nl, tpb):
    n = (nc + nl) * TM
    row = lambda i: (i, 0)
    const = lambda i: (0, 0)
    tab = lambda i: (jnp.where(i < nc, 0, 1 + (i - nc) % tpb), 0)
    return pl.pallas_call(
        functools.partial(_pre_kernel, nc=nc, tpb=tpb),
        grid=(nc + nl,),
        in_specs=[pl.BlockSpec((TM, D_MODEL), lambda i: (jnp.minimum(i, nc - 1), 0)),
                  pl.BlockSpec((TM, D_MODEL), lambda i: (jnp.maximum(i - nc, 0), 0)),
                  pl.BlockSpec((8, D_MODEL), lambda i: (0, 0)),
                  pl.BlockSpec((8, D_MODEL), lambda i: (0, 1)),
                  pl.BlockSpec((1, D_MODEL), const),
                  pl.BlockSpec((D_MODEL, D_IN), const),
                  pl.BlockSpec((1, ATTN_W), const),
                  pl.BlockSpec((1, KV_W), const),
                  pl.BlockSpec((ATTN_W, ATTN_W), const),
                  pl.BlockSpec((TM, LANES), tab),
                  pl.BlockSpec((TM, LANES), tab)],
        out_specs=[pl.BlockSpec((ATTN_W, TM), lambda i: (0, i)), pl.BlockSpec((TM, KV_W), row),
                   pl.BlockSpec((TM, KV_W), row),
                   pl.BlockSpec((TM, FOURIER_W), row), pl.BlockSpec((TM, KV_W), row), pl.BlockSpec((TM, KV_W), row)],
        out_shape=[jax.ShapeDtypeStruct((ATTN_W, n), BF16), jax.ShapeDtypeStruct((n, KV_W), BF16),
                   jax.ShapeDtypeStruct((n, KV_W), BF16), jax.ShapeDtypeStruct((n, FOURIER_W), BF16),
                   jax.ShapeDtypeStruct((n, KV_W), F32), jax.ShapeDtypeStruct((n, KV_W), F32)],
        compiler_params=_cparams(("parallel",)),
        name="pre_proj",
    )(xc, xl, mod, mod, norm1_g, w_in, q_g, k_g, bd, cos_t, sin_t)


V_ROWS = 80


def _attn_kernel(qt_ref, k_ref, vt_ref, o_ref, ot_ref, st_ref, pt_ref):
    def scores(h):
        qt = qt_ref[h * HEAD_DIM:(h + 1) * HEAD_DIM, :]
        st_ref[h % 2] = jnp.dot(k_ref[h // HEADS_PER_KV], qt, preferred_element_type=F32)

    scores(0)
    for h in range(N_HEADS):
        if h + 1 < N_HEADS:
            scores(h + 1)
        st = st_ref[h % 2]
        pt_ref[h % 2] = jnp.exp2(st - jnp.max(st, axis=0, keepdims=True)).astype(BF16)
        ot = jnp.dot(vt_ref[h // HEADS_PER_KV], pt_ref[h % 2], preferred_element_type=F32)
        ot_ref[h * HEAD_DIM:(h + 1) * HEAD_DIM, :] = ot[:HEAD_DIM] * (1.0 / ot[HEAD_DIM:HEAD_DIM + 1])
    o_ref[...] = ot_ref[...].T


def _attention(qt, k_heads, vt_ext, *, batch, t, s, q_row0):
    tq = min(TQ, t)
    qpb = t // tq
    q0 = q_row0 // tq
    return pl.pallas_call(
        _attn_kernel,
        grid=(batch, qpb),
        in_specs=[pl.BlockSpec((ATTN_W, tq), lambda b, i: (0, q0 + b * qpb + i)),
                  pl.BlockSpec((None, N_KV_HEADS, s, HEAD_DIM), lambda b, i: (b, 0, 0, 0)),
                  pl.BlockSpec((None, N_KV_HEADS, V_ROWS, s), lambda b, i: (b, 0, 0, 0))],
        out_specs=pl.BlockSpec((tq, ATTN_W), lambda b, i: (b * qpb + i, 0)),
        out_shape=jax.ShapeDtypeStruct((batch * t, ATTN_W), F32),
        scratch_shapes=[pltpu.VMEM((ATTN_W, tq), F32), pltpu.VMEM((2, s, tq), F32), pltpu.VMEM((2, s, tq), BF16)],
        compiler_params=_cparams(("parallel", "parallel")),
        name="attention",
    )(qt, k_heads, vt_ext)


def _split_kv_heads(keys, vals):
    b, s, _ = keys.shape
    k_heads = keys.reshape(b, s, N_KV_HEADS, HEAD_DIM).transpose(0, 2, 1, 3)
    vt = vals.reshape(b, s, N_KV_HEADS, HEAD_DIM).transpose(0, 2, 3, 1)
    extra = jnp.concatenate([jnp.ones((b, N_KV_HEADS, 1, s), BF16),
                             jnp.zeros((b, N_KV_HEADS, V_ROWS - HEAD_DIM - 1, s), BF16)], axis=2)
    return k_heads, jnp.concatenate([vt, extra], axis=2)


def _fourier_kernel(x_ref, cd_ref, ct_ref, st_ref, twc_ref, tws_ref, wf_ref, o_ref, *, n1, n2):
    pw = 2 * FOURIER_GROUP_W
    scale = 1.0 / math.sqrt(n1 * n2 * FOURIER_GROUP_W)
    for gp in range(FOURIER_W // pw):
        z = jnp.dot(x_ref[:, gp * pw:(gp + 1) * pw], cd_ref[...], preferred_element_type=F32)
        zr = [z[t1 * n2:(t1 + 1) * n2, :pw] for t1 in range(n1)]
        zi = [z[t1 * n2:(t1 + 1) * n2, pw:] for t1 in range(n1)]
        if n1 == 1:
            a = [(zr[0], zi[0])]
        else:
            a = [(zr[0] + zr[1] + zr[2] + zr[3], zi[0] + zi[1] + zi[2] + zi[3]),
                 (zr[0] + zi[1] - zr[2] - zi[3], zi[0] - zr[1] - zi[2] + zr[3]),
                 (zr[0] - zr[1] + zr[2] - zr[3], zi[0] - zi[1] + zi[2] - zi[3]),
                 (zr[0] - zi[1] - zr[2] + zi[3], zi[0] + zr[1] - zi[2] - zr[3])]
        for u1 in range(n1):
            ar, ai = a[u1]
            if u1 > 0:
                c = jnp.concatenate([twc_ref[u1 - 1]] * 2, axis=1)
                s = jnp.concatenate([tws_ref[u1 - 1]] * 2, axis=1)
                ar, ai = ar * c + ai * s, ai * c - ar * s
            y = (jnp.dot(ct_ref[...], ar.astype(BF16), preferred_element_type=F32)
                 + jnp.dot(st_ref[...], ai.astype(BF16), preferred_element_type=F32)) * scale
            y = jnp.dot(y.astype(BF16), wf_ref[gp], preferred_element_type=F32)
            rows = pl.ds(u1, n2, stride=n1) if n1 > 1 else slice(None)
            for half in range(2):
                o_ref[2 * gp + half, rows, :] = y[:, half * FOURIER_GROUP_W:(half + 1) * FOURIER_GROUP_W]


def _fourier(f, wf_pairs, *, batch, t, row0, n1):
    n2 = t // n1
    cd, ct, st, twc, tws = _dft_tables(n1, n2)
    cd, ct, st = cd.astype(BF16), ct.astype(BF16), st.astype(BF16)
    nt = twc.shape[0]
    b0 = row0 // t
    c2 = lambda b: (0, 0)
    c3 = lambda b: (0, 0, 0)
    return pl.pallas_call(
        functools.partial(_fourier_kernel, n1=n1, n2=n2),
        grid=(batch,),
        in_specs=[pl.BlockSpec((t, FOURIER_W), lambda b: (b0 + b, 0)),
                  pl.BlockSpec(cd.shape, c2), pl.BlockSpec((n2, n2), c2), pl.BlockSpec((n2, n2), c2),
                  pl.BlockSpec((nt, n2, LANES), c3), pl.BlockSpec((nt, n2, LANES), c3),
                  pl.BlockSpec(wf_pairs.shape, c3)],
        out_specs=pl.BlockSpec((FOURIER_W // FOURIER_GROUP_W, t, FOURIER_GROUP_W), lambda b: (0, b, 0)),
        out_shape=jax.ShapeDtypeStruct((FOURIER_W // FOURIER_GROUP_W, batch * t, FOURIER_GROUP_W), F32),
        compiler_params=_cparams(("parallel",)),
        name="fourier",
    )(f, cd, ct, st, twc, tws, wf_pairs)


def _mix_kernel(xc_ref, xl_ref, ac_ref, al_ref, fc_ref, fl_ref, g1_ref, sh2_ref, sc2_ref, ag_ref, fg_ref,
                wout_ref, n2g_ref, x1_ref, h2_ref, *, nc, tpb):
    i = pl.program_id(0)
    r = _mod_row(i, nc, tpb)
    is_ctx = i < nc
    x = jnp.where(is_ctx, xc_ref[...], xl_ref[...])
    a = _rms(jnp.where(is_ctx, ac_ref[...], al_ref[...])) * ag_ref[...]
    n_groups = FOURIER_W // FOURIER_GROUP_W
    f = jnp.concatenate([jnp.where(is_ctx, fc_ref[g], fl_ref[g]) for g in range(n_groups)], axis=1)
    f = _rms(f) * fg_ref[...]
    mixed = (jnp.dot(a.astype(BF16), wout_ref[:ATTN_W, :], preferred_element_type=F32)
             + jnp.dot(f.astype(BF16), wout_ref[ATTN_W:, :], preferred_element_type=F32))
    x1 = x + g1_ref[pl.ds(r, 1), :] * mixed
    x1_ref[...] = x1
    h2 = _rms(x1) * n2g_ref[...]
    _to_tiles(h2_ref, h2 * (1.0 + sc2_ref[pl.ds(r, 1), :]) + sh2_ref[pl.ds(r, 1), :])


def _mix(xc, xl, ac, al, fc, fl, mod, attn_g, four_g, w_out, norm2_g, *, nc, nl, tpb):
    n = (nc + nl) * TM
    lo = lambda i: (jnp.minimum(i, nc - 1), 0)
    hi = lambda i: (jnp.maximum(i - nc, 0), 0)
    const = lambda i: (0, 0)
    row = lambda i: (i, 0)
    return pl.pallas_call(
        functools.partial(_mix_kernel, nc=nc, tpb=tpb),
        grid=(nc + nl,),
        in_specs=[pl.BlockSpec((TM, D_MODEL), lo), pl.BlockSpec((TM, D_MODEL), hi),
                  pl.BlockSpec((TM, ATTN_W), lo), pl.BlockSpec((TM, ATTN_W), hi),
                  pl.BlockSpec((FOURIER_W // FOURIER_GROUP_W, TM, FOURIER_GROUP_W),
                               lambda i: (0, jnp.minimum(i, nc - 1), 0)),
                  pl.BlockSpec((FOURIER_W // FOURIER_GROUP_W, TM, FOURIER_GROUP_W),
                               lambda i: (0, jnp.maximum(i - nc, 0), 0)),
                  pl.BlockSpec((8, D_MODEL), lambda i: (0, 2)),
                  pl.BlockSpec((8, D_MODEL), lambda i: (0, 3)),
                  pl.BlockSpec((8, D_MODEL), lambda i: (0, 4)),
                  pl.BlockSpec((1, ATTN_W), const), pl.BlockSpec((1, FOURIER_W), const),
                  pl.BlockSpec((D_MODEL, D_MODEL), const), pl.BlockSpec((1, D_MODEL), const)],
        out_specs=[pl.BlockSpec((TM, D_MODEL), row), pl.BlockSpec((TM * TILE_ROWS, LANES), row)],
        out_shape=[jax.ShapeDtypeStruct((n, D_MODEL), F32), jax.ShapeDtypeStruct((n * TILE_ROWS, LANES), F32)],
        compiler_params=_cparams(("parallel",)),
        name="mix_proj",
    )(xc, xl, ac, al, fc, fl, mod, mod, mod, attn_g, four_g, w_out, norm2_g)


def _router_kernel(h2_ref, whi_ref, wlo_ref, br_ref, idx_ref, w_ref, cnt_ref):
    i = pl.program_id(0)
    h = _from_tiles(h2_ref)
    h_hi = h.astype(BF16)
    h_lo = (h - h_hi.astype(F32)).astype(BF16)
    logits = (jnp.dot(h_hi, whi_ref[...], preferred_element_type=F32)
              + jnp.dot(h_lo, whi_ref[...], preferred_element_type=F32)
              + jnp.dot(h_hi, wlo_ref[...], preferred_element_type=F32)) + br_ref[...]
    lane = lax.broadcasted_iota(I32, logits.shape, 1)
    l = logits
    vals, idxs = [], []
    hits = jnp.zeros(logits.shape, F32)
    for _ in range(TOP_K):
        m = jnp.max(l, axis=-1, keepdims=True)
        ix = jnp.min(jnp.where(l == m, lane, LANES), axis=-1, keepdims=True)
        sel = lane == ix
        hits = hits + sel.astype(F32)
        l = jnp.where(sel, -jnp.inf, l)
        vals.append(m)
        idxs.append(ix)
    es = [jnp.exp(v - vals[0]) for v in vals]
    den = es[0] + es[1] + es[2] + es[3]
    idx_ref[...] = _pack_cols(idxs, lane)
    w_ref[...] = _pack_cols([e / den for e in es], lane)

    @pl.when(i == 0)
    def _():
        cnt_ref[...] = jnp.zeros(cnt_ref.shape, F32)

    cnt_ref[...] += jnp.sum(hits, axis=0, keepdims=True)


def _router(h2t, w_hi, w_lo, br):
    n = h2t.shape[0] // TILE_ROWS
    row = lambda i: (i, 0)
    const = lambda i: (0, 0)
    return pl.pallas_call(
        _router_kernel,
        grid=(n // TM,),
        in_specs=[pl.BlockSpec((TM * TILE_ROWS, LANES), row), pl.BlockSpec((D_MODEL, LANES), const),
                  pl.BlockSpec((D_MODEL, LANES), const), pl.BlockSpec((1, LANES), const)],
        out_specs=[pl.BlockSpec((TM, LANES), row), pl.BlockSpec((TM, LANES), row), pl.BlockSpec((8, LANES), const)],
        out_shape=[jax.ShapeDtypeStruct((n, LANES), I32), jax.ShapeDtypeStruct((n, LANES), F32),
                   jax.ShapeDtypeStruct((8, LANES), F32)],
        compiler_params=_cparams(("arbitrary",)),
        name="router",
    )(h2t, w_hi, w_lo, br)


def _pos_kernel(idx_ref, gs_ref, pos_ref, carry_ref):
    i = pl.program_id(0)

    @pl.when(i == 0)
    def _():
        carry_ref[...] = jnp.zeros(carry_ref.shape, F32)

    idx = idx_ref[...]
    lane = lax.broadcasted_iota(I32, idx.shape, 1)
    sels = [lane == idx[:, k:k + 1] for k in range(TOP_K)]
    hits = jnp.zeros(idx.shape, F32)
    for s in sels:
        hits = hits + s.astype(F32)
    r = lax.broadcasted_iota(I32, (TM, TM), 0)
    c = lax.broadcasted_iota(I32, (TM, TM), 1)
    before = (c < r).astype(BF16)
    rank = jnp.dot(before, hits.astype(BF16), preferred_element_type=F32) + carry_ref[0:1, :] + gs_ref[0:1, :]
    cols = [jnp.sum(jnp.where(s, rank, 0.0), axis=-1, keepdims=True) for s in sels]
    pos_ref[...] = _pack_cols(cols, lane).T[:8, :].astype(I32)
    carry_ref[...] += jnp.sum(hits, axis=0, keepdims=True)


def _positions(idx, gs_rows):
    n = idx.shape[0]
    return pl.pallas_call(
        _pos_kernel,
        grid=(n // TM,),
        in_specs=[pl.BlockSpec((TM, LANES), lambda i: (i, 0)), pl.BlockSpec((8, LANES), lambda i: (0, 0))],
        out_specs=pl.BlockSpec((8, TM), lambda i: (0, i)),
        out_shape=jax.ShapeDtypeStruct((8, n), I32),
        scratch_shapes=[pltpu.VMEM((8, LANES), F32)],
        compiler_params=_cparams(("arbitrary",)),
        name="positions",
    )(idx, gs_rows)


def _sources_kernel(cnt_ref, starts_ref, ends_ref, p0_ref, p1_ref, p2_ref, p3_ref, src_ref, *, n_tok):
    i = pl.program_id(0)
    pad_base = n_tok * TOP_K

    def pad(p, carry):
        src_ref[p] = pad_base + (p & (TG - 1))
        return carry

    @pl.when(i == 0)
    def _():
        for e in range(N_EXPERTS):
            lax.fori_loop(starts_ref[e] + cnt_ref[e], ends_ref[e], pad, 0)
        lax.fori_loop(ends_ref[N_EXPERTS - 1], src_ref.shape[0], pad, 0)

    batch = 8

    def place(tb, carry):
        t0 = tb * batch
        code0 = (i * TM + t0) * TOP_K
        for u in range(batch):
            for k, p_ref in enumerate((p0_ref, p1_ref, p2_ref, p3_ref)):
                src_ref[p_ref[t0 + u]] = code0 + (u * TOP_K + k)
        return carry

    lax.fori_loop(0, TM // batch, place, 0)


def _sources(cnt, starts, ends, pos, n_rows):
    n_tok = pos.shape[1]
    nt = n_tok // TM
    pos_flat = pos[:TOP_K].reshape(-1)
    slot_spec = lambda k: pl.BlockSpec((TM,), lambda i, *_: (k * nt + i,), memory_space=pltpu.SMEM)
    grid_spec = pltpu.PrefetchScalarGridSpec(
        num_scalar_prefetch=3,
        grid=(nt,),
        in_specs=[slot_spec(k) for k in range(TOP_K)],
        out_specs=pl.BlockSpec(memory_space=pltpu.SMEM),
    )
    return pl.pallas_call(
        functools.partial(_sources_kernel, n_tok=n_tok),
        grid_spec=grid_spec,
        out_shape=jax.ShapeDtypeStruct((n_rows,), I32),
        compiler_params=_cparams(("arbitrary",)),
        name="sources",
    )(cnt, starts, ends, pos_flat, pos_flat, pos_flat, pos_flat)


def _moe_kernel(te_ref, nv_ref, par_ref, nxt_ref, prev_ref, cur_ref, nxtsrc_ref, bg_ref, bl_ref, bdn_ref, perm_ref,
                h2t_ref, wgu_hbm, wdn_hbm, yt_ref,
                xflat, yflat, wgu_buf, wdn_buf, wg_s, wl_s, wd_s, gsem, ssem, wsem, *, n_tok):
    i = pl.program_id(0)
    e = te_ref[i]
    valid = i < nv_ref[0]
    fresh = jnp.logical_or(i == 0, e != te_ref[jnp.maximum(i - 1, 0)])
    slot = i % 2
    wslot = par_ref[i]
    pad_base = n_tok * TOP_K
    tile_rows = TG * TILE_ROWS

    def weight_copies(expert, s):
        return (pltpu.make_async_copy(wgu_hbm.at[expert], wgu_buf.at[s], wsem.at[s]),
                pltpu.make_async_copy(wdn_hbm.at[expert], wdn_buf.at[s], wsem.at[s]))

    def gather_row(code, r, s):
        tok = jnp.where(code >= pad_base, 0, code >> 2)
        return pltpu.make_async_copy(h2t_ref.at[pl.ds(pl.multiple_of(tok * TILE_ROWS, TILE_ROWS), TILE_ROWS), :],
                                     xflat.at[s, pl.ds(r * TILE_ROWS, TILE_ROWS), :], gsem.at[s])

    def gather_all(s):
        return pltpu.make_async_copy(h2t_ref.at[pl.ds(0, tile_rows), :], xflat.at[s], gsem.at[s])

    def scatter_dst(code):
        return jnp.where(code >= pad_base, code, (code & 3) * n_tok + (code >> 2))

    def scatter_row(dst, r, s):
        return pltpu.make_async_copy(yflat.at[s, pl.ds(r * TILE_ROWS, TILE_ROWS), :],
                                     yt_ref.at[pl.ds(pl.multiple_of(dst * TILE_ROWS, TILE_ROWS), TILE_ROWS), :],
                                     ssem.at[s])

    def scatter_all(s):
        return pltpu.make_async_copy(yflat.at[s], yt_ref.at[pl.ds(pad_base * TILE_ROWS, tile_rows), :], ssem.at[s])

    @pl.when(i == 0)
    def _():
        for cp in weight_copies(e, wslot):
            cp.start()
        yflat[...] = jnp.zeros(yflat.shape, F32)
        scatter_all(0).start()
        for r in range(TG):
            gather_row(cur_ref[r], r, 0).start()

    @pl.when(jnp.logical_and(valid, fresh))
    def _():
        for cp in weight_copies(e, wslot):
            cp.wait()
        for j in range(2 * D_EXPERT // (2 * LANES)):
            blk = wgu_buf[wslot, :, j * 2 * LANES:(j + 1) * 2 * LANES].astype(BF16)
            d = jnp.dot(blk, perm_ref[...], preferred_element_type=F32)
            wg_s[:, j * LANES:(j + 1) * LANES] = d[:, :LANES].astype(BF16)
            wl_s[:, j * LANES:(j + 1) * LANES] = d[:, LANES:].astype(BF16)
        wd_s[...] = wdn_buf[wslot].astype(BF16)

        @pl.when(nxt_ref[i] >= 0)
        def _():
            for cp in weight_copies(nxt_ref[i], 1 - wslot):
                cp.start()

    @pl.when(valid)
    def _():
        gather_all(slot).wait()
        scatter_all(slot).wait()
        for r in range(TG):
            gather_row(nxtsrc_ref[r], r, 1 - slot).start()

    @pl.when(valid)
    def _():
        x = _from_tiles(xflat.at[slot]).astype(BF16)
        for r in range(TG):
            scatter_row(scatter_dst(prev_ref[r]), r, 1 - slot).start()
        g = jnp.dot(x, wg_s[...], preferred_element_type=F32) + bg_ref[...]
        l = jnp.dot(x, wl_s[...], preferred_element_type=F32) + bl_ref[...]
        g = jnp.minimum(g, SWIGLU_LIMIT)
        l = jnp.clip(l, -SWIGLU_LIMIT, SWIGLU_LIMIT)
        a = (l + 1.0) * (g * (1.0 / (1.0 + jnp.exp(-SWIGLU_ALPHA * g))))
        y = jnp.dot(a.astype(BF16), wd_s[...], preferred_element_type=F32) + bdn_ref[...]
        _to_tiles(yflat.at[slot], y)

    @pl.when(i == nv_ref[0] - 1)
    def _():
        scatter_all(1 - slot).wait()
        for r in range(TG):
            scatter_row(scatter_dst(cur_ref[r]), r, slot).start()
        scatter_all(slot).wait()
        gather_all(1 - slot).wait()


def _moe_ffn(tile_expert, n_valid, parity, next_expert, src, h2t, w_gu, b_g, b_l, w_dn, b_dn, perm, *, n_tok):
    n_tiles = src.shape[0] // TG
    ex3 = lambda i, te, *_: (te[i], 0, 0)
    grid_spec = pltpu.PrefetchScalarGridSpec(
        num_scalar_prefetch=4,
        grid=(n_tiles,),
        in_specs=[pl.BlockSpec((TG,), lambda i, *_: (jnp.maximum(i - 1, 0),), memory_space=pltpu.SMEM),
                  pl.BlockSpec((TG,), lambda i, *_: (i,), memory_space=pltpu.SMEM),
                  pl.BlockSpec((TG,), lambda i, *_: (jnp.minimum(i + 1, n_tiles - 1),), memory_space=pltpu.SMEM),
                  pl.BlockSpec((None, 1, D_EXPERT), ex3),
                  pl.BlockSpec((None, 1, D_EXPERT), ex3),
                  pl.BlockSpec((None, 1, D_MODEL), ex3),
                  pl.BlockSpec((2 * LANES, 2 * LANES), lambda i, *_: (0, 0)),
                  pl.BlockSpec(memory_space=pl.ANY),
                  pl.BlockSpec(memory_space=pl.ANY),
                  pl.BlockSpec(memory_space=pl.ANY)],
        out_specs=pl.BlockSpec(memory_space=pl.ANY),
        scratch_shapes=[pltpu.VMEM((2, TG * TILE_ROWS, LANES), F32), pltpu.VMEM((2, TG * TILE_ROWS, LANES), F32),
                        pltpu.VMEM((2, D_MODEL, 2 * D_EXPERT), F32), pltpu.VMEM((2, D_EXPERT, D_MODEL), F32),
                        pltpu.VMEM((D_MODEL, D_EXPERT), BF16), pltpu.VMEM((D_MODEL, D_EXPERT), BF16),
                        pltpu.VMEM((D_EXPERT, D_MODEL), BF16),
                        pltpu.SemaphoreType.DMA((2,)), pltpu.SemaphoreType.DMA((2,)), pltpu.SemaphoreType.DMA((2,))],
    )
    return pl.pallas_call(
        functools.partial(_moe_kernel, n_tok=n_tok),
        grid_spec=grid_spec,
        out_shape=jax.ShapeDtypeStruct(((n_tok * TOP_K + TG) * TILE_ROWS, LANES), F32),
        compiler_params=_cparams(("arbitrary",)),
        name="moe_ffn",
    )(tile_expert, n_valid, parity, next_expert, src, src, src, b_g, b_l, b_dn, perm, h2t, w_gu, w_dn)


def _combine_kernel(w_ref, x1_ref, g2_ref, fg_ref, y0_ref, y1_ref, y2_ref, y3_ref, o_ref, *, tile0, nc, tpb):
    i = pl.program_id(0) + tile0
    r = _mod_row(i, nc, tpb)
    w = w_ref[...]
    moe = w[:, 0:1] * _from_tiles(y0_ref)
    for k, y_ref in ((1, y1_ref), (2, y2_ref), (3, y3_ref)):
        moe = moe + w[:, k:k + 1] * _from_tiles(y_ref)
    y = x1_ref[...] + g2_ref[pl.ds(r, 1), :] * moe
    o_ref[...] = _rms(y) * fg_ref[...]


def _combine(w, x1, mod, final_g, yt, *, tile0, ntiles, nc, tpb, n_tok):
    slot_tiles = n_tok // TM
    slot_spec = lambda k: pl.BlockSpec((TM * TILE_ROWS, LANES), lambda i: (k * slot_tiles + tile0 + i, 0))
    return pl.pallas_call(
        functools.partial(_combine_kernel, tile0=tile0, nc=nc, tpb=tpb),
        grid=(ntiles,),
        in_specs=[pl.BlockSpec((TM, LANES), lambda i: (i + tile0, 0)),
                  pl.BlockSpec((TM, D_MODEL), lambda i: (i + tile0, 0)),
                  pl.BlockSpec((8, D_MODEL), lambda i: (0, 5)),
                  pl.BlockSpec((1, D_MODEL), lambda i: (0, 0))] + [slot_spec(k) for k in range(TOP_K)],
        out_specs=pl.BlockSpec((TM, D_MODEL), lambda i: (i, 0)),
        out_shape=jax.ShapeDtypeStruct((ntiles * TM, D_MODEL), F32),
        compiler_params=_cparams(("parallel",)),
        name="combine",
    )(w, x1, mod, final_g, yt, yt, yt, yt)


def kernel(x_prompt, x_sample, cache_k, cache_v, c, c_ctx, norm1_g, w_mod, b_mod, w_in, q_norm_g, k_norm_g,
           w_fourier, attn_out_g, fourier_out_g, w_out, norm2_g, w_router, b_router, w_gate_up, b_gate_up,
           w_down, b_down, final_g):
    bc, tc, _ = x_prompt.shape
    bl, tl, _ = x_sample.shape
    depth = w_in.shape[0]
    past = cache_k.shape[2]
    assert depth == 1 and tl % TQ == 0 and tl % TM == 0 and (bc * tc) % TM == 0
    n_ctx, n_lat = bc * tc, bl * tl
    n_tok = n_ctx + n_lat
    nc, nl, tpb = n_ctx // TM, n_lat // TM, tl // TM

    xc = x_prompt.reshape(n_ctx, D_MODEL)
    xl = x_sample.reshape(n_lat, D_MODEL)

    c_rows = jnp.concatenate([c_ctx[None, :], c, jnp.zeros((8 - 1 - bl, D_MODEL), F32)], axis=0)
    mod = _modulation(c_rows, w_mod[0], b_mod[0])

    cos_t, sin_t = _rope_tables(tl)
    q, k, v, f, k_new, v_new = _pre(
        xc, xl, mod, norm1_g, w_in[0].astype(BF16), jnp.tile(q_norm_g, (1, N_HEADS)),
        jnp.tile(k_norm_g, (1, N_KV_HEADS)), _head_mean_matrix(), cos_t, sin_t, nc=nc, nl=nl, tpb=tpb)

    kc, vtc = _split_kv_heads(k[:n_ctx].reshape(bc, tc, KV_W), v[:n_ctx].reshape(bc, tc, KV_W))
    attn_c = _attention(q, kc, vtc, batch=bc, t=tc, s=tc, q_row0=0)
    keys = jnp.concatenate([cache_k[:, 0].reshape(bl, past, KV_W).astype(BF16),
                            k[n_ctx:].reshape(bl, tl, KV_W)], axis=1)
    vals = jnp.concatenate([cache_v[:, 0].reshape(bl, past, KV_W).astype(BF16),
                            v[n_ctx:].reshape(bl, tl, KV_W)], axis=1)
    kl, vtl = _split_kv_heads(keys, vals)
    attn_l = _attention(q, kl, vtl, batch=bl, t=tl, s=past + tl, q_row0=n_ctx)

    wf = w_fourier[0].astype(BF16)
    zero = jnp.zeros_like(wf[0])
    wf_pairs = jnp.stack([jnp.block([[wf[0], zero], [zero, wf[1]]]), jnp.block([[wf[2], zero], [zero, wf[3]]])])
    four_c = _fourier(f, wf_pairs, batch=bc, t=tc, row0=0, n1=1)
    four_l = _fourier(f, wf_pairs, batch=bl, t=tl, row0=n_ctx, n1=FFT_N1)

    x1, h2t = _mix(xc, xl, attn_c, attn_l, four_c, four_l, mod, attn_out_g, fourier_out_g,
                  w_out[0].astype(BF16), norm2_g, nc=nc, nl=nl, tpb=tpb)

    wr = jnp.pad(w_router[0], ((0, 0), (0, LANES - N_EXPERTS)))
    wr_hi = wr.astype(BF16)
    wr_lo = (wr - wr_hi.astype(F32)).astype(BF16)
    br = jnp.pad(b_router[0], (0, LANES - N_EXPERTS), constant_values=-1e30).reshape(1, LANES)
    idx, gate_w, counts = _router(h2t, wr_hi, wr_lo, br)
    cnt = counts[0, :N_EXPERTS].astype(I32)
    padded = ((cnt + TG - 1) // TG) * TG
    ends = jnp.cumsum(padded)
    starts = ends - padded
    n_rows = n_tok * TOP_K + N_EXPERTS * TG
    n_tiles = n_rows // TG
    tile_ids = jnp.arange(n_tiles, dtype=I32)
    tile_expert = jnp.minimum(jnp.sum((ends[None, :] // TG <= tile_ids[:, None]).astype(I32), axis=1),
                              N_EXPERTS - 1)
    n_valid = (ends[N_EXPERTS - 1] // TG).reshape(1)
    gs_rows = jnp.broadcast_to(jnp.pad(starts.astype(F32), (0, LANES - N_EXPERTS))[None, :], (8, LANES))
    pos = _positions(idx, gs_rows)
    src = _sources(cnt, starts, ends, pos, n_rows)
    fresh_tile = jnp.concatenate([jnp.ones((1,), I32), (tile_expert[1:] != tile_expert[:-1]).astype(I32)])
    parity = (jnp.cumsum(fresh_tile) - 1) % 2
    experts = jnp.arange(N_EXPERTS, dtype=I32)
    later = jnp.logical_and(padded[None, :] > 0, experts[None, :] > experts[:, None])
    next_of = jnp.min(jnp.where(later, experts[None, :], N_EXPERTS), axis=1)
    next_of = jnp.where(next_of == N_EXPERTS, -1, next_of)
    next_expert = jnp.sum(jnp.where(tile_expert[:, None] == experts[None, :], next_of[None, :], 0), axis=1)

    b_gu = b_gate_up[0].reshape(N_EXPERTS, 1, D_EXPERT, 2)
    yt = _moe_ffn(tile_expert, n_valid, parity.astype(I32), next_expert.astype(I32), src, h2t, w_gate_up[0],
                  b_gu[..., 0], b_gu[..., 1], w_down[0], b_down[0].reshape(N_EXPERTS, 1, D_MODEL),
                  _deinterleave_matrix(), n_tok=n_tok)

    y_c = _combine(gate_w, x1, mod, final_g.reshape(1, -1), yt, tile0=0, ntiles=nc, nc=nc, tpb=tpb, n_tok=n_tok)
    y_l = _combine(gate_w, x1, mod, final_g.reshape(1, -1), yt, tile0=nc, ntiles=nl, nc=nc, tpb=tpb, n_tok=n_tok)

    new_k = k_new[:n_ctx].reshape(bc, 1, tc, N_KV_HEADS, HEAD_DIM)
    new_v = v_new[:n_ctx].reshape(bc, 1, tc, N_KV_HEADS, HEAD_DIM)
    return (y_c.reshape(bc, tc, D_MODEL), y_l.reshape(bl, tl, D_MODEL), new_k, new_v)
```

```python
import functools
import math

import numpy as np
import jax
import jax.numpy as jnp
from jax import lax
from jax.experimental import pallas as pl
from jax.experimental.pallas import tpu as pltpu

F32 = jnp.float32
BF16 = jnp.bfloat16
I32 = jnp.int32

D_MODEL = 1024
HEAD_DIM = 64
N_HEADS = 8
N_KV_HEADS = 2
HEADS_PER_KV = N_HEADS // N_KV_HEADS
ATTN_W = N_HEADS * HEAD_DIM
KV_W = N_KV_HEADS * HEAD_DIM
FOURIER_W = 512
FOURIER_GROUP_W = 128
D_IN = ATTN_W + 2 * KV_W + FOURIER_W
N_EXPERTS = 32
TOP_K = 4
D_EXPERT = 1024
SWIGLU_ALPHA = 1.702
SWIGLU_LIMIT = 7.0
ROPE_THETA = 10000.0
ROT_PAIRS = HEAD_DIM // 4
GRID_W = 64
EPS = 1e-6

LANES = 128
TM = 512
TQ = 512
TG = 256
FFT_N1 = 4
VMEM_LIMIT = 56 * 1024 * 1024


def _cparams(sem, vmem=VMEM_LIMIT):
    return pltpu.CompilerParams(dimension_semantics=sem, vmem_limit_bytes=vmem)


def _rope_tables(t_lat):
    pos = np.arange(t_lat)
    row = (pos // GRID_W).astype(np.float64)
    col = (pos % GRID_W).astype(np.float64)
    inv = ROPE_THETA ** (-np.arange(ROT_PAIRS, dtype=np.float64) / ROT_PAIRS)
    lane = np.arange(LANES)
    d = lane % HEAD_DIM
    axis = d // (2 * ROT_PAIRS)
    second = (d // ROT_PAIRS) % 2
    p = d % ROT_PAIRS
    ang = np.where(axis[None, :] == 0, row[:, None], col[:, None]) * inv[p][None, :]
    cos = np.cos(ang)
    sin = np.sin(ang) * np.where(second == 0, -1.0, 1.0)[None, :]
    cos = np.concatenate([np.ones((TM, LANES)), cos], axis=0)
    sin = np.concatenate([np.zeros((TM, LANES)), sin], axis=0)
    return jnp.asarray(cos, F32), jnp.asarray(sin, F32)


def _dft_tables(n1, n2):
    c = np.arange(FOURIER_GROUP_W)
    ang = 2.0 * np.pi * np.outer(c, c) / FOURIER_GROUP_W
    z = np.zeros_like(ang)
    cbd = np.block([[np.cos(ang), z], [z, np.cos(ang)]])
    sbd = np.block([[np.sin(ang), z], [z, np.sin(ang)]])
    cd = np.concatenate([cbd, -sbd], axis=1)
    u = np.arange(n2)
    ang2 = 2.0 * np.pi * np.outer(u, u) / n2
    ct, st = np.cos(ang2), np.sin(ang2)
    nt = max(n1 - 1, 1)
    tw_ang = 2.0 * np.pi * np.outer(np.arange(1, nt + 1), u) / (n1 * n2)
    twc = np.repeat(np.cos(tw_ang)[:, :, None], LANES, axis=2)
    tws = np.repeat(np.sin(tw_ang)[:, :, None], LANES, axis=2)
    return (jnp.asarray(cd, F32), jnp.asarray(ct, F32), jnp.asarray(st, F32),
            jnp.asarray(twc, F32), jnp.asarray(tws, F32))


def _head_mean_matrix():
    h = np.arange(ATTN_W) // HEAD_DIM
    return jnp.asarray((h[:, None] == h[None, :]) / HEAD_DIM, BF16)


def _deinterleave_matrix():
    p = np.zeros((2 * LANES, 2 * LANES))
    m = np.arange(LANES)
    p[2 * m, m] = 1.0
    p[2 * m + 1, LANES + m] = 1.0
    return jnp.asarray(p, BF16)


def _mod_row(i, nc, tpb):
    return jnp.where(i < nc, 0, 1 + (i - nc) // tpb)


def _rms(x):
    return x * lax.rsqrt(jnp.mean(x * x, axis=-1, keepdims=True) + EPS)


def _pack_cols(cols, lane):
    out = jnp.zeros(lane.shape, cols[0].dtype)
    for k, c in enumerate(cols):
        out = jnp.where(lane == k, c, out)
    return out


def _mod_kernel(c_ref, w_ref, b_ref, o_ref):
    c = c_ref[...]
    s = c * (1.0 / (1.0 + jnp.exp(-c)))
    o_ref[...] = jnp.dot(s.astype(BF16), w_ref[...].astype(BF16), preferred_element_type=F32) + b_ref[...]


def _modulation(c_rows, w_mod, b_mod):
    n = w_mod.shape[1] // D_MODEL
    return pl.pallas_call(
        _mod_kernel,
        grid=(n,),
        in_specs=[pl.BlockSpec((8, D_MODEL), lambda j: (0, 0)),
                  pl.BlockSpec((D_MODEL, D_MODEL), lambda j: (0, j)),
                  pl.BlockSpec((1, D_MODEL), lambda j: (0, j))],
        out_specs=pl.BlockSpec((8, D_MODEL), lambda j: (0, j)),
        out_shape=jax.ShapeDtypeStruct((8, w_mod.shape[1]), F32),
        compiler_params=_cparams(("parallel",)),
        name="modulation",
    )(c_rows, w_mod, b_mod.reshape(1, -1))


def _pre_kernel(xc_ref, xl_ref, sh_ref, sc_ref, g1_ref, win_ref, qg_ref, kg_ref, bd_ref, cos_ref, sin_ref,
                q_ref, k_ref, v_ref, f_ref, kn_ref, vn_ref, *, nc, tpb):
    i = pl.program_id(0)
    r = _mod_row(i, nc, tpb)
    x = jnp.where(i < nc, xc_ref[...], xl_ref[...])
    h = _rms(x) * g1_ref[...]
    h = h * (1.0 + sc_ref[pl.ds(r, 1), :]) + sh_ref[pl.ds(r, 1), :]
    proj = jnp.dot(h.astype(BF16), win_ref[...], preferred_element_type=F32)
    q = proj[:, :ATTN_W]
    k = proj[:, ATTN_W:ATTN_W + KV_W]
    v = proj[:, ATTN_W + KV_W:ATTN_W + 2 * KV_W]
    f_ref[...] = proj[:, ATTN_W + 2 * KV_W:].astype(BF16)
    vn_ref[...] = v
    v_ref[...] = v.astype(BF16)
    q_ms = jnp.dot((q * q).astype(BF16), bd_ref[...], preferred_element_type=F32)
    k_ms = jnp.dot((k * k).astype(BF16), bd_ref[:KV_W, :KV_W], preferred_element_type=F32)
    qn = q * lax.rsqrt(q_ms + EPS) * qg_ref[...]
    kn = k * lax.rsqrt(k_ms + EPS) * kg_ref[...]
    kn_ref[...] = kn
    cos = cos_ref[...]
    sin = sin_ref[...]
    lane = lax.broadcasted_iota(I32, cos.shape, 1)
    first = (lane & ROT_PAIRS) == 0

    def rope(c):
        partner = jnp.where(first, pltpu.roll(c, LANES - ROT_PAIRS, 1), pltpu.roll(c, ROT_PAIRS, 1))
        return c * cos + partner * sin

    k_ref[...] = rope(kn).astype(BF16)
    scale = HEAD_DIM ** -0.5 * math.log2(math.e)
    q_rot = jnp.concatenate([rope(qn[:, j * LANES:(j + 1) * LANES]) for j in range(ATTN_W // LANES)], axis=1)
    q_ref[...] = (q_rot * scale).T.astype(BF16)


def _pre(xc, xl, mod, norm1_g, w_in, q_g, k_g, bd, cos_t, sin_t, *, nc, nl, tpb):
    n = (nc + nl) * TM
    row = lambda i: (i, 0)
    const = lambda i: (0, 0)
    tab = lambda i: (jnp.where(i < nc, 0, 1 + (i - nc) % tpb), 0)
    return pl.pallas_call(
        functools.partial(_pre_kernel, nc=nc, tpb=tpb),
        grid=(nc + nl,),
        in_specs=[pl.BlockSpec((TM, D_MODEL), lambda i: (jnp.minimum(i, nc - 1), 0)),
                  pl.BlockSpec((TM, D_MODEL), lambda i: (jnp.maximum(i - nc, 0), 0)),
                  pl.BlockSpec((8, D_MODEL), lambda i: (0, 0)),
                  pl.BlockSpec((8, D_MODEL), lambda i: (0, 1)),
                  pl.BlockSpec((1, D_MODEL), const),
                  pl.BlockSpec((D_MODEL, D_IN), const),
                  pl.BlockSpec((1, ATTN_W), const),
                  pl.BlockSpec((1, KV_W), const),
                  pl.BlockSpec((ATTN_W, ATTN_W), const),
                  pl.BlockSpec((TM, LANES), tab),
                  pl.BlockSpec((TM, LANES), tab)],
        out_specs=[pl.BlockSpec((ATTN_W, TM), lambda i: (0, i)), pl.BlockSpec((TM, KV_W), row),
                   pl.BlockSpec((TM, KV_W), row),
                   pl.BlockSpec((TM, FOURIER_W), row), pl.BlockSpec((TM, KV_W), row), pl.BlockSpec((TM, KV_W), row)],
        out_shape=[jax.ShapeDtypeStruct((ATTN_W, n), BF16), jax.ShapeDtypeStruct((n, KV_W), BF16),
                   jax.ShapeDtypeStruct((n, KV_W), BF16), jax.ShapeDtypeStruct((n, FOURIER_W), BF16),
                   jax.ShapeDtypeStruct((n, KV_W), F32), jax.ShapeDtypeStruct((n, KV_W), F32)],
        compiler_params=_cparams(("parallel",)),
        name="pre_proj",
    )(xc, xl, mod, mod, norm1_g, w_in, q_g, k_g, bd, cos_t, sin_t)


V_ROWS = 80


def _attn_kernel(qt_ref, k_ref, vt_ref, o_ref, ot_ref, st_ref, pt_ref):
    def scores(h):
        qt = qt_ref[h * HEAD_DIM:(h + 1) * HEAD_DIM, :]
        st_ref[h % 2] = jnp.dot(k_ref[h // HEADS_PER_KV], qt, preferred_element_type=F32)

    scores(0)
    for h in range(N_HEADS):
        if h + 1 < N_HEADS:
            scores(h + 1)
        st = st_ref[h % 2]
        pt_ref[h % 2] = jnp.exp2(st - jnp.max(st, axis=0, keepdims=True)).astype(BF16)
        ot = jnp.dot(vt_ref[h // HEADS_PER_KV], pt_ref[h % 2], preferred_element_type=F32)
        ot_ref[h * HEAD_DIM:(h + 1) * HEAD_DIM, :] = ot[:HEAD_DIM] * (1.0 / ot[HEAD_DIM:HEAD_DIM + 1])
    o_ref[...] = ot_ref[...].T


def _attention(qt, k_heads, vt_ext, *, batch, t, s, q_row0):
    tq = min(TQ, t)
    qpb = t // tq
    q0 = q_row0 // tq
    return pl.pallas_call(
        _attn_kernel,
        grid=(batch, qpb),
        in_specs=[pl.BlockSpec((ATTN_W, tq), lambda b, i: (0, q0 + b * qpb + i)),
                  pl.BlockSpec((None, N_KV_HEADS, s, HEAD_DIM), lambda b, i: (b, 0, 0, 0)),
                  pl.BlockSpec((None, N_KV_HEADS, V_ROWS, s), lambda b, i: (b, 0, 0, 0))],
        out_specs=pl.BlockSpec((tq, ATTN_W), lambda b, i: (b * qpb + i, 0)),
        out_shape=jax.ShapeDtypeStruct((batch * t, ATTN_W), F32),
        scratch_shapes=[pltpu.VMEM((ATTN_W, tq), F32), pltpu.VMEM((2, s, tq), F32), pltpu.VMEM((2, s, tq), BF16)],
        compiler_params=_cparams(("parallel", "parallel")),
        name="attention",
    )(qt, k_heads, vt_ext)


def _split_kv_heads(keys, vals):
    b, s, _ = keys.shape
    k_heads = keys.reshape(b, s, N_KV_HEADS, HEAD_DIM).transpose(0, 2, 1, 3)
    vt = vals.reshape(b, s, N_KV_HEADS, HEAD_DIM).transpose(0, 2, 3, 1)
    extra = jnp.concatenate([jnp.ones((b, N_KV_HEADS, 1, s), BF16),
                             jnp.zeros((b, N_KV_HEADS, V_ROWS - HEAD_DIM - 1, s), BF16)], axis=2)
    return k_heads, jnp.concatenate([vt, extra], axis=2)


def _fourier_kernel(x_ref, cd_ref, ct_ref, st_ref, twc_ref, tws_ref, wf_ref, o_ref, *, n1, n2):
    pw = 2 * FOURIER_GROUP_W
    scale = 1.0 / math.sqrt(n1 * n2 * FOURIER_GROUP_W)
    for gp in range(FOURIER_W // pw):
        z = jnp.dot(x_ref[:, gp * pw:(gp + 1) * pw], cd_ref[...], preferred_element_type=F32)
        zr = [z[t1 * n2:(t1 + 1) * n2, :pw] for t1 in range(n1)]
        zi = [z[t1 * n2:(t1 + 1) * n2, pw:] for t1 in range(n1)]
        if n1 == 1:
            a = [(zr[0], zi[0])]
        else:
            a = [(zr[0] + zr[1] + zr[2] + zr[3], zi[0] + zi[1] + zi[2] + zi[3]),
                 (zr[0] + zi[1] - zr[2] - zi[3], zi[0] - zr[1] - zi[2] + zr[3]),
                 (zr[0] - zr[1] + zr[2] - zr[3], zi[0] - zi[1] + zi[2] - zi[3]),
                 (zr[0] - zi[1] - zr[2] + zi[3], zi[0] + zr[1] - zi[2] - zr[3])]
        for u1 in range(n1):
            ar, ai = a[u1]
            if u1 > 0:
                c = jnp.concatenate([twc_ref[u1 - 1]] * 2, axis=1)
                s = jnp.concatenate([tws_ref[u1 - 1]] * 2, axis=1)
                ar, ai = ar * c + ai * s, ai * c - ar * s
            y = (jnp.dot(ct_ref[...], ar.astype(BF16), preferred_element_type=F32)
                 + jnp.dot(st_ref[...], ai.astype(BF16), preferred_element_type=F32)) * scale
            y = jnp.dot(y.astype(BF16), wf_ref[gp], preferred_element_type=F32)
            rows = pl.ds(u1, n2, stride=n1) if n1 > 1 else slice(None)
            for half in range(2):
                o_ref[2 * gp + half, rows, :] = y[:, half * FOURIER_GROUP_W:(half + 1) * FOURIER_GROUP_W]


def _fourier(f, wf_pairs, *, batch, t, row0, n1):
    n2 = t // n1
    cd, ct, st, twc, tws = _dft_tables(n1, n2)
    cd, ct, st = cd.astype(BF16), ct.astype(BF16), st.astype(BF16)
    nt = twc.shape[0]
    b0 = row0 // t
    c2 = lambda b: (0, 0)
    c3 = lambda b: (0, 0, 0)
    return pl.pallas_call(
        functools.partial(_fourier_kernel, n1=n1, n2=n2),
        grid=(batch,),
        in_specs=[pl.BlockSpec((t, FOURIER_W), lambda b: (b0 + b, 0)),
                  pl.BlockSpec(cd.shape, c2), pl.BlockSpec((n2, n2), c2), pl.BlockSpec((n2, n2), c2),
                  pl.BlockSpec((nt, n2, LANES), c3), pl.BlockSpec((nt, n2, LANES), c3),
                  pl.BlockSpec(wf_pairs.shape, c3)],
        out_specs=pl.BlockSpec((FOURIER_W // FOURIER_GROUP_W, t, FOURIER_GROUP_W), lambda b: (0, b, 0)),
        out_shape=jax.ShapeDtypeStruct((FOURIER_W // FOURIER_GROUP_W, batch * t, FOURIER_GROUP_W), F32),
        compiler_params=_cparams(("parallel",)),
        name="fourier",
    )(f, cd, ct, st, twc, tws, wf_pairs)


def _mix_kernel(xc_ref, xl_ref, ac_ref, al_ref, fc_ref, fl_ref, g1_ref, sh2_ref, sc2_ref, ag_ref, fg_ref,
                wout_ref, n2g_ref, x1_ref, h2_ref, *, nc, tpb):
    i = pl.program_id(0)
    r = _mod_row(i, nc, tpb)
    is_ctx = i < nc
    x = jnp.where(is_ctx, xc_ref[...], xl_ref[...])
    a = _rms(jnp.where(is_ctx, ac_ref[...], al_ref[...])) * ag_ref[...]
    n_groups = FOURIER_W // FOURIER_GROUP_W
    f = jnp.concatenate([jnp.where(is_ctx, fc_ref[g], fl_ref[g]) for g in range(n_groups)], axis=1)
    f = _rms(f) * fg_ref[...]
    mixed = (jnp.dot(a.astype(BF16), wout_ref[:ATTN_W, :], preferred_element_type=F32)
             + jnp.dot(f.astype(BF16), wout_ref[ATTN_W:, :], preferred_element_type=F32))
    x1 = x + g1_ref[pl.ds(r, 1), :] * mixed
    x1_ref[...] = x1
    h2 = _rms(x1) * n2g_ref[...]
    h2_ref[...] = h2 * (1.0 + sc2_ref[pl.ds(r, 1), :]) + sh2_ref[pl.ds(r, 1), :]


def _mix(xc, xl, ac, al, fc, fl, mod, attn_g, four_g, w_out, norm2_g, *, nc, nl, tpb):
    n = (nc + nl) * TM
    lo = lambda i: (jnp.minimum(i, nc - 1), 0)
    hi = lambda i: (jnp.maximum(i - nc, 0), 0)
    const = lambda i: (0, 0)
    row = lambda i: (i, 0)
    return pl.pallas_call(
        functools.partial(_mix_kernel, nc=nc, tpb=tpb),
        grid=(nc + nl,),
        in_specs=[pl.BlockSpec((TM, D_MODEL), lo), pl.BlockSpec((TM, D_MODEL), hi),
                  pl.BlockSpec((TM, ATTN_W), lo), pl.BlockSpec((TM, ATTN_W), hi),
                  pl.BlockSpec((FOURIER_W // FOURIER_GROUP_W, TM, FOURIER_GROUP_W),
                               lambda i: (0, jnp.minimum(i, nc - 1), 0)),
                  pl.BlockSpec((FOURIER_W // FOURIER_GROUP_W, TM, FOURIER_GROUP_W),
                               lambda i: (0, jnp.maximum(i - nc, 0), 0)),
                  pl.BlockSpec((8, D_MODEL), lambda i: (0, 2)),
                  pl.BlockSpec((8, D_MODEL), lambda i: (0, 3)),
                  pl.BlockSpec((8, D_MODEL), lambda i: (0, 4)),
                  pl.BlockSpec((1, ATTN_W), const), pl.BlockSpec((1, FOURIER_W), const),
                  pl.BlockSpec((D_MODEL, D_MODEL), const), pl.BlockSpec((1, D_MODEL), const)],
        out_specs=[pl.BlockSpec((TM, D_MODEL), row), pl.BlockSpec((TM, D_MODEL), row)],
        out_shape=[jax.ShapeDtypeStruct((n, D_MODEL), F32), jax.ShapeDtypeStruct((n, D_MODEL), F32)],
        compiler_params=_cparams(("parallel",)),
        name="mix_proj",
    )(xc, xl, ac, al, fc, fl, mod, mod, mod, attn_g, four_g, w_out, norm2_g)


def _router_kernel(h2_ref, whi_ref, wlo_ref, br_ref, idx_ref, w_ref, cnt_ref):
    i = pl.program_id(0)
    h = h2_ref[...]
    h_hi = h.astype(BF16)
    h_lo = (h - h_hi.astype(F32)).astype(BF16)
    logits = (jnp.dot(h_hi, whi_ref[...], preferred_element_type=F32)
              + jnp.dot(h_lo, whi_ref[...], preferred_element_type=F32)
              + jnp.dot(h_hi, wlo_ref[...], preferred_element_type=F32)) + br_ref[...]
    lane = lax.broadcasted_iota(I32, logits.shape, 1)
    l = logits
    vals, idxs = [], []
    hits = jnp.zeros(logits.shape, F32)
    for _ in range(TOP_K):
        m = jnp.max(l, axis=-1, keepdims=True)
        ix = jnp.min(jnp.where(l == m, lane, LANES), axis=-1, keepdims=True)
        sel = lane == ix
        hits = hits + sel.astype(F32)
        l = jnp.where(sel, -jnp.inf, l)
        vals.append(m)
        idxs.append(ix)
    es = [jnp.exp(v - vals[0]) for v in vals]
    den = es[0] + es[1] + es[2] + es[3]
    idx_ref[...] = _pack_cols(idxs, lane)
    w_ref[...] = _pack_cols([e / den for e in es], lane)

    @pl.when(i == 0)
    def _():
        cnt_ref[...] = jnp.zeros(cnt_ref.shape, F32)

    cnt_ref[...] += jnp.sum(hits, axis=0, keepdims=True)


def _router(h2, w_hi, w_lo, br):
    n = h2.shape[0]
    row = lambda i: (i, 0)
    const = lambda i: (0, 0)
    return pl.pallas_call(
        _router_kernel,
        grid=(n // TM,),
        in_specs=[pl.BlockSpec((TM, D_MODEL), row), pl.BlockSpec((D_MODEL, LANES), const),
                  pl.BlockSpec((D_MODEL, LANES), const), pl.BlockSpec((1, LANES), const)],
        out_specs=[pl.BlockSpec((TM, LANES), row), pl.BlockSpec((TM, LANES), row), pl.BlockSpec((8, LANES), const)],
        out_shape=[jax.ShapeDtypeStruct((n, LANES), I32), jax.ShapeDtypeStruct((n, LANES), F32),
                   jax.ShapeDtypeStruct((8, LANES), F32)],
        compiler_params=_cparams(("arbitrary",)),
        name="router",
    )(h2, w_hi, w_lo, br)


def _pos_kernel(idx_ref, gs_ref, pos_ref, carry_ref):
    i = pl.program_id(0)

    @pl.when(i == 0)
    def _():
        carry_ref[...] = jnp.zeros(carry_ref.shape, F32)

    idx = idx_ref[...]
    lane = lax.broadcasted_iota(I32, idx.shape, 1)
    sels = [lane == idx[:, k:k + 1] for k in range(TOP_K)]
    hits = jnp.zeros(idx.shape, F32)
    for s in sels:
        hits = hits + s.astype(F32)
    r = lax.broadcasted_iota(I32, (TM, TM), 0)
    c = lax.broadcasted_iota(I32, (TM, TM), 1)
    before = (c < r).astype(BF16)
    rank = jnp.dot(before, hits.astype(BF16), preferred_element_type=F32) + carry_ref[0:1, :] + gs_ref[0:1, :]
    cols = [jnp.sum(jnp.where(s, rank, 0.0), axis=-1, keepdims=True) for s in sels]
    pos_ref[...] = _pack_cols(cols, lane).T[:8, :].astype(I32)
    carry_ref[...] += jnp.sum(hits, axis=0, keepdims=True)


def _positions(idx, gs_rows):
    n = idx.shape[0]
    return pl.pallas_call(
        _pos_kernel,
        grid=(n // TM,),
        in_specs=[pl.BlockSpec((TM, LANES), lambda i: (i, 0)), pl.BlockSpec((8, LANES), lambda i: (0, 0))],
        out_specs=pl.BlockSpec((8, TM), lambda i: (0, i)),
        out_shape=jax.ShapeDtypeStruct((8, n), I32),
        scratch_shapes=[pltpu.VMEM((8, LANES), F32)],
        compiler_params=_cparams(("arbitrary",)),
        name="positions",
    )(idx, gs_rows)


def _row_copy_wait(src, dst, sem):
    pltpu.make_async_copy(src, dst, sem).wait()


def _dispatch_kernel(ends_ref, pos_ref, h2_ref, xs_ref, zero_ref, sem):
    i = pl.program_id(0)

    @pl.when(i == 0)
    def _():
        zero_ref[...] = jnp.zeros(zero_ref.shape, F32)

        def last_tile(e):
            lo = ends_ref[e - 1] if e > 0 else 0
            start = pl.multiple_of(jnp.maximum(ends_ref[e] - TG, 0), TG)
            return ends_ref[e] > lo, pltpu.make_async_copy(zero_ref, xs_ref.at[pl.ds(start, TG), :], sem)

        for e in range(N_EXPERTS):
            nonempty, cp = last_tile(e)
            pl.when(nonempty)(cp.start)
        for e in range(N_EXPERTS):
            nonempty, cp = last_tile(e)
            pl.when(nonempty)(cp.wait)

        def tail_tile(j):
            return pltpu.make_async_copy(zero_ref, xs_ref.at[pl.ds(pl.multiple_of(j * TG, TG), TG), :], sem)

        first_unused = ends_ref[N_EXPERTS - 1] // TG
        n_tiles = xs_ref.shape[0] // TG
        lax.fori_loop(first_unused, n_tiles, lambda j, c: (tail_tile(j).start(), c)[1], 0)
        lax.fori_loop(first_unused, n_tiles, lambda j, c: (tail_tile(j).wait(), c)[1], 0)

    def issue(t, carry):
        for k in range(TOP_K):
            p = pos_ref[k, t]
            pltpu.make_async_copy(h2_ref.at[pl.ds(t, 1), :], xs_ref.at[pl.ds(p, 1), :], sem).start()
        return carry

    lax.fori_loop(0, TM, issue, 0)
    for k in range(TOP_K):
        _row_copy_wait(h2_ref, xs_ref.at[pl.ds(0, TM), :], sem)


def _dispatch(ends, pos, h2, n_rows):
    n = h2.shape[0]
    grid_spec = pltpu.PrefetchScalarGridSpec(
        num_scalar_prefetch=1,
        grid=(n // TM,),
        in_specs=[pl.BlockSpec((8, TM), lambda i, ends: (0, i), memory_space=pltpu.SMEM),
                  pl.BlockSpec((TM, D_MODEL), lambda i, ends: (i, 0))],
        out_specs=pl.BlockSpec(memory_space=pl.ANY),
        scratch_shapes=[pltpu.VMEM((TG, D_MODEL), F32), pltpu.SemaphoreType.DMA(())],
    )
    return pl.pallas_call(
        _dispatch_kernel,
        grid_spec=grid_spec,
        out_shape=jax.ShapeDtypeStruct((n_rows, D_MODEL), F32),
        compiler_params=_cparams(("arbitrary",)),
        name="dispatch",
    )(ends, pos, h2)


def _ffn_kernel(te_ref, nv_ref, par_ref, nxt_ref, x_ref, bg_ref, bl_ref, bdn_ref, perm_ref, wgu_hbm, wdn_hbm, o_ref,
                wgu_buf, wdn_buf, wg_s, wl_s, wd_s, wsem):
    i = pl.program_id(0)
    e = te_ref[i]
    valid = i < nv_ref[0]
    fresh = jnp.logical_or(i == 0, e != te_ref[jnp.maximum(i - 1, 0)])
    wslot = par_ref[i]

    def weight_copies(expert, s):
        return (pltpu.make_async_copy(wgu_hbm.at[expert], wgu_buf.at[s], wsem.at[s]),
                pltpu.make_async_copy(wdn_hbm.at[expert], wdn_buf.at[s], wsem.at[s]))

    @pl.when(i == 0)
    def _():
        for cp in weight_copies(e, wslot):
            cp.start()

    @pl.when(jnp.logical_and(valid, fresh))
    def _():
        for cp in weight_copies(e, wslot):
            cp.wait()
        for j in range(2 * D_EXPERT // (2 * LANES)):
            blk = wgu_buf[wslot, :, j * 2 * LANES:(j + 1) * 2 * LANES].astype(BF16)
            d = jnp.dot(blk, perm_ref[...], preferred_element_type=F32)
            wg_s[:, j * LANES:(j + 1) * LANES] = d[:, :LANES].astype(BF16)
            wl_s[:, j * LANES:(j + 1) * LANES] = d[:, LANES:].astype(BF16)
        wd_s[...] = wdn_buf[wslot].astype(BF16)

        @pl.when(nxt_ref[i] >= 0)
        def _():
            for cp in weight_copies(nxt_ref[i], 1 - wslot):
                cp.start()

    @pl.when(valid)
    def _():
        x = x_ref[...].astype(BF16)
        g = jnp.dot(x, wg_s[...], preferred_element_type=F32) + bg_ref[...]
        l = jnp.dot(x, wl_s[...], preferred_element_type=F32) + bl_ref[...]
        g = jnp.minimum(g, SWIGLU_LIMIT)
        l = jnp.clip(l, -SWIGLU_LIMIT, SWIGLU_LIMIT)
        a = (l + 1.0) * (g * (1.0 / (1.0 + jnp.exp(-SWIGLU_ALPHA * g))))
        o_ref[...] = jnp.dot(a.astype(BF16), wd_s[...], preferred_element_type=F32) + bdn_ref[...]

    @pl.when(jnp.logical_not(valid))
    def _():
        o_ref[...] = jnp.zeros(o_ref.shape, F32)


def _expert_ffn(tile_expert, n_valid, parity, next_expert, xs, w_gu, b_g, b_l, w_dn, b_dn, perm):
    p = xs.shape[0]
    nt = p // TG
    tile = lambda i, te, nv, *_: (jnp.minimum(i, nv[0] - 1), 0)
    ex3 = lambda i, te, *_: (te[i], 0, 0)
    grid_spec = pltpu.PrefetchScalarGridSpec(
        num_scalar_prefetch=4,
        grid=(nt,),
        in_specs=[pl.BlockSpec((TG, D_MODEL), tile),
                  pl.BlockSpec((None, 1, D_EXPERT), ex3),
                  pl.BlockSpec((None, 1, D_EXPERT), ex3),
                  pl.BlockSpec((None, 1, D_MODEL), ex3),
                  pl.BlockSpec((2 * LANES, 2 * LANES), lambda i, *_: (0, 0)),
                  pl.BlockSpec(memory_space=pl.ANY),
                  pl.BlockSpec(memory_space=pl.ANY)],
        out_specs=pl.BlockSpec((TG, D_MODEL), lambda i, *_: (i, 0)),
        scratch_shapes=[pltpu.VMEM((2, D_MODEL, 2 * D_EXPERT), F32), pltpu.VMEM((2, D_EXPERT, D_MODEL), F32),
                        pltpu.VMEM((D_MODEL, D_EXPERT), BF16), pltpu.VMEM((D_MODEL, D_EXPERT), BF16),
                        pltpu.VMEM((D_EXPERT, D_MODEL), BF16), pltpu.SemaphoreType.DMA((2,))],
    )
    return pl.pallas_call(
        _ffn_kernel,
        grid_spec=grid_spec,
        out_shape=jax.ShapeDtypeStruct((p, D_MODEL), F32),
        compiler_params=_cparams(("arbitrary",)),
        name="expert_ffn",
    )(tile_expert, n_valid, parity, next_expert, xs, b_g, b_l, b_dn, perm, w_gu, w_dn)


def _combine_kernel(pos_ref, w_ref, x1_ref, g2_ref, fg_ref, ys_ref, o_ref, buf, sem, *, tile0, nc, tpb):
    i = pl.program_id(0) + tile0
    r = _mod_row(i, nc, tpb)

    def issue(t, carry):
        for k in range(TOP_K):
            p = pos_ref[k, t]
            pltpu.make_async_copy(ys_ref.at[pl.ds(p, 1), :], buf.at[k, pl.ds(t, 1), :], sem).start()
        return carry

    lax.fori_loop(0, TM, issue, 0)
    for k in range(TOP_K):
        _row_copy_wait(ys_ref.at[pl.ds(0, TM), :], buf.at[k], sem)
    w = w_ref[...]
    moe = w[:, 0:1] * buf[0]
    for k in range(1, TOP_K):
        moe = moe + w[:, k:k + 1] * buf[k]
    y = x1_ref[...] + g2_ref[pl.ds(r, 1), :] * moe
    o_ref[...] = _rms(y) * fg_ref[...]


def _combine(pos, w, x1, mod, final_g, ys, *, tile0, ntiles, nc, tpb):
    return pl.pallas_call(
        functools.partial(_combine_kernel, tile0=tile0, nc=nc, tpb=tpb),
        grid=(ntiles,),
        in_specs=[pl.BlockSpec((8, TM), lambda i: (0, i + tile0), memory_space=pltpu.SMEM),
                  pl.BlockSpec((TM, LANES), lambda i: (i + tile0, 0)),
                  pl.BlockSpec((TM, D_MODEL), lambda i: (i + tile0, 0)),
                  pl.BlockSpec((8, D_MODEL), lambda i: (0, 5)),
                  pl.BlockSpec((1, D_MODEL), lambda i: (0, 0)),
                  pl.BlockSpec(memory_space=pl.ANY)],
        out_specs=pl.BlockSpec((TM, D_MODEL), lambda i: (i, 0)),
        out_shape=jax.ShapeDtypeStruct((ntiles * TM, D_MODEL), F32),
        scratch_shapes=[pltpu.VMEM((TOP_K, TM, D_MODEL), F32), pltpu.SemaphoreType.DMA(())],
        compiler_params=_cparams(("arbitrary",)),
        name="combine",
    )(pos, w, x1, mod, final_g, ys)


def kernel(x_prompt, x_sample, cache_k, cache_v, c, c_ctx, norm1_g, w_mod, b_mod, w_in, q_norm_g, k_norm_g,
           w_fourier, attn_out_g, fourier_out_g, w_out, norm2_g, w_router, b_router, w_gate_up, b_gate_up,
           w_down, b_down, final_g):
    bc, tc, _ = x_prompt.shape
    bl, tl, _ = x_sample.shape
    depth = w_in.shape[0]
    past = cache_k.shape[2]
    assert depth == 1 and tl % TQ == 0 and tl % TM == 0 and (bc * tc) % TM == 0
    n_ctx, n_lat = bc * tc, bl * tl
    n_tok = n_ctx + n_lat
    nc, nl, tpb = n_ctx // TM, n_lat // TM, tl // TM

    xc = x_prompt.reshape(n_ctx, D_MODEL)
    xl = x_sample.reshape(n_lat, D_MODEL)

    c_rows = jnp.concatenate([c_ctx[None, :], c, jnp.zeros((8 - 1 - bl, D_MODEL), F32)], axis=0)
    mod = _modulation(c_rows, w_mod[0], b_mod[0])

    cos_t, sin_t = _rope_tables(tl)
    q, k, v, f, k_new, v_new = _pre(
        xc, xl, mod, norm1_g, w_in[0].astype(BF16), jnp.tile(q_norm_g, (1, N_HEADS)),
        jnp.tile(k_norm_g, (1, N_KV_HEADS)), _head_mean_matrix(), cos_t, sin_t, nc=nc, nl=nl, tpb=tpb)

    kc, vtc = _split_kv_heads(k[:n_ctx].reshape(bc, tc, KV_W), v[:n_ctx].reshape(bc, tc, KV_W))
    attn_c = _attention(q, kc, vtc, batch=bc, t=tc, s=tc, q_row0=0)
    keys = jnp.concatenate([cache_k[:, 0].reshape(bl, past, KV_W).astype(BF16),
                            k[n_ctx:].reshape(bl, tl, KV_W)], axis=1)
    vals = jnp.concatenate([cache_v[:, 0].reshape(bl, past, KV_W).astype(BF16),
                            v[n_ctx:].reshape(bl, tl, KV_W)], axis=1)
    kl, vtl = _split_kv_heads(keys, vals)
    attn_l = _attention(q, kl, vtl, batch=bl, t=tl, s=past + tl, q_row0=n_ctx)

    wf = w_fourier[0].astype(BF16)
    zero = jnp.zeros_like(wf[0])
    wf_pairs = jnp.stack([jnp.block([[wf[0], zero], [zero, wf[1]]]), jnp.block([[wf[2], zero], [zero, wf[3]]])])
    four_c = _fourier(f, wf_pairs, batch=bc, t=tc, row0=0, n1=1)
    four_l = _fourier(f, wf_pairs, batch=bl, t=tl, row0=n_ctx, n1=FFT_N1)

    x1, h2 = _mix(xc, xl, attn_c, attn_l, four_c, four_l, mod, attn_out_g, fourier_out_g,
                  w_out[0].astype(BF16), norm2_g, nc=nc, nl=nl, tpb=tpb)

    wr = jnp.pad(w_router[0], ((0, 0), (0, LANES - N_EXPERTS)))
    wr_hi = wr.astype(BF16)
    wr_lo = (wr - wr_hi.astype(F32)).astype(BF16)
    br = jnp.pad(b_router[0], (0, LANES - N_EXPERTS), constant_values=-1e30).reshape(1, LANES)
    idx, gate_w, counts = _router(h2, wr_hi, wr_lo, br)
    cnt = counts[0, :N_EXPERTS].astype(I32)
    padded = ((cnt + TG - 1) // TG) * TG
    ends = jnp.cumsum(padded)
    starts = ends - padded
    n_rows = n_tok * TOP_K + N_EXPERTS * TG
    n_tiles = n_rows // TG
    tile_ids = jnp.arange(n_tiles, dtype=I32)
    tile_expert = jnp.minimum(jnp.sum((ends[None, :] // TG <= tile_ids[:, None]).astype(I32), axis=1),
                              N_EXPERTS - 1)
    n_valid = (ends[N_EXPERTS - 1] // TG).reshape(1)
    gs_rows = jnp.broadcast_to(jnp.pad(starts.astype(F32), (0, LANES - N_EXPERTS))[None, :], (8, LANES))
    pos = _positions(idx, gs_rows)

    xs = _dispatch(ends, pos, h2, n_rows)
    fresh_tile = jnp.concatenate([jnp.ones((1,), I32), (tile_expert[1:] != tile_expert[:-1]).astype(I32)])
    parity = (jnp.cumsum(fresh_tile) - 1) % 2
    experts = jnp.arange(N_EXPERTS, dtype=I32)
    later = jnp.logical_and(padded[None, :] > 0, experts[None, :] > experts[:, None])
    next_of = jnp.min(jnp.where(later, experts[None, :], N_EXPERTS), axis=1)
    next_of = jnp.where(next_of == N_EXPERTS, -1, next_of)
    next_expert = jnp.sum(jnp.where(tile_expert[:, None] == experts[None, :], next_of[None, :], 0), axis=1)
    b_gu = b_gate_up[0].reshape(N_EXPERTS, 1, D_EXPERT, 2)
    ys = _expert_ffn(tile_expert, n_valid, parity.astype(I32), next_expert.astype(I32), xs, w_gate_up[0],
                     b_gu[..., 0], b_gu[..., 1], w_down[0], b_down[0].reshape(N_EXPERTS, 1, D_MODEL),
                     _deinterleave_matrix())

    y_c = _combine(pos, gate_w, x1, mod, final_g.reshape(1, -1), ys, tile0=0, ntiles=nc, nc=nc, tpb=tpb)
    y_l = _combine(pos, gate_w, x1, mod, final_g.reshape(1, -1), ys, tile0=nc, ntiles=nl, nc=nc, tpb=tpb)

    new_k = k_new[:n_ctx].reshape(bc, 1, tc, N_KV_HEADS, HEAD_DIM)
    new_v = v_new[:n_ctx].reshape(bc, 1, tc, N_KV_HEADS, HEAD_DIM)
    return (y_c.reshape(bc, tc, D_MODEL), y_l.reshape(bl, tl, D_MODEL), new_k, new_v)
```

```python
import functools
import math

import numpy as np
import jax
import jax.numpy as jnp
from jax import lax
from jax.experimental import pallas as pl
from jax.experimental.pallas import tpu as pltpu

F32 = jnp.float32
BF16 = jnp.bfloat16
I32 = jnp.int32

D_MODEL = 1024
HEAD_DIM = 64
N_HEADS = 8
N_KV_HEADS = 2
HEADS_PER_KV = N_HEADS // N_KV_HEADS
ATTN_W = N_HEADS * HEAD_DIM
KV_W = N_KV_HEADS * HEAD_DIM
FOURIER_W = 512
FOURIER_GROUP_W = 128
D_IN = ATTN_W + 2 * KV_W + FOURIER_W
N_EXPERTS = 32
TOP_K = 4
D_EXPERT = 1024
SWIGLU_ALPHA = 1.702
SWIGLU_LIMIT = 7.0
ROPE_THETA = 10000.0
ROT_PAIRS = HEAD_DIM // 4
GRID_W = 64
EPS = 1e-6

LANES = 128
TM = 512
TQ = 512
TG = 256
FFT_N1 = 4
TILE_ROWS = D_MODEL // LANES
VMEM_LIMIT = 56 * 1024 * 1024


def _cparams(sem, vmem=VMEM_LIMIT):
    return pltpu.CompilerParams(dimension_semantics=sem, vmem_limit_bytes=vmem)


def _rope_tables(t_lat):
    pos = np.arange(t_lat)
    row = (pos // GRID_W).astype(np.float64)
    col = (pos % GRID_W).astype(np.float64)
    inv = ROPE_THETA ** (-np.arange(ROT_PAIRS, dtype=np.float64) / ROT_PAIRS)
    lane = np.arange(LANES)
    d = lane % HEAD_DIM
    axis = d // (2 * ROT_PAIRS)
    second = (d // ROT_PAIRS) % 2
    p = d % ROT_PAIRS
    ang = np.where(axis[None, :] == 0, row[:, None], col[:, None]) * inv[p][None, :]
    cos = np.cos(ang)
    sin = np.sin(ang) * np.where(second == 0, -1.0, 1.0)[None, :]
    cos = np.concatenate([np.ones((TM, LANES)), cos], axis=0)
    sin = np.concatenate([np.zeros((TM, LANES)), sin], axis=0)
    return jnp.asarray(cos, F32), jnp.asarray(sin, F32)


def _dft_tables(n1, n2):
    c = np.arange(FOURIER_GROUP_W)
    ang = 2.0 * np.pi * np.outer(c, c) / FOURIER_GROUP_W
    z = np.zeros_like(ang)
    cbd = np.block([[np.cos(ang), z], [z, np.cos(ang)]])
    sbd = np.block([[np.sin(ang), z], [z, np.sin(ang)]])
    cd = np.concatenate([cbd, -sbd], axis=1)
    u = np.arange(n2)
    ang2 = 2.0 * np.pi * np.outer(u, u) / n2
    ct, st = np.cos(ang2), np.sin(ang2)
    nt = max(n1 - 1, 1)
    tw_ang = 2.0 * np.pi * np.outer(np.arange(1, nt + 1), u) / (n1 * n2)
    twc = np.repeat(np.cos(tw_ang)[:, :, None], LANES, axis=2)
    tws = np.repeat(np.sin(tw_ang)[:, :, None], LANES, axis=2)
    return (jnp.asarray(cd, F32), jnp.asarray(ct, F32), jnp.asarray(st, F32),
            jnp.asarray(twc, F32), jnp.asarray(tws, F32))


def _head_mean_matrix():
    h = np.arange(ATTN_W) // HEAD_DIM
    return jnp.asarray((h[:, None] == h[None, :]) / HEAD_DIM, BF16)


def _deinterleave_matrix():
    p = np.zeros((2 * LANES, 2 * LANES))
    m = np.arange(LANES)
    p[2 * m, m] = 1.0
    p[2 * m + 1, LANES + m] = 1.0
    return jnp.asarray(p, BF16)


def _mod_row(i, nc, tpb):
    return jnp.where(i < nc, 0, 1 + (i - nc) // tpb)


def _rms(x):
    return x * lax.rsqrt(jnp.mean(x * x, axis=-1, keepdims=True) + EPS)


def _pack_cols(cols, lane):
    out = jnp.zeros(lane.shape, cols[0].dtype)
    for k, c in enumerate(cols):
        out = jnp.where(lane == k, c, out)
    return out


def _to_tiles(ref, x):
    r = x.shape[0]
    for s in range(TILE_ROWS):
        ref[pl.ds(s, r, stride=TILE_ROWS), :] = x[:, s * LANES:(s + 1) * LANES]


def _from_tiles(ref):
    r = ref.shape[0] // TILE_ROWS
    return jnp.concatenate([ref[pl.ds(s, r, stride=TILE_ROWS), :] for s in range(TILE_ROWS)], axis=1)


def _mod_kernel(c_ref, w_ref, b_ref, o_ref):
    c = c_ref[...]
    s = c * (1.0 / (1.0 + jnp.exp(-c)))
    o_ref[...] = jnp.dot(s.astype(BF16), w_ref[...].astype(BF16), preferred_element_type=F32) + b_ref[...]


def _modulation(c_rows, w_mod, b_mod):
    n = w_mod.shape[1] // D_MODEL
    return pl.pallas_call(
        _mod_kernel,
        grid=(n,),
        in_specs=[pl.BlockSpec((8, D_MODEL), lambda j: (0, 0)),
                  pl.BlockSpec((D_MODEL, D_MODEL), lambda j: (0, j)),
                  pl.BlockSpec((1, D_MODEL), lambda j: (0, j))],
        out_specs=pl.BlockSpec((8, D_MODEL), lambda j: (0, j)),
        out_shape=jax.ShapeDtypeStruct((8, w_mod.shape[1]), F32),
        compiler_params=_cparams(("parallel",)),
        name="modulation",
    )(c_rows, w_mod, b_mod.reshape(1, -1))


def _pre_kernel(xc_ref, xl_ref, sh_ref, sc_ref, g1_ref, win_ref, qg_ref, kg_ref, bd_ref, cos_ref, sin_ref,
                q_ref, k_ref, v_ref, f_ref, kn_ref, vn_ref, *, nc, tpb):
    i = pl.program_id(0)
    r = _mod_row(i, nc, tpb)
    x = jnp.where(i < nc, xc_ref[...], xl_ref[...])
    h = _rms(x) * g1_ref[...]
    h = h * (1.0 + sc_ref[pl.ds(r, 1), :]) + sh_ref[pl.ds(r, 1), :]
    proj = jnp.dot(h.astype(BF16), win_ref[...], preferred_element_type=F32)
    q = proj[:, :ATTN_W]
    k = proj[:, ATTN_W:ATTN_W + KV_W]
    v = proj[:, ATTN_W + KV_W:ATTN_W + 2 * KV_W]
    f_ref[...] = proj[:, ATTN_W + 2 * KV_W:].astype(BF16)
    vn_ref[...] = v
    v_ref[...] = v.astype(BF16)
    q_ms = jnp.dot((q * q).astype(BF16), bd_ref[...], preferred_element_type=F32)
    k_ms = jnp.dot((k * k).astype(BF16), bd_ref[:KV_W, :KV_W], preferred_element_type=F32)
    qn = q * lax.rsqrt(q_ms + EPS) * qg_ref[...]
    kn = k * lax.rsqrt(k_ms + EPS) * kg_ref[...]
    kn_ref[...] = kn
    cos = cos_ref[...]
    sin = sin_ref[...]
    lane = lax.broadcasted_iota(I32, cos.shape, 1)
    first = (lane & ROT_PAIRS) == 0

    def rope(c):
        partner = jnp.where(first, pltpu.roll(c, LANES - ROT_PAIRS, 1), pltpu.roll(c, ROT_PAIRS, 1))
        return c * cos + partner * sin

    k_ref[...] = rope(kn).astype(BF16)
    scale = HEAD_DIM ** -0.5 * math.log2(math.e)
    q_rot = jnp.concatenate([rope(qn[:, j * LANES:(j + 1) * LANES]) for j in range(ATTN_W // LANES)], axis=1)
    q_ref[...] = (q_rot * scale).T.astype(BF16)


def _pre(xc, xl, mod, norm1_g, w_in, q_g, k_g, bd, cos_t, sin_t, *, nc, nl, tpb):
    n = (nc + nl) * TM
    row = lambda i: (i, 0)
    const = lambda i: (0, 0)
    tab = lambda i: (jnp.where(i < nc, 0, 1 + (i - nc) % tpb), 0)
    return pl.pallas_call(
        functools.partial(_pre_kernel, nc=nc, tpb=tpb),
        grid=(nc + nl,),
        in_specs=[pl.BlockSpec((TM, D_MODEL), lambda i: (jnp.minimum(i, nc - 1), 0)),
                  pl.BlockSpec((TM, D_MODEL), lambda i: (jnp.maximum(i - nc, 0), 0)),
                  pl.BlockSpec((8, D_MODEL), lambda i: (0, 0)),
                  pl.BlockSpec((8, D_MODEL), lambda i: (0, 1)),
                  pl.BlockSpec((1, D_MODEL), const),
                  pl.BlockSpec((D_MODEL, D_IN), const),
                  pl.BlockSpec((1, ATTN_W), const),
                  pl.BlockSpec((1, KV_W), const),
                  pl.BlockSpec((ATTN_W, ATTN_W), const),
                  pl.BlockSpec((TM, LANES), tab),
                  pl.BlockSpec((TM, LANES), tab)],
        out_specs=[pl.BlockSpec((ATTN_W, TM), lambda i: (0, i)), pl.BlockSpec((TM, KV_W), row),
                   pl.BlockSpec((TM, KV_W), row),
                   pl.BlockSpec((TM, FOURIER_W), row), pl.BlockSpec((TM, KV_W), row), pl.BlockSpec((TM, KV_W), row)],
        out_shape=[jax.ShapeDtypeStruct((ATTN_W, n), BF16), jax.ShapeDtypeStruct((n, KV_W), BF16),
                   jax.ShapeDtypeStruct((n, KV_W), BF16), jax.ShapeDtypeStruct((n, FOURIER_W), BF16),
                   jax.ShapeDtypeStruct((n, KV_W), F32), jax.ShapeDtypeStruct((n, KV_W), F32)],
        compiler_params=_cparams(("parallel",)),
        name="pre_proj",
    )(xc, xl, mod, mod, norm1_g, w_in, q_g, k_g, bd, cos_t, sin_t)


V_ROWS = 80


def _attn_kernel(qt_ref, k_ref, vt_ref, o_ref, ot_ref, st_ref, pt_ref):
    def scores(h):
        qt = qt_ref[h * HEAD_DIM:(h + 1) * HEAD_DIM, :]
        st_ref[h % 2] = jnp.dot(k_ref[h // HEADS_PER_KV], qt, preferred_element_type=F32)

    scores(0)
    for h in range(N_HEADS):
        if h + 1 < N_HEADS:
            scores(h + 1)
        st = st_ref[h % 2]
        pt_ref[h % 2] = jnp.exp2(st - jnp.max(st, axis=0, keepdims=True)).astype(BF16)
        ot = jnp.dot(vt_ref[h // HEADS_PER_KV], pt_ref[h % 2], preferred_element_type=F32)
        ot_ref[h * HEAD_DIM:(h + 1) * HEAD_DIM, :] = ot[:HEAD_DIM] * (1.0 / ot[HEAD_DIM:HEAD_DIM + 1])
    o_ref[...] = ot_ref[...].T


def _attention(qt, k_heads, vt_ext, *, batch, t, s, q_row0):
    tq = min(TQ, t)
    qpb = t // tq
    q0 = q_row0 // tq
    return pl.pallas_call(
        _attn_kernel,
        grid=(batch, qpb),
        in_specs=[pl.BlockSpec((ATTN_W, tq), lambda b, i: (0, q0 + b * qpb + i)),
                  pl.BlockSpec((None, N_KV_HEADS, s, HEAD_DIM), lambda b, i: (b, 0, 0, 0)),
                  pl.BlockSpec((None, N_KV_HEADS, V_ROWS, s), lambda b, i: (b, 0, 0, 0))],
        out_specs=pl.BlockSpec((tq, ATTN_W), lambda b, i: (b * qpb + i, 0)),
        out_shape=jax.ShapeDtypeStruct((batch * t, ATTN_W), F32),
        scratch_shapes=[pltpu.VMEM((ATTN_W, tq), F32), pltpu.VMEM((2, s, tq), F32), pltpu.VMEM((2, s, tq), BF16)],
        compiler_params=_cparams(("parallel", "parallel")),
        name="attention",
    )(qt, k_heads, vt_ext)


def _split_kv_heads(keys, vals):
    b, s, _ = keys.shape
    k_heads = keys.reshape(b, s, N_KV_HEADS, HEAD_DIM).transpose(0, 2, 1, 3)
    vt = vals.reshape(b, s, N_KV_HEADS, HEAD_DIM).transpose(0, 2, 3, 1)
    extra = jnp.concatenate([jnp.ones((b, N_KV_HEADS, 1, s), BF16),
                             jnp.zeros((b, N_KV_HEADS, V_ROWS - HEAD_DIM - 1, s), BF16)], axis=2)
    return k_heads, jnp.concatenate([vt, extra], axis=2)


def _fourier_kernel(x_ref, cd_ref, ct_ref, st_ref, twc_ref, tws_ref, wf_ref, o_ref, *, n1, n2):
    pw = 2 * FOURIER_GROUP_W
    scale = 1.0 / math.sqrt(n1 * n2 * FOURIER_GROUP_W)
    for gp in range(FOURIER_W // pw):
        z = jnp.dot(x_ref[:, gp * pw:(gp + 1) * pw], cd_ref[...], preferred_element_type=F32)
        zr = [z[t1 * n2:(t1 + 1) * n2, :pw] for t1 in range(n1)]
        zi = [z[t1 * n2:(t1 + 1) * n2, pw:] for t1 in range(n1)]
        if n1 == 1:
            a = [(zr[0], zi[0])]
        else:
            a = [(zr[0] + zr[1] + zr[2] + zr[3], zi[0] + zi[1] + zi[2] + zi[3]),
                 (zr[0] + zi[1] - zr[2] - zi[3], zi[0] - zr[1] - zi[2] + zr[3]),
                 (zr[0] - zr[1] + zr[2] - zr[3], zi[0] - zi[1] + zi[2] - zi[3]),
                 (zr[0] - zi[1] - zr[2] + zi[3], zi[0] + zr[1] - zi[2] - zr[3])]
        for u1 in range(n1):
            ar, ai = a[u1]
            if u1 > 0:
                c = jnp.concatenate([twc_ref[u1 - 1]] * 2, axis=1)
                s = jnp.concatenate([tws_ref[u1 - 1]] * 2, axis=1)
                ar, ai = ar * c + ai * s, ai * c - ar * s
            y = (jnp.dot(ct_ref[...], ar.astype(BF16), preferred_element_type=F32)
                 + jnp.dot(st_ref[...], ai.astype(BF16), preferred_element_type=F32)) * scale
            y = jnp.dot(y.astype(BF16), wf_ref[gp], preferred_element_type=F32)
            rows = pl.ds(u1, n2, stride=n1) if n1 > 1 else slice(None)
            for half in range(2):
                o_ref[2 * gp + half, rows, :] = y[:, half * FOURIER_GROUP_W:(half + 1) * FOURIER_GROUP_W]


def _fourier(f, wf_pairs, *, batch, t, row0, n1):
    n2 = t // n1
    cd, ct, st, twc, tws = _dft_tables(n1, n2)
    cd, ct, st = cd.astype(BF16), ct.astype(BF16), st.astype(BF16)
    nt = twc.shape[0]
    b0 = row0 // t
    c2 = lambda b: (0, 0)
    c3 = lambda b: (0, 0, 0)
    return pl.pallas_call(
        functools.partial(_fourier_kernel, n1=n1, n2=n2),
        grid=(batch,),
        in_specs=[pl.BlockSpec((t, FOURIER_W), lambda b: (b0 + b, 0)),
                  pl.BlockSpec(cd.shape, c2), pl.BlockSpec((n2, n2), c2), pl.BlockSpec((n2, n2), c2),
                  pl.BlockSpec((nt, n2, LANES), c3), pl.BlockSpec((nt, n2, LANES), c3),
                  pl.BlockSpec(wf_pairs.shape, c3)],
        out_specs=pl.BlockSpec((FOURIER_W // FOURIER_GROUP_W, t, FOURIER_GROUP_W), lambda b: (0, b, 0)),
        out_shape=jax.ShapeDtypeStruct((FOURIER_W // FOURIER_GROUP_W, batch * t, FOURIER_GROUP_W), F32),
        compiler_params=_cparams(("parallel",)),
        name="fourier",
    )(f, cd, ct, st, twc, tws, wf_pairs)


def _mix_kernel(xc_ref, xl_ref, ac_ref, al_ref, fc_ref, fl_ref, g1_ref, sh2_ref, sc2_ref, ag_ref, fg_ref,
                wout_ref, n2g_ref, x1_ref, h2_ref, *, nc, tpb):
    i = pl.program_id(0)
    r = _mod_row(i, nc, tpb)
    is_ctx = i < nc
    x = jnp.where(is_ctx, xc_ref[...], xl_ref[...])
    a = _rms(jnp.where(is_ctx, ac_ref[...], al_ref[...])) * ag_ref[...]
    n_groups = FOURIER_W // FOURIER_GROUP_W
    f = jnp.concatenate([jnp.where(is_ctx, fc_ref[g], fl_ref[g]) for g in range(n_groups)], axis=1)
    f = _rms(f) * fg_ref[...]
    mixed = (jnp.dot(a.astype(BF16), wout_ref[:ATTN_W, :], preferred_element_type=F32)
             + jnp.dot(f.astype(BF16), wout_ref[ATTN_W:, :], preferred_element_type=F32))
    x1 = x + g1_ref[pl.ds(r, 1), :] * mixed
    x1_ref[...] = x1
    h2 = _rms(x1) * n2g_ref[...]
    h2_ref[...] = h2 * (1.0 + sc2_ref[pl.ds(r, 1), :]) + sh2_ref[pl.ds(r, 1), :]


def _mix(xc, xl, ac, al, fc, fl, mod, attn_g, four_g, w_out, norm2_g, *, nc, nl, tpb):
    n = (nc + nl) * TM
    lo = lambda i: (jnp.minimum(i, nc - 1), 0)
    hi = lambda i: (jnp.maximum(i - nc, 0), 0)
    const = lambda i: (0, 0)
    row = lambda i: (i, 0)
    return pl.pallas_call(
        functools.partial(_mix_kernel, nc=nc, tpb=tpb),
        grid=(nc + nl,),
        in_specs=[pl.BlockSpec((TM, D_MODEL), lo), pl.BlockSpec((TM, D_MODEL), hi),
                  pl.BlockSpec((TM, ATTN_W), lo), pl.BlockSpec((TM, ATTN_W), hi),
                  pl.BlockSpec((FOURIER_W // FOURIER_GROUP_W, TM, FOURIER_GROUP_W),
                               lambda i: (0, jnp.minimum(i, nc - 1), 0)),
                  pl.BlockSpec((FOURIER_W // FOURIER_GROUP_W, TM, FOURIER_GROUP_W),
                               lambda i: (0, jnp.maximum(i - nc, 0), 0)),
                  pl.BlockSpec((8, D_MODEL), lambda i: (0, 2)),
                  pl.BlockSpec((8, D_MODEL), lambda i: (0, 3)),
                  pl.BlockSpec((8, D_MODEL), lambda i: (0, 4)),
                  pl.BlockSpec((1, ATTN_W), const), pl.BlockSpec((1, FOURIER_W), const),
                  pl.BlockSpec((D_MODEL, D_MODEL), const), pl.BlockSpec((1, D_MODEL), const)],
        out_specs=[pl.BlockSpec((TM, D_MODEL), row), pl.BlockSpec((TM, D_MODEL), row)],
        out_shape=[jax.ShapeDtypeStruct((n, D_MODEL), F32), jax.ShapeDtypeStruct((n, D_MODEL), F32)],
        compiler_params=_cparams(("parallel",)),
        name="mix_proj",
    )(xc, xl, ac, al, fc, fl, mod, mod, mod, attn_g, four_g, w_out, norm2_g)


def _router_kernel(h2_ref, whi_ref, wlo_ref, br_ref, idx_ref, w_ref, cnt_ref):
    i = pl.program_id(0)
    h = h2_ref[...]
    h_hi = h.astype(BF16)
    h_lo = (h - h_hi.astype(F32)).astype(BF16)
    logits = (jnp.dot(h_hi, whi_ref[...], preferred_element_type=F32)
              + jnp.dot(h_lo, whi_ref[...], preferred_element_type=F32)
              + jnp.dot(h_hi, wlo_ref[...], preferred_element_type=F32)) + br_ref[...]
    lane = lax.broadcasted_iota(I32, logits.shape, 1)
    l = logits
    vals, idxs = [], []
    hits = jnp.zeros(logits.shape, F32)
    for _ in range(TOP_K):
        m = jnp.max(l, axis=-1, keepdims=True)
        ix = jnp.min(jnp.where(l == m, lane, LANES), axis=-1, keepdims=True)
        sel = lane == ix
        hits = hits + sel.astype(F32)
        l = jnp.where(sel, -jnp.inf, l)
        vals.append(m)
        idxs.append(ix)
    es = [jnp.exp(v - vals[0]) for v in vals]
    den = es[0] + es[1] + es[2] + es[3]
    idx_ref[...] = _pack_cols(idxs, lane)
    w_ref[...] = _pack_cols([e / den for e in es], lane)

    @pl.when(i == 0)
    def _():
        cnt_ref[...] = jnp.zeros(cnt_ref.shape, F32)

    cnt_ref[...] += jnp.sum(hits, axis=0, keepdims=True)


def _router(h2, w_hi, w_lo, br):
    n = h2.shape[0]
    row = lambda i: (i, 0)
    const = lambda i: (0, 0)
    return pl.pallas_call(
        _router_kernel,
        grid=(n // TM,),
        in_specs=[pl.BlockSpec((TM, D_MODEL), row), pl.BlockSpec((D_MODEL, LANES), const),
                  pl.BlockSpec((D_MODEL, LANES), const), pl.BlockSpec((1, LANES), const)],
        out_specs=[pl.BlockSpec((TM, LANES), row), pl.BlockSpec((TM, LANES), row), pl.BlockSpec((8, LANES), const)],
        out_shape=[jax.ShapeDtypeStruct((n, LANES), I32), jax.ShapeDtypeStruct((n, LANES), F32),
                   jax.ShapeDtypeStruct((8, LANES), F32)],
        compiler_params=_cparams(("arbitrary",)),
        name="router",
    )(h2, w_hi, w_lo, br)


def _pos_kernel(idx_ref, gs_ref, pos_ref, carry_ref):
    i = pl.program_id(0)

    @pl.when(i == 0)
    def _():
        carry_ref[...] = jnp.zeros(carry_ref.shape, F32)

    idx = idx_ref[...]
    lane = lax.broadcasted_iota(I32, idx.shape, 1)
    sels = [lane == idx[:, k:k + 1] for k in range(TOP_K)]
    hits = jnp.zeros(idx.shape, F32)
    for s in sels:
        hits = hits + s.astype(F32)
    r = lax.broadcasted_iota(I32, (TM, TM), 0)
    c = lax.broadcasted_iota(I32, (TM, TM), 1)
    before = (c < r).astype(BF16)
    rank = jnp.dot(before, hits.astype(BF16), preferred_element_type=F32) + carry_ref[0:1, :] + gs_ref[0:1, :]
    cols = [jnp.sum(jnp.where(s, rank, 0.0), axis=-1, keepdims=True) for s in sels]
    pos_ref[...] = _pack_cols(cols, lane).T[:8, :].astype(I32)
    carry_ref[...] += jnp.sum(hits, axis=0, keepdims=True)


def _positions(idx, gs_rows):
    n = idx.shape[0]
    return pl.pallas_call(
        _pos_kernel,
        grid=(n // TM,),
        in_specs=[pl.BlockSpec((TM, LANES), lambda i: (i, 0)), pl.BlockSpec((8, LANES), lambda i: (0, 0))],
        out_specs=pl.BlockSpec((8, TM), lambda i: (0, i)),
        out_shape=jax.ShapeDtypeStruct((8, n), I32),
        scratch_shapes=[pltpu.VMEM((8, LANES), F32)],
        compiler_params=_cparams(("arbitrary",)),
        name="positions",
    )(idx, gs_rows)


def _row_copy_wait(src, dst, sem):
    pltpu.make_async_copy(src, dst, sem).wait()


def _dispatch_kernel(ends_ref, pos_ref, h2_ref, xs_ref, zero_ref, sem):
    i = pl.program_id(0)

    @pl.when(i == 0)
    def _():
        zero_ref[...] = jnp.zeros(zero_ref.shape, F32)

        def last_tile(e):
            lo = ends_ref[e - 1] if e > 0 else 0
            start = pl.multiple_of(jnp.maximum(ends_ref[e] - TG, 0), TG)
            return ends_ref[e] > lo, pltpu.make_async_copy(zero_ref, xs_ref.at[pl.ds(start, TG), :], sem)

        for e in range(N_EXPERTS):
            nonempty, cp = last_tile(e)
            pl.when(nonempty)(cp.start)
        for e in range(N_EXPERTS):
            nonempty, cp = last_tile(e)
            pl.when(nonempty)(cp.wait)

        def tail_tile(j):
            return pltpu.make_async_copy(zero_ref, xs_ref.at[pl.ds(pl.multiple_of(j * TG, TG), TG), :], sem)

        first_unused = ends_ref[N_EXPERTS - 1] // TG
        n_tiles = xs_ref.shape[0] // TG
        lax.fori_loop(first_unused, n_tiles, lambda j, c: (tail_tile(j).start(), c)[1], 0)
        lax.fori_loop(first_unused, n_tiles, lambda j, c: (tail_tile(j).wait(), c)[1], 0)

    def issue(t, carry):
        for k in range(TOP_K):
            p = pos_ref[k, t]
            pltpu.make_async_copy(h2_ref.at[pl.ds(t, 1), :], xs_ref.at[pl.ds(p, 1), :], sem).start()
        return carry

    lax.fori_loop(0, TM, issue, 0)
    for k in range(TOP_K):
        _row_copy_wait(h2_ref, xs_ref.at[pl.ds(0, TM), :], sem)


def _dispatch(ends, pos, h2, n_rows):
    n = h2.shape[0]
    grid_spec = pltpu.PrefetchScalarGridSpec(
        num_scalar_prefetch=1,
        grid=(n // TM,),
        in_specs=[pl.BlockSpec((8, TM), lambda i, ends: (0, i), memory_space=pltpu.SMEM),
                  pl.BlockSpec((TM, D_MODEL), lambda i, ends: (i, 0))],
        out_specs=pl.BlockSpec(memory_space=pl.ANY),
        scratch_shapes=[pltpu.VMEM((TG, D_MODEL), F32), pltpu.SemaphoreType.DMA(())],
    )
    return pl.pallas_call(
        _dispatch_kernel,
        grid_spec=grid_spec,
        out_shape=jax.ShapeDtypeStruct((n_rows, D_MODEL), F32),
        compiler_params=_cparams(("arbitrary",)),
        name="dispatch",
    )(ends, pos, h2)


def _ffn_kernel(te_ref, nv_ref, par_ref, nxt_ref, x_ref, bg_ref, bl_ref, bdn_ref, perm_ref, wgu_hbm, wdn_hbm, o_ref,
                wgu_buf, wdn_buf, wg_s, wl_s, wd_s, wsem):
    i = pl.program_id(0)
    e = te_ref[i]
    valid = i < nv_ref[0]
    fresh = jnp.logical_or(i == 0, e != te_ref[jnp.maximum(i - 1, 0)])
    wslot = par_ref[i]

    def weight_copies(expert, s):
        return (pltpu.make_async_copy(wgu_hbm.at[expert], wgu_buf.at[s], wsem.at[s]),
                pltpu.make_async_copy(wdn_hbm.at[expert], wdn_buf.at[s], wsem.at[s]))

    @pl.when(i == 0)
    def _():
        for cp in weight_copies(e, wslot):
            cp.start()

    @pl.when(jnp.logical_and(valid, fresh))
    def _():
        for cp in weight_copies(e, wslot):
            cp.wait()
        for j in range(2 * D_EXPERT // (2 * LANES)):
            blk = wgu_buf[wslot, :, j * 2 * LANES:(j + 1) * 2 * LANES].astype(BF16)
            d = jnp.dot(blk, perm_ref[...], preferred_element_type=F32)
            wg_s[:, j * LANES:(j + 1) * LANES] = d[:, :LANES].astype(BF16)
            wl_s[:, j * LANES:(j + 1) * LANES] = d[:, LANES:].astype(BF16)
        wd_s[...] = wdn_buf[wslot].astype(BF16)

        @pl.when(nxt_ref[i] >= 0)
        def _():
            for cp in weight_copies(nxt_ref[i], 1 - wslot):
                cp.start()

    @pl.when(valid)
    def _():
        x = x_ref[...].astype(BF16)
        g = jnp.dot(x, wg_s[...], preferred_element_type=F32) + bg_ref[...]
        l = jnp.dot(x, wl_s[...], preferred_element_type=F32) + bl_ref[...]
        g = jnp.minimum(g, SWIGLU_LIMIT)
        l = jnp.clip(l, -SWIGLU_LIMIT, SWIGLU_LIMIT)
        a = (l + 1.0) * (g * (1.0 / (1.0 + jnp.exp(-SWIGLU_ALPHA * g))))
        _to_tiles(o_ref, jnp.dot(a.astype(BF16), wd_s[...], preferred_element_type=F32) + bdn_ref[...])

    @pl.when(jnp.logical_not(valid))
    def _():
        o_ref[...] = jnp.zeros(o_ref.shape, F32)


def _expert_ffn(tile_expert, n_valid, parity, next_expert, xs, w_gu, b_g, b_l, w_dn, b_dn, perm):
    p = xs.shape[0]
    nt = p // TG
    tile = lambda i, te, nv, *_: (jnp.minimum(i, nv[0] - 1), 0)
    ex3 = lambda i, te, *_: (te[i], 0, 0)
    grid_spec = pltpu.PrefetchScalarGridSpec(
        num_scalar_prefetch=4,
        grid=(nt,),
        in_specs=[pl.BlockSpec((TG, D_MODEL), tile),
                  pl.BlockSpec((None, 1, D_EXPERT), ex3),
                  pl.BlockSpec((None, 1, D_EXPERT), ex3),
                  pl.BlockSpec((None, 1, D_MODEL), ex3),
                  pl.BlockSpec((2 * LANES, 2 * LANES), lambda i, *_: (0, 0)),
                  pl.BlockSpec(memory_space=pl.ANY),
                  pl.BlockSpec(memory_space=pl.ANY)],
        out_specs=pl.BlockSpec((TG * TILE_ROWS, LANES), lambda i, *_: (i, 0)),
        scratch_shapes=[pltpu.VMEM((2, D_MODEL, 2 * D_EXPERT), F32), pltpu.VMEM((2, D_EXPERT, D_MODEL), F32),
                        pltpu.VMEM((D_MODEL, D_EXPERT), BF16), pltpu.VMEM((D_MODEL, D_EXPERT), BF16),
                        pltpu.VMEM((D_EXPERT, D_MODEL), BF16), pltpu.SemaphoreType.DMA((2,))],
    )
    return pl.pallas_call(
        _ffn_kernel,
        grid_spec=grid_spec,
        out_shape=jax.ShapeDtypeStruct((p * TILE_ROWS, LANES), F32),
        compiler_params=_cparams(("arbitrary",)),
        name="expert_ffn",
    )(tile_expert, n_valid, parity, next_expert, xs, b_g, b_l, b_dn, perm, w_gu, w_dn)


def _combine_kernel(p0_ref, p1_ref, p2_ref, p3_ref, w_ref, x1_ref, g2_ref, fg_ref, ys_ref, o_ref, buf, sem,
                    *, tile0, nc, tpb):
    i = pl.program_id(0) + tile0
    r = _mod_row(i, nc, tpb)
    batch = 8

    def tile_at(row):
        return pl.ds(pl.multiple_of(row * TILE_ROWS, TILE_ROWS), TILE_ROWS)

    def issue(tb, carry):
        t0 = pl.multiple_of(tb * batch, batch)
        for u in range(batch):
            for k, p_ref in enumerate((p0_ref, p1_ref, p2_ref, p3_ref)):
                pltpu.make_async_copy(ys_ref.at[tile_at(p_ref[t0 + u]), :], buf.at[k, tile_at(t0 + u), :], sem).start()
        return carry

    lax.fori_loop(0, TM // batch, issue, 0)
    for k in range(TOP_K):
        pltpu.make_async_copy(ys_ref.at[pl.ds(0, TM * TILE_ROWS), :], buf.at[k], sem).wait()
    w = w_ref[...]
    moe = w[:, 0:1] * _from_tiles(buf.at[0])
    for k in range(1, TOP_K):
        moe = moe + w[:, k:k + 1] * _from_tiles(buf.at[k])
    y = x1_ref[...] + g2_ref[pl.ds(r, 1), :] * moe
    o_ref[...] = _rms(y) * fg_ref[...]


def _combine(pos_flat, w, x1, mod, final_g, ys, *, tile0, ntiles, nc, tpb):
    nt = x1.shape[0] // TM
    slot_spec = lambda k: pl.BlockSpec((TM,), lambda i: (k * nt + tile0 + i,), memory_space=pltpu.SMEM)
    return pl.pallas_call(
        functools.partial(_combine_kernel, tile0=tile0, nc=nc, tpb=tpb),
        grid=(ntiles,),
        in_specs=[slot_spec(k) for k in range(TOP_K)] + [
                  pl.BlockSpec((TM, LANES), lambda i: (i + tile0, 0)),
                  pl.BlockSpec((TM, D_MODEL), lambda i: (i + tile0, 0)),
                  pl.BlockSpec((8, D_MODEL), lambda i: (0, 5)),
                  pl.BlockSpec((1, D_MODEL), lambda i: (0, 0)),
                  pl.BlockSpec(memory_space=pl.ANY)],
        out_specs=pl.BlockSpec((TM, D_MODEL), lambda i: (i, 0)),
        out_shape=jax.ShapeDtypeStruct((ntiles * TM, D_MODEL), F32),
        scratch_shapes=[pltpu.VMEM((TOP_K, TM * TILE_ROWS, LANES), F32), pltpu.SemaphoreType.DMA(())],
        compiler_params=_cparams(("arbitrary",)),
        name="combine",
    )(pos_flat, pos_flat, pos_flat, pos_flat, w, x1, mod, final_g, ys)


def kernel(x_prompt, x_sample, cache_k, cache_v, c, c_ctx, norm1_g, w_mod, b_mod, w_in, q_norm_g, k_norm_g,
           w_fourier, attn_out_g, fourier_out_g, w_out, norm2_g, w_router, b_router, w_gate_up, b_gate_up,
           w_down, b_down, final_g):
    bc, tc, _ = x_prompt.shape
    bl, tl, _ = x_sample.shape
    depth = w_in.shape[0]
    past = cache_k.shape[2]
    assert depth == 1 and tl % TQ == 0 and tl % TM == 0 and (bc * tc) % TM == 0
    n_ctx, n_lat = bc * tc, bl * tl
    n_tok = n_ctx + n_lat
    nc, nl, tpb = n_ctx // TM, n_lat // TM, tl // TM

    xc = x_prompt.reshape(n_ctx, D_MODEL)
    xl = x_sample.reshape(n_lat, D_MODEL)

    c_rows = jnp.concatenate([c_ctx[None, :], c, jnp.zeros((8 - 1 - bl, D_MODEL), F32)], axis=0)
    mod = _modulation(c_rows, w_mod[0], b_mod[0])

    cos_t, sin_t = _rope_tables(tl)
    q, k, v, f, k_new, v_new = _pre(
        xc, xl, mod, norm1_g, w_in[0].astype(BF16), jnp.tile(q_norm_g, (1, N_HEADS)),
        jnp.tile(k_norm_g, (1, N_KV_HEADS)), _head_mean_matrix(), cos_t, sin_t, nc=nc, nl=nl, tpb=tpb)

    kc, vtc = _split_kv_heads(k[:n_ctx].reshape(bc, tc, KV_W), v[:n_ctx].reshape(bc, tc, KV_W))
    attn_c = _attention(q, kc, vtc, batch=bc, t=tc, s=tc, q_row0=0)
    keys = jnp.concatenate([cache_k[:, 0].reshape(bl, past, KV_W).astype(BF16),
                            k[n_ctx:].reshape(bl, tl, KV_W)], axis=1)
    vals = jnp.concatenate([cache_v[:, 0].reshape(bl, past, KV_W).astype(BF16),
                            v[n_ctx:].reshape(bl, tl, KV_W)], axis=1)
    kl, vtl = _split_kv_heads(keys, vals)
    attn_l = _attention(q, kl, vtl, batch=bl, t=tl, s=past + tl, q_row0=n_ctx)

    wf = w_fourier[0].astype(BF16)
    zero = jnp.zeros_like(wf[0])
    wf_pairs = jnp.stack([jnp.block([[wf[0], zero], [zero, wf[1]]]), jnp.block([[wf[2], zero], [zero, wf[3]]])])
    four_c = _fourier(f, wf_pairs, batch=bc, t=tc, row0=0, n1=1)
    four_l = _fourier(f, wf_pairs, batch=bl, t=tl, row0=n_ctx, n1=FFT_N1)

    x1, h2 = _mix(xc, xl, attn_c, attn_l, four_c, four_l, mod, attn_out_g, fourier_out_g,
                  w_out[0].astype(BF16), norm2_g, nc=nc, nl=nl, tpb=tpb)

    wr = jnp.pad(w_router[0], ((0, 0), (0, LANES - N_EXPERTS)))
    wr_hi = wr.astype(BF16)
    wr_lo = (wr - wr_hi.astype(F32)).astype(BF16)
    br = jnp.pad(b_router[0], (0, LANES - N_EXPERTS), constant_values=-1e30).reshape(1, LANES)
    idx, gate_w, counts = _router(h2, wr_hi, wr_lo, br)
    cnt = counts[0, :N_EXPERTS].astype(I32)
    padded = ((cnt + TG - 1) // TG) * TG
    ends = jnp.cumsum(padded)
    starts = ends - padded
    n_rows = n_tok * TOP_K + N_EXPERTS * TG
    n_tiles = n_rows // TG
    tile_ids = jnp.arange(n_tiles, dtype=I32)
    tile_expert = jnp.minimum(jnp.sum((ends[None, :] // TG <= tile_ids[:, None]).astype(I32), axis=1),
                              N_EXPERTS - 1)
    n_valid = (ends[N_EXPERTS - 1] // TG).reshape(1)
    gs_rows = jnp.broadcast_to(jnp.pad(starts.astype(F32), (0, LANES - N_EXPERTS))[None, :], (8, LANES))
    pos = _positions(idx, gs_rows)

    xs = _dispatch(ends, pos, h2, n_rows)
    fresh_tile = jnp.concatenate([jnp.ones((1,), I32), (tile_expert[1:] != tile_expert[:-1]).astype(I32)])
    parity = (jnp.cumsum(fresh_tile) - 1) % 2
    experts = jnp.arange(N_EXPERTS, dtype=I32)
    later = jnp.logical_and(padded[None, :] > 0, experts[None, :] > experts[:, None])
    next_of = jnp.min(jnp.where(later, experts[None, :], N_EXPERTS), axis=1)
    next_of = jnp.where(next_of == N_EXPERTS, -1, next_of)
    next_expert = jnp.sum(jnp.where(tile_expert[:, None] == experts[None, :], next_of[None, :], 0), axis=1)
    b_gu = b_gate_up[0].reshape(N_EXPERTS, 1, D_EXPERT, 2)
    ys = _expert_ffn(tile_expert, n_valid, parity.astype(I32), next_expert.astype(I32), xs, w_gate_up[0],
                     b_gu[..., 0], b_gu[..., 1], w_down[0], b_down[0].reshape(N_EXPERTS, 1, D_MODEL),
                     _deinterleave_matrix())

    pos_flat = pos[:TOP_K].reshape(-1)
    y_c = _combine(pos_flat, gate_w, x1, mod, final_g.reshape(1, -1), ys, tile0=0, ntiles=nc, nc=nc, tpb=tpb)
    y_l = _combine(pos_flat, gate_w, x1, mod, final_g.reshape(1, -1), ys, tile0=nc, ntiles=nl, nc=nc, tpb=tpb)

    new_k = k_new[:n_ctx].reshape(bc, 1, tc, N_KV_HEADS, HEAD_DIM)
    new_v = v_new[:n_ctx].reshape(bc, 1, tc, N_KV_HEADS, HEAD_DIM)
    return (y_c.reshape(bc, tc, D_MODEL), y_l.reshape(bl, tl, D_MODEL), new_k, new_v)
```

```python
import functools
import math

import numpy as np
import jax
import jax.numpy as jnp
from jax import lax
from jax.experimental import pallas as pl
from jax.experimental.pallas import tpu as pltpu

F32 = jnp.float32
BF16 = jnp.bfloat16
I32 = jnp.int32

D_MODEL = 1024
HEAD_DIM = 64
N_HEADS = 8
N_KV_HEADS = 2
HEADS_PER_KV = N_HEADS // N_KV_HEADS
ATTN_W = N_HEADS * HEAD_DIM
KV_W = N_KV_HEADS * HEAD_DIM
FOURIER_W = 512
FOURIER_GROUP_W = 128
D_IN = ATTN_W + 2 * KV_W + FOURIER_W
N_EXPERTS = 32
TOP_K = 4
D_EXPERT = 1024
SWIGLU_ALPHA = 1.702
SWIGLU_LIMIT = 7.0
ROPE_THETA = 10000.0
ROT_PAIRS = HEAD_DIM // 4
GRID_W = 64
EPS = 1e-6

LANES = 128
TM = 512
TQ = 512
TG = 256
FFT_N1 = 4
V_ROWS = 80
TILE_ROWS = D_MODEL // LANES
VMEM_LIMIT = 56 * 1024 * 1024


def _cparams(sem, vmem=VMEM_LIMIT):
    return pltpu.CompilerParams(dimension_semantics=sem, vmem_limit_bytes=vmem)


def _rope_tables(t_lat):
    pos = np.arange(t_lat)
    row = (pos // GRID_W).astype(np.float64)
    col = (pos % GRID_W).astype(np.float64)
    inv = ROPE_THETA ** (-np.arange(ROT_PAIRS, dtype=np.float64) / ROT_PAIRS)
    lane = np.arange(LANES)
    d = lane % HEAD_DIM
    axis = d // (2 * ROT_PAIRS)
    second = (d // ROT_PAIRS) % 2
    p = d % ROT_PAIRS
    ang = np.where(axis[None, :] == 0, row[:, None], col[:, None]) * inv[p][None, :]
    cos = np.cos(ang)
    sin = np.sin(ang) * np.where(second == 0, -1.0, 1.0)[None, :]
    cos = np.concatenate([np.ones((TM, LANES)), cos], axis=0)
    sin = np.concatenate([np.zeros((TM, LANES)), sin], axis=0)
    return jnp.asarray(cos, F32), jnp.asarray(sin, F32)


def _dft_tables(n1, n2):
    c = np.arange(FOURIER_GROUP_W)
    ang = 2.0 * np.pi * np.outer(c, c) / FOURIER_GROUP_W
    z = np.zeros_like(ang)
    cbd = np.block([[np.cos(ang), z], [z, np.cos(ang)]])
    sbd = np.block([[np.sin(ang), z], [z, np.sin(ang)]])
    cd = np.concatenate([cbd, -sbd], axis=1)
    u = np.arange(n2)
    ang2 = 2.0 * np.pi * np.outer(u, u) / n2
    ct, st = np.cos(ang2), np.sin(ang2)
    nt = max(n1 - 1, 1)
    tw_ang = 2.0 * np.pi * np.outer(np.arange(1, nt + 1), u) / (n1 * n2)
    twc = np.repeat(np.cos(tw_ang)[:, :, None], LANES, axis=2)
    tws = np.repeat(np.sin(tw_ang)[:, :, None], LANES, axis=2)
    return (jnp.asarray(cd, F32), jnp.asarray(ct, F32), jnp.asarray(st, F32),
            jnp.asarray(twc, F32), jnp.asarray(tws, F32))


def _head_mean_matrix():
    h = np.arange(ATTN_W) // HEAD_DIM
    return jnp.asarray((h[:, None] == h[None, :]) / HEAD_DIM, BF16)


def _deinterleave_matrix():
    p = np.zeros((2 * LANES, 2 * LANES))
    m = np.arange(LANES)
    p[2 * m, m] = 1.0
    p[2 * m + 1, LANES + m] = 1.0
    return jnp.asarray(p, BF16)


def _mod_row(i, nc, tpb):
    return jnp.where(i < nc, 0, 1 + (i - nc) // tpb)


def _rms(x):
    return x * lax.rsqrt(jnp.mean(x * x, axis=-1, keepdims=True) + EPS)


def _pack_cols(cols, lane):
    out = jnp.zeros(lane.shape, cols[0].dtype)
    for k, c in enumerate(cols):
        out = jnp.where(lane == k, c, out)
    return out


def _to_tiles(ref, x):
    r = x.shape[0]
    for s in range(TILE_ROWS):
        ref[pl.ds(s, r, stride=TILE_ROWS), :] = x[:, s * LANES:(s + 1) * LANES]


def _from_tiles(ref):
    r = ref.shape[0] // TILE_ROWS
    return jnp.concatenate([ref[pl.ds(s, r, stride=TILE_ROWS), :] for s in range(TILE_ROWS)], axis=1)


def _mod_kernel(c_ref, w_ref, b_ref, o_ref):
    c = c_ref[...]
    s = c * (1.0 / (1.0 + jnp.exp(-c)))
    o_ref[...] = jnp.dot(s.astype(BF16), w_ref[...].astype(BF16), preferred_element_type=F32) + b_ref[...]


def _modulation(c_rows, w_mod, b_mod):
    n = w_mod.shape[1] // D_MODEL
    return pl.pallas_call(
        _mod_kernel,
        grid=(n,),
        in_specs=[pl.BlockSpec((8, D_MODEL), lambda j: (0, 0)),
                  pl.BlockSpec((D_MODEL, D_MODEL), lambda j: (0, j)),
                  pl.BlockSpec((1, D_MODEL), lambda j: (0, j))],
        out_specs=pl.BlockSpec((8, D_MODEL), lambda j: (0, j)),
        out_shape=jax.ShapeDtypeStruct((8, w_mod.shape[1]), F32),
        compiler_params=_cparams(("parallel",)),
        name="modulation",
    )(c_rows, w_mod, b_mod.reshape(1, -1))


def _pre_kernel(xc_ref, xl_ref, sh_ref, sc_ref, g1_ref, win_ref, qg_ref, kg_ref, bd_ref, cos_ref, sin_ref,
                q_ref, k_ref, vt_ref, f_ref, kn_ref, vn_ref, *, nc, tpb):
    i = pl.program_id(0)
    r = _mod_row(i, nc, tpb)
    x = jnp.where(i < nc, xc_ref[...], xl_ref[...])
    h = _rms(x) * g1_ref[...]
    h = h * (1.0 + sc_ref[pl.ds(r, 1), :]) + sh_ref[pl.ds(r, 1), :]
    proj = jnp.dot(h.astype(BF16), win_ref[...], preferred_element_type=F32)
    q = proj[:, :ATTN_W]
    k = proj[:, ATTN_W:ATTN_W + KV_W]
    v = proj[:, ATTN_W + KV_W:ATTN_W + 2 * KV_W]
    f_ref[...] = proj[:, ATTN_W + 2 * KV_W:].astype(BF16)
    vn_ref[...] = v
    vt = v.T
    tail = (lax.broadcasted_iota(I32, (V_ROWS - HEAD_DIM, vt.shape[1]), 0) == 0).astype(BF16)
    for j in range(N_KV_HEADS):
        vt_ref[j, :HEAD_DIM, :] = vt[j * HEAD_DIM:(j + 1) * HEAD_DIM, :].astype(BF16)
        vt_ref[j, HEAD_DIM:, :] = tail
    q_ms = jnp.dot((q * q).astype(BF16), bd_ref[...], preferred_element_type=F32)
    k_ms = jnp.dot((k * k).astype(BF16), bd_ref[:KV_W, :KV_W], preferred_element_type=F32)
    qn = q * lax.rsqrt(q_ms + EPS) * qg_ref[...]
    kn = k * lax.rsqrt(k_ms + EPS) * kg_ref[...]
    kn_ref[...] = kn
    cos = cos_ref[...]
    sin = sin_ref[...]
    lane = lax.broadcasted_iota(I32, cos.shape, 1)
    first = (lane & ROT_PAIRS) == 0

    def rope(c):
        partner = jnp.where(first, pltpu.roll(c, LANES - ROT_PAIRS, 1), pltpu.roll(c, ROT_PAIRS, 1))
        return c * cos + partner * sin

    k_rot = rope(kn).astype(BF16)
    for j in range(N_KV_HEADS):
        k_ref[j] = k_rot[:, j * HEAD_DIM:(j + 1) * HEAD_DIM]
    scale = HEAD_DIM ** -0.5 * math.log2(math.e)
    q_rot = jnp.concatenate([rope(qn[:, j * LANES:(j + 1) * LANES]) for j in range(ATTN_W // LANES)], axis=1)
    q_ref[...] = (q_rot * scale).T.astype(BF16)


def _pre(xc, xl, mod, norm1_g, w_in, q_g, k_g, bd, cos_t, sin_t, *, nc, nl, tpb):
    n = (nc + nl) * TM
    row = lambda i: (i, 0)
    const = lambda i: (0, 0)
    tab = lambda i: (jnp.where(i < nc, 0, 1 + (i - nc) % tpb), 0)
    return pl.pallas_call(
        functools.partial(_pre_kernel, nc=nc, tpb=tpb),
        grid=(nc + nl,),
        in_specs=[pl.BlockSpec((TM, D_MODEL), lambda i: (jnp.minimum(i, nc - 1), 0)),
                  pl.BlockSpec((TM, D_MODEL), lambda i: (jnp.maximum(i - nc, 0), 0)),
                  pl.BlockSpec((8, D_MODEL), lambda i: (0, 0)),
                  pl.BlockSpec((8, D_MODEL), lambda i: (0, 1)),
                  pl.BlockSpec((1, D_MODEL), const),
                  pl.BlockSpec((D_MODEL, D_IN), const),
                  pl.BlockSpec((1, ATTN_W), const),
                  pl.BlockSpec((1, KV_W), const),
                  pl.BlockSpec((ATTN_W, ATTN_W), const),
                  pl.BlockSpec((TM, LANES), tab),
                  pl.BlockSpec((TM, LANES), tab)],
        out_specs=[pl.BlockSpec((ATTN_W, TM), lambda i: (0, i)),
                   pl.BlockSpec((N_KV_HEADS, TM, HEAD_DIM), lambda i: (0, i, 0)),
                   pl.BlockSpec((N_KV_HEADS, V_ROWS, TM), lambda i: (0, 0, i)),
                   pl.BlockSpec((TM, FOURIER_W), row), pl.BlockSpec((TM, KV_W), row), pl.BlockSpec((TM, KV_W), row)],
        out_shape=[jax.ShapeDtypeStruct((ATTN_W, n), BF16), jax.ShapeDtypeStruct((N_KV_HEADS, n, HEAD_DIM), BF16),
                   jax.ShapeDtypeStruct((N_KV_HEADS, V_ROWS, n), BF16), jax.ShapeDtypeStruct((n, FOURIER_W), BF16),
                   jax.ShapeDtypeStruct((n, KV_W), F32), jax.ShapeDtypeStruct((n, KV_W), F32)],
        compiler_params=_cparams(("parallel",)),
        name="pre_proj",
    )(xc, xl, mod, mod, norm1_g, w_in, q_g, k_g, bd, cos_t, sin_t)


def _attn_kernel(qt_ref, k_ref, vt_ref, o_ref, ot_ref, st_ref, pt_ref):
    def scores(h):
        qt = qt_ref[h * HEAD_DIM:(h + 1) * HEAD_DIM, :]
        st_ref[h % 2] = jnp.dot(k_ref[h // HEADS_PER_KV], qt, preferred_element_type=F32)

    scores(0)
    for h in range(N_HEADS):
        if h + 1 < N_HEADS:
            scores(h + 1)
        st = st_ref[h % 2]
        pt_ref[h % 2] = jnp.exp2(st - jnp.max(st, axis=0, keepdims=True)).astype(BF16)
        ot = jnp.dot(vt_ref[h // HEADS_PER_KV], pt_ref[h % 2], preferred_element_type=F32)
        ot_ref[h * HEAD_DIM:(h + 1) * HEAD_DIM, :] = ot[:HEAD_DIM] * (1.0 / ot[HEAD_DIM:HEAD_DIM + 1])
    o_ref[...] = ot_ref[...].T


def _attention(qt, k_heads, vt_ext, *, batch, t, s, q_row0):
    tq = min(TQ, t)
    qpb = t // tq
    q0 = q_row0 // tq
    return pl.pallas_call(
        _attn_kernel,
        grid=(batch, qpb),
        in_specs=[pl.BlockSpec((ATTN_W, tq), lambda b, i: (0, q0 + b * qpb + i)),
                  pl.BlockSpec((N_KV_HEADS, s, HEAD_DIM), lambda b, i: (0, b, 0)),
                  pl.BlockSpec((N_KV_HEADS, V_ROWS, s), lambda b, i: (0, 0, b))],
        out_specs=pl.BlockSpec((tq, ATTN_W), lambda b, i: (b * qpb + i, 0)),
        out_shape=jax.ShapeDtypeStruct((batch * t, ATTN_W), F32),
        scratch_shapes=[pltpu.VMEM((ATTN_W, tq), F32), pltpu.VMEM((2, s, tq), F32), pltpu.VMEM((2, s, tq), BF16)],
        compiler_params=_cparams(("parallel", "parallel")),
        name="attention",
    )(qt, k_heads, vt_ext)


def _with_cache(k_heads, vt_ext, cache_k, cache_v, n_lat, batch, t):
    past = cache_k.shape[1]
    ck = cache_k.astype(BF16).transpose(2, 0, 1, 3)
    cv = cache_v.astype(BF16).transpose(2, 3, 0, 1)
    tail = jnp.concatenate([jnp.ones((N_KV_HEADS, 1, batch, past), BF16),
                            jnp.zeros((N_KV_HEADS, V_ROWS - HEAD_DIM - 1, batch, past), BF16)], axis=1)
    k_lat = k_heads[:, -n_lat:].reshape(N_KV_HEADS, batch, t, HEAD_DIM)
    v_lat = vt_ext[:, :, -n_lat:].reshape(N_KV_HEADS, V_ROWS, batch, t)
    keys = jnp.concatenate([ck, k_lat], axis=2).reshape(N_KV_HEADS, batch * (past + t), HEAD_DIM)
    vals = jnp.concatenate([jnp.concatenate([cv, tail], axis=1), v_lat], axis=3)
    return keys, vals.reshape(N_KV_HEADS, V_ROWS, batch * (past + t))


def _fourier_kernel(x_ref, cd_ref, ct_ref, st_ref, twc_ref, tws_ref, wf_ref, o_ref, *, n1, n2):
    pw = 2 * FOURIER_GROUP_W
    scale = 1.0 / math.sqrt(n1 * n2 * FOURIER_GROUP_W)
    for gp in range(FOURIER_W // pw):
        z = jnp.dot(x_ref[:, gp * pw:(gp + 1) * pw], cd_ref[...], preferred_element_type=F32)
        zr = [z[t1 * n2:(t1 + 1) * n2, :pw] for t1 in range(n1)]
        zi = [z[t1 * n2:(t1 + 1) * n2, pw:] for t1 in range(n1)]
        if n1 == 1:
            a = [(zr[0], zi[0])]
        else:
            a = [(zr[0] + zr[1] + zr[2] + zr[3], zi[0] + zi[1] + zi[2] + zi[3]),
                 (zr[0] + zi[1] - zr[2] - zi[3], zi[0] - zr[1] - zi[2] + zr[3]),
                 (zr[0] - zr[1] + zr[2] - zr[3], zi[0] - zi[1] + zi[2] - zi[3]),
                 (zr[0] - zi[1] - zr[2] + zi[3], zi[0] + zr[1] - zi[2] - zr[3])]
        for u1 in range(n1):
            ar, ai = a[u1]
            if u1 > 0:
                c = jnp.concatenate([twc_ref[u1 - 1]] * 2, axis=1)
                s = jnp.concatenate([tws_ref[u1 - 1]] * 2, axis=1)
                ar, ai = ar * c + ai * s, ai * c - ar * s
            y = (jnp.dot(ct_ref[...], ar.astype(BF16), preferred_element_type=F32)
                 + jnp.dot(st_ref[...], ai.astype(BF16), preferred_element_type=F32)) * scale
            y = jnp.dot(y.astype(BF16), wf_ref[gp], preferred_element_type=F32)
            rows = pl.ds(u1, n2, stride=n1) if n1 > 1 else slice(None)
            for half in range(2):
                o_ref[2 * gp + half, rows, :] = y[:, half * FOURIER_GROUP_W:(half + 1) * FOURIER_GROUP_W]


def _fourier(f, wf_pairs, *, batch, t, row0, n1):
    n2 = t // n1
    cd, ct, st, twc, tws = _dft_tables(n1, n2)
    cd, ct, st = cd.astype(BF16), ct.astype(BF16), st.astype(BF16)
    nt = twc.shape[0]
    b0 = row0 // t
    c2 = lambda b: (0, 0)
    c3 = lambda b: (0, 0, 0)
    return pl.pallas_call(
        functools.partial(_fourier_kernel, n1=n1, n2=n2),
        grid=(batch,),
        in_specs=[pl.BlockSpec((t, FOURIER_W), lambda b: (b0 + b, 0)),
                  pl.BlockSpec(cd.shape, c2), pl.BlockSpec((n2, n2), c2), pl.BlockSpec((n2, n2), c2),
                  pl.BlockSpec((nt, n2, LANES), c3), pl.BlockSpec((nt, n2, LANES), c3),
                  pl.BlockSpec(wf_pairs.shape, c3)],
        out_specs=pl.BlockSpec((FOURIER_W // FOURIER_GROUP_W, t, FOURIER_GROUP_W), lambda b: (0, b, 0)),
        out_shape=jax.ShapeDtypeStruct((FOURIER_W // FOURIER_GROUP_W, batch * t, FOURIER_GROUP_W), F32),
        compiler_params=_cparams(("parallel",)),
        name="fourier",
    )(f, cd, ct, st, twc, tws, wf_pairs)


def _route(i, h, whi_ref, wlo_ref, br_ref, idx_ref, w_ref, cnt_ref):
    h_hi = h.astype(BF16)
    h_lo = (h - h_hi.astype(F32)).astype(BF16)
    logits = (jnp.dot(h_hi, whi_ref[...], preferred_element_type=F32)
              + jnp.dot(h_lo, whi_ref[...], preferred_element_type=F32)
              + jnp.dot(h_hi, wlo_ref[...], preferred_element_type=F32)) + br_ref[...]
    lane = lax.broadcasted_iota(I32, logits.shape, 1)
    l = logits
    vals, idxs = [], []
    hits = jnp.zeros(logits.shape, F32)
    for _ in range(TOP_K):
        m = jnp.max(l, axis=-1, keepdims=True)
        ix = jnp.min(jnp.where(l == m, lane, LANES), axis=-1, keepdims=True)
        sel = lane == ix
        hits = hits + sel.astype(F32)
        l = jnp.where(sel, -jnp.inf, l)
        vals.append(m)
        idxs.append(ix)
    es = [jnp.exp(v - vals[0]) for v in vals]
    den = es[0] + es[1] + es[2] + es[3]
    idx_ref[...] = _pack_cols(idxs, lane)
    w_ref[...] = _pack_cols([e / den for e in es], lane)

    @pl.when(i == 0)
    def _():
        cnt_ref[...] = jnp.zeros(cnt_ref.shape, F32)

    cnt_ref[...] += jnp.sum(hits, axis=0, keepdims=True)


def _mix_kernel(xc_ref, xl_ref, ac_ref, al_ref, fc_ref, fl_ref, g1_ref, sh2_ref, sc2_ref, ag_ref, fg_ref,
                wout_ref, n2g_ref, whi_ref, wlo_ref, br_ref, x1_ref, h2_ref, idx_ref, w_ref, cnt_ref, *, nc, tpb):
    i = pl.program_id(0)
    r = _mod_row(i, nc, tpb)
    is_ctx = i < nc
    x = jnp.where(is_ctx, xc_ref[...], xl_ref[...])
    a = _rms(jnp.where(is_ctx, ac_ref[...], al_ref[...])) * ag_ref[...]
    n_groups = FOURIER_W // FOURIER_GROUP_W
    f = jnp.concatenate([jnp.where(is_ctx, fc_ref[g], fl_ref[g]) for g in range(n_groups)], axis=1)
    f = _rms(f) * fg_ref[...]
    mixed = (jnp.dot(a.astype(BF16), wout_ref[:ATTN_W, :], preferred_element_type=F32)
             + jnp.dot(f.astype(BF16), wout_ref[ATTN_W:, :], preferred_element_type=F32))
    x1 = x + g1_ref[pl.ds(r, 1), :] * mixed
    x1_ref[...] = x1
    h2 = _rms(x1) * n2g_ref[...]
    h2 = h2 * (1.0 + sc2_ref[pl.ds(r, 1), :]) + sh2_ref[pl.ds(r, 1), :]
    h2_ref[...] = h2
    _route(i, h2, whi_ref, wlo_ref, br_ref, idx_ref, w_ref, cnt_ref)


def _mix(xc, xl, ac, al, fc, fl, mod, attn_g, four_g, w_out, norm2_g, wr_hi, wr_lo, br, *, nc, nl, tpb):
    n = (nc + nl) * TM
    lo = lambda i: (jnp.minimum(i, nc - 1), 0)
    hi = lambda i: (jnp.maximum(i - nc, 0), 0)
    const = lambda i: (0, 0)
    row = lambda i: (i, 0)
    return pl.pallas_call(
        functools.partial(_mix_kernel, nc=nc, tpb=tpb),
        grid=(nc + nl,),
        in_specs=[pl.BlockSpec((TM, D_MODEL), lo), pl.BlockSpec((TM, D_MODEL), hi),
                  pl.BlockSpec((TM, ATTN_W), lo), pl.BlockSpec((TM, ATTN_W), hi),
                  pl.BlockSpec((FOURIER_W // FOURIER_GROUP_W, TM, FOURIER_GROUP_W),
                               lambda i: (0, jnp.minimum(i, nc - 1), 0)),
                  pl.BlockSpec((FOURIER_W // FOURIER_GROUP_W, TM, FOURIER_GROUP_W),
                               lambda i: (0, jnp.maximum(i - nc, 0), 0)),
                  pl.BlockSpec((8, D_MODEL), lambda i: (0, 2)),
                  pl.BlockSpec((8, D_MODEL), lambda i: (0, 3)),
                  pl.BlockSpec((8, D_MODEL), lambda i: (0, 4)),
                  pl.BlockSpec((1, ATTN_W), const), pl.BlockSpec((1, FOURIER_W), const),
                  pl.BlockSpec((D_MODEL, D_MODEL), const), pl.BlockSpec((1, D_MODEL), const),
                  pl.BlockSpec((D_MODEL, LANES), const), pl.BlockSpec((D_MODEL, LANES), const),
                  pl.BlockSpec((1, LANES), const)],
        out_specs=[pl.BlockSpec((TM, D_MODEL), row), pl.BlockSpec((TM, D_MODEL), row),
                   pl.BlockSpec((TM, LANES), row), pl.BlockSpec((TM, LANES), row), pl.BlockSpec((8, LANES), const)],
        out_shape=[jax.ShapeDtypeStruct((n, D_MODEL), F32), jax.ShapeDtypeStruct((n, D_MODEL), F32),
                   jax.ShapeDtypeStruct((n, LANES), I32), jax.ShapeDtypeStruct((n, LANES), F32),
                   jax.ShapeDtypeStruct((8, LANES), F32)],
        compiler_params=_cparams(("arbitrary",)),
        name="mix_proj",
    )(xc, xl, ac, al, fc, fl, mod, mod, mod, attn_g, four_g, w_out, norm2_g, wr_hi, wr_lo, br)


def _pos_kernel(idx_ref, gs_ref, pos_ref, carry_ref):
    i = pl.program_id(0)

    @pl.when(i == 0)
    def _():
        carry_ref[...] = jnp.zeros(carry_ref.shape, F32)

    idx = idx_ref[...]
    lane = lax.broadcasted_iota(I32, idx.shape, 1)
    sels = [lane == idx[:, k:k + 1] for k in range(TOP_K)]
    hits = jnp.zeros(idx.shape, F32)
    for s in sels:
        hits = hits + s.astype(F32)
    r = lax.broadcasted_iota(I32, (TM, TM), 0)
    c = lax.broadcasted_iota(I32, (TM, TM), 1)
    before = (c < r).astype(BF16)
    rank = jnp.dot(before, hits.astype(BF16), preferred_element_type=F32) + carry_ref[0:1, :] + gs_ref[0:1, :]
    cols = [jnp.sum(jnp.where(s, rank, 0.0), axis=-1, keepdims=True) for s in sels]
    pos_ref[...] = _pack_cols(cols, lane).T[:8, :].astype(I32)
    carry_ref[...] += jnp.sum(hits, axis=0, keepdims=True)


def _positions(idx, gs_rows):
    n = idx.shape[0]
    return pl.pallas_call(
        _pos_kernel,
        grid=(n // TM,),
        in_specs=[pl.BlockSpec((TM, LANES), lambda i: (i, 0)), pl.BlockSpec((8, LANES), lambda i: (0, 0))],
        out_specs=pl.BlockSpec((8, TM), lambda i: (0, i)),
        out_shape=jax.ShapeDtypeStruct((8, n), I32),
        scratch_shapes=[pltpu.VMEM((8, LANES), F32)],
        compiler_params=_cparams(("arbitrary",)),
        name="positions",
    )(idx, gs_rows)


def _row_copy_wait(src, dst, sem):
    pltpu.make_async_copy(src, dst, sem).wait()


def _dispatch_kernel(ends_ref, pos_ref, h2_ref, xs_ref, zero_ref, sem):
    i = pl.program_id(0)

    @pl.when(i == 0)
    def _():
        zero_ref[...] = jnp.zeros(zero_ref.shape, F32)

        def last_tile(e):
            lo = ends_ref[e - 1] if e > 0 else 0
            start = pl.multiple_of(jnp.maximum(ends_ref[e] - TG, 0), TG)
            return ends_ref[e] > lo, pltpu.make_async_copy(zero_ref, xs_ref.at[pl.ds(start, TG), :], sem)

        for e in range(N_EXPERTS):
            nonempty, cp = last_tile(e)
            pl.when(nonempty)(cp.start)
        for e in range(N_EXPERTS):
            nonempty, cp = last_tile(e)
            pl.when(nonempty)(cp.wait)

        def tail_tile(j):
            return pltpu.make_async_copy(zero_ref, xs_ref.at[pl.ds(pl.multiple_of(j * TG, TG), TG), :], sem)

        first_unused = ends_ref[N_EXPERTS - 1] // TG
        n_tiles = xs_ref.shape[0] // TG
        lax.fori_loop(first_unused, n_tiles, lambda j, c: (tail_tile(j).start(), c)[1], 0)
        lax.fori_loop(first_unused, n_tiles, lambda j, c: (tail_tile(j).wait(), c)[1], 0)

    def issue(t, carry):
        for k in range(TOP_K):
            p = pos_ref[k, t]
            pltpu.make_async_copy(h2_ref.at[pl.ds(t, 1), :], xs_ref.at[pl.ds(p, 1), :], sem).start()
        return carry

    lax.fori_loop(0, TM, issue, 0)
    for k in range(TOP_K):
        _row_copy_wait(h2_ref, xs_ref.at[pl.ds(0, TM), :], sem)


def _dispatch(ends, pos, h2, n_rows):
    n = h2.shape[0]
    grid_spec = pltpu.PrefetchScalarGridSpec(
        num_scalar_prefetch=1,
        grid=(n // TM,),
        in_specs=[pl.BlockSpec((8, TM), lambda i, ends: (0, i), memory_space=pltpu.SMEM),
                  pl.BlockSpec((TM, D_MODEL), lambda i, ends: (i, 0))],
        out_specs=pl.BlockSpec(memory_space=pl.ANY),
        scratch_shapes=[pltpu.VMEM((TG, D_MODEL), F32), pltpu.SemaphoreType.DMA(())],
    )
    return pl.pallas_call(
        _dispatch_kernel,
        grid_spec=grid_spec,
        out_shape=jax.ShapeDtypeStruct((n_rows, D_MODEL), F32),
        compiler_params=_cparams(("arbitrary",)),
        name="dispatch",
    )(ends, pos, h2)


def _ffn_kernel(te_ref, nv_ref, par_ref, nxt_ref, x_ref, bg_ref, bl_ref, bdn_ref, perm_ref, wgu_hbm, wdn_hbm, o_ref,
                wgu_buf, wdn_buf, wg_s, wl_s, wd_s, wsem):
    i = pl.program_id(0)
    e = te_ref[i]
    valid = i < nv_ref[0]
    fresh = jnp.logical_or(i == 0, e != te_ref[jnp.maximum(i - 1, 0)])
    wslot = par_ref[i]

    def weight_copies(expert, s):
        return (pltpu.make_async_copy(wgu_hbm.at[expert], wgu_buf.at[s], wsem.at[s]),
                pltpu.make_async_copy(wdn_hbm.at[expert], wdn_buf.at[s], wsem.at[s]))

    @pl.when(i == 0)
    def _():
        for cp in weight_copies(e, wslot):
            cp.start()

    @pl.when(jnp.logical_and(valid, fresh))
    def _():
        for cp in weight_copies(e, wslot):
            cp.wait()
        for j in range(2 * D_EXPERT // (2 * LANES)):
            blk = wgu_buf[wslot, :, j * 2 * LANES:(j + 1) * 2 * LANES].astype(BF16)
            d = jnp.dot(blk, perm_ref[...], preferred_element_type=F32)
            wg_s[:, j * LANES:(j + 1) * LANES] = d[:, :LANES].astype(BF16)
            wl_s[:, j * LANES:(j + 1) * LANES] = d[:, LANES:].astype(BF16)
        wd_s[...] = wdn_buf[wslot].astype(BF16)

        @pl.when(nxt_ref[i] >= 0)
        def _():
            for cp in weight_copies(nxt_ref[i], 1 - wslot):
                cp.start()

    @pl.when(valid)
    def _():
        x = x_ref[...].astype(BF16)
        g = jnp.dot(x, wg_s[...], preferred_element_type=F32) + bg_ref[...]
        l = jnp.dot(x, wl_s[...], preferred_element_type=F32) + bl_ref[...]
        g = jnp.minimum(g, SWIGLU_LIMIT)
        l = jnp.clip(l, -SWIGLU_LIMIT, SWIGLU_LIMIT)
        a = (l + 1.0) * (g * (1.0 / (1.0 + jnp.exp(-SWIGLU_ALPHA * g))))
        _to_tiles(o_ref, jnp.dot(a.astype(BF16), wd_s[...], preferred_element_type=F32) + bdn_ref[...])

    @pl.when(jnp.logical_not(valid))
    def _():
        o_ref[...] = jnp.zeros(o_ref.shape, F32)


def _expert_ffn(tile_expert, n_valid, parity, next_expert, xs, w_gu, b_g, b_l, w_dn, b_dn, perm):
    p = xs.shape[0]
    nt = p // TG
    tile = lambda i, te, nv, *_: (jnp.minimum(i, nv[0] - 1), 0)
    ex3 = lambda i, te, *_: (te[i], 0, 0)
    grid_spec = pltpu.PrefetchScalarGridSpec(
        num_scalar_prefetch=4,
        grid=(nt,),
        in_specs=[pl.BlockSpec((TG, D_MODEL), tile),
                  pl.BlockSpec((None, 1, D_EXPERT), ex3),
                  pl.BlockSpec((None, 1, D_EXPERT), ex3),
                  pl.BlockSpec((None, 1, D_MODEL), ex3),
                  pl.BlockSpec((2 * LANES, 2 * LANES), lambda i, *_: (0, 0)),
                  pl.BlockSpec(memory_space=pl.ANY),
                  pl.BlockSpec(memory_space=pl.ANY)],
        out_specs=pl.BlockSpec((TG * TILE_ROWS, LANES), lambda i, *_: (i, 0)),
        scratch_shapes=[pltpu.VMEM((2, D_MODEL, 2 * D_EXPERT), F32), pltpu.VMEM((2, D_EXPERT, D_MODEL), F32),
                        pltpu.VMEM((D_MODEL, D_EXPERT), BF16), pltpu.VMEM((D_MODEL, D_EXPERT), BF16),
                        pltpu.VMEM((D_EXPERT, D_MODEL), BF16), pltpu.SemaphoreType.DMA((2,))],
    )
    return pl.pallas_call(
        _ffn_kernel,
        grid_spec=grid_spec,
        out_shape=jax.ShapeDtypeStruct((p * TILE_ROWS, LANES), F32),
        compiler_params=_cparams(("arbitrary",)),
        name="expert_ffn",
    )(tile_expert, n_valid, parity, next_expert, xs, b_g, b_l, b_dn, perm, w_gu, w_dn)


def _combine_kernel(p0_ref, p1_ref, p2_ref, p3_ref, w_ref, x1_ref, g2_ref, fg_ref, ys_ref, o_ref, buf, sem,
                    *, tile0, nc, tpb):
    i = pl.program_id(0) + tile0
    r = _mod_row(i, nc, tpb)
    batch = 8

    def tile_at(row):
        return pl.ds(pl.multiple_of(row * TILE_ROWS, TILE_ROWS), TILE_ROWS)

    def issue(tb, carry):
        t0 = pl.multiple_of(tb * batch, batch)
        for u in range(batch):
            for k, p_ref in enumerate((p0_ref, p1_ref, p2_ref, p3_ref)):
                pltpu.make_async_copy(ys_ref.at[tile_at(p_ref[t0 + u]), :], buf.at[k, tile_at(t0 + u), :], sem).start()
        return carry

    lax.fori_loop(0, TM // batch, issue, 0)
    for k in range(TOP_K):
        pltpu.make_async_copy(ys_ref.at[pl.ds(0, TM * TILE_ROWS), :], buf.at[k], sem).wait()
    w = w_ref[...]
    moe = w[:, 0:1] * _from_tiles(buf.at[0])
    for k in range(1, TOP_K):
        moe = moe + w[:, k:k + 1] * _from_tiles(buf.at[k])
    y = x1_ref[...] + g2_ref[pl.ds(r, 1), :] * moe
    o_ref[...] = _rms(y) * fg_ref[...]


def _combine(pos_flat, w, x1, mod, final_g, ys, *, tile0, ntiles, nc, tpb):
    nt = x1.shape[0] // TM
    slot_spec = lambda k: pl.BlockSpec((TM,), lambda i: (k * nt + tile0 + i,), memory_space=pltpu.SMEM)
    return pl.pallas_call(
        functools.partial(_combine_kernel, tile0=tile0, nc=nc, tpb=tpb),
        grid=(ntiles,),
        in_specs=[slot_spec(k) for k in range(TOP_K)] + [
                  pl.BlockSpec((TM, LANES), lambda i: (i + tile0, 0)),
                  pl.BlockSpec((TM, D_MODEL), lambda i: (i + tile0, 0)),
                  pl.BlockSpec((8, D_MODEL), lambda i: (0, 5)),
                  pl.BlockSpec((1, D_MODEL), lambda i: (0, 0)),
                  pl.BlockSpec(memory_space=pl.ANY)],
        out_specs=pl.BlockSpec((TM, D_MODEL), lambda i: (i, 0)),
        out_shape=jax.ShapeDtypeStruct((ntiles * TM, D_MODEL), F32),
        scratch_shapes=[pltpu.VMEM((TOP_K, TM * TILE_ROWS, LANES), F32), pltpu.SemaphoreType.DMA(())],
        compiler_params=_cparams(("arbitrary",)),
        name="combine",
    )(pos_flat, pos_flat, pos_flat, pos_flat, w, x1, mod, final_g, ys)


def kernel(x_prompt, x_sample, cache_k, cache_v, c, c_ctx, norm1_g, w_mod, b_mod, w_in, q_norm_g, k_norm_g,
           w_fourier, attn_out_g, fourier_out_g, w_out, norm2_g, w_router, b_router, w_gate_up, b_gate_up,
           w_down, b_down, final_g):
    bc, tc, _ = x_prompt.shape
    bl, tl, _ = x_sample.shape
    depth = w_in.shape[0]
    past = cache_k.shape[2]
    assert depth == 1 and tl % TQ == 0 and tl % TM == 0 and (bc * tc) % TM == 0
    n_ctx, n_lat = bc * tc, bl * tl
    n_tok = n_ctx + n_lat
    nc, nl, tpb = n_ctx // TM, n_lat // TM, tl // TM

    xc = x_prompt.reshape(n_ctx, D_MODEL)
    xl = x_sample.reshape(n_lat, D_MODEL)

    c_rows = jnp.concatenate([c_ctx[None, :], c, jnp.zeros((8 - 1 - bl, D_MODEL), F32)], axis=0)
    mod = _modulation(c_rows, w_mod[0], b_mod[0])

    cos_t, sin_t = _rope_tables(tl)
    q, k_heads, vt_ext, f, k_new, v_new = _pre(
        xc, xl, mod, norm1_g, w_in[0].astype(BF16), jnp.tile(q_norm_g, (1, N_HEADS)),
        jnp.tile(k_norm_g, (1, N_KV_HEADS)), _head_mean_matrix(), cos_t, sin_t, nc=nc, nl=nl, tpb=tpb)

    attn_c = _attention(q, k_heads, vt_ext, batch=bc, t=tc, s=tc, q_row0=0)
    keys, vals = _with_cache(k_heads, vt_ext, cache_k[:, 0], cache_v[:, 0], n_lat, bl, tl)
    attn_l = _attention(q, keys, vals, batch=bl, t=tl, s=past + tl, q_row0=n_ctx)

    wf = w_fourier[0].astype(BF16)
    zero = jnp.zeros_like(wf[0])
    wf_pairs = jnp.stack([jnp.block([[wf[0], zero], [zero, wf[1]]]), jnp.block([[wf[2], zero], [zero, wf[3]]])])
    four_c = _fourier(f, wf_pairs, batch=bc, t=tc, row0=0, n1=1)
    four_l = _fourier(f, wf_pairs, batch=bl, t=tl, row0=n_ctx, n1=FFT_N1)

    wr = jnp.pad(w_router[0], ((0, 0), (0, LANES - N_EXPERTS)))
    wr_hi = wr.astype(BF16)
    wr_lo = (wr - wr_hi.astype(F32)).astype(BF16)
    br = jnp.pad(b_router[0], (0, LANES - N_EXPERTS), constant_values=-1e30).reshape(1, LANES)
    x1, h2, idx, gate_w, counts = _mix(xc, xl, attn_c, attn_l, four_c, four_l, mod, attn_out_g, fourier_out_g,
                                       w_out[0].astype(BF16), norm2_g, wr_hi, wr_lo, br, nc=nc, nl=nl, tpb=tpb)
    cnt = counts[0, :N_EXPERTS].astype(I32)
    padded = ((cnt + TG - 1) // TG) * TG
    ends = jnp.cumsum(padded)
    starts = ends - padded
    n_rows = n_tok * TOP_K + N_EXPERTS * TG
    n_tiles = n_rows // TG
    tile_ids = jnp.arange(n_tiles, dtype=I32)
    tile_expert = jnp.minimum(jnp.sum((ends[None, :] // TG <= tile_ids[:, None]).astype(I32), axis=1),
                              N_EXPERTS - 1)
    n_valid = (ends[N_EXPERTS - 1] // TG).reshape(1)
    gs_rows = jnp.broadcast_to(jnp.pad(starts.astype(F32), (0, LANES - N_EXPERTS))[None, :], (8, LANES))
    pos = _positions(idx, gs_rows)

    xs = _dispatch(ends, pos, h2, n_rows)
    fresh_tile = jnp.concatenate([jnp.ones((1,), I32), (tile_expert[1:] != tile_expert[:-1]).astype(I32)])
    parity = (jnp.cumsum(fresh_tile) - 1) % 2
    experts = jnp.arange(N_EXPERTS, dtype=I32)
    later = jnp.logical_and(padded[None, :] > 0, experts[None, :] > experts[:, None])
    next_of = jnp.min(jnp.where(later, experts[None, :], N_EXPERTS), axis=1)
    next_of = jnp.where(next_of == N_EXPERTS, -1, next_of)
    next_expert = jnp.sum(jnp.where(tile_expert[:, None] == experts[None, :], next_of[None, :], 0), axis=1)
    b_gu = b_gate_up[0].reshape(N_EXPERTS, 1, D_EXPERT, 2)
    ys = _expert_ffn(tile_expert, n_valid, parity.astype(I32), next_expert.astype(I32), xs, w_gate_up[0],
                     b_gu[..., 0], b_gu[..., 1], w_down[0], b_down[0].reshape(N_EXPERTS, 1, D_MODEL),
                     _deinterleave_matrix())

    pos_flat = pos[:TOP_K].reshape(-1)
    y_c = _combine(pos_flat, gate_w, x1, mod, final_g.reshape(1, -1), ys, tile0=0, ntiles=nc, nc=nc, tpb=tpb)
    y_l = _combine(pos_flat, gate_w, x1, mod, final_g.reshape(1, -1), ys, tile0=nc, ntiles=nl, nc=nc, tpb=tpb)

    new_k = k_new[:n_ctx].reshape(bc, 1, tc, N_KV_HEADS, HEAD_DIM)
    new_v = v_new[:n_ctx].reshape(bc, 1, tc, N_KV_HEADS, HEAD_DIM)
    return (y_c.reshape(bc, tc, D_MODEL), y_l.reshape(bl, tl, D_MODEL), new_k, new_v)
```

```python
import functools
import math

import numpy as np
import jax
import jax.numpy as jnp
from jax import lax
from jax.experimental import pallas as pl
from jax.experimental.pallas import tpu as pltpu

F32 = jnp.float32
BF16 = jnp.bfloat16
I32 = jnp.int32

D_MODEL = 1024
HEAD_DIM = 64
N_HEADS = 8
N_KV_HEADS = 2
HEADS_PER_KV = N_HEADS // N_KV_HEADS
ATTN_W = N_HEADS * HEAD_DIM
KV_W = N_KV_HEADS * HEAD_DIM
FOURIER_W = 512
FOURIER_GROUP_W = 128
D_IN = ATTN_W + 2 * KV_W + FOURIER_W
N_EXPERTS = 32
TOP_K = 4
D_EXPERT = 1024
SWIGLU_ALPHA = 1.702
SWIGLU_LIMIT = 7.0
ROPE_THETA = 10000.0
ROT_PAIRS = HEAD_DIM // 4
GRID_W = 64
EPS = 1e-6

LANES = 128
TM = 512
TQ = 512
TG = 256
FFT_N1 = 4
V_ROWS = 80
TILE_ROWS = D_MODEL // LANES
VMEM_LIMIT = 56 * 1024 * 1024


def _cparams(sem, vmem=VMEM_LIMIT):
    return pltpu.CompilerParams(dimension_semantics=sem, vmem_limit_bytes=vmem)


def _rope_tables(t_lat):
    pos = np.arange(t_lat)
    row = (pos // GRID_W).astype(np.float64)
    col = (pos % GRID_W).astype(np.float64)
    inv = ROPE_THETA ** (-np.arange(ROT_PAIRS, dtype=np.float64) / ROT_PAIRS)
    lane = np.arange(LANES)
    d = lane % HEAD_DIM
    axis = d // (2 * ROT_PAIRS)
    second = (d // ROT_PAIRS) % 2
    p = d % ROT_PAIRS
    ang = np.where(axis[None, :] == 0, row[:, None], col[:, None]) * inv[p][None, :]
    cos = np.cos(ang)
    sin = np.sin(ang) * np.where(second == 0, -1.0, 1.0)[None, :]
    cos = np.concatenate([np.ones((TM, LANES)), cos], axis=0)
    sin = np.concatenate([np.zeros((TM, LANES)), sin], axis=0)
    return jnp.asarray(cos, F32), jnp.asarray(sin, F32)


def _dft_tables(n1, n2):
    c = np.arange(FOURIER_GROUP_W)
    ang = 2.0 * np.pi * np.outer(c, c) / FOURIER_GROUP_W
    z = np.zeros_like(ang)
    cbd = np.block([[np.cos(ang), z], [z, np.cos(ang)]])
    sbd = np.block([[np.sin(ang), z], [z, np.sin(ang)]])
    cd = np.concatenate([cbd, -sbd], axis=1)
    u = np.arange(n2)
    ang2 = 2.0 * np.pi * np.outer(u, u) / n2
    ct, st = np.cos(ang2), np.sin(ang2)
    nt = max(n1 - 1, 1)
    tw_ang = 2.0 * np.pi * np.outer(np.arange(1, nt + 1), u) / (n1 * n2)
    twc = np.repeat(np.cos(tw_ang)[:, :, None], LANES, axis=2)
    tws = np.repeat(np.sin(tw_ang)[:, :, None], LANES, axis=2)
    return (jnp.asarray(cd, F32), jnp.asarray(ct, F32), jnp.asarray(st, F32),
            jnp.asarray(twc, F32), jnp.asarray(tws, F32))


def _head_mean_matrix():
    h = np.arange(ATTN_W) // HEAD_DIM
    return jnp.asarray((h[:, None] == h[None, :]) / HEAD_DIM, BF16)


def _deinterleave_matrix():
    p = np.zeros((2 * LANES, 2 * LANES))
    m = np.arange(LANES)
    p[2 * m, m] = 1.0
    p[2 * m + 1, LANES + m] = 1.0
    return jnp.asarray(p, BF16)


def _mod_row(i, nc, tpb):
    return jnp.where(i < nc, 0, 1 + (i - nc) // tpb)


def _rms(x):
    return x * lax.rsqrt(jnp.mean(x * x, axis=-1, keepdims=True) + EPS)


def _pack_cols(cols, lane):
    out = jnp.zeros(lane.shape, cols[0].dtype)
    for k, c in enumerate(cols):
        out = jnp.where(lane == k, c, out)
    return out


def _to_tiles(ref, x):
    r = x.shape[0]
    for s in range(TILE_ROWS):
        ref[pl.ds(s, r, stride=TILE_ROWS), :] = x[:, s * LANES:(s + 1) * LANES]


def _from_tiles(ref):
    r = ref.shape[0] // TILE_ROWS
    return jnp.concatenate([ref[pl.ds(s, r, stride=TILE_ROWS), :] for s in range(TILE_ROWS)], axis=1)


def _mod_kernel(c_ref, w_ref, b_ref, o_ref):
    c = c_ref[...]
    s = c * (1.0 / (1.0 + jnp.exp(-c)))
    o_ref[...] = jnp.dot(s.astype(BF16), w_ref[...].astype(BF16), preferred_element_type=F32) + b_ref[...]


def _modulation(c_rows, w_mod, b_mod):
    n = w_mod.shape[1] // D_MODEL
    return pl.pallas_call(
        _mod_kernel,
        grid=(n,),
        in_specs=[pl.BlockSpec((8, D_MODEL), lambda j: (0, 0)),
                  pl.BlockSpec((D_MODEL, D_MODEL), lambda j: (0, j)),
                  pl.BlockSpec((1, D_MODEL), lambda j: (0, j))],
        out_specs=pl.BlockSpec((8, D_MODEL), lambda j: (0, j)),
        out_shape=jax.ShapeDtypeStruct((8, w_mod.shape[1]), F32),
        compiler_params=_cparams(("parallel",)),
        name="modulation",
    )(c_rows, w_mod, b_mod.reshape(1, -1))


def _pre_kernel(xc_ref, xl_ref, sh_ref, sc_ref, g1_ref, win_ref, qg_ref, kg_ref, bd_ref, cos_ref, sin_ref,
                q_ref, k_ref, vt_ref, f_ref, kn_ref, vn_ref, *, nc, tpb):
    i = pl.program_id(0)
    r = _mod_row(i, nc, tpb)
    x = jnp.where(i < nc, xc_ref[...], xl_ref[...])
    h = _rms(x) * g1_ref[...]
    h = h * (1.0 + sc_ref[pl.ds(r, 1), :]) + sh_ref[pl.ds(r, 1), :]
    proj = jnp.dot(h.astype(BF16), win_ref[...], preferred_element_type=F32)
    q = proj[:, :ATTN_W]
    k = proj[:, ATTN_W:ATTN_W + KV_W]
    v = proj[:, ATTN_W + KV_W:ATTN_W + 2 * KV_W]
    f_ref[...] = proj[:, ATTN_W + 2 * KV_W:].astype(BF16)
    vn_ref[...] = v
    vt = v.T
    tail = (lax.broadcasted_iota(I32, (V_ROWS - HEAD_DIM, vt.shape[1]), 0) == 0).astype(BF16)
    for j in range(N_KV_HEADS):
        vt_ref[j, :HEAD_DIM, :] = vt[j * HEAD_DIM:(j + 1) * HEAD_DIM, :].astype(BF16)
        vt_ref[j, HEAD_DIM:, :] = tail
    q_ms = jnp.dot((q * q).astype(BF16), bd_ref[...], preferred_element_type=F32)
    k_ms = jnp.dot((k * k).astype(BF16), bd_ref[:KV_W, :KV_W], preferred_element_type=F32)
    qn = q * lax.rsqrt(q_ms + EPS) * qg_ref[...]
    kn = k * lax.rsqrt(k_ms + EPS) * kg_ref[...]
    kn_ref[...] = kn
    cos = cos_ref[...]
    sin = sin_ref[...]
    lane = lax.broadcasted_iota(I32, cos.shape, 1)
    first = (lane & ROT_PAIRS) == 0

    def rope(c):
        partner = jnp.where(first, pltpu.roll(c, LANES - ROT_PAIRS, 1), pltpu.roll(c, ROT_PAIRS, 1))
        return c * cos + partner * sin

    k_rot = rope(kn).astype(BF16)
    for j in range(N_KV_HEADS):
        k_ref[j] = k_rot[:, j * HEAD_DIM:(j + 1) * HEAD_DIM]
    scale = HEAD_DIM ** -0.5 * math.log2(math.e)
    q_rot = jnp.concatenate([rope(qn[:, j * LANES:(j + 1) * LANES]) for j in range(ATTN_W // LANES)], axis=1)
    q_ref[...] = (q_rot * scale).T.astype(BF16)


def _pre(xc, xl, mod, norm1_g, w_in, q_g, k_g, bd, cos_t, sin_t, *, nc, nl, tpb):
    n = (nc + nl) * TM
    row = lambda i: (i, 0)
    const = lambda i: (0, 0)
    tab = lambda i: (jnp.where(i < nc, 0, 1 + (i - nc) % tpb), 0)
    return pl.pallas_call(
        functools.partial(_pre_kernel, nc=nc, tpb=tpb),
        grid=(nc + nl,),
        in_specs=[pl.BlockSpec((TM, D_MODEL), lambda i: (jnp.minimum(i, nc - 1), 0)),
                  pl.BlockSpec((TM, D_MODEL), lambda i: (jnp.maximum(i - nc, 0), 0)),
                  pl.BlockSpec((8, D_MODEL), lambda i: (0, 0)),
                  pl.BlockSpec((8, D_MODEL), lambda i: (0, 1)),
                  pl.BlockSpec((1, D_MODEL), const),
                  pl.BlockSpec((D_MODEL, D_IN), const),
                  pl.BlockSpec((1, ATTN_W), const),
                  pl.BlockSpec((1, KV_W), const),
                  pl.BlockSpec((ATTN_W, ATTN_W), const),
                  pl.BlockSpec((TM, LANES), tab),
                  pl.BlockSpec((TM, LANES), tab)],
        out_specs=[pl.BlockSpec((ATTN_W, TM), lambda i: (0, i)),
                   pl.BlockSpec((N_KV_HEADS, TM, HEAD_DIM), lambda i: (0, i, 0)),
                   pl.BlockSpec((N_KV_HEADS, V_ROWS, TM), lambda i: (0, 0, i)),
                   pl.BlockSpec((TM, FOURIER_W), row), pl.BlockSpec((TM, KV_W), row), pl.BlockSpec((TM, KV_W), row)],
        out_shape=[jax.ShapeDtypeStruct((ATTN_W, n), BF16), jax.ShapeDtypeStruct((N_KV_HEADS, n, HEAD_DIM), BF16),
                   jax.ShapeDtypeStruct((N_KV_HEADS, V_ROWS, n), BF16), jax.ShapeDtypeStruct((n, FOURIER_W), BF16),
                   jax.ShapeDtypeStruct((n, KV_W), F32), jax.ShapeDtypeStruct((n, KV_W), F32)],
        compiler_params=_cparams(("parallel",)),
        name="pre_proj",
    )(xc, xl, mod, mod, norm1_g, w_in, q_g, k_g, bd, cos_t, sin_t)


def _attn_kernel(qt_ref, k_ref, vt_ref, o_ref, ot_ref, st_ref, pt_ref):
    def scores(h):
        qt = qt_ref[h * HEAD_DIM:(h + 1) * HEAD_DIM, :]
        st_ref[h % 2] = jnp.dot(k_ref[h // HEADS_PER_KV], qt, preferred_element_type=F32)

    scores(0)
    for h in range(N_HEADS):
        if h + 1 < N_HEADS:
            scores(h + 1)
        st = st_ref[h % 2]
        pt_ref[h % 2] = jnp.exp2(st - jnp.max(st, axis=0, keepdims=True)).astype(BF16)
        ot = jnp.dot(vt_ref[h // HEADS_PER_KV], pt_ref[h % 2], preferred_element_type=F32)
        ot_ref[h * HEAD_DIM:(h + 1) * HEAD_DIM, :] = ot[:HEAD_DIM] * (1.0 / ot[HEAD_DIM:HEAD_DIM + 1])
    o_ref[...] = ot_ref[...].T


def _attention(qt, k_heads, vt_ext, *, batch, t, s, q_row0):
    tq = min(TQ, t)
    qpb = t // tq
    q0 = q_row0 // tq
    return pl.pallas_call(
        _attn_kernel,
        grid=(batch, qpb),
        in_specs=[pl.BlockSpec((ATTN_W, tq), lambda b, i: (0, q0 + b * qpb + i)),
                  pl.BlockSpec((N_KV_HEADS, s, HEAD_DIM), lambda b, i: (0, b, 0)),
                  pl.BlockSpec((N_KV_HEADS, V_ROWS, s), lambda b, i: (0, 0, b))],
        out_specs=pl.BlockSpec((tq, ATTN_W), lambda b, i: (b * qpb + i, 0)),
        out_shape=jax.ShapeDtypeStruct((batch * t, ATTN_W), F32),
        scratch_shapes=[pltpu.VMEM((ATTN_W, tq), F32), pltpu.VMEM((2, s, tq), F32), pltpu.VMEM((2, s, tq), BF16)],
        compiler_params=_cparams(("parallel", "parallel")),
        name="attention",
    )(qt, k_heads, vt_ext)


def _with_cache(k_heads, vt_ext, cache_k, cache_v, n_lat, batch, t):
    past = cache_k.shape[1]
    ck = cache_k.astype(BF16).transpose(2, 0, 1, 3)
    cv = cache_v.astype(BF16).transpose(2, 3, 0, 1)
    tail = jnp.concatenate([jnp.ones((N_KV_HEADS, 1, batch, past), BF16),
                            jnp.zeros((N_KV_HEADS, V_ROWS - HEAD_DIM - 1, batch, past), BF16)], axis=1)
    k_lat = k_heads[:, -n_lat:].reshape(N_KV_HEADS, batch, t, HEAD_DIM)
    v_lat = vt_ext[:, :, -n_lat:].reshape(N_KV_HEADS, V_ROWS, batch, t)
    keys = jnp.concatenate([ck, k_lat], axis=2).reshape(N_KV_HEADS, batch * (past + t), HEAD_DIM)
    vals = jnp.concatenate([jnp.concatenate([cv, tail], axis=1), v_lat], axis=3)
    return keys, vals.reshape(N_KV_HEADS, V_ROWS, batch * (past + t))


def _fourier_kernel(x_ref, cd_ref, ct_ref, st_ref, twc_ref, tws_ref, wf_ref, o_ref, *, n1, n2):
    pw = 2 * FOURIER_GROUP_W
    scale = 1.0 / math.sqrt(n1 * n2 * FOURIER_GROUP_W)
    for gp in range(FOURIER_W // pw):
        z = jnp.dot(x_ref[:, gp * pw:(gp + 1) * pw], cd_ref[...], preferred_element_type=F32)
        zr = [z[t1 * n2:(t1 + 1) * n2, :pw] for t1 in range(n1)]
        zi = [z[t1 * n2:(t1 + 1) * n2, pw:] for t1 in range(n1)]
        if n1 == 1:
            a = [(zr[0], zi[0])]
        else:
            a = [(zr[0] + zr[1] + zr[2] + zr[3], zi[0] + zi[1] + zi[2] + zi[3]),
                 (zr[0] + zi[1] - zr[2] - zi[3], zi[0] - zr[1] - zi[2] + zr[3]),
                 (zr[0] - zr[1] + zr[2] - zr[3], zi[0] - zi[1] + zi[2] - zi[3]),
                 (zr[0] - zi[1] - zr[2] + zi[3], zi[0] + zr[1] - zi[2] - zr[3])]
        for u1 in range(n1):
            ar, ai = a[u1]
            if u1 > 0:
                c = jnp.concatenate([twc_ref[u1 - 1]] * 2, axis=1)
                s = jnp.concatenate([tws_ref[u1 - 1]] * 2, axis=1)
                ar, ai = ar * c + ai * s, ai * c - ar * s
            y = (jnp.dot(ct_ref[...], ar.astype(BF16), preferred_element_type=F32)
                 + jnp.dot(st_ref[...], ai.astype(BF16), preferred_element_type=F32)) * scale
            y = jnp.dot(y.astype(BF16), wf_ref[gp], preferred_element_type=F32)
            rows = pl.ds(u1, n2, stride=n1) if n1 > 1 else slice(None)
            for half in range(2):
                o_ref[2 * gp + half, rows, :] = y[:, half * FOURIER_GROUP_W:(half + 1) * FOURIER_GROUP_W]


def _fourier(f, wf_pairs, *, batch, t, row0, n1):
    n2 = t // n1
    cd, ct, st, twc, tws = _dft_tables(n1, n2)
    cd, ct, st = cd.astype(BF16), ct.astype(BF16), st.astype(BF16)
    nt = twc.shape[0]
    b0 = row0 // t
    c2 = lambda b: (0, 0)
    c3 = lambda b: (0, 0, 0)
    return pl.pallas_call(
        functools.partial(_fourier_kernel, n1=n1, n2=n2),
        grid=(batch,),
        in_specs=[pl.BlockSpec((t, FOURIER_W), lambda b: (b0 + b, 0)),
                  pl.BlockSpec(cd.shape, c2), pl.BlockSpec((n2, n2), c2), pl.BlockSpec((n2, n2), c2),
                  pl.BlockSpec((nt, n2, LANES), c3), pl.BlockSpec((nt, n2, LANES), c3),
                  pl.BlockSpec(wf_pairs.shape, c3)],
        out_specs=pl.BlockSpec((FOURIER_W // FOURIER_GROUP_W, t, FOURIER_GROUP_W), lambda b: (0, b, 0)),
        out_shape=jax.ShapeDtypeStruct((FOURIER_W // FOURIER_GROUP_W, batch * t, FOURIER_GROUP_W), F32),
        compiler_params=_cparams(("parallel",)),
        name="fourier",
    )(f, cd, ct, st, twc, tws, wf_pairs)


def _route(i, h, whi_ref, wlo_ref, br_ref, idx_ref, w_ref, cnt_ref):
    h_hi = h.astype(BF16)
    h_lo = (h - h_hi.astype(F32)).astype(BF16)
    logits = (jnp.dot(h_hi, whi_ref[...], preferred_element_type=F32)
              + jnp.dot(h_lo, whi_ref[...], preferred_element_type=F32)
              + jnp.dot(h_hi, wlo_ref[...], preferred_element_type=F32)) + br_ref[...]
    lane = lax.broadcasted_iota(I32, logits.shape, 1)
    l = logits
    vals, idxs = [], []
    hits = jnp.zeros(logits.shape, F32)
    for _ in range(TOP_K):
        m = jnp.max(l, axis=-1, keepdims=True)
        ix = jnp.min(jnp.where(l == m, lane, LANES), axis=-1, keepdims=True)
        sel = lane == ix
        hits = hits + sel.astype(F32)
        l = jnp.where(sel, -jnp.inf, l)
        vals.append(m)
        idxs.append(ix)
    es = [jnp.exp(v - vals[0]) for v in vals]
    den = es[0] + es[1] + es[2] + es[3]
    idx_ref[...] = _pack_cols(idxs, lane)
    w_ref[...] = _pack_cols([e / den for e in es], lane)

    @pl.when(i == 0)
    def _():
        cnt_ref[...] = jnp.zeros(cnt_ref.shape, F32)

    cnt_ref[...] += _round_up_runs(jnp.sum(hits, axis=0, keepdims=True))


def _mix_kernel(xc_ref, xl_ref, ac_ref, al_ref, fc_ref, fl_ref, g1_ref, sh2_ref, sc2_ref, ag_ref, fg_ref,
                wout_ref, n2g_ref, whi_ref, wlo_ref, br_ref, x1_ref, h2_ref, idx_ref, w_ref, cnt_ref, *, nc, tpb):
    i = pl.program_id(0)
    r = _mod_row(i, nc, tpb)
    is_ctx = i < nc
    x = jnp.where(is_ctx, xc_ref[...], xl_ref[...])
    a = _rms(jnp.where(is_ctx, ac_ref[...], al_ref[...])) * ag_ref[...]
    n_groups = FOURIER_W // FOURIER_GROUP_W
    f = jnp.concatenate([jnp.where(is_ctx, fc_ref[g], fl_ref[g]) for g in range(n_groups)], axis=1)
    f = _rms(f) * fg_ref[...]
    mixed = (jnp.dot(a.astype(BF16), wout_ref[:ATTN_W, :], preferred_element_type=F32)
             + jnp.dot(f.astype(BF16), wout_ref[ATTN_W:, :], preferred_element_type=F32))
    x1 = x + g1_ref[pl.ds(r, 1), :] * mixed
    x1_ref[...] = x1
    h2 = _rms(x1) * n2g_ref[...]
    h2 = h2 * (1.0 + sc2_ref[pl.ds(r, 1), :]) + sh2_ref[pl.ds(r, 1), :]
    h2_ref[...] = h2
    _route(i, h2, whi_ref, wlo_ref, br_ref, idx_ref, w_ref, cnt_ref)


def _mix(xc, xl, ac, al, fc, fl, mod, attn_g, four_g, w_out, norm2_g, wr_hi, wr_lo, br, *, nc, nl, tpb):
    n = (nc + nl) * TM
    lo = lambda i: (jnp.minimum(i, nc - 1), 0)
    hi = lambda i: (jnp.maximum(i - nc, 0), 0)
    const = lambda i: (0, 0)
    row = lambda i: (i, 0)
    return pl.pallas_call(
        functools.partial(_mix_kernel, nc=nc, tpb=tpb),
        grid=(nc + nl,),
        in_specs=[pl.BlockSpec((TM, D_MODEL), lo), pl.BlockSpec((TM, D_MODEL), hi),
                  pl.BlockSpec((TM, ATTN_W), lo), pl.BlockSpec((TM, ATTN_W), hi),
                  pl.BlockSpec((FOURIER_W // FOURIER_GROUP_W, TM, FOURIER_GROUP_W),
                               lambda i: (0, jnp.minimum(i, nc - 1), 0)),
                  pl.BlockSpec((FOURIER_W // FOURIER_GROUP_W, TM, FOURIER_GROUP_W),
                               lambda i: (0, jnp.maximum(i - nc, 0), 0)),
                  pl.BlockSpec((8, D_MODEL), lambda i: (0, 2)),
                  pl.BlockSpec((8, D_MODEL), lambda i: (0, 3)),
                  pl.BlockSpec((8, D_MODEL), lambda i: (0, 4)),
                  pl.BlockSpec((1, ATTN_W), const), pl.BlockSpec((1, FOURIER_W), const),
                  pl.BlockSpec((D_MODEL, D_MODEL), const), pl.BlockSpec((1, D_MODEL), const),
                  pl.BlockSpec((D_MODEL, LANES), const), pl.BlockSpec((D_MODEL, LANES), const),
                  pl.BlockSpec((1, LANES), const)],
        out_specs=[pl.BlockSpec((TM, D_MODEL), row), pl.BlockSpec((TM, D_MODEL), row),
                   pl.BlockSpec((TM, LANES), row), pl.BlockSpec((TM, LANES), row), pl.BlockSpec((8, LANES), const)],
        out_shape=[jax.ShapeDtypeStruct((n, D_MODEL), F32), jax.ShapeDtypeStruct((n, D_MODEL), F32),
                   jax.ShapeDtypeStruct((n, LANES), I32), jax.ShapeDtypeStruct((n, LANES), F32),
                   jax.ShapeDtypeStruct((8, LANES), F32)],
        compiler_params=_cparams(("arbitrary",)),
        name="mix_proj",
    )(xc, xl, ac, al, fc, fl, mod, mod, mod, attn_g, four_g, w_out, norm2_g, wr_hi, wr_lo, br)


RUN_ALIGN = 8
RUN_ROWS = TM * TOP_K + N_EXPERTS * RUN_ALIGN
RUN_SIZES = tuple(TM >> s for s in range(7))


def _round_up_runs(x):
    return jnp.floor((x + (RUN_ALIGN - 1)) * (1.0 / RUN_ALIGN)) * RUN_ALIGN


def _pos_kernel(idx_ref, gs_ref, pos_ref, loc_ref, info_ref, carry_ref):
    i = pl.program_id(0)

    @pl.when(i == 0)
    def _():
        carry_ref[...] = jnp.zeros(carry_ref.shape, F32)

    idx = idx_ref[...]
    lane = lax.broadcasted_iota(I32, idx.shape, 1)
    sels = [lane == idx[:, k:k + 1] for k in range(TOP_K)]
    hits = jnp.zeros(idx.shape, F32)
    for s in sels:
        hits = hits + s.astype(F32)
    r = lax.broadcasted_iota(I32, (TM, TM), 0)
    c = lax.broadcasted_iota(I32, (TM, TM), 1)
    before = (c < r).astype(BF16)
    rank = jnp.dot(before, hits.astype(BF16), preferred_element_type=F32)
    run_len = _round_up_runs(jnp.sum(hits, axis=0, keepdims=True))
    a = lax.broadcasted_iota(I32, (LANES, LANES), 0)
    b = lax.broadcasted_iota(I32, (LANES, LANES), 1)
    units = jnp.broadcast_to(run_len * (1.0 / RUN_ALIGN), (8, LANES)).astype(BF16)
    run_off = jnp.dot(units, (a < b).astype(BF16), preferred_element_type=F32)[0:1, :] * RUN_ALIGN
    run_start = carry_ref[0:1, :] + gs_ref[0:1, :]
    pos_cols = [jnp.sum(jnp.where(s, rank + run_start, 0.0), axis=-1, keepdims=True) for s in sels]
    loc_cols = [jnp.sum(jnp.where(s, rank + run_off, 0.0), axis=-1, keepdims=True) for s in sels]
    pos_ref[...] = _pack_cols(pos_cols, lane).T[:8, :].astype(I32)
    loc_ref[...] = _pack_cols(loc_cols, lane).T[:8, :].astype(I32)
    row = lax.broadcasted_iota(I32, (8, LANES), 0)
    info = jnp.where(row == 0, run_start, jnp.where(row == 1, run_len, jnp.where(row == 2, run_off, 0.0)))
    info_ref[...] = info.astype(I32)
    carry_ref[...] += run_len


def _positions(idx, gs_rows):
    n = idx.shape[0]
    nt = n // TM
    return pl.pallas_call(
        _pos_kernel,
        grid=(nt,),
        in_specs=[pl.BlockSpec((TM, LANES), lambda i: (i, 0)), pl.BlockSpec((8, LANES), lambda i: (0, 0))],
        out_specs=[pl.BlockSpec((8, TM), lambda i: (0, i)), pl.BlockSpec((8, TM), lambda i: (0, i)),
                   pl.BlockSpec((None, 8, LANES), lambda i: (i, 0, 0))],
        out_shape=[jax.ShapeDtypeStruct((8, n), I32), jax.ShapeDtypeStruct((8, n), I32),
                   jax.ShapeDtypeStruct((nt, 8, LANES), I32)],
        scratch_shapes=[pltpu.VMEM((8, LANES), F32)],
        compiler_params=_cparams(("arbitrary",)),
        name="positions",
    )(idx, gs_rows)


def _run_copies(info_ref, tile, make_copy, act):
    base = tile * (3 * N_EXPERTS)

    def per_expert(e, carry):
        start = info_ref[base + e]
        length = info_ref[base + N_EXPERTS + e]
        off = info_ref[base + 2 * N_EXPERTS + e]
        for size in RUN_SIZES:
            @pl.when((length & size) != 0)
            def _():
                act(make_copy(pl.multiple_of(off, RUN_ALIGN), pl.multiple_of(start, RUN_ALIGN), size))
            taken = jnp.where((length & size) != 0, size, 0)
            off = off + taken
            start = start + taken
        return carry

    lax.fori_loop(0, N_EXPERTS, per_expert, 0)


def _dispatch_kernel(info_ref, ends_ref, loc_ref, h2_ref, xs_ref, runs_ref, zero_ref, sem):
    i = pl.program_id(0)

    @pl.when(i == 0)
    def _():
        zero_ref[...] = jnp.zeros(zero_ref.shape, F32)

        def last_tile(e):
            lo = ends_ref[e - 1] if e > 0 else 0
            start = pl.multiple_of(jnp.maximum(ends_ref[e] - TG, 0), TG)
            return ends_ref[e] > lo, pltpu.make_async_copy(zero_ref, xs_ref.at[pl.ds(start, TG), :], sem)

        for e in range(N_EXPERTS):
            nonempty, cp = last_tile(e)
            pl.when(nonempty)(cp.start)
        for e in range(N_EXPERTS):
            nonempty, cp = last_tile(e)
            pl.when(nonempty)(cp.wait)

        def tail_tile(j):
            return pltpu.make_async_copy(zero_ref, xs_ref.at[pl.ds(pl.multiple_of(j * TG, TG), TG), :], sem)

        first_unused = ends_ref[N_EXPERTS - 1] // TG
        n_tiles = xs_ref.shape[0] // TG
        lax.fori_loop(first_unused, n_tiles, lambda j, c: (tail_tile(j).start(), c)[1], 0)
        lax.fori_loop(first_unused, n_tiles, lambda j, c: (tail_tile(j).wait(), c)[1], 0)

    loc = loc_ref[...]
    row = lax.broadcasted_iota(I32, (RUN_ROWS, TM), 0)
    sel = row == loc[0:1, :]
    for k in range(1, TOP_K):
        sel = jnp.logical_or(sel, row == loc[k:k + 1, :])
    runs_ref[...] = jnp.dot(sel.astype(BF16), h2_ref[...].astype(BF16), preferred_element_type=F32)

    def piece(stacked_row, global_row, size):
        return pltpu.make_async_copy(runs_ref.at[pl.ds(stacked_row, size), :], xs_ref.at[pl.ds(global_row, size), :], sem)

    _run_copies(info_ref, i, piece, lambda cp: cp.start())
    _run_copies(info_ref, i, piece, lambda cp: cp.wait())


def _dispatch(info, ends, loc, h2, n_rows):
    n = h2.shape[0]
    grid_spec = pltpu.PrefetchScalarGridSpec(
        num_scalar_prefetch=2,
        grid=(n // TM,),
        in_specs=[pl.BlockSpec((8, TM), lambda i, *_: (0, i)),
                  pl.BlockSpec((TM, D_MODEL), lambda i, *_: (i, 0))],
        out_specs=pl.BlockSpec(memory_space=pl.ANY),
        scratch_shapes=[pltpu.VMEM((RUN_ROWS, D_MODEL), F32), pltpu.VMEM((TG, D_MODEL), F32),
                        pltpu.SemaphoreType.DMA(())],
    )
    return pl.pallas_call(
        _dispatch_kernel,
        grid_spec=grid_spec,
        out_shape=jax.ShapeDtypeStruct((n_rows, D_MODEL), F32),
        compiler_params=_cparams(("arbitrary",)),
        name="dispatch",
    )(info, ends, loc, h2)


def _ffn_kernel(te_ref, nv_ref, par_ref, nxt_ref, x_ref, bg_ref, bl_ref, bdn_ref, perm_ref, wgu_hbm, wdn_hbm, o_ref,
                wgu_buf, wdn_buf, wg_s, wl_s, wd_s, wsem):
    i = pl.program_id(0)
    e = te_ref[i]
    valid = i < nv_ref[0]
    fresh = jnp.logical_or(i == 0, e != te_ref[jnp.maximum(i - 1, 0)])
    wslot = par_ref[i]

    def weight_copies(expert, s):
        return (pltpu.make_async_copy(wgu_hbm.at[expert], wgu_buf.at[s], wsem.at[s]),
                pltpu.make_async_copy(wdn_hbm.at[expert], wdn_buf.at[s], wsem.at[s]))

    @pl.when(i == 0)
    def _():
        for cp in weight_copies(e, wslot):
            cp.start()

    @pl.when(jnp.logical_and(valid, fresh))
    def _():
        for cp in weight_copies(e, wslot):
            cp.wait()
        for j in range(2 * D_EXPERT // (2 * LANES)):
            blk = wgu_buf[wslot, :, j * 2 * LANES:(j + 1) * 2 * LANES].astype(BF16)
            d = jnp.dot(blk, perm_ref[...], preferred_element_type=F32)
            wg_s[:, j * LANES:(j + 1) * LANES] = d[:, :LANES].astype(BF16)
            wl_s[:, j * LANES:(j + 1) * LANES] = d[:, LANES:].astype(BF16)
        wd_s[...] = wdn_buf[wslot].astype(BF16)

        @pl.when(nxt_ref[i] >= 0)
        def _():
            for cp in weight_copies(nxt_ref[i], 1 - wslot):
                cp.start()

    @pl.when(valid)
    def _():
        x = x_ref[...].astype(BF16)
        g = jnp.dot(x, wg_s[...], preferred_element_type=F32) + bg_ref[...]
        l = jnp.dot(x, wl_s[...], preferred_element_type=F32) + bl_ref[...]
        g = jnp.minimum(g, SWIGLU_LIMIT)
        l = jnp.clip(l, -SWIGLU_LIMIT, SWIGLU_LIMIT)
        a = (l + 1.0) * (g * (1.0 / (1.0 + jnp.exp(-SWIGLU_ALPHA * g))))
        _to_tiles(o_ref, jnp.dot(a.astype(BF16), wd_s[...], preferred_element_type=F32) + bdn_ref[...])

    @pl.when(jnp.logical_not(valid))
    def _():
        o_ref[...] = jnp.zeros(o_ref.shape, F32)


def _expert_ffn(tile_expert, n_valid, parity, next_expert, xs, w_gu, b_g, b_l, w_dn, b_dn, perm):
    p = xs.shape[0]
    nt = p // TG
    tile = lambda i, te, nv, *_: (jnp.minimum(i, nv[0] - 1), 0)
    ex3 = lambda i, te, *_: (te[i], 0, 0)
    grid_spec = pltpu.PrefetchScalarGridSpec(
        num_scalar_prefetch=4,
        grid=(nt,),
        in_specs=[pl.BlockSpec((TG, D_MODEL), tile),
                  pl.BlockSpec((None, 1, D_EXPERT), ex3),
                  pl.BlockSpec((None, 1, D_EXPERT), ex3),
                  pl.BlockSpec((None, 1, D_MODEL), ex3),
                  pl.BlockSpec((2 * LANES, 2 * LANES), lambda i, *_: (0, 0)),
                  pl.BlockSpec(memory_space=pl.ANY),
                  pl.BlockSpec(memory_space=pl.ANY)],
        out_specs=pl.BlockSpec((TG * TILE_ROWS, LANES), lambda i, *_: (i, 0)),
        scratch_shapes=[pltpu.VMEM((2, D_MODEL, 2 * D_EXPERT), F32), pltpu.VMEM((2, D_EXPERT, D_MODEL), F32),
                        pltpu.VMEM((D_MODEL, D_EXPERT), BF16), pltpu.VMEM((D_MODEL, D_EXPERT), BF16),
                        pltpu.VMEM((D_EXPERT, D_MODEL), BF16), pltpu.SemaphoreType.DMA((2,))],
    )
    return pl.pallas_call(
        _ffn_kernel,
        grid_spec=grid_spec,
        out_shape=jax.ShapeDtypeStruct((p * TILE_ROWS, LANES), F32),
        compiler_params=_cparams(("arbitrary",)),
        name="expert_ffn",
    )(tile_expert, n_valid, parity, next_expert, xs, b_g, b_l, b_dn, perm, w_gu, w_dn)


def _combine_kernel(p0_ref, p1_ref, p2_ref, p3_ref, w_ref, x1_ref, g2_ref, fg_ref, ys_ref, o_ref, buf, sem,
                    *, tile0, nc, tpb):
    i = pl.program_id(0) + tile0
    r = _mod_row(i, nc, tpb)
    batch = 8

    def tile_at(row):
        return pl.ds(pl.multiple_of(row * TILE_ROWS, TILE_ROWS), TILE_ROWS)

    def issue(tb, carry):
        t0 = pl.multiple_of(tb * batch, batch)
        for u in range(batch):
            for k, p_ref in enumerate((p0_ref, p1_ref, p2_ref, p3_ref)):
                pltpu.make_async_copy(ys_ref.at[tile_at(p_ref[t0 + u]), :], buf.at[k, tile_at(t0 + u), :], sem).start()
        return carry

    lax.fori_loop(0, TM // batch, issue, 0)
    for k in range(TOP_K):
        pltpu.make_async_copy(ys_ref.at[pl.ds(0, TM * TILE_ROWS), :], buf.at[k], sem).wait()
    w = w_ref[...]
    moe = w[:, 0:1] * _from_tiles(buf.at[0])
    for k in range(1, TOP_K):
        moe = moe + w[:, k:k + 1] * _from_tiles(buf.at[k])
    y = x1_ref[...] + g2_ref[pl.ds(r, 1), :] * moe
    o_ref[...] = _rms(y) * fg_ref[...]


def _combine(pos_flat, w, x1, mod, final_g, ys, *, tile0, ntiles, nc, tpb):
    nt = x1.shape[0] // TM
    slot_spec = lambda k: pl.BlockSpec((TM,), lambda i: (k * nt + tile0 + i,), memory_space=pltpu.SMEM)
    return pl.pallas_call(
        functools.partial(_combine_kernel, tile0=tile0, nc=nc, tpb=tpb),
        grid=(ntiles,),
        in_specs=[slot_spec(k) for k in range(TOP_K)] + [
                  pl.BlockSpec((TM, LANES), lambda i: (i + tile0, 0)),
                  pl.BlockSpec((TM, D_MODEL), lambda i: (i + tile0, 0)),
                  pl.BlockSpec((8, D_MODEL), lambda i: (0, 5)),
                  pl.BlockSpec((1, D_MODEL), lambda i: (0, 0)),
                  pl.BlockSpec(memory_space=pl.ANY)],
        out_specs=pl.BlockSpec((TM, D_MODEL), lambda i: (i, 0)),
        out_shape=jax.ShapeDtypeStruct((ntiles * TM, D_MODEL), F32),
        scratch_shapes=[pltpu.VMEM((TOP_K, TM * TILE_ROWS, LANES), F32), pltpu.SemaphoreType.DMA(())],
        compiler_params=_cparams(("arbitrary",)),
        name="combine",
    )(pos_flat, pos_flat, pos_flat, pos_flat, w, x1, mod, final_g, ys)


def kernel(x_prompt, x_sample, cache_k, cache_v, c, c_ctx, norm1_g, w_mod, b_mod, w_in, q_norm_g, k_norm_g,
           w_fourier, attn_out_g, fourier_out_g, w_out, norm2_g, w_router, b_router, w_gate_up, b_gate_up,
           w_down, b_down, final_g):
    bc, tc, _ = x_prompt.shape
    bl, tl, _ = x_sample.shape
    depth = w_in.shape[0]
    past = cache_k.shape[2]
    assert depth == 1 and tl % TQ == 0 and tl % TM == 0 and (bc * tc) % TM == 0
    n_ctx, n_lat = bc * tc, bl * tl
    n_tok = n_ctx + n_lat
    nc, nl, tpb = n_ctx // TM, n_lat // TM, tl // TM

    xc = x_prompt.reshape(n_ctx, D_MODEL)
    xl = x_sample.reshape(n_lat, D_MODEL)

    c_rows = jnp.concatenate([c_ctx[None, :], c, jnp.zeros((8 - 1 - bl, D_MODEL), F32)], axis=0)
    mod = _modulation(c_rows, w_mod[0], b_mod[0])

    cos_t, sin_t = _rope_tables(tl)
    q, k_heads, vt_ext, f, k_new, v_new = _pre(
        xc, xl, mod, norm1_g, w_in[0].astype(BF16), jnp.tile(q_norm_g, (1, N_HEADS)),
        jnp.tile(k_norm_g, (1, N_KV_HEADS)), _head_mean_matrix(), cos_t, sin_t, nc=nc, nl=nl, tpb=tpb)

    attn_c = _attention(q, k_heads, vt_ext, batch=bc, t=tc, s=tc, q_row0=0)
    keys, vals = _with_cache(k_heads, vt_ext, cache_k[:, 0], cache_v[:, 0], n_lat, bl, tl)
    attn_l = _attention(q, keys, vals, batch=bl, t=tl, s=past + tl, q_row0=n_ctx)

    wf = w_fourier[0].astype(BF16)
    zero = jnp.zeros_like(wf[0])
    wf_pairs = jnp.stack([jnp.block([[wf[0], zero], [zero, wf[1]]]), jnp.block([[wf[2], zero], [zero, wf[3]]])])
    four_c = _fourier(f, wf_pairs, batch=bc, t=tc, row0=0, n1=1)
    four_l = _fourier(f, wf_pairs, batch=bl, t=tl, row0=n_ctx, n1=FFT_N1)

    wr = jnp.pad(w_router[0], ((0, 0), (0, LANES - N_EXPERTS)))
    wr_hi = wr.astype(BF16)
    wr_lo = (wr - wr_hi.astype(F32)).astype(BF16)
    br = jnp.pad(b_router[0], (0, LANES - N_EXPERTS), constant_values=-1e30).reshape(1, LANES)
    x1, h2, idx, gate_w, counts = _mix(xc, xl, attn_c, attn_l, four_c, four_l, mod, attn_out_g, fourier_out_g,
                                       w_out[0].astype(BF16), norm2_g, wr_hi, wr_lo, br, nc=nc, nl=nl, tpb=tpb)
    cnt = counts[0, :N_EXPERTS].astype(I32)
    padded = ((cnt + TG - 1) // TG) * TG
    ends = jnp.cumsum(padded)
    starts = ends - padded
    n_rows = n_tok * TOP_K + (n_tok // TM) * N_EXPERTS * RUN_ALIGN + N_EXPERTS * TG
    n_tiles = n_rows // TG
    tile_ids = jnp.arange(n_tiles, dtype=I32)
    tile_expert = jnp.minimum(jnp.sum((ends[None, :] // TG <= tile_ids[:, None]).astype(I32), axis=1),
                              N_EXPERTS - 1)
    n_valid = (ends[N_EXPERTS - 1] // TG).reshape(1)
    gs_rows = jnp.broadcast_to(jnp.pad(starts.astype(F32), (0, LANES - N_EXPERTS))[None, :], (8, LANES))
    pos, loc, info = _positions(idx, gs_rows)

    xs = _dispatch(info[:, :3, :N_EXPERTS].reshape(-1), ends, loc, h2, n_rows)
    fresh_tile = jnp.concatenate([jnp.ones((1,), I32), (tile_expert[1:] != tile_expert[:-1]).astype(I32)])
    parity = (jnp.cumsum(fresh_tile) - 1) % 2
    experts = jnp.arange(N_EXPERTS, dtype=I32)
    later = jnp.logical_and(padded[None, :] > 0, experts[None, :] > experts[:, None])
    next_of = jnp.min(jnp.where(later, experts[None, :], N_EXPERTS), axis=1)
    next_of = jnp.where(next_of == N_EXPERTS, -1, next_of)
    next_expert = jnp.sum(jnp.where(tile_expert[:, None] == experts[None, :], next_of[None, :], 0), axis=1)
    b_gu = b_gate_up[0].reshape(N_EXPERTS, 1, D_EXPERT, 2)
    ys = _expert_ffn(tile_expert, n_valid, parity.astype(I32), next_expert.astype(I32), xs, w_gate_up[0],
                     b_gu[..., 0], b_gu[..., 1], w_down[0], b_down[0].reshape(N_EXPERTS, 1, D_MODEL),
                     _deinterleave_matrix())

    pos_flat = pos[:TOP_K].reshape(-1)
    y_c = _combine(pos_flat, gate_w, x1, mod, final_g.reshape(1, -1), ys, tile0=0, ntiles=nc, nc=nc, tpb=tpb)
    y_l = _combine(pos_flat, gate_w, x1, mod, final_g.reshape(1, -1), ys, tile0=nc, ntiles=nl, nc=nc, tpb=tpb)

    new_k = k_new[:n_ctx].reshape(bc, 1, tc, N_KV_HEADS, HEAD_DIM)
    new_v = v_new[:n_ctx].reshape(bc, 1, tc, N_KV_HEADS, HEAD_DIM)
    return (y_c.reshape(bc, tc, D_MODEL), y_l.reshape(bl, tl, D_MODEL), new_k, new_v)
```

```python
import functools
import math

import numpy as np
import jax
import jax.numpy as jnp
from jax import lax
from jax.experimental import pallas as pl
from jax.experimental.pallas import tpu as pltpu

F32 = jnp.float32
BF16 = jnp.bfloat16
I32 = jnp.int32

D_MODEL = 1024
HEAD_DIM = 64
N_HEADS = 8
N_KV_HEADS = 2
HEADS_PER_KV = N_HEADS // N_KV_HEADS
ATTN_W = N_HEADS * HEAD_DIM
KV_W = N_KV_HEADS * HEAD_DIM
FOURIER_W = 512
FOURIER_GROUP_W = 128
D_IN = ATTN_W + 2 * KV_W + FOURIER_W
N_EXPERTS = 32
TOP_K = 4
D_EXPERT = 1024
SWIGLU_ALPHA = 1.702
SWIGLU_LIMIT = 7.0
ROPE_THETA = 10000.0
ROT_PAIRS = HEAD_DIM // 4
GRID_W = 64
EPS = 1e-6

LANES = 128
TM = 512
TQ = 512
TG = 256
FFT_N1 = 4
V_ROWS = 80
VMEM_LIMIT = 56 * 1024 * 1024


def _cparams(sem, vmem=VMEM_LIMIT):
    return pltpu.CompilerParams(dimension_semantics=sem, vmem_limit_bytes=vmem)


def _rope_tables(t_lat):
    pos = np.arange(t_lat)
    row = (pos // GRID_W).astype(np.float64)
    col = (pos % GRID_W).astype(np.float64)
    inv = ROPE_THETA ** (-np.arange(ROT_PAIRS, dtype=np.float64) / ROT_PAIRS)
    lane = np.arange(LANES)
    d = lane % HEAD_DIM
    axis = d // (2 * ROT_PAIRS)
    second = (d // ROT_PAIRS) % 2
    p = d % ROT_PAIRS
    ang = np.where(axis[None, :] == 0, row[:, None], col[:, None]) * inv[p][None, :]
    cos = np.cos(ang)
    sin = np.sin(ang) * np.where(second == 0, -1.0, 1.0)[None, :]
    cos = np.concatenate([np.ones((TM, LANES)), cos], axis=0)
    sin = np.concatenate([np.zeros((TM, LANES)), sin], axis=0)
    return jnp.asarray(cos, F32), jnp.asarray(sin, F32)


def _dft_tables(n1, n2):
    c = np.arange(FOURIER_GROUP_W)
    ang = 2.0 * np.pi * np.outer(c, c) / FOURIER_GROUP_W
    z = np.zeros_like(ang)
    cbd = np.block([[np.cos(ang), z], [z, np.cos(ang)]])
    sbd = np.block([[np.sin(ang), z], [z, np.sin(ang)]])
    cd = np.concatenate([cbd, -sbd], axis=1)
    u = np.arange(n2)
    ang2 = 2.0 * np.pi * np.outer(u, u) / n2
    ct, st = np.cos(ang2), np.sin(ang2)
    nt = max(n1 - 1, 1)
    tw_ang = 2.0 * np.pi * np.outer(np.arange(1, nt + 1), u) / (n1 * n2)
    twc = np.repeat(np.cos(tw_ang)[:, :, None], LANES, axis=2)
    tws = np.repeat(np.sin(tw_ang)[:, :, None], LANES, axis=2)
    return (jnp.asarray(cd, F32), jnp.asarray(ct, F32), jnp.asarray(st, F32),
            jnp.asarray(twc, F32), jnp.asarray(tws, F32))


def _head_mean_matrix():
    h = np.arange(ATTN_W) // HEAD_DIM
    return jnp.asarray((h[:, None] == h[None, :]) / HEAD_DIM, BF16)


def _deinterleave_matrix():
    p = np.zeros((2 * LANES, 2 * LANES))
    m = np.arange(LANES)
    p[2 * m, m] = 1.0
    p[2 * m + 1, LANES + m] = 1.0
    return jnp.asarray(p, BF16)


def _mod_row(i, nc, tpb):
    return jnp.where(i < nc, 0, 1 + (i - nc) // tpb)


def _rms(x):
    return x * lax.rsqrt(jnp.mean(x * x, axis=-1, keepdims=True) + EPS)


def _pack_cols(cols, lane):
    out = jnp.zeros(lane.shape, cols[0].dtype)
    for k, c in enumerate(cols):
        out = jnp.where(lane == k, c, out)
    return out


def _mod_kernel(c_ref, w_ref, b_ref, o_ref):
    c = c_ref[...]
    s = c * (1.0 / (1.0 + jnp.exp(-c)))
    o_ref[...] = jnp.dot(s.astype(BF16), w_ref[...].astype(BF16), preferred_element_type=F32) + b_ref[...]


def _modulation(c_rows, w_mod, b_mod):
    n = w_mod.shape[1] // D_MODEL
    return pl.pallas_call(
        _mod_kernel,
        grid=(n,),
        in_specs=[pl.BlockSpec((8, D_MODEL), lambda j: (0, 0)),
                  pl.BlockSpec((D_MODEL, D_MODEL), lambda j: (0, j)),
                  pl.BlockSpec((1, D_MODEL), lambda j: (0, j))],
        out_specs=pl.BlockSpec((8, D_MODEL), lambda j: (0, j)),
        out_shape=jax.ShapeDtypeStruct((8, w_mod.shape[1]), F32),
        compiler_params=_cparams(("parallel",)),
        name="modulation",
    )(c_rows, w_mod, b_mod.reshape(1, -1))


def _pre_kernel(xc_ref, xl_ref, sh_ref, sc_ref, g1_ref, win_ref, qg_ref, kg_ref, bd_ref, cos_ref, sin_ref,
                q_ref, k_ref, vt_ref, f_ref, kn_ref, vn_ref, *, nc, tpb):
    i = pl.program_id(0)
    r = _mod_row(i, nc, tpb)
    x = jnp.where(i < nc, xc_ref[...], xl_ref[...])
    h = _rms(x) * g1_ref[...]
    h = h * (1.0 + sc_ref[pl.ds(r, 1), :]) + sh_ref[pl.ds(r, 1), :]
    proj = jnp.dot(h.astype(BF16), win_ref[...], preferred_element_type=F32)
    q = proj[:, :ATTN_W]
    k = proj[:, ATTN_W:ATTN_W + KV_W]
    v = proj[:, ATTN_W + KV_W:ATTN_W + 2 * KV_W]
    f_ref[...] = proj[:, ATTN_W + 2 * KV_W:].astype(BF16)
    vn_ref[...] = v
    vt = v.T
    tail = (lax.broadcasted_iota(I32, (V_ROWS - HEAD_DIM, vt.shape[1]), 0) == 0).astype(BF16)
    for j in range(N_KV_HEADS):
        vt_ref[j, :HEAD_DIM, :] = vt[j * HEAD_DIM:(j + 1) * HEAD_DIM, :].astype(BF16)
        vt_ref[j, HEAD_DIM:, :] = tail
    q_ms = jnp.dot((q * q).astype(BF16), bd_ref[...], preferred_element_type=F32)
    k_ms = jnp.dot((k * k).astype(BF16), bd_ref[:KV_W, :KV_W], preferred_element_type=F32)
    qn = q * lax.rsqrt(q_ms + EPS) * qg_ref[...]
    kn = k * lax.rsqrt(k_ms + EPS) * kg_ref[...]
    kn_ref[...] = kn
    cos = cos_ref[...]
    sin = sin_ref[...]
    lane = lax.broadcasted_iota(I32, cos.shape, 1)
    first = (lane & ROT_PAIRS) == 0

    def rope(c):
        partner = jnp.where(first, pltpu.roll(c, LANES - ROT_PAIRS, 1), pltpu.roll(c, ROT_PAIRS, 1))
        return c * cos + partner * sin

    k_rot = rope(kn).astype(BF16)
    for j in range(N_KV_HEADS):
        k_ref[j] = k_rot[:, j * HEAD_DIM:(j + 1) * HEAD_DIM]
    scale = HEAD_DIM ** -0.5 * math.log2(math.e)
    q_rot = jnp.concatenate([rope(qn[:, j * LANES:(j + 1) * LANES]) for j in range(ATTN_W // LANES)], axis=1)
    q_ref[...] = (q_rot * scale).T.astype(BF16)


def _pre(xc, xl, mod, norm1_g, w_in, q_g, k_g, bd, cos_t, sin_t, *, nc, nl, tpb):
    n = (nc + nl) * TM
    row = lambda i: (i, 0)
    const = lambda i: (0, 0)
    tab = lambda i: (jnp.where(i < nc, 0, 1 + (i - nc) % tpb), 0)
    return pl.pallas_call(
        functools.partial(_pre_kernel, nc=nc, tpb=tpb),
        grid=(nc + nl,),
        in_specs=[pl.BlockSpec((TM, D_MODEL), lambda i: (jnp.minimum(i, nc - 1), 0)),
                  pl.BlockSpec((TM, D_MODEL), lambda i: (jnp.maximum(i - nc, 0), 0)),
                  pl.BlockSpec((8, D_MODEL), lambda i: (0, 0)),
                  pl.BlockSpec((8, D_MODEL), lambda i: (0, 1)),
                  pl.BlockSpec((1, D_MODEL), const),
                  pl.BlockSpec((D_MODEL, D_IN), const),
                  pl.BlockSpec((1, ATTN_W), const),
                  pl.BlockSpec((1, KV_W), const),
                  pl.BlockSpec((ATTN_W, ATTN_W), const),
                  pl.BlockSpec((TM, LANES), tab),
                  pl.BlockSpec((TM, LANES), tab)],
        out_specs=[pl.BlockSpec((ATTN_W, TM), lambda i: (0, i)),
                   pl.BlockSpec((N_KV_HEADS, TM, HEAD_DIM), lambda i: (0, i, 0)),
                   pl.BlockSpec((N_KV_HEADS, V_ROWS, TM), lambda i: (0, 0, i)),
                   pl.BlockSpec((TM, FOURIER_W), row), pl.BlockSpec((TM, KV_W), row), pl.BlockSpec((TM, KV_W), row)],
        out_shape=[jax.ShapeDtypeStruct((ATTN_W, n), BF16), jax.ShapeDtypeStruct((N_KV_HEADS, n, HEAD_DIM), BF16),
                   jax.ShapeDtypeStruct((N_KV_HEADS, V_ROWS, n), BF16), jax.ShapeDtypeStruct((n, FOURIER_W), BF16),
                   jax.ShapeDtypeStruct((n, KV_W), F32), jax.ShapeDtypeStruct((n, KV_W), F32)],
        compiler_params=_cparams(("parallel",)),
        name="pre_proj",
    )(xc, xl, mod, mod, norm1_g, w_in, q_g, k_g, bd, cos_t, sin_t)


def _attn_kernel(qt_ref, k_ref, vt_ref, o_ref, ot_ref, st_ref, pt_ref):
    def scores(h):
        qt = qt_ref[h * HEAD_DIM:(h + 1) * HEAD_DIM, :]
        st_ref[h % 2] = jnp.dot(k_ref[h // HEADS_PER_KV], qt, preferred_element_type=F32)

    scores(0)
    for h in range(N_HEADS):
        if h + 1 < N_HEADS:
            scores(h + 1)
        st = st_ref[h % 2]
        pt_ref[h % 2] = jnp.exp2(st - jnp.max(st, axis=0, keepdims=True)).astype(BF16)
        ot = jnp.dot(vt_ref[h // HEADS_PER_KV], pt_ref[h % 2], preferred_element_type=F32)
        ot_ref[h * HEAD_DIM:(h + 1) * HEAD_DIM, :] = ot[:HEAD_DIM] * (1.0 / ot[HEAD_DIM:HEAD_DIM + 1])
    o_ref[...] = ot_ref[...].T


def _attention(qt, k_heads, vt_ext, *, batch, t, s, q_row0):
    tq = min(TQ, t)
    qpb = t // tq
    q0 = q_row0 // tq
    return pl.pallas_call(
        _attn_kernel,
        grid=(batch, qpb),
        in_specs=[pl.BlockSpec((ATTN_W, tq), lambda b, i: (0, q0 + b * qpb + i)),
                  pl.BlockSpec((N_KV_HEADS, s, HEAD_DIM), lambda b, i: (0, b, 0)),
                  pl.BlockSpec((N_KV_HEADS, V_ROWS, s), lambda b, i: (0, 0, b))],
        out_specs=pl.BlockSpec((tq, ATTN_W), lambda b, i: (b * qpb + i, 0)),
        out_shape=jax.ShapeDtypeStruct((batch * t, ATTN_W), F32),
        scratch_shapes=[pltpu.VMEM((ATTN_W, tq), F32), pltpu.VMEM((2, s, tq), F32), pltpu.VMEM((2, s, tq), BF16)],
        compiler_params=_cparams(("parallel", "parallel")),
        name="attention",
    )(qt, k_heads, vt_ext)


def _with_cache(k_heads, vt_ext, cache_k, cache_v, n_lat, batch, t):
    past = cache_k.shape[1]
    ck = cache_k.astype(BF16).transpose(2, 0, 1, 3)
    cv = cache_v.astype(BF16).transpose(2, 3, 0, 1)
    tail = jnp.concatenate([jnp.ones((N_KV_HEADS, 1, batch, past), BF16),
                            jnp.zeros((N_KV_HEADS, V_ROWS - HEAD_DIM - 1, batch, past), BF16)], axis=1)
    k_lat = k_heads[:, -n_lat:].reshape(N_KV_HEADS, batch, t, HEAD_DIM)
    v_lat = vt_ext[:, :, -n_lat:].reshape(N_KV_HEADS, V_ROWS, batch, t)
    keys = jnp.concatenate([ck, k_lat], axis=2).reshape(N_KV_HEADS, batch * (past + t), HEAD_DIM)
    vals = jnp.concatenate([jnp.concatenate([cv, tail], axis=1), v_lat], axis=3)
    return keys, vals.reshape(N_KV_HEADS, V_ROWS, batch * (past + t))


def _fourier_kernel(x_ref, cd_ref, ct_ref, st_ref, twc_ref, tws_ref, wf_ref, o_ref, *, n1, n2):
    pw = 2 * FOURIER_GROUP_W
    scale = 1.0 / math.sqrt(n1 * n2 * FOURIER_GROUP_W)
    for gp in range(FOURIER_W // pw):
        z = jnp.dot(x_ref[:, gp * pw:(gp + 1) * pw], cd_ref[...], preferred_element_type=F32)
        zr = [z[t1 * n2:(t1 + 1) * n2, :pw] for t1 in range(n1)]
        zi = [z[t1 * n2:(t1 + 1) * n2, pw:] for t1 in range(n1)]
        if n1 == 1:
            a = [(zr[0], zi[0])]
        else:
            a = [(zr[0] + zr[1] + zr[2] + zr[3], zi[0] + zi[1] + zi[2] + zi[3]),
                 (zr[0] + zi[1] - zr[2] - zi[3], zi[0] - zr[1] - zi[2] + zr[3]),
                 (zr[0] - zr[1] + zr[2] - zr[3], zi[0] - zi[1] + zi[2] - zi[3]),
                 (zr[0] - zi[1] - zr[2] + zi[3], zi[0] + zr[1] - zi[2] - zr[3])]
        for u1 in range(n1):
            ar, ai = a[u1]
            if u1 > 0:
                c = jnp.concatenate([twc_ref[u1 - 1]] * 2, axis=1)
                s = jnp.concatenate([tws_ref[u1 - 1]] * 2, axis=1)
                ar, ai = ar * c + ai * s, ai * c - ar * s
            y = (jnp.dot(ct_ref[...], ar.astype(BF16), preferred_element_type=F32)
                 + jnp.dot(st_ref[...], ai.astype(BF16), preferred_element_type=F32)) * scale
            y = jnp.dot(y.astype(BF16), wf_ref[gp], preferred_element_type=F32)
            rows = pl.ds(u1, n2, stride=n1) if n1 > 1 else slice(None)
            for half in range(2):
                o_ref[2 * gp + half, rows, :] = y[:, half * FOURIER_GROUP_W:(half + 1) * FOURIER_GROUP_W]


def _fourier(f, wf_pairs, *, batch, t, row0, n1):
    n2 = t // n1
    cd, ct, st, twc, tws = _dft_tables(n1, n2)
    cd, ct, st = cd.astype(BF16), ct.astype(BF16), st.astype(BF16)
    nt = twc.shape[0]
    b0 = row0 // t
    c2 = lambda b: (0, 0)
    c3 = lambda b: (0, 0, 0)
    return pl.pallas_call(
        functools.partial(_fourier_kernel, n1=n1, n2=n2),
        grid=(batch,),
        in_specs=[pl.BlockSpec((t, FOURIER_W), lambda b: (b0 + b, 0)),
                  pl.BlockSpec(cd.shape, c2), pl.BlockSpec((n2, n2), c2), pl.BlockSpec((n2, n2), c2),
                  pl.BlockSpec((nt, n2, LANES), c3), pl.BlockSpec((nt, n2, LANES), c3),
                  pl.BlockSpec(wf_pairs.shape, c3)],
        out_specs=pl.BlockSpec((FOURIER_W // FOURIER_GROUP_W, t, FOURIER_GROUP_W), lambda b: (0, b, 0)),
        out_shape=jax.ShapeDtypeStruct((FOURIER_W // FOURIER_GROUP_W, batch * t, FOURIER_GROUP_W), F32),
        compiler_params=_cparams(("parallel",)),
        name="fourier",
    )(f, cd, ct, st, twc, tws, wf_pairs)


def _route(i, h, whi_ref, wlo_ref, br_ref, idx_ref, w_ref, cnt_ref):
    h_hi = h.astype(BF16)
    h_lo = (h - h_hi.astype(F32)).astype(BF16)
    logits = (jnp.dot(h_hi, whi_ref[...], preferred_element_type=F32)
              + jnp.dot(h_lo, whi_ref[...], preferred_element_type=F32)
              + jnp.dot(h_hi, wlo_ref[...], preferred_element_type=F32)) + br_ref[...]
    lane = lax.broadcasted_iota(I32, logits.shape, 1)
    l = logits
    vals, idxs = [], []
    hits = jnp.zeros(logits.shape, F32)
    for _ in range(TOP_K):
        m = jnp.max(l, axis=-1, keepdims=True)
        ix = jnp.min(jnp.where(l == m, lane, LANES), axis=-1, keepdims=True)
        sel = lane == ix
        hits = hits + sel.astype(F32)
        l = jnp.where(sel, -jnp.inf, l)
        vals.append(m)
        idxs.append(ix)
    es = [jnp.exp(v - vals[0]) for v in vals]
    den = es[0] + es[1] + es[2] + es[3]
    idx_ref[...] = _pack_cols(idxs, lane)
    w_ref[...] = _pack_cols([e / den for e in es], lane)

    @pl.when(i == 0)
    def _():
        cnt_ref[...] = jnp.zeros(cnt_ref.shape, F32)

    cnt_ref[...] += _round_up_runs(jnp.sum(hits, axis=0, keepdims=True))


def _mix_kernel(xc_ref, xl_ref, ac_ref, al_ref, fc_ref, fl_ref, g1_ref, sh2_ref, sc2_ref, ag_ref, fg_ref,
                wout_ref, n2g_ref, whi_ref, wlo_ref, br_ref, x1_ref, h2_ref, idx_ref, w_ref, cnt_ref, *, nc, tpb):
    i = pl.program_id(0)
    r = _mod_row(i, nc, tpb)
    is_ctx = i < nc
    x = jnp.where(is_ctx, xc_ref[...], xl_ref[...])
    a = _rms(jnp.where(is_ctx, ac_ref[...], al_ref[...])) * ag_ref[...]
    n_groups = FOURIER_W // FOURIER_GROUP_W
    f = jnp.concatenate([jnp.where(is_ctx, fc_ref[g], fl_ref[g]) for g in range(n_groups)], axis=1)
    f = _rms(f) * fg_ref[...]
    mixed = (jnp.dot(a.astype(BF16), wout_ref[:ATTN_W, :], preferred_element_type=F32)
             + jnp.dot(f.astype(BF16), wout_ref[ATTN_W:, :], preferred_element_type=F32))
    x1 = x + g1_ref[pl.ds(r, 1), :] * mixed
    x1_ref[...] = x1
    h2 = _rms(x1) * n2g_ref[...]
    h2 = h2 * (1.0 + sc2_ref[pl.ds(r, 1), :]) + sh2_ref[pl.ds(r, 1), :]
    h2_ref[...] = h2
    _route(i, h2, whi_ref, wlo_ref, br_ref, idx_ref, w_ref, cnt_ref)


def _mix(xc, xl, ac, al, fc, fl, mod, attn_g, four_g, w_out, norm2_g, wr_hi, wr_lo, br, *, nc, nl, tpb):
    n = (nc + nl) * TM
    lo = lambda i: (jnp.minimum(i, nc - 1), 0)
    hi = lambda i: (jnp.maximum(i - nc, 0), 0)
    const = lambda i: (0, 0)
    row = lambda i: (i, 0)
    return pl.pallas_call(
        functools.partial(_mix_kernel, nc=nc, tpb=tpb),
        grid=(nc + nl,),
        in_specs=[pl.BlockSpec((TM, D_MODEL), lo), pl.BlockSpec((TM, D_MODEL), hi),
                  pl.BlockSpec((TM, ATTN_W), lo), pl.BlockSpec((TM, ATTN_W), hi),
                  pl.BlockSpec((FOURIER_W // FOURIER_GROUP_W, TM, FOURIER_GROUP_W),
                               lambda i: (0, jnp.minimum(i, nc - 1), 0)),
                  pl.BlockSpec((FOURIER_W // FOURIER_GROUP_W, TM, FOURIER_GROUP_W),
                               lambda i: (0, jnp.maximum(i - nc, 0), 0)),
                  pl.BlockSpec((8, D_MODEL), lambda i: (0, 2)),
                  pl.BlockSpec((8, D_MODEL), lambda i: (0, 3)),
                  pl.BlockSpec((8, D_MODEL), lambda i: (0, 4)),
                  pl.BlockSpec((1, ATTN_W), const), pl.BlockSpec((1, FOURIER_W), const),
                  pl.BlockSpec((D_MODEL, D_MODEL), const), pl.BlockSpec((1, D_MODEL), const),
                  pl.BlockSpec((D_MODEL, LANES), const), pl.BlockSpec((D_MODEL, LANES), const),
                  pl.BlockSpec((1, LANES), const)],
        out_specs=[pl.BlockSpec((TM, D_MODEL), row), pl.BlockSpec((TM, D_MODEL), row),
                   pl.BlockSpec((TM, LANES), row), pl.BlockSpec((TM, LANES), row), pl.BlockSpec((8, LANES), const)],
        out_shape=[jax.ShapeDtypeStruct((n, D_MODEL), F32), jax.ShapeDtypeStruct((n, D_MODEL), F32),
                   jax.ShapeDtypeStruct((n, LANES), I32), jax.ShapeDtypeStruct((n, LANES), F32),
                   jax.ShapeDtypeStruct((8, LANES), F32)],
        compiler_params=_cparams(("arbitrary",)),
        name="mix_proj",
    )(xc, xl, ac, al, fc, fl, mod, mod, mod, attn_g, four_g, w_out, norm2_g, wr_hi, wr_lo, br)


RUN_ALIGN = 8
RUN_ROWS = TM * TOP_K + N_EXPERTS * RUN_ALIGN
RUN_SIZES = tuple(TM >> s for s in range(7))


def _round_up_runs(x):
    return jnp.floor((x + (RUN_ALIGN - 1)) * (1.0 / RUN_ALIGN)) * RUN_ALIGN


def _pos_kernel(idx_ref, gs_ref, loct_ref, loc_ref, info_ref, carry_ref):
    i = pl.program_id(0)

    @pl.when(i == 0)
    def _():
        carry_ref[...] = jnp.zeros(carry_ref.shape, F32)

    idx = idx_ref[...]
    lane = lax.broadcasted_iota(I32, idx.shape, 1)
    sels = [lane == idx[:, k:k + 1] for k in range(TOP_K)]
    hits = jnp.zeros(idx.shape, F32)
    for s in sels:
        hits = hits + s.astype(F32)
    r = lax.broadcasted_iota(I32, (TM, TM), 0)
    c = lax.broadcasted_iota(I32, (TM, TM), 1)
    before = (c < r).astype(BF16)
    rank = jnp.dot(before, hits.astype(BF16), preferred_element_type=F32)
    run_len = _round_up_runs(jnp.sum(hits, axis=0, keepdims=True))
    a = lax.broadcasted_iota(I32, (LANES, LANES), 0)
    b = lax.broadcasted_iota(I32, (LANES, LANES), 1)
    units = jnp.broadcast_to(run_len * (1.0 / RUN_ALIGN), (8, LANES)).astype(BF16)
    run_off = jnp.dot(units, (a < b).astype(BF16), preferred_element_type=F32)[0:1, :] * RUN_ALIGN
    run_start = carry_ref[0:1, :] + gs_ref[0:1, :]
    loc_cols = [jnp.sum(jnp.where(s, rank + run_off, 0.0), axis=-1, keepdims=True) for s in sels]
    loc = _pack_cols(loc_cols, lane)
    loc_ref[...] = loc.astype(I32)
    loct_ref[...] = loc.T[:8, :].astype(I32)
    row = lax.broadcasted_iota(I32, (8, LANES), 0)
    info = jnp.where(row == 0, run_start, jnp.where(row == 1, run_len, jnp.where(row == 2, run_off, 0.0)))
    info_ref[...] = info.astype(I32)
    carry_ref[...] += run_len


def _positions(idx, gs_rows):
    n = idx.shape[0]
    nt = n // TM
    return pl.pallas_call(
        _pos_kernel,
        grid=(nt,),
        in_specs=[pl.BlockSpec((TM, LANES), lambda i: (i, 0)), pl.BlockSpec((8, LANES), lambda i: (0, 0))],
        out_specs=[pl.BlockSpec((8, TM), lambda i: (0, i)), pl.BlockSpec((TM, LANES), lambda i: (i, 0)),
                   pl.BlockSpec((None, 8, LANES), lambda i: (i, 0, 0))],
        out_shape=[jax.ShapeDtypeStruct((8, n), I32), jax.ShapeDtypeStruct((n, LANES), I32),
                   jax.ShapeDtypeStruct((nt, 8, LANES), I32)],
        scratch_shapes=[pltpu.VMEM((8, LANES), F32)],
        compiler_params=_cparams(("arbitrary",)),
        name="positions",
    )(idx, gs_rows)


def _run_copies(info_ref, tile, make_copy, act):
    base = tile * (3 * N_EXPERTS)

    def per_expert(e, carry):
        start = info_ref[base + e]
        length = info_ref[base + N_EXPERTS + e]
        off = info_ref[base + 2 * N_EXPERTS + e]
        for size in RUN_SIZES:
            @pl.when((length & size) != 0)
            def _():
                act(make_copy(pl.multiple_of(off, RUN_ALIGN), pl.multiple_of(start, RUN_ALIGN), size))
            taken = jnp.where((length & size) != 0, size, 0)
            off = off + taken
            start = start + taken
        return carry

    lax.fori_loop(0, N_EXPERTS, per_expert, 0)


def _dispatch_kernel(info_ref, ends_ref, loc_ref, h2_ref, xs_ref, runs_ref, zero_ref, sem):
    i = pl.program_id(0)

    @pl.when(i == 0)
    def _():
        zero_ref[...] = jnp.zeros(zero_ref.shape, F32)

        def last_tile(e):
            lo = ends_ref[e - 1] if e > 0 else 0
            start = pl.multiple_of(jnp.maximum(ends_ref[e] - TG, 0), TG)
            return ends_ref[e] > lo, pltpu.make_async_copy(zero_ref, xs_ref.at[pl.ds(start, TG), :], sem)

        for e in range(N_EXPERTS):
            nonempty, cp = last_tile(e)
            pl.when(nonempty)(cp.start)
        for e in range(N_EXPERTS):
            nonempty, cp = last_tile(e)
            pl.when(nonempty)(cp.wait)

        def tail_tile(j):
            return pltpu.make_async_copy(zero_ref, xs_ref.at[pl.ds(pl.multiple_of(j * TG, TG), TG), :], sem)

        first_unused = ends_ref[N_EXPERTS - 1] // TG
        n_tiles = xs_ref.shape[0] // TG
        lax.fori_loop(first_unused, n_tiles, lambda j, c: (tail_tile(j).start(), c)[1], 0)
        lax.fori_loop(first_unused, n_tiles, lambda j, c: (tail_tile(j).wait(), c)[1], 0)

    loc = loc_ref[...]
    row = lax.broadcasted_iota(I32, (RUN_ROWS, TM), 0)
    sel = row == loc[0:1, :]
    for k in range(1, TOP_K):
        sel = jnp.logical_or(sel, row == loc[k:k + 1, :])
    runs_ref[...] = jnp.dot(sel.astype(BF16), h2_ref[...].astype(BF16), preferred_element_type=F32)

    def piece(stacked_row, global_row, size):
        return pltpu.make_async_copy(runs_ref.at[pl.ds(stacked_row, size), :], xs_ref.at[pl.ds(global_row, size), :], sem)

    _run_copies(info_ref, i, piece, lambda cp: cp.start())
    _run_copies(info_ref, i, piece, lambda cp: cp.wait())


def _dispatch(info, ends, loc, h2, n_rows):
    n = h2.shape[0]
    grid_spec = pltpu.PrefetchScalarGridSpec(
        num_scalar_prefetch=2,
        grid=(n // TM,),
        in_specs=[pl.BlockSpec((8, TM), lambda i, *_: (0, i)),
                  pl.BlockSpec((TM, D_MODEL), lambda i, *_: (i, 0))],
        out_specs=pl.BlockSpec(memory_space=pl.ANY),
        scratch_shapes=[pltpu.VMEM((RUN_ROWS, D_MODEL), F32), pltpu.VMEM((TG, D_MODEL), F32),
                        pltpu.SemaphoreType.DMA(())],
    )
    return pl.pallas_call(
        _dispatch_kernel,
        grid_spec=grid_spec,
        out_shape=jax.ShapeDtypeStruct((n_rows, D_MODEL), F32),
        compiler_params=_cparams(("arbitrary",)),
        name="dispatch",
    )(info, ends, loc, h2)


def _ffn_kernel(te_ref, nv_ref, par_ref, nxt_ref, x_ref, bg_ref, bl_ref, bdn_ref, perm_ref, wgu_hbm, wdn_hbm, o_ref,
                wgu_buf, wdn_buf, wg_s, wl_s, wd_s, wsem):
    i = pl.program_id(0)
    e = te_ref[i]
    valid = i < nv_ref[0]
    fresh = jnp.logical_or(i == 0, e != te_ref[jnp.maximum(i - 1, 0)])
    wslot = par_ref[i]

    def weight_copies(expert, s):
        return (pltpu.make_async_copy(wgu_hbm.at[expert], wgu_buf.at[s], wsem.at[s]),
                pltpu.make_async_copy(wdn_hbm.at[expert], wdn_buf.at[s], wsem.at[s]))

    @pl.when(i == 0)
    def _():
        for cp in weight_copies(e, wslot):
            cp.start()

    @pl.when(jnp.logical_and(valid, fresh))
    def _():
        for cp in weight_copies(e, wslot):
            cp.wait()
        for j in range(2 * D_EXPERT // (2 * LANES)):
            blk = wgu_buf[wslot, :, j * 2 * LANES:(j + 1) * 2 * LANES].astype(BF16)
            d = jnp.dot(blk, perm_ref[...], preferred_element_type=F32)
            wg_s[:, j * LANES:(j + 1) * LANES] = d[:, :LANES].astype(BF16)
            wl_s[:, j * LANES:(j + 1) * LANES] = d[:, LANES:].astype(BF16)
        wd_s[...] = wdn_buf[wslot].astype(BF16)

        @pl.when(nxt_ref[i] >= 0)
        def _():
            for cp in weight_copies(nxt_ref[i], 1 - wslot):
                cp.start()

    @pl.when(valid)
    def _():
        x = x_ref[...].astype(BF16)
        g = jnp.dot(x, wg_s[...], preferred_element_type=F32) + bg_ref[...]
        l = jnp.dot(x, wl_s[...], preferred_element_type=F32) + bl_ref[...]
        g = jnp.minimum(g, SWIGLU_LIMIT)
        l = jnp.clip(l, -SWIGLU_LIMIT, SWIGLU_LIMIT)
        a = (l + 1.0) * (g * (1.0 / (1.0 + jnp.exp(-SWIGLU_ALPHA * g))))
        o_ref[...] = jnp.dot(a.astype(BF16), wd_s[...], preferred_element_type=F32) + bdn_ref[...]

    @pl.when(jnp.logical_not(valid))
    def _():
        o_ref[...] = jnp.zeros(o_ref.shape, F32)


def _expert_ffn(tile_expert, n_valid, parity, next_expert, xs, w_gu, b_g, b_l, w_dn, b_dn, perm):
    p = xs.shape[0]
    nt = p // TG
    tile = lambda i, te, nv, *_: (jnp.minimum(i, nv[0] - 1), 0)
    ex3 = lambda i, te, *_: (te[i], 0, 0)
    grid_spec = pltpu.PrefetchScalarGridSpec(
        num_scalar_prefetch=4,
        grid=(nt,),
        in_specs=[pl.BlockSpec((TG, D_MODEL), tile),
                  pl.BlockSpec((None, 1, D_EXPERT), ex3),
                  pl.BlockSpec((None, 1, D_EXPERT), ex3),
                  pl.BlockSpec((None, 1, D_MODEL), ex3),
                  pl.BlockSpec((2 * LANES, 2 * LANES), lambda i, *_: (0, 0)),
                  pl.BlockSpec(memory_space=pl.ANY),
                  pl.BlockSpec(memory_space=pl.ANY)],
        out_specs=pl.BlockSpec((TG, D_MODEL), lambda i, *_: (i, 0)),
        scratch_shapes=[pltpu.VMEM((2, D_MODEL, 2 * D_EXPERT), F32), pltpu.VMEM((2, D_EXPERT, D_MODEL), F32),
                        pltpu.VMEM((D_MODEL, D_EXPERT), BF16), pltpu.VMEM((D_MODEL, D_EXPERT), BF16),
                        pltpu.VMEM((D_EXPERT, D_MODEL), BF16), pltpu.SemaphoreType.DMA((2,))],
    )
    return pl.pallas_call(
        _ffn_kernel,
        grid_spec=grid_spec,
        out_shape=jax.ShapeDtypeStruct((p, D_MODEL), F32),
        compiler_params=_cparams(("arbitrary",)),
        name="expert_ffn",
    )(tile_expert, n_valid, parity, next_expert, xs, b_g, b_l, b_dn, perm, w_gu, w_dn)


FILL_SIZES = tuple(2048 >> s for s in range(9))


def _combine_kernel(info_ref, loc_ref, w_ref, x1_ref, g2_ref, fg_ref, ys_ref, o_ref, runs_ref, sem, *, tile0, nc, tpb):
    j = pl.program_id(0)
    i = j + tile0
    r = _mod_row(i, nc, tpb)
    slot = j % 2

    def gather(tile, s, act):
        def piece(stacked_row, global_row, size):
            return pltpu.make_async_copy(ys_ref.at[pl.ds(global_row, size), :],
                                         runs_ref.at[s, pl.ds(stacked_row, size), :], sem.at[s])

        _run_copies(info_ref, tile, piece, act)
        last = tile * (3 * N_EXPERTS) + N_EXPERTS - 1
        used = info_ref[last + 2 * N_EXPERTS] + info_ref[last + N_EXPERTS]
        rest = RUN_ROWS - used
        done = used * 0
        for size in FILL_SIZES:
            @pl.when((rest & size) != 0)
            def _():
                act(piece(pl.multiple_of(used + done, RUN_ALIGN), pl.multiple_of(done, RUN_ALIGN), size))
            done = done + jnp.where((rest & size) != 0, size, 0)

    @pl.when(j == 0)
    def _():
        gather(i, 0, lambda cp: cp.start())

    @pl.when(j + 1 < pl.num_programs(0))
    def _():
        gather(i + 1, 1 - slot, lambda cp: cp.start())

    pltpu.make_async_copy(ys_ref.at[pl.ds(0, RUN_ROWS), :], runs_ref.at[slot], sem.at[slot]).wait()
    loc = loc_ref[...]
    w = w_ref[...]
    col = lax.broadcasted_iota(I32, (TM, RUN_ROWS), 1)
    mix = jnp.where(col == loc[:, 0:1], w[:, 0:1], 0.0)
    for k in range(1, TOP_K):
        mix = mix + jnp.where(col == loc[:, k:k + 1], w[:, k:k + 1], 0.0)
    moe = jnp.dot(mix.astype(BF16), runs_ref[slot].astype(BF16), preferred_element_type=F32)
    y = x1_ref[...] + g2_ref[pl.ds(r, 1), :] * moe
    o_ref[...] = _rms(y) * fg_ref[...]


def _combine(info, loc, w, x1, mod, final_g, ys, *, tile0, ntiles, nc, tpb):
    grid_spec = pltpu.PrefetchScalarGridSpec(
        num_scalar_prefetch=1,
        grid=(ntiles,),
        in_specs=[pl.BlockSpec((TM, LANES), lambda i, *_: (i + tile0, 0)),
                  pl.BlockSpec((TM, LANES), lambda i, *_: (i + tile0, 0)),
                  pl.BlockSpec((TM, D_MODEL), lambda i, *_: (i + tile0, 0)),
                  pl.BlockSpec((8, D_MODEL), lambda i, *_: (0, 5)),
                  pl.BlockSpec((1, D_MODEL), lambda i, *_: (0, 0)),
                  pl.BlockSpec(memory_space=pl.ANY)],
        out_specs=pl.BlockSpec((TM, D_MODEL), lambda i, *_: (i, 0)),
        scratch_shapes=[pltpu.VMEM((2, RUN_ROWS, D_MODEL), F32), pltpu.SemaphoreType.DMA((2,))],
    )
    return pl.pallas_call(
        functools.partial(_combine_kernel, tile0=tile0, nc=nc, tpb=tpb),
        grid_spec=grid_spec,
        out_shape=jax.ShapeDtypeStruct((ntiles * TM, D_MODEL), F32),
        compiler_params=_cparams(("arbitrary",)),
        name="combine",
    )(info, loc, w, x1, mod, final_g, ys)


def kernel(x_prompt, x_sample, cache_k, cache_v, c, c_ctx, norm1_g, w_mod, b_mod, w_in, q_norm_g, k_norm_g,
           w_fourier, attn_out_g, fourier_out_g, w_out, norm2_g, w_router, b_router, w_gate_up, b_gate_up,
           w_down, b_down, final_g):
    bc, tc, _ = x_prompt.shape
    bl, tl, _ = x_sample.shape
    depth = w_in.shape[0]
    past = cache_k.shape[2]
    assert depth == 1 and tl % TQ == 0 and tl % TM == 0 and (bc * tc) % TM == 0
    n_ctx, n_lat = bc * tc, bl * tl
    n_tok = n_ctx + n_lat
    nc, nl, tpb = n_ctx // TM, n_lat // TM, tl // TM

    xc = x_prompt.reshape(n_ctx, D_MODEL)
    xl = x_sample.reshape(n_lat, D_MODEL)

    c_rows = jnp.concatenate([c_ctx[None, :], c, jnp.zeros((8 - 1 - bl, D_MODEL), F32)], axis=0)
    mod = _modulation(c_rows, w_mod[0], b_mod[0])

    cos_t, sin_t = _rope_tables(tl)
    q, k_heads, vt_ext, f, k_new, v_new = _pre(
        xc, xl, mod, norm1_g, w_in[0].astype(BF16), jnp.tile(q_norm_g, (1, N_HEADS)),
        jnp.tile(k_norm_g, (1, N_KV_HEADS)), _head_mean_matrix(), cos_t, sin_t, nc=nc, nl=nl, tpb=tpb)

    attn_c = _attention(q, k_heads, vt_ext, batch=bc, t=tc, s=tc, q_row0=0)
    keys, vals = _with_cache(k_heads, vt_ext, cache_k[:, 0], cache_v[:, 0], n_lat, bl, tl)
    attn_l = _attention(q, keys, vals, batch=bl, t=tl, s=past + tl, q_row0=n_ctx)

    wf = w_fourier[0].astype(BF16)
    zero = jnp.zeros_like(wf[0])
    wf_pairs = jnp.stack([jnp.block([[wf[0], zero], [zero, wf[1]]]), jnp.block([[wf[2], zero], [zero, wf[3]]])])
    four_c = _fourier(f, wf_pairs, batch=bc, t=tc, row0=0, n1=1)
    four_l = _fourier(f, wf_pairs, batch=bl, t=tl, row0=n_ctx, n1=FFT_N1)

    wr = jnp.pad(w_router[0], ((0, 0), (0, LANES - N_EXPERTS)))
    wr_hi = wr.astype(BF16)
    wr_lo = (wr - wr_hi.astype(F32)).astype(BF16)
    br = jnp.pad(b_router[0], (0, LANES - N_EXPERTS), constant_values=-1e30).reshape(1, LANES)
    x1, h2, idx, gate_w, counts = _mix(xc, xl, attn_c, attn_l, four_c, four_l, mod, attn_out_g, fourier_out_g,
                                       w_out[0].astype(BF16), norm2_g, wr_hi, wr_lo, br, nc=nc, nl=nl, tpb=tpb)
    cnt = counts[0, :N_EXPERTS].astype(I32)
    padded = ((cnt + TG - 1) // TG) * TG
    ends = jnp.cumsum(padded)
    starts = ends - padded
    n_rows = n_tok * TOP_K + (n_tok // TM) * N_EXPERTS * RUN_ALIGN + N_EXPERTS * TG
    n_tiles = n_rows // TG
    tile_ids = jnp.arange(n_tiles, dtype=I32)
    tile_expert = jnp.minimum(jnp.sum((ends[None, :] // TG <= tile_ids[:, None]).astype(I32), axis=1),
                              N_EXPERTS - 1)
    n_valid = (ends[N_EXPERTS - 1] // TG).reshape(1)
    gs_rows = jnp.broadcast_to(jnp.pad(starts.astype(F32), (0, LANES - N_EXPERTS))[None, :], (8, LANES))
    loc_t, loc, info = _positions(idx, gs_rows)
    info = info[:, :3, :N_EXPERTS].reshape(-1)

    xs = _dispatch(info, ends, loc_t, h2, n_rows)
    fresh_tile = jnp.concatenate([jnp.ones((1,), I32), (tile_expert[1:] != tile_expert[:-1]).astype(I32)])
    parity = (jnp.cumsum(fresh_tile) - 1) % 2
    experts = jnp.arange(N_EXPERTS, dtype=I32)
    later = jnp.logical_and(padded[None, :] > 0, experts[None, :] > experts[:, None])
    next_of = jnp.min(jnp.where(later, experts[None, :], N_EXPERTS), axis=1)
    next_of = jnp.where(next_of == N_EXPERTS, -1, next_of)
    next_expert = jnp.sum(jnp.where(tile_expert[:, None] == experts[None, :], next_of[None, :], 0), axis=1)
    b_gu = b_gate_up[0].reshape(N_EXPERTS, 1, D_EXPERT, 2)
    ys = _expert_ffn(tile_expert, n_valid, parity.astype(I32), next_expert.astype(I32), xs, w_gate_up[0],
                     b_gu[..., 0], b_gu[..., 1], w_down[0], b_down[0].reshape(N_EXPERTS, 1, D_MODEL),
                     _deinterleave_matrix())

    y_c = _combine(info, loc, gate_w, x1, mod, final_g.reshape(1, -1), ys, tile0=0, ntiles=nc, nc=nc, tpb=tpb)
    y_l = _combine(info, loc, gate_w, x1, mod, final_g.reshape(1, -1), ys, tile0=nc, ntiles=nl, nc=nc, tpb=tpb)

    new_k = k_new[:n_ctx].reshape(bc, 1, tc, N_KV_HEADS, HEAD_DIM)
    new_v = v_new[:n_ctx].reshape(bc, 1, tc, N_KV_HEADS, HEAD_DIM)
    return (y_c.reshape(bc, tc, D_MODEL), y_l.reshape(bl, tl, D_MODEL), new_k, new_v)
```

```python
import functools
import math

import numpy as np
import jax
import jax.numpy as jnp
from jax import lax
from jax.experimental import pallas as pl
from jax.experimental.pallas import tpu as pltpu

F32 = jnp.float32
BF16 = jnp.bfloat16
I32 = jnp.int32

D_MODEL = 1024
HEAD_DIM = 64
N_HEADS = 8
N_KV_HEADS = 2
HEADS_PER_KV = N_HEADS // N_KV_HEADS
ATTN_W = N_HEADS * HEAD_DIM
KV_W = N_KV_HEADS * HEAD_DIM
FOURIER_W = 512
FOURIER_GROUP_W = 128
D_IN = ATTN_W + 2 * KV_W + FOURIER_W
N_EXPERTS = 32
TOP_K = 4
D_EXPERT = 1024
SWIGLU_ALPHA = 1.702
SWIGLU_LIMIT = 7.0
ROPE_THETA = 10000.0
ROT_PAIRS = HEAD_DIM // 4
GRID_W = 64
EPS = 1e-6

LANES = 128
TM = 512
TQ = 512
TG = 256
FFT_N1 = 4
V_ROWS = 80
VMEM_LIMIT = 56 * 1024 * 1024


def _cparams(sem, vmem=VMEM_LIMIT):
    return pltpu.CompilerParams(dimension_semantics=sem, vmem_limit_bytes=vmem)


def _rope_tables(t_lat):
    pos = np.arange(t_lat)
    row = (pos // GRID_W).astype(np.float64)
    col = (pos % GRID_W).astype(np.float64)
    inv = ROPE_THETA ** (-np.arange(ROT_PAIRS, dtype=np.float64) / ROT_PAIRS)
    lane = np.arange(LANES)
    d = lane % HEAD_DIM
    axis = d // (2 * ROT_PAIRS)
    second = (d // ROT_PAIRS) % 2
    p = d % ROT_PAIRS
    ang = np.where(axis[None, :] == 0, row[:, None], col[:, None]) * inv[p][None, :]
    cos = np.cos(ang)
    sin = np.sin(ang) * np.where(second == 0, -1.0, 1.0)[None, :]
    cos = np.concatenate([np.ones((TM, LANES)), cos], axis=0)
    sin = np.concatenate([np.zeros((TM, LANES)), sin], axis=0)
    return jnp.asarray(cos, F32), jnp.asarray(sin, F32)


def _dft_tables(n1, n2):
    c = np.arange(FOURIER_GROUP_W)
    ang = 2.0 * np.pi * np.outer(c, c) / FOURIER_GROUP_W
    z = np.zeros_like(ang)
    cbd = np.block([[np.cos(ang), z], [z, np.cos(ang)]])
    sbd = np.block([[np.sin(ang), z], [z, np.sin(ang)]])
    cd = np.concatenate([cbd, -sbd], axis=1)
    u = np.arange(n2)
    ang2 = 2.0 * np.pi * np.outer(u, u) / n2
    ct, st = np.cos(ang2), np.sin(ang2)
    nt = max(n1 - 1, 1)
    tw_ang = 2.0 * np.pi * np.outer(np.arange(1, nt + 1), u) / (n1 * n2)
    twc = np.repeat(np.cos(tw_ang)[:, :, None], LANES, axis=2)
    tws = np.repeat(np.sin(tw_ang)[:, :, None], LANES, axis=2)
    return (jnp.asarray(cd, F32), jnp.asarray(ct, F32), jnp.asarray(st, F32),
            jnp.asarray(twc, F32), jnp.asarray(tws, F32))


def _head_mean_matrix():
    h = np.arange(ATTN_W) // HEAD_DIM
    return jnp.asarray((h[:, None] == h[None, :]) / HEAD_DIM, BF16)


def _deinterleave_matrix():
    p = np.zeros((2 * LANES, 2 * LANES))
    m = np.arange(LANES)
    p[2 * m, m] = 1.0
    p[2 * m + 1, LANES + m] = 1.0
    return jnp.asarray(p, BF16)


def _mod_row(i, nc, tpb):
    return jnp.where(i < nc, 0, 1 + (i - nc) // tpb)


def _rms(x):
    return x * lax.rsqrt(jnp.mean(x * x, axis=-1, keepdims=True) + EPS)


def _pack_cols(cols, lane):
    out = jnp.zeros(lane.shape, cols[0].dtype)
    for k, c in enumerate(cols):
        out = jnp.where(lane == k, c, out)
    return out


def _mod_kernel(c_ref, w_ref, b_ref, o_ref):
    c = c_ref[...]
    s = c * (1.0 / (1.0 + jnp.exp(-c)))
    o_ref[...] = jnp.dot(s.astype(BF16), w_ref[...].astype(BF16), preferred_element_type=F32) + b_ref[...]


def _modulation(c_rows, w_mod, b_mod):
    n = w_mod.shape[1] // D_MODEL
    return pl.pallas_call(
        _mod_kernel,
        grid=(n,),
        in_specs=[pl.BlockSpec((8, D_MODEL), lambda j: (0, 0)),
                  pl.BlockSpec((D_MODEL, D_MODEL), lambda j: (0, j)),
                  pl.BlockSpec((1, D_MODEL), lambda j: (0, j))],
        out_specs=pl.BlockSpec((8, D_MODEL), lambda j: (0, j)),
        out_shape=jax.ShapeDtypeStruct((8, w_mod.shape[1]), F32),
        compiler_params=_cparams(("parallel",)),
        name="modulation",
    )(c_rows, w_mod, b_mod.reshape(1, -1))


def _pre_kernel(xc_ref, xl_ref, sh_ref, sc_ref, g1_ref, win_ref, qg_ref, kg_ref, bd_ref, cos_ref, sin_ref,
                q_ref, k_ref, vt_ref, f_ref, kn_ref, vn_ref, *, nc, tpb):
    i = pl.program_id(0)
    r = _mod_row(i, nc, tpb)
    x = jnp.where(i < nc, xc_ref[...], xl_ref[...])
    h = _rms(x) * g1_ref[...]
    h = h * (1.0 + sc_ref[pl.ds(r, 1), :]) + sh_ref[pl.ds(r, 1), :]
    proj = jnp.dot(h.astype(BF16), win_ref[...], preferred_element_type=F32)
    q = proj[:, :ATTN_W]
    k = proj[:, ATTN_W:ATTN_W + KV_W]
    v = proj[:, ATTN_W + KV_W:ATTN_W + 2 * KV_W]
    f_ref[...] = proj[:, ATTN_W + 2 * KV_W:].astype(BF16)
    vn_ref[...] = v
    vt = v.T
    tail = (lax.broadcasted_iota(I32, (V_ROWS - HEAD_DIM, vt.shape[1]), 0) == 0).astype(BF16)
    for j in range(N_KV_HEADS):
        vt_ref[j, :HEAD_DIM, :] = vt[j * HEAD_DIM:(j + 1) * HEAD_DIM, :].astype(BF16)
        vt_ref[j, HEAD_DIM:, :] = tail
    q_ms = jnp.dot((q * q).astype(BF16), bd_ref[...], preferred_element_type=F32)
    k_ms = jnp.dot((k * k).astype(BF16), bd_ref[:KV_W, :KV_W], preferred_element_type=F32)
    qn = q * lax.rsqrt(q_ms + EPS) * qg_ref[...]
    kn = k * lax.rsqrt(k_ms + EPS) * kg_ref[...]
    kn_ref[...] = kn
    cos = cos_ref[...]
    sin = sin_ref[...]
    lane = lax.broadcasted_iota(I32, cos.shape, 1)
    first = (lane & ROT_PAIRS) == 0

    def rope(c):
        partner = jnp.where(first, pltpu.roll(c, LANES - ROT_PAIRS, 1), pltpu.roll(c, ROT_PAIRS, 1))
        return c * cos + partner * sin

    k_rot = rope(kn).astype(BF16)
    for j in range(N_KV_HEADS):
        k_ref[j] = k_rot[:, j * HEAD_DIM:(j + 1) * HEAD_DIM]
    scale = HEAD_DIM ** -0.5 * math.log2(math.e)
    q_rot = jnp.concatenate([rope(qn[:, j * LANES:(j + 1) * LANES]) for j in range(ATTN_W // LANES)], axis=1)
    q_ref[...] = (q_rot * scale).T.astype(BF16)


def _pre(xc, xl, mod, norm1_g, w_in, q_g, k_g, bd, cos_t, sin_t, *, nc, nl, tpb):
    n = (nc + nl) * TM
    row = lambda i: (i, 0)
    const = lambda i: (0, 0)
    tab = lambda i: (jnp.where(i < nc, 0, 1 + (i - nc) % tpb), 0)
    return pl.pallas_call(
        functools.partial(_pre_kernel, nc=nc, tpb=tpb),
        grid=(nc + nl,),
        in_specs=[pl.BlockSpec((TM, D_MODEL), lambda i: (jnp.minimum(i, nc - 1), 0)),
                  pl.BlockSpec((TM, D_MODEL), lambda i: (jnp.maximum(i - nc, 0), 0)),
                  pl.BlockSpec((8, D_MODEL), lambda i: (0, 0)),
                  pl.BlockSpec((8, D_MODEL), lambda i: (0, 1)),
                  pl.BlockSpec((1, D_MODEL), const),
                  pl.BlockSpec((D_MODEL, D_IN), const),
                  pl.BlockSpec((1, ATTN_W), const),
                  pl.BlockSpec((1, KV_W), const),
                  pl.BlockSpec((ATTN_W, ATTN_W), const),
                  pl.BlockSpec((TM, LANES), tab),
                  pl.BlockSpec((TM, LANES), tab)],
        out_specs=[pl.BlockSpec((ATTN_W, TM), lambda i: (0, i)),
                   pl.BlockSpec((N_KV_HEADS, TM, HEAD_DIM), lambda i: (0, i, 0)),
                   pl.BlockSpec((N_KV_HEADS, V_ROWS, TM), lambda i: (0, 0, i)),
                   pl.BlockSpec((TM, FOURIER_W), row), pl.BlockSpec((TM, KV_W), row), pl.BlockSpec((TM, KV_W), row)],
        out_shape=[jax.ShapeDtypeStruct((ATTN_W, n), BF16), jax.ShapeDtypeStruct((N_KV_HEADS, n, HEAD_DIM), BF16),
                   jax.ShapeDtypeStruct((N_KV_HEADS, V_ROWS, n), BF16), jax.ShapeDtypeStruct((n, FOURIER_W), BF16),
                   jax.ShapeDtypeStruct((n, KV_W), F32), jax.ShapeDtypeStruct((n, KV_W), F32)],
        compiler_params=_cparams(("parallel",)),
        name="pre_proj",
    )(xc, xl, mod, mod, norm1_g, w_in, q_g, k_g, bd, cos_t, sin_t)


def _attn_kernel(qt_ref, k_ref, vt_ref, o_ref, ot_ref, st_ref, pt_ref):
    def scores(h):
        qt = qt_ref[h * HEAD_DIM:(h + 1) * HEAD_DIM, :]
        st_ref[h % 2] = jnp.dot(k_ref[h // HEADS_PER_KV], qt, preferred_element_type=F32)

    scores(0)
    for h in range(N_HEADS):
        if h + 1 < N_HEADS:
            scores(h + 1)
        st = st_ref[h % 2]
        pt_ref[h % 2] = jnp.exp2(st - jnp.max(st, axis=0, keepdims=True)).astype(BF16)
        ot = jnp.dot(vt_ref[h // HEADS_PER_KV], pt_ref[h % 2], preferred_element_type=F32)
        ot_ref[h * HEAD_DIM:(h + 1) * HEAD_DIM, :] = ot[:HEAD_DIM] * (1.0 / ot[HEAD_DIM:HEAD_DIM + 1])
    o_ref[...] = ot_ref[...].T


def _attention(qt, k_heads, vt_ext, *, batch, t, s, q_row0):
    tq = min(TQ, t)
    qpb = t // tq
    q0 = q_row0 // tq
    return pl.pallas_call(
        _attn_kernel,
        grid=(batch, qpb),
        in_specs=[pl.BlockSpec((ATTN_W, tq), lambda b, i: (0, q0 + b * qpb + i)),
                  pl.BlockSpec((N_KV_HEADS, s, HEAD_DIM), lambda b, i: (0, b, 0)),
                  pl.BlockSpec((N_KV_HEADS, V_ROWS, s), lambda b, i: (0, 0, b))],
        out_specs=pl.BlockSpec((tq, ATTN_W), lambda b, i: (b * qpb + i, 0)),
        out_shape=jax.ShapeDtypeStruct((batch * t, ATTN_W), F32),
        scratch_shapes=[pltpu.VMEM((ATTN_W, tq), F32), pltpu.VMEM((2, s, tq), F32), pltpu.VMEM((2, s, tq), BF16)],
        compiler_params=_cparams(("parallel", "parallel")),
        name="attention",
    )(qt, k_heads, vt_ext)


def _with_cache(k_heads, vt_ext, cache_k, cache_v, n_lat, batch, t):
    past = cache_k.shape[1]
    ck = cache_k.astype(BF16).transpose(2, 0, 1, 3)
    cv = cache_v.astype(BF16).transpose(2, 3, 0, 1)
    tail = jnp.concatenate([jnp.ones((N_KV_HEADS, 1, batch, past), BF16),
                            jnp.zeros((N_KV_HEADS, V_ROWS - HEAD_DIM - 1, batch, past), BF16)], axis=1)
    k_lat = k_heads[:, -n_lat:].reshape(N_KV_HEADS, batch, t, HEAD_DIM)
    v_lat = vt_ext[:, :, -n_lat:].reshape(N_KV_HEADS, V_ROWS, batch, t)
    keys = jnp.concatenate([ck, k_lat], axis=2).reshape(N_KV_HEADS, batch * (past + t), HEAD_DIM)
    vals = jnp.concatenate([jnp.concatenate([cv, tail], axis=1), v_lat], axis=3)
    return keys, vals.reshape(N_KV_HEADS, V_ROWS, batch * (past + t))


def _fourier_kernel(x_ref, cd_ref, ct_ref, st_ref, twc_ref, tws_ref, wf_ref, o_ref, *, n1, n2):
    pw = 2 * FOURIER_GROUP_W
    scale = 1.0 / math.sqrt(n1 * n2 * FOURIER_GROUP_W)
    for gp in range(FOURIER_W // pw):
        z = jnp.dot(x_ref[:, gp * pw:(gp + 1) * pw], cd_ref[...], preferred_element_type=F32)
        zr = [z[t1 * n2:(t1 + 1) * n2, :pw] for t1 in range(n1)]
        zi = [z[t1 * n2:(t1 + 1) * n2, pw:] for t1 in range(n1)]
        if n1 == 1:
            a = [(zr[0], zi[0])]
        else:
            a = [(zr[0] + zr[1] + zr[2] + zr[3], zi[0] + zi[1] + zi[2] + zi[3]),
                 (zr[0] + zi[1] - zr[2] - zi[3], zi[0] - zr[1] - zi[2] + zr[3]),
                 (zr[0] - zr[1] + zr[2] - zr[3], zi[0] - zi[1] + zi[2] - zi[3]),
                 (zr[0] - zi[1] - zr[2] + zi[3], zi[0] + zr[1] - zi[2] - zr[3])]
        for u1 in range(n1):
            ar, ai = a[u1]
            if u1 > 0:
                c = jnp.concatenate([twc_ref[u1 - 1]] * 2, axis=1)
                s = jnp.concatenate([tws_ref[u1 - 1]] * 2, axis=1)
                ar, ai = ar * c + ai * s, ai * c - ar * s
            y = (jnp.dot(ct_ref[...], ar.astype(BF16), preferred_element_type=F32)
                 + jnp.dot(st_ref[...], ai.astype(BF16), preferred_element_type=F32)) * scale
            y = jnp.dot(y.astype(BF16), wf_ref[gp], preferred_element_type=F32)
            rows = pl.ds(u1, n2, stride=n1) if n1 > 1 else slice(None)
            for half in range(2):
                o_ref[2 * gp + half, rows, :] = y[:, half * FOURIER_GROUP_W:(half + 1) * FOURIER_GROUP_W]


def _fourier(f, wf_pairs, *, batch, t, row0, n1):
    n2 = t // n1
    cd, ct, st, twc, tws = _dft_tables(n1, n2)
    cd, ct, st = cd.astype(BF16), ct.astype(BF16), st.astype(BF16)
    nt = twc.shape[0]
    b0 = row0 // t
    c2 = lambda b: (0, 0)
    c3 = lambda b: (0, 0, 0)
    return pl.pallas_call(
        functools.partial(_fourier_kernel, n1=n1, n2=n2),
        grid=(batch,),
        in_specs=[pl.BlockSpec((t, FOURIER_W), lambda b: (b0 + b, 0)),
                  pl.BlockSpec(cd.shape, c2), pl.BlockSpec((n2, n2), c2), pl.BlockSpec((n2, n2), c2),
                  pl.BlockSpec((nt, n2, LANES), c3), pl.BlockSpec((nt, n2, LANES), c3),
                  pl.BlockSpec(wf_pairs.shape, c3)],
        out_specs=pl.BlockSpec((FOURIER_W // FOURIER_GROUP_W, t, FOURIER_GROUP_W), lambda b: (0, b, 0)),
        out_shape=jax.ShapeDtypeStruct((FOURIER_W // FOURIER_GROUP_W, batch * t, FOURIER_GROUP_W), F32),
        compiler_params=_cparams(("parallel",)),
        name="fourier",
    )(f, cd, ct, st, twc, tws, wf_pairs)


def _route(i, h, whi_ref, wlo_ref, br_ref, idx_ref, w_ref, cnt_ref):
    h_hi = h.astype(BF16)
    h_lo = (h - h_hi.astype(F32)).astype(BF16)
    logits = (jnp.dot(h_hi, whi_ref[...], preferred_element_type=F32)
              + jnp.dot(h_lo, whi_ref[...], preferred_element_type=F32)
              + jnp.dot(h_hi, wlo_ref[...], preferred_element_type=F32)) + br_ref[...]
    lane = lax.broadcasted_iota(I32, logits.shape, 1)
    l = logits
    vals, idxs = [], []
    hits = jnp.zeros(logits.shape, F32)
    for _ in range(TOP_K):
        m = jnp.max(l, axis=-1, keepdims=True)
        ix = jnp.min(jnp.where(l == m, lane, LANES), axis=-1, keepdims=True)
        sel = lane == ix
        hits = hits + sel.astype(F32)
        l = jnp.where(sel, -jnp.inf, l)
        vals.append(m)
        idxs.append(ix)
    es = [jnp.exp(v - vals[0]) for v in vals]
    den = es[0] + es[1] + es[2] + es[3]
    idx_ref[...] = _pack_cols(idxs, lane)
    w_ref[...] = _pack_cols([e / den for e in es], lane)

    @pl.when(i == 0)
    def _():
        cnt_ref[...] = jnp.zeros(cnt_ref.shape, F32)

    cnt_ref[...] += _round_up_runs(jnp.sum(hits, axis=0, keepdims=True))


def _mix_kernel(xc_ref, xl_ref, ac_ref, al_ref, fc_ref, fl_ref, g1_ref, sh2_ref, sc2_ref, ag_ref, fg_ref,
                wout_ref, n2g_ref, whi_ref, wlo_ref, br_ref, x1_ref, h2_ref, idx_ref, w_ref, cnt_ref, *, nc, tpb):
    i = pl.program_id(0)
    r = _mod_row(i, nc, tpb)
    is_ctx = i < nc
    x = jnp.where(is_ctx, xc_ref[...], xl_ref[...])
    a = _rms(jnp.where(is_ctx, ac_ref[...], al_ref[...])) * ag_ref[...]
    n_groups = FOURIER_W // FOURIER_GROUP_W
    f = jnp.concatenate([jnp.where(is_ctx, fc_ref[g], fl_ref[g]) for g in range(n_groups)], axis=1)
    f = _rms(f) * fg_ref[...]
    mixed = (jnp.dot(a.astype(BF16), wout_ref[:ATTN_W, :], preferred_element_type=F32)
             + jnp.dot(f.astype(BF16), wout_ref[ATTN_W:, :], preferred_element_type=F32))
    x1 = x + g1_ref[pl.ds(r, 1), :] * mixed
    x1_ref[...] = x1
    h2 = _rms(x1) * n2g_ref[...]
    h2 = h2 * (1.0 + sc2_ref[pl.ds(r, 1), :]) + sh2_ref[pl.ds(r, 1), :]
    h2_ref[...] = h2
    _route(i, h2, whi_ref, wlo_ref, br_ref, idx_ref, w_ref, cnt_ref)


def _mix(xc, xl, ac, al, fc, fl, mod, attn_g, four_g, w_out, norm2_g, wr_hi, wr_lo, br, *, nc, nl, tpb):
    n = (nc + nl) * TM
    lo = lambda i: (jnp.minimum(i, nc - 1), 0)
    hi = lambda i: (jnp.maximum(i - nc, 0), 0)
    const = lambda i: (0, 0)
    row = lambda i: (i, 0)
    return pl.pallas_call(
        functools.partial(_mix_kernel, nc=nc, tpb=tpb),
        grid=(nc + nl,),
        in_specs=[pl.BlockSpec((TM, D_MODEL), lo), pl.BlockSpec((TM, D_MODEL), hi),
                  pl.BlockSpec((TM, ATTN_W), lo), pl.BlockSpec((TM, ATTN_W), hi),
                  pl.BlockSpec((FOURIER_W // FOURIER_GROUP_W, TM, FOURIER_GROUP_W),
                               lambda i: (0, jnp.minimum(i, nc - 1), 0)),
                  pl.BlockSpec((FOURIER_W // FOURIER_GROUP_W, TM, FOURIER_GROUP_W),
                               lambda i: (0, jnp.maximum(i - nc, 0), 0)),
                  pl.BlockSpec((8, D_MODEL), lambda i: (0, 2)),
                  pl.BlockSpec((8, D_MODEL), lambda i: (0, 3)),
                  pl.BlockSpec((8, D_MODEL), lambda i: (0, 4)),
                  pl.BlockSpec((1, ATTN_W), const), pl.BlockSpec((1, FOURIER_W), const),
                  pl.BlockSpec((D_MODEL, D_MODEL), const), pl.BlockSpec((1, D_MODEL), const),
                  pl.BlockSpec((D_MODEL, LANES), const), pl.BlockSpec((D_MODEL, LANES), const),
                  pl.BlockSpec((1, LANES), const)],
        out_specs=[pl.BlockSpec((TM, D_MODEL), row), pl.BlockSpec((TM, D_MODEL), row),
                   pl.BlockSpec((TM, LANES), row), pl.BlockSpec((TM, LANES), row), pl.BlockSpec((8, LANES), const)],
        out_shape=[jax.ShapeDtypeStruct((n, D_MODEL), F32), jax.ShapeDtypeStruct((n, D_MODEL), F32),
                   jax.ShapeDtypeStruct((n, LANES), I32), jax.ShapeDtypeStruct((n, LANES), F32),
                   jax.ShapeDtypeStruct((8, LANES), F32)],
        compiler_params=_cparams(("arbitrary",)),
        name="mix_proj",
    )(xc, xl, ac, al, fc, fl, mod, mod, mod, attn_g, four_g, w_out, norm2_g, wr_hi, wr_lo, br)


RUN_ALIGN = 8
RUN_ROWS = TM * TOP_K + N_EXPERTS * RUN_ALIGN
RUN_SIZES = tuple(TM >> s for s in range(7))
RUN_PAD = RUN_ROWS - TM * TOP_K
FILL_SIZES = tuple(RUN_PAD >> s for s in range(6))


def _round_up_runs(x):
    return jnp.floor((x + (RUN_ALIGN - 1)) * (1.0 / RUN_ALIGN)) * RUN_ALIGN


def _pos_kernel(idx_ref, gs_ref, loct_ref, loc_ref, info_ref, carry_ref):
    i = pl.program_id(0)

    @pl.when(i == 0)
    def _():
        carry_ref[...] = jnp.zeros(carry_ref.shape, F32)

    idx = idx_ref[...]
    lane = lax.broadcasted_iota(I32, idx.shape, 1)
    sels = [lane == idx[:, k:k + 1] for k in range(TOP_K)]
    hits = jnp.zeros(idx.shape, F32)
    for s in sels:
        hits = hits + s.astype(F32)
    r = lax.broadcasted_iota(I32, (TM, TM), 0)
    c = lax.broadcasted_iota(I32, (TM, TM), 1)
    before = (c < r).astype(BF16)
    rank = jnp.dot(before, hits.astype(BF16), preferred_element_type=F32)
    run_len = _round_up_runs(jnp.sum(hits, axis=0, keepdims=True))
    a = lax.broadcasted_iota(I32, (LANES, LANES), 0)
    b = lax.broadcasted_iota(I32, (LANES, LANES), 1)
    units = jnp.broadcast_to(run_len * (1.0 / RUN_ALIGN), (8, LANES)).astype(BF16)
    run_off = jnp.dot(units, (a < b).astype(BF16), preferred_element_type=F32)[0:1, :] * RUN_ALIGN
    run_start = carry_ref[0:1, :] + gs_ref[0:1, :]
    loc_cols = [jnp.sum(jnp.where(s, rank + run_off, 0.0), axis=-1, keepdims=True) for s in sels]
    loc = _pack_cols(loc_cols, lane)
    loc_ref[...] = loc.astype(I32)
    loct_ref[...] = loc.T[:8, :].astype(I32)
    row = lax.broadcasted_iota(I32, (8, LANES), 0)
    info = jnp.where(row == 0, run_start, jnp.where(row == 1, run_len, jnp.where(row == 2, run_off, 0.0)))
    info_ref[...] = info.astype(I32)
    carry_ref[...] += run_len


def _positions(idx, gs_rows):
    n = idx.shape[0]
    nt = n // TM
    return pl.pallas_call(
        _pos_kernel,
        grid=(nt,),
        in_specs=[pl.BlockSpec((TM, LANES), lambda i: (i, 0)), pl.BlockSpec((8, LANES), lambda i: (0, 0))],
        out_specs=[pl.BlockSpec((8, TM), lambda i: (0, i)), pl.BlockSpec((TM, LANES), lambda i: (i, 0)),
                   pl.BlockSpec((None, 8, LANES), lambda i: (i, 0, 0))],
        out_shape=[jax.ShapeDtypeStruct((8, n), I32), jax.ShapeDtypeStruct((n, LANES), I32),
                   jax.ShapeDtypeStruct((nt, 8, LANES), I32)],
        scratch_shapes=[pltpu.VMEM((8, LANES), F32)],
        compiler_params=_cparams(("arbitrary",)),
        name="positions",
    )(idx, gs_rows)


def _run_copies(info_ref, tile, make_copy, act):
    base = tile * (3 * N_EXPERTS)

    def per_expert(e, carry):
        start = info_ref[base + e]
        length = info_ref[base + N_EXPERTS + e]
        off = info_ref[base + 2 * N_EXPERTS + e]
        for size in RUN_SIZES:
            @pl.when((length & size) != 0)
            def _():
                act(make_copy(pl.multiple_of(off, RUN_ALIGN), pl.multiple_of(start, RUN_ALIGN), size))
            taken = jnp.where((length & size) != 0, size, 0)
            off = off + taken
            start = start + taken
        return carry

    lax.fori_loop(0, N_EXPERTS, per_expert, 0)


PAD_SIZES = tuple((TG // 2) >> s for s in range(5))


def _dispatch_kernel(info_ref, ends_ref, cnt_ref, loc_ref, h2_ref, xs_ref, dump_ref, runs_ref, zero_ref, sem, zsem):
    i = pl.program_id(0)
    last_step = pl.num_programs(0) - 1
    slot = i % 2

    def zero_copies(act):
        def per_expert(e, carry):
            lo = jnp.where(e == 0, 0, ends_ref[jnp.maximum(e - 1, 0)])
            first = lo + cnt_ref[e]
            rest = ends_ref[e] - first
            done = rest * 0
            for size in PAD_SIZES:
                @pl.when((rest & size) != 0)
                def _():
                    act(pltpu.make_async_copy(zero_ref.at[pl.ds(0, size), :],
                                              xs_ref.at[pl.ds(pl.multiple_of(first + done, RUN_ALIGN), size), :], zsem))
                done = done + jnp.where((rest & size) != 0, size, 0)
            return carry

        lax.fori_loop(0, N_EXPERTS, per_expert, 0)

        def tail_tile(t, carry):
            act(pltpu.make_async_copy(zero_ref, xs_ref.at[pl.ds(pl.multiple_of(t * TG, TG), TG), :], zsem))
            return carry

        lax.fori_loop(ends_ref[N_EXPERTS - 1] // TG, xs_ref.shape[0] // TG, tail_tile, 0)

    def slot_done(s):
        return pltpu.make_async_copy(runs_ref.at[s], xs_ref.at[pl.ds(0, RUN_ROWS), :], sem.at[s])

    @pl.when(i == 0)
    def _():
        zero_ref[...] = jnp.zeros(zero_ref.shape, F32)
        for s in range(2):
            pltpu.make_async_copy(zero_ref.at[pl.ds(0, RUN_PAD), :], dump_ref.at[s], sem.at[s]).start()
        for s in range(2):
            pltpu.make_async_copy(zero_ref.at[pl.ds(0, RUN_PAD), :], dump_ref.at[s], sem.at[s]).wait()
        zero_copies(lambda cp: cp.start())

    @pl.when(i >= 2)
    def _():
        slot_done(slot).wait()

    loc = loc_ref[...]
    row = lax.broadcasted_iota(I32, (RUN_ROWS, TM), 0)
    sel = row == loc[0:1, :]
    for k in range(1, TOP_K):
        sel = jnp.logical_or(sel, row == loc[k:k + 1, :])
    runs_ref[slot] = jnp.dot(sel.astype(BF16), h2_ref[...].astype(BF16), preferred_element_type=F32)

    def piece(stacked_row, global_row, size):
        return pltpu.make_async_copy(runs_ref.at[slot, pl.ds(stacked_row, size), :],
                                     xs_ref.at[pl.ds(global_row, size), :], sem.at[slot])

    _run_copies(info_ref, i, piece, lambda cp: cp.start())
    last = i * (3 * N_EXPERTS) + N_EXPERTS - 1
    used = info_ref[last + 2 * N_EXPERTS] + info_ref[last + N_EXPERTS]
    rest = RUN_ROWS - used
    done = used * 0
    for size in FILL_SIZES:
        @pl.when((rest & size) != 0)
        def _():
            at = pl.multiple_of(used + done, RUN_ALIGN)
            to = pl.multiple_of(at - TM * TOP_K, RUN_ALIGN)
            pltpu.make_async_copy(runs_ref.at[slot, pl.ds(at, size), :], dump_ref.at[slot, pl.ds(to, size), :],
                                  sem.at[slot]).start()
        done = done + jnp.where((rest & size) != 0, size, 0)

    @pl.when(i == last_step)
    def _():
        slot_done(slot).wait()

        @pl.when(i >= 1)
        def _():
            slot_done(1 - slot).wait()

        zero_copies(lambda cp: cp.wait())


def _dispatch(info, ends, cnt, loc, h2, n_rows):
    n = h2.shape[0]
    grid_spec = pltpu.PrefetchScalarGridSpec(
        num_scalar_prefetch=3,
        grid=(n // TM,),
        in_specs=[pl.BlockSpec((8, TM), lambda i, *_: (0, i)),
                  pl.BlockSpec((TM, D_MODEL), lambda i, *_: (i, 0))],
        out_specs=[pl.BlockSpec(memory_space=pl.ANY), pl.BlockSpec(memory_space=pl.ANY)],
        scratch_shapes=[pltpu.VMEM((2, RUN_ROWS, D_MODEL), F32), pltpu.VMEM((TG, D_MODEL), F32),
                        pltpu.SemaphoreType.DMA((2,)), pltpu.SemaphoreType.DMA(())],
    )
    xs, _ = pl.pallas_call(
        _dispatch_kernel,
        grid_spec=grid_spec,
        out_shape=[jax.ShapeDtypeStruct((n_rows, D_MODEL), F32), jax.ShapeDtypeStruct((2, RUN_PAD, D_MODEL), F32)],
        compiler_params=_cparams(("arbitrary",)),
        name="dispatch",
    )(info, ends, cnt, loc, h2)
    return xs


def _ffn_kernel(te_ref, nv_ref, par_ref, nxt_ref, x_ref, bg_ref, bl_ref, bdn_ref, perm_ref, wgu_hbm, wdn_hbm, o_ref,
                wgu_buf, wdn_buf, wg_s, wl_s, wd_s, wsem):
    i = pl.program_id(0)
    e = te_ref[i]
    valid = i < nv_ref[0]
    fresh = jnp.logical_or(i == 0, e != te_ref[jnp.maximum(i - 1, 0)])
    wslot = par_ref[i]

    def weight_copies(expert, s):
        return (pltpu.make_async_copy(wgu_hbm.at[expert], wgu_buf.at[s], wsem.at[s]),
                pltpu.make_async_copy(wdn_hbm.at[expert], wdn_buf.at[s], wsem.at[s]))

    @pl.when(i == 0)
    def _():
        for cp in weight_copies(e, wslot):
            cp.start()

    @pl.when(jnp.logical_and(valid, fresh))
    def _():
        for cp in weight_copies(e, wslot):
            cp.wait()
        for j in range(2 * D_EXPERT // (2 * LANES)):
            blk = wgu_buf[wslot, :, j * 2 * LANES:(j + 1) * 2 * LANES].astype(BF16)
            d = jnp.dot(blk, perm_ref[...], preferred_element_type=F32)
            wg_s[:, j * LANES:(j + 1) * LANES] = d[:, :LANES].astype(BF16)
            wl_s[:, j * LANES:(j + 1) * LANES] = d[:, LANES:].astype(BF16)
        wd_s[...] = wdn_buf[wslot].astype(BF16)

        @pl.when(nxt_ref[i] >= 0)
        def _():
            for cp in weight_copies(nxt_ref[i], 1 - wslot):
                cp.start()

    @pl.when(valid)
    def _():
        x = x_ref[...].astype(BF16)
        g = jnp.dot(x, wg_s[...], preferred_element_type=F32) + bg_ref[...]
        l = jnp.dot(x, wl_s[...], preferred_element_type=F32) + bl_ref[...]
        g = jnp.minimum(g, SWIGLU_LIMIT)
        l = jnp.clip(l, -SWIGLU_LIMIT, SWIGLU_LIMIT)
        a = (l + 1.0) * (g * (1.0 / (1.0 + jnp.exp(-SWIGLU_ALPHA * g))))
        o_ref[...] = jnp.dot(a.astype(BF16), wd_s[...], preferred_element_type=F32) + bdn_ref[...]

    @pl.when(jnp.logical_not(valid))
    def _():
        o_ref[...] = jnp.zeros(o_ref.shape, F32)


def _expert_ffn(tile_expert, n_valid, parity, next_expert, xs, w_gu, b_g, b_l, w_dn, b_dn, perm):
    p = xs.shape[0]
    nt = p // TG
    tile = lambda i, te, nv, *_: (jnp.minimum(i, nv[0] - 1), 0)
    ex3 = lambda i, te, *_: (te[i], 0, 0)
    grid_spec = pltpu.PrefetchScalarGridSpec(
        num_scalar_prefetch=4,
        grid=(nt,),
        in_specs=[pl.BlockSpec((TG, D_MODEL), tile),
                  pl.BlockSpec((None, 1, D_EXPERT), ex3),
                  pl.BlockSpec((None, 1, D_EXPERT), ex3),
                  pl.BlockSpec((None, 1, D_MODEL), ex3),
                  pl.BlockSpec((2 * LANES, 2 * LANES), lambda i, *_: (0, 0)),
                  pl.BlockSpec(memory_space=pl.ANY),
                  pl.BlockSpec(memory_space=pl.ANY)],
        out_specs=pl.BlockSpec((TG, D_MODEL), lambda i, *_: (i, 0)),
        scratch_shapes=[pltpu.VMEM((2, D_MODEL, 2 * D_EXPERT), F32), pltpu.VMEM((2, D_EXPERT, D_MODEL), F32),
                        pltpu.VMEM((D_MODEL, D_EXPERT), BF16), pltpu.VMEM((D_MODEL, D_EXPERT), BF16),
                        pltpu.VMEM((D_EXPERT, D_MODEL), BF16), pltpu.SemaphoreType.DMA((2,))],
    )
    return pl.pallas_call(
        _ffn_kernel,
        grid_spec=grid_spec,
        out_shape=jax.ShapeDtypeStruct((p, D_MODEL), F32),
        compiler_params=_cparams(("arbitrary",)),
        name="expert_ffn",
    )(tile_expert, n_valid, parity, next_expert, xs, b_g, b_l, b_dn, perm, w_gu, w_dn)


def _combine_kernel(info_ref, loc_ref, w_ref, x1_ref, g2_ref, fg_ref, ys_ref, o_ref, runs_ref, sem, *, tile0, nc, tpb):
    j = pl.program_id(0)
    i = j + tile0
    r = _mod_row(i, nc, tpb)
    slot = j % 2

    def gather(tile, s, act):
        def piece(stacked_row, global_row, size):
            return pltpu.make_async_copy(ys_ref.at[pl.ds(global_row, size), :],
                                         runs_ref.at[s, pl.ds(stacked_row, size), :], sem.at[s])

        _run_copies(info_ref, tile, piece, act)
        last = tile * (3 * N_EXPERTS) + N_EXPERTS - 1
        used = info_ref[last + 2 * N_EXPERTS] + info_ref[last + N_EXPERTS]
        rest = RUN_ROWS - used
        done = used * 0
        for size in FILL_SIZES:
            @pl.when((rest & size) != 0)
            def _():
                act(piece(pl.multiple_of(used + done, RUN_ALIGN), pl.multiple_of(done, RUN_ALIGN), size))
            done = done + jnp.where((rest & size) != 0, size, 0)

    @pl.when(j == 0)
    def _():
        gather(i, 0, lambda cp: cp.start())

    @pl.when(j + 1 < pl.num_programs(0))
    def _():
        gather(i + 1, 1 - slot, lambda cp: cp.start())

    pltpu.make_async_copy(ys_ref.at[pl.ds(0, RUN_ROWS), :], runs_ref.at[slot], sem.at[slot]).wait()
    loc = loc_ref[...]
    w = w_ref[...]
    col = lax.broadcasted_iota(I32, (TM, RUN_ROWS), 1)
    mix = jnp.where(col == loc[:, 0:1], w[:, 0:1], 0.0)
    for k in range(1, TOP_K):
        mix = mix + jnp.where(col == loc[:, k:k + 1], w[:, k:k + 1], 0.0)
    moe = jnp.dot(mix.astype(BF16), runs_ref[slot].astype(BF16), preferred_element_type=F32)
    y = x1_ref[...] + g2_ref[pl.ds(r, 1), :] * moe
    o_ref[...] = _rms(y) * fg_ref[...]


def _combine(info, loc, w, x1, mod, final_g, ys, *, tile0, ntiles, nc, tpb):
    grid_spec = pltpu.PrefetchScalarGridSpec(
        num_scalar_prefetch=1,
        grid=(ntiles,),
        in_specs=[pl.BlockSpec((TM, LANES), lambda i, *_: (i + tile0, 0)),
                  pl.BlockSpec((TM, LANES), lambda i, *_: (i + tile0, 0)),
                  pl.BlockSpec((TM, D_MODEL), lambda i, *_: (i + tile0, 0)),
                  pl.BlockSpec((8, D_MODEL), lambda i, *_: (0, 5)),
                  pl.BlockSpec((1, D_MODEL), lambda i, *_: (0, 0)),
                  pl.BlockSpec(memory_space=pl.ANY)],
        out_specs=pl.BlockSpec((TM, D_MODEL), lambda i, *_: (i, 0)),
        scratch_shapes=[pltpu.VMEM((2, RUN_ROWS, D_MODEL), F32), pltpu.SemaphoreType.DMA((2,))],
    )
    return pl.pallas_call(
        functools.partial(_combine_kernel, tile0=tile0, nc=nc, tpb=tpb),
        grid_spec=grid_spec,
        out_shape=jax.ShapeDtypeStruct((ntiles * TM, D_MODEL), F32),
        compiler_params=_cparams(("arbitrary",)),
        name="combine",
    )(info, loc, w, x1, mod, final_g, ys)


def kernel(x_prompt, x_sample, cache_k, cache_v, c, c_ctx, norm1_g, w_mod, b_mod, w_in, q_norm_g, k_norm_g,
           w_fourier, attn_out_g, fourier_out_g, w_out, norm2_g, w_router, b_router, w_gate_up, b_gate_up,
           w_down, b_down, final_g):
    bc, tc, _ = x_prompt.shape
    bl, tl, _ = x_sample.shape
    depth = w_in.shape[0]
    past = cache_k.shape[2]
    assert depth == 1 and tl % TQ == 0 and tl % TM == 0 and (bc * tc) % TM == 0
    n_ctx, n_lat = bc * tc, bl * tl
    n_tok = n_ctx + n_lat
    nc, nl, tpb = n_ctx // TM, n_lat // TM, tl // TM

    xc = x_prompt.reshape(n_ctx, D_MODEL)
    xl = x_sample.reshape(n_lat, D_MODEL)

    c_rows = jnp.concatenate([c_ctx[None, :], c, jnp.zeros((8 - 1 - bl, D_MODEL), F32)], axis=0)
    mod = _modulation(c_rows, w_mod[0], b_mod[0])

    cos_t, sin_t = _rope_tables(tl)
    q, k_heads, vt_ext, f, k_new, v_new = _pre(
        xc, xl, mod, norm1_g, w_in[0].astype(BF16), jnp.tile(q_norm_g, (1, N_HEADS)),
        jnp.tile(k_norm_g, (1, N_KV_HEADS)), _head_mean_matrix(), cos_t, sin_t, nc=nc, nl=nl, tpb=tpb)

    attn_c = _attention(q, k_heads, vt_ext, batch=bc, t=tc, s=tc, q_row0=0)
    keys, vals = _with_cache(k_heads, vt_ext, cache_k[:, 0], cache_v[:, 0], n_lat, bl, tl)
    attn_l = _attention(q, keys, vals, batch=bl, t=tl, s=past + tl, q_row0=n_ctx)

    wf = w_fourier[0].astype(BF16)
    zero = jnp.zeros_like(wf[0])
    wf_pairs = jnp.stack([jnp.block([[wf[0], zero], [zero, wf[1]]]), jnp.block([[wf[2], zero], [zero, wf[3]]])])
    four_c = _fourier(f, wf_pairs, batch=bc, t=tc, row0=0, n1=1)
    four_l = _fourier(f, wf_pairs, batch=bl, t=tl, row0=n_ctx, n1=FFT_N1)

    wr = jnp.pad(w_router[0], ((0, 0), (0, LANES - N_EXPERTS)))
    wr_hi = wr.astype(BF16)
    wr_lo = (wr - wr_hi.astype(F32)).astype(BF16)
    br = jnp.pad(b_router[0], (0, LANES - N_EXPERTS), constant_values=-1e30).reshape(1, LANES)
    x1, h2, idx, gate_w, counts = _mix(xc, xl, attn_c, attn_l, four_c, four_l, mod, attn_out_g, fourier_out_g,
                                       w_out[0].astype(BF16), norm2_g, wr_hi, wr_lo, br, nc=nc, nl=nl, tpb=tpb)
    cnt = counts[0, :N_EXPERTS].astype(I32)
    padded = ((cnt + TG - 1) // TG) * TG
    ends = jnp.cumsum(padded)
    starts = ends - padded
    n_rows = n_tok * TOP_K + (n_tok // TM) * N_EXPERTS * RUN_ALIGN + N_EXPERTS * TG
    n_tiles = n_rows // TG
    tile_ids = jnp.arange(n_tiles, dtype=I32)
    tile_expert = jnp.minimum(jnp.sum((ends[None, :] // TG <= tile_ids[:, None]).astype(I32), axis=1),
                              N_EXPERTS - 1)
    n_valid = (ends[N_EXPERTS - 1] // TG).reshape(1)
    gs_rows = jnp.broadcast_to(jnp.pad(starts.astype(F32), (0, LANES - N_EXPERTS))[None, :], (8, LANES))
    loc_t, loc, info = _positions(idx, gs_rows)
    info = info[:, :3, :N_EXPERTS].reshape(-1)

    xs = _dispatch(info, ends, cnt, loc_t, h2, n_rows)
    fresh_tile = jnp.concatenate([jnp.ones((1,), I32), (tile_expert[1:] != tile_expert[:-1]).astype(I32)])
    parity = (jnp.cumsum(fresh_tile) - 1) % 2
    experts = jnp.arange(N_EXPERTS, dtype=I32)
    later = jnp.logical_and(padded[None, :] > 0, experts[None, :] > experts[:, None])
    next_of = jnp.min(jnp.where(later, experts[None, :], N_EXPERTS), axis=1)
    next_of = jnp.where(next_of == N_EXPERTS, -1, next_of)
    next_expert = jnp.sum(jnp.where(tile_expert[:, None] == experts[None, :], next_of[None, :], 0), axis=1)
    b_gu = b_gate_up[0].reshape(N_EXPERTS, 1, D_EXPERT, 2)
    ys = _expert_ffn(tile_expert, n_valid, parity.astype(I32), next_expert.astype(I32), xs, w_gate_up[0],
                     b_gu[..., 0], b_gu[..., 1], w_down[0], b_down[0].reshape(N_EXPERTS, 1, D_MODEL),
                     _deinterleave_matrix())

    y_c = _combine(info, loc, gate_w, x1, mod, final_g.reshape(1, -1), ys, tile0=0, ntiles=nc, nc=nc, tpb=tpb)
    y_l = _combine(info, loc, gate_w, x1, mod, final_g.reshape(1, -1), ys, tile0=nc, ntiles=nl, nc=nc, tpb=tpb)

    new_k = k_new[:n_ctx].reshape(bc, 1, tc, N_KV_HEADS, HEAD_DIM)
    new_v = v_new[:n_ctx].reshape(bc, 1, tc, N_KV_HEADS, HEAD_DIM)
    return (y_c.reshape(bc, tc, D_MODEL), y_l.reshape(bl, tl, D_MODEL), new_k, new_v)
```

```python
import functools
import math

import numpy as np
import jax
import jax.numpy as jnp
from jax import lax
from jax.experimental import pallas as pl
from jax.experimental.pallas import tpu as pltpu

F32 = jnp.float32
BF16 = jnp.bfloat16
I32 = jnp.int32

D_MODEL = 1024
HEAD_DIM = 64
N_HEADS = 8
N_KV_HEADS = 2
HEADS_PER_KV = N_HEADS // N_KV_HEADS
ATTN_W = N_HEADS * HEAD_DIM
KV_W = N_KV_HEADS * HEAD_DIM
FOURIER_W = 512
FOURIER_GROUP_W = 128
D_IN = ATTN_W + 2 * KV_W + FOURIER_W
N_EXPERTS = 32
TOP_K = 4
D_EXPERT = 1024
SWIGLU_ALPHA = 1.702
SWIGLU_LIMIT = 7.0
ROPE_THETA = 10000.0
ROT_PAIRS = HEAD_DIM // 4
GRID_W = 64
EPS = 1e-6

LANES = 128
TM = 512
TQ = 512
TG = 256
FFT_N1 = 4
V_ROWS = 80
VMEM_LIMIT = 56 * 1024 * 1024


def _cparams(sem, vmem=VMEM_LIMIT):
    return pltpu.CompilerParams(dimension_semantics=sem, vmem_limit_bytes=vmem)


def _rope_tables(t_lat):
    pos = np.arange(t_lat)
    row = (pos // GRID_W).astype(np.float64)
    col = (pos % GRID_W).astype(np.float64)
    inv = ROPE_THETA ** (-np.arange(ROT_PAIRS, dtype=np.float64) / ROT_PAIRS)
    lane = np.arange(LANES)
    d = lane % HEAD_DIM
    axis = d // (2 * ROT_PAIRS)
    second = (d // ROT_PAIRS) % 2
    p = d % ROT_PAIRS
    ang = np.where(axis[None, :] == 0, row[:, None], col[:, None]) * inv[p][None, :]
    cos = np.cos(ang)
    sin = np.sin(ang) * np.where(second == 0, -1.0, 1.0)[None, :]
    cos = np.concatenate([np.ones((TM, LANES)), cos], axis=0)
    sin = np.concatenate([np.zeros((TM, LANES)), sin], axis=0)
    return jnp.asarray(cos, F32), jnp.asarray(sin, F32)


def _dft_tables(n1, n2):
    c = np.arange(FOURIER_GROUP_W)
    ang = 2.0 * np.pi * np.outer(c, c) / FOURIER_GROUP_W
    z = np.zeros_like(ang)
    cbd = np.block([[np.cos(ang), z], [z, np.cos(ang)]])
    sbd = np.block([[np.sin(ang), z], [z, np.sin(ang)]])
    cd = np.concatenate([cbd, -sbd], axis=1)
    u = np.arange(n2)
    ang2 = 2.0 * np.pi * np.outer(u, u) / n2
    ct, st = np.cos(ang2), np.sin(ang2)
    nt = max(n1 - 1, 1)
    tw_ang = 2.0 * np.pi * np.outer(np.arange(1, nt + 1), u) / (n1 * n2)
    twc = np.repeat(np.cos(tw_ang)[:, :, None], LANES, axis=2)
    tws = np.repeat(np.sin(tw_ang)[:, :, None], LANES, axis=2)
    return (jnp.asarray(cd, F32), jnp.asarray(ct, F32), jnp.asarray(st, F32),
            jnp.asarray(twc, F32), jnp.asarray(tws, F32))


def _head_mean_matrix():
    h = np.arange(ATTN_W) // HEAD_DIM
    return jnp.asarray((h[:, None] == h[None, :]) / HEAD_DIM, BF16)


def _deinterleave_matrix():
    p = np.zeros((2 * LANES, 2 * LANES))
    m = np.arange(LANES)
    p[2 * m, m] = 1.0
    p[2 * m + 1, LANES + m] = 1.0
    return jnp.asarray(p, BF16)


def _mod_row(i, nc, tpb):
    return jnp.where(i < nc, 0, 1 + (i - nc) // tpb)


def _rms(x):
    return x * lax.rsqrt(jnp.mean(x * x, axis=-1, keepdims=True) + EPS)


def _pack_cols(cols, lane):
    out = jnp.zeros(lane.shape, cols[0].dtype)
    for k, c in enumerate(cols):
        out = jnp.where(lane == k, c, out)
    return out


def _mod_kernel(c_ref, w_ref, b_ref, o_ref):
    c = c_ref[...]
    s = c * (1.0 / (1.0 + jnp.exp(-c)))
    o_ref[...] = jnp.dot(s.astype(BF16), w_ref[...].astype(BF16), preferred_element_type=F32) + b_ref[...]


def _modulation(c_rows, w_mod, b_mod):
    n = w_mod.shape[1] // D_MODEL
    return pl.pallas_call(
        _mod_kernel,
        grid=(n,),
        in_specs=[pl.BlockSpec((8, D_MODEL), lambda j: (0, 0)),
                  pl.BlockSpec((D_MODEL, D_MODEL), lambda j: (0, j)),
                  pl.BlockSpec((1, D_MODEL), lambda j: (0, j))],
        out_specs=pl.BlockSpec((8, D_MODEL), lambda j: (0, j)),
        out_shape=jax.ShapeDtypeStruct((8, w_mod.shape[1]), F32),
        compiler_params=_cparams(("parallel",)),
        name="modulation",
    )(c_rows, w_mod, b_mod.reshape(1, -1))


def _pre_kernel(xc_ref, xl_ref, sh_ref, sc_ref, g1_ref, win_ref, qg_ref, kg_ref, bd_ref, cos_ref, sin_ref,
                q_ref, k_ref, vt_ref, f_ref, kn_ref, vn_ref, *, nc, tpb):
    i = pl.program_id(0)
    r = _mod_row(i, nc, tpb)
    x = jnp.where(i < nc, xc_ref[...], xl_ref[...])
    h = _rms(x) * g1_ref[...]
    h = h * (1.0 + sc_ref[pl.ds(r, 1), :]) + sh_ref[pl.ds(r, 1), :]
    proj = jnp.dot(h.astype(BF16), win_ref[...], preferred_element_type=F32)
    q = proj[:, :ATTN_W]
    k = proj[:, ATTN_W:ATTN_W + KV_W]
    v = proj[:, ATTN_W + KV_W:ATTN_W + 2 * KV_W]
    f_ref[...] = proj[:, ATTN_W + 2 * KV_W:].astype(BF16)
    @pl.when(i < nc)
    def _():
        vn_ref[...] = v

    vt = v.T
    tail = (lax.broadcasted_iota(I32, (V_ROWS - HEAD_DIM, vt.shape[1]), 0) == 0).astype(BF16)
    for j in range(N_KV_HEADS):
        vt_ref[j, :HEAD_DIM, :] = vt[j * HEAD_DIM:(j + 1) * HEAD_DIM, :].astype(BF16)
        vt_ref[j, HEAD_DIM:, :] = tail
    q_ms = jnp.dot((q * q).astype(BF16), bd_ref[...], preferred_element_type=F32)
    k_ms = jnp.dot((k * k).astype(BF16), bd_ref[:KV_W, :KV_W], preferred_element_type=F32)
    qn = q * lax.rsqrt(q_ms + EPS) * qg_ref[...]
    kn = k * lax.rsqrt(k_ms + EPS) * kg_ref[...]
    @pl.when(i < nc)
    def _():
        kn_ref[...] = kn

    cos = cos_ref[...]
    sin = sin_ref[...]
    lane = lax.broadcasted_iota(I32, cos.shape, 1)
    first = (lane & ROT_PAIRS) == 0

    def rope(c):
        partner = jnp.where(first, pltpu.roll(c, LANES - ROT_PAIRS, 1), pltpu.roll(c, ROT_PAIRS, 1))
        return c * cos + partner * sin

    k_rot = rope(kn).astype(BF16)
    for j in range(N_KV_HEADS):
        k_ref[j] = k_rot[:, j * HEAD_DIM:(j + 1) * HEAD_DIM]
    scale = HEAD_DIM ** -0.5 * math.log2(math.e)
    q_rot = jnp.concatenate([rope(qn[:, j * LANES:(j + 1) * LANES]) for j in range(ATTN_W // LANES)], axis=1)
    q_ref[...] = (q_rot * scale).T.astype(BF16)


def _pre(xc, xl, mod, norm1_g, w_in, q_g, k_g, bd, cos_t, sin_t, *, nc, nl, tpb):
    n = (nc + nl) * TM
    row = lambda i: (i, 0)
    const = lambda i: (0, 0)
    tab = lambda i: (jnp.where(i < nc, 0, 1 + (i - nc) % tpb), 0)
    return pl.pallas_call(
        functools.partial(_pre_kernel, nc=nc, tpb=tpb),
        grid=(nc + nl,),
        in_specs=[pl.BlockSpec((TM, D_MODEL), lambda i: (jnp.minimum(i, nc - 1), 0)),
                  pl.BlockSpec((TM, D_MODEL), lambda i: (jnp.maximum(i - nc, 0), 0)),
                  pl.BlockSpec((8, D_MODEL), lambda i: (0, 0)),
                  pl.BlockSpec((8, D_MODEL), lambda i: (0, 1)),
                  pl.BlockSpec((1, D_MODEL), const),
                  pl.BlockSpec((D_MODEL, D_IN), const),
                  pl.BlockSpec((1, ATTN_W), const),
                  pl.BlockSpec((1, KV_W), const),
                  pl.BlockSpec((ATTN_W, ATTN_W), const),
                  pl.BlockSpec((TM, LANES), tab),
                  pl.BlockSpec((TM, LANES), tab)],
        out_specs=[pl.BlockSpec((ATTN_W, TM), lambda i: (0, i)),
                   pl.BlockSpec((N_KV_HEADS, TM, HEAD_DIM), lambda i: (0, i, 0)),
                   pl.BlockSpec((N_KV_HEADS, V_ROWS, TM), lambda i: (0, 0, i)),
                   pl.BlockSpec((TM, FOURIER_W), row),
                   pl.BlockSpec((TM, KV_W), lambda i: (jnp.minimum(i, nc - 1), 0)),
                   pl.BlockSpec((TM, KV_W), lambda i: (jnp.minimum(i, nc - 1), 0))],
        out_shape=[jax.ShapeDtypeStruct((ATTN_W, n), BF16), jax.ShapeDtypeStruct((N_KV_HEADS, n, HEAD_DIM), BF16),
                   jax.ShapeDtypeStruct((N_KV_HEADS, V_ROWS, n), BF16), jax.ShapeDtypeStruct((n, FOURIER_W), BF16),
                   jax.ShapeDtypeStruct((nc * TM, KV_W), F32), jax.ShapeDtypeStruct((nc * TM, KV_W), F32)],
        compiler_params=_cparams(("arbitrary",)),
        name="pre_proj",
    )(xc, xl, mod, mod, norm1_g, w_in, q_g, k_g, bd, cos_t, sin_t)


def _attn_kernel(*refs, seg_lens):
    n_seg = len(seg_lens)
    qt_ref = refs[0]
    k_refs = refs[1:1 + n_seg]
    vt_refs = refs[1 + n_seg:1 + 2 * n_seg]
    o_ref, ot_ref, st_ref, pt_ref = refs[1 + 2 * n_seg:]
    offs = [sum(seg_lens[:m]) for m in range(n_seg)]

    def scores(h):
        qt = qt_ref[h * HEAD_DIM:(h + 1) * HEAD_DIM, :]
        for k_ref, off, ln in zip(k_refs, offs, seg_lens):
            st_ref[h % 2, off:off + ln] = jnp.dot(k_ref[h // HEADS_PER_KV], qt, preferred_element_type=F32)

    scores(0)
    for h in range(N_HEADS):
        if h + 1 < N_HEADS:
            scores(h + 1)
        st = st_ref[h % 2]
        pt_ref[h % 2] = jnp.exp2(st - jnp.max(st, axis=0, keepdims=True)).astype(BF16)
        ot = None
        for vt_ref, off, ln in zip(vt_refs, offs, seg_lens):
            part = jnp.dot(vt_ref[h // HEADS_PER_KV], pt_ref[h % 2, off:off + ln], preferred_element_type=F32)
            ot = part if ot is None else ot + part
        ot_ref[h * HEAD_DIM:(h + 1) * HEAD_DIM, :] = ot[:HEAD_DIM] * (1.0 / ot[HEAD_DIM:HEAD_DIM + 1])
    o_ref[...] = ot_ref[...].T


def _attention(qt, segments, *, batch, t, q_row0):
    tq = min(TQ, t)
    qpb = t // tq
    q0 = q_row0 // tq
    seg_lens = tuple(ln for _, _, ln, _ in segments)
    s_total = sum(seg_lens)
    k_specs = [pl.BlockSpec((N_KV_HEADS, ln, HEAD_DIM), functools.partial(lambda b, i, f: (0, f + b, 0), f=fb))
               for _, _, ln, fb in segments]
    v_specs = [pl.BlockSpec((N_KV_HEADS, V_ROWS, ln), functools.partial(lambda b, i, f: (0, 0, f + b), f=fb))
               for _, _, ln, fb in segments]
    return pl.pallas_call(
        functools.partial(_attn_kernel, seg_lens=seg_lens),
        grid=(batch, qpb),
        in_specs=[pl.BlockSpec((ATTN_W, tq), lambda b, i: (0, q0 + b * qpb + i))] + k_specs + v_specs,
        out_specs=pl.BlockSpec((tq, ATTN_W), lambda b, i: (b * qpb + i, 0)),
        out_shape=jax.ShapeDtypeStruct((batch * t, ATTN_W), F32),
        scratch_shapes=[pltpu.VMEM((ATTN_W, tq), F32), pltpu.VMEM((2, s_total, tq), F32),
                        pltpu.VMEM((2, s_total, tq), BF16)],
        compiler_params=_cparams(("parallel", "parallel")),
        name="attention",
    )(qt, *[k for k, _, _, _ in segments], *[v for _, v, _, _ in segments])


def _cache_heads(cache_k, cache_v):
    batch, past = cache_k.shape[:2]
    ck = cache_k.astype(BF16).transpose(2, 0, 1, 3).reshape(N_KV_HEADS, batch * past, HEAD_DIM)
    cv = cache_v.astype(BF16).transpose(2, 3, 0, 1).reshape(N_KV_HEADS, HEAD_DIM, batch * past)
    tail = jnp.concatenate([jnp.ones((N_KV_HEADS, 1, batch * past), BF16),
                            jnp.zeros((N_KV_HEADS, V_ROWS - HEAD_DIM - 1, batch * past), BF16)], axis=1)
    return ck, jnp.concatenate([cv, tail], axis=1)


def _fourier_kernel(x_ref, cd_ref, ct_ref, st_ref, twc_ref, tws_ref, wf_ref, o_ref, *, n1, n2):
    pw = 2 * FOURIER_GROUP_W
    scale = 1.0 / math.sqrt(n1 * n2 * FOURIER_GROUP_W)
    for gp in range(FOURIER_W // pw):
        z = jnp.dot(x_ref[:, gp * pw:(gp + 1) * pw], cd_ref[...], preferred_element_type=F32)
        zr = [z[t1 * n2:(t1 + 1) * n2, :pw] for t1 in range(n1)]
        zi = [z[t1 * n2:(t1 + 1) * n2, pw:] for t1 in range(n1)]
        if n1 == 1:
            a = [(zr[0], zi[0])]
        else:
            a = [(zr[0] + zr[1] + zr[2] + zr[3], zi[0] + zi[1] + zi[2] + zi[3]),
                 (zr[0] + zi[1] - zr[2] - zi[3], zi[0] - zr[1] - zi[2] + zr[3]),
                 (zr[0] - zr[1] + zr[2] - zr[3], zi[0] - zi[1] + zi[2] - zi[3]),
                 (zr[0] - zi[1] - zr[2] + zi[3], zi[0] + zr[1] - zi[2] - zr[3])]
        for u1 in range(n1):
            ar, ai = a[u1]
            if u1 > 0:
                c = jnp.concatenate([twc_ref[u1 - 1]] * 2, axis=1)
                s = jnp.concatenate([tws_ref[u1 - 1]] * 2, axis=1)
                ar, ai = ar * c + ai * s, ai * c - ar * s
            y = (jnp.dot(ct_ref[...], ar.astype(BF16), preferred_element_type=F32)
                 + jnp.dot(st_ref[...], ai.astype(BF16), preferred_element_type=F32)) * scale
            y = jnp.dot(y.astype(BF16), wf_ref[gp], preferred_element_type=F32)
            rows = pl.ds(u1, n2, stride=n1) if n1 > 1 else slice(None)
            for half in range(2):
                o_ref[2 * gp + half, rows, :] = y[:, half * FOURIER_GROUP_W:(half + 1) * FOURIER_GROUP_W]


def _fourier(f, wf_pairs, *, batch, t, row0, n1):
    n2 = t // n1
    cd, ct, st, twc, tws = _dft_tables(n1, n2)
    cd, ct, st = cd.astype(BF16), ct.astype(BF16), st.astype(BF16)
    nt = twc.shape[0]
    b0 = row0 // t
    c2 = lambda b: (0, 0)
    c3 = lambda b: (0, 0, 0)
    return pl.pallas_call(
        functools.partial(_fourier_kernel, n1=n1, n2=n2),
        grid=(batch,),
        in_specs=[pl.BlockSpec((t, FOURIER_W), lambda b: (b0 + b, 0)),
                  pl.BlockSpec(cd.shape, c2), pl.BlockSpec((n2, n2), c2), pl.BlockSpec((n2, n2), c2),
                  pl.BlockSpec((nt, n2, LANES), c3), pl.BlockSpec((nt, n2, LANES), c3),
                  pl.BlockSpec(wf_pairs.shape, c3)],
        out_specs=pl.BlockSpec((FOURIER_W // FOURIER_GROUP_W, t, FOURIER_GROUP_W), lambda b: (0, b, 0)),
        out_shape=jax.ShapeDtypeStruct((FOURIER_W // FOURIER_GROUP_W, batch * t, FOURIER_GROUP_W), F32),
        compiler_params=_cparams(("parallel",)),
        name="fourier",
    )(f, cd, ct, st, twc, tws, wf_pairs)


def _route(i, h, whi_ref, wlo_ref, br_ref, idx_ref, w_ref, cnt_ref):
    h_hi = h.astype(BF16)
    h_lo = (h - h_hi.astype(F32)).astype(BF16)
    logits = (jnp.dot(h_hi, whi_ref[...], preferred_element_type=F32)
              + jnp.dot(h_lo, whi_ref[...], preferred_element_type=F32)
              + jnp.dot(h_hi, wlo_ref[...], preferred_element_type=F32)) + br_ref[...]
    lane = lax.broadcasted_iota(I32, logits.shape, 1)
    l = logits
    vals, idxs = [], []
    hits = jnp.zeros(logits.shape, F32)
    for _ in range(TOP_K):
        m = jnp.max(l, axis=-1, keepdims=True)
        ix = jnp.min(jnp.where(l == m, lane, LANES), axis=-1, keepdims=True)
        sel = lane == ix
        hits = hits + sel.astype(F32)
        l = jnp.where(sel, -jnp.inf, l)
        vals.append(m)
        idxs.append(ix)
    es = [jnp.exp(v - vals[0]) for v in vals]
    den = es[0] + es[1] + es[2] + es[3]
    idx_ref[...] = _pack_cols(idxs, lane)
    w_ref[...] = _pack_cols([e / den for e in es], lane)

    @pl.when(i == 0)
    def _():
        cnt_ref[...] = jnp.zeros(cnt_ref.shape, F32)

    cnt_ref[...] += _round_up_runs(jnp.sum(hits, axis=0, keepdims=True))


def _mix_kernel(xc_ref, xl_ref, ac_ref, al_ref, fc_ref, fl_ref, g1_ref, sh2_ref, sc2_ref, ag_ref, fg_ref,
                wout_ref, n2g_ref, whi_ref, wlo_ref, br_ref, x1_ref, h2_ref, idx_ref, w_ref, cnt_ref, *, nc, tpb):
    i = pl.program_id(0)
    r = _mod_row(i, nc, tpb)
    is_ctx = i < nc
    x = jnp.where(is_ctx, xc_ref[...], xl_ref[...])
    a = _rms(jnp.where(is_ctx, ac_ref[...], al_ref[...])) * ag_ref[...]
    n_groups = FOURIER_W // FOURIER_GROUP_W
    f = jnp.concatenate([jnp.where(is_ctx, fc_ref[g], fl_ref[g]) for g in range(n_groups)], axis=1)
    f = _rms(f) * fg_ref[...]
    mixed = (jnp.dot(a.astype(BF16), wout_ref[:ATTN_W, :], preferred_element_type=F32)
             + jnp.dot(f.astype(BF16), wout_ref[ATTN_W:, :], preferred_element_type=F32))
    x1 = x + g1_ref[pl.ds(r, 1), :] * mixed
    x1_ref[...] = x1
    h2 = _rms(x1) * n2g_ref[...]
    h2 = h2 * (1.0 + sc2_ref[pl.ds(r, 1), :]) + sh2_ref[pl.ds(r, 1), :]
    h2_ref[...] = h2
    _route(i, h2, whi_ref, wlo_ref, br_ref, idx_ref, w_ref, cnt_ref)


def _mix(xc, xl, ac, al, fc, fl, mod, attn_g, four_g, w_out, norm2_g, wr_hi, wr_lo, br, *, nc, nl, tpb):
    n = (nc + nl) * TM
    lo = lambda i: (jnp.minimum(i, nc - 1), 0)
    hi = lambda i: (jnp.maximum(i - nc, 0), 0)
    const = lambda i: (0, 0)
    row = lambda i: (i, 0)
    return pl.pallas_call(
        functools.partial(_mix_kernel, nc=nc, tpb=tpb),
        grid=(nc + nl,),
        in_specs=[pl.BlockSpec((TM, D_MODEL), lo), pl.BlockSpec((TM, D_MODEL), hi),
                  pl.BlockSpec((TM, ATTN_W), lo), pl.BlockSpec((TM, ATTN_W), hi),
                  pl.BlockSpec((FOURIER_W // FOURIER_GROUP_W, TM, FOURIER_GROUP_W),
                               lambda i: (0, jnp.minimum(i, nc - 1), 0)),
                  pl.BlockSpec((FOURIER_W // FOURIER_GROUP_W, TM, FOURIER_GROUP_W),
                               lambda i: (0, jnp.maximum(i - nc, 0), 0)),
                  pl.BlockSpec((8, D_MODEL), lambda i: (0, 2)),
                  pl.BlockSpec((8, D_MODEL), lambda i: (0, 3)),
                  pl.BlockSpec((8, D_MODEL), lambda i: (0, 4)),
                  pl.BlockSpec((1, ATTN_W), const), pl.BlockSpec((1, FOURIER_W), const),
                  pl.BlockSpec((D_MODEL, D_MODEL), const), pl.BlockSpec((1, D_MODEL), const),
                  pl.BlockSpec((D_MODEL, LANES), const), pl.BlockSpec((D_MODEL, LANES), const),
                  pl.BlockSpec((1, LANES), const)],
        out_specs=[pl.BlockSpec((TM, D_MODEL), row), pl.BlockSpec((TM, D_MODEL), row),
                   pl.BlockSpec((TM, LANES), row), pl.BlockSpec((TM, LANES), row), pl.BlockSpec((8, LANES), const)],
        out_shape=[jax.ShapeDtypeStruct((n, D_MODEL), F32), jax.ShapeDtypeStruct((n, D_MODEL), F32),
                   jax.ShapeDtypeStruct((n, LANES), I32), jax.ShapeDtypeStruct((n, LANES), F32),
                   jax.ShapeDtypeStruct((8, LANES), F32)],
        compiler_params=_cparams(("arbitrary",)),
        name="mix_proj",
    )(xc, xl, ac, al, fc, fl, mod, mod, mod, attn_g, four_g, w_out, norm2_g, wr_hi, wr_lo, br)


RUN_ALIGN = 8
RUN_ROWS = TM * TOP_K + N_EXPERTS * RUN_ALIGN
RUN_SIZES = tuple(TM >> s for s in range(7))
RUN_PAD = RUN_ROWS - TM * TOP_K
FILL_SIZES = tuple(RUN_PAD >> s for s in range(6))


def _round_up_runs(x):
    return jnp.floor((x + (RUN_ALIGN - 1)) * (1.0 / RUN_ALIGN)) * RUN_ALIGN


def _pos_kernel(idx_ref, gs_ref, loct_ref, loc_ref, info_ref, carry_ref):
    i = pl.program_id(0)

    @pl.when(i == 0)
    def _():
        carry_ref[...] = jnp.zeros(carry_ref.shape, F32)

    idx = idx_ref[...]
    lane = lax.broadcasted_iota(I32, idx.shape, 1)
    sels = [lane == idx[:, k:k + 1] for k in range(TOP_K)]
    hits = jnp.zeros(idx.shape, F32)
    for s in sels:
        hits = hits + s.astype(F32)
    r = lax.broadcasted_iota(I32, (TM, TM), 0)
    c = lax.broadcasted_iota(I32, (TM, TM), 1)
    before = (c < r).astype(BF16)
    rank = jnp.dot(before, hits.astype(BF16), preferred_element_type=F32)
    run_len = _round_up_runs(jnp.sum(hits, axis=0, keepdims=True))
    a = lax.broadcasted_iota(I32, (LANES, LANES), 0)
    b = lax.broadcasted_iota(I32, (LANES, LANES), 1)
    units = jnp.broadcast_to(run_len * (1.0 / RUN_ALIGN), (8, LANES)).astype(BF16)
    run_off = jnp.dot(units, (a < b).astype(BF16), preferred_element_type=F32)[0:1, :] * RUN_ALIGN
    run_start = carry_ref[0:1, :] + gs_ref[0:1, :]
    loc_cols = [jnp.sum(jnp.where(s, rank + run_off, 0.0), axis=-1, keepdims=True) for s in sels]
    loc = _pack_cols(loc_cols, lane)
    loc_ref[...] = loc.astype(I32)
    loct_ref[...] = loc.T[:8, :].astype(I32)
    row = lax.broadcasted_iota(I32, (8, LANES), 0)
    info = jnp.where(row == 0, run_start, jnp.where(row == 1, run_len, jnp.where(row == 2, run_off, 0.0)))
    info_ref[...] = info.astype(I32)
    carry_ref[...] += run_len


def _positions(idx, gs_rows):
    n = idx.shape[0]
    nt = n // TM
    return pl.pallas_call(
        _pos_kernel,
        grid=(nt,),
        in_specs=[pl.BlockSpec((TM, LANES), lambda i: (i, 0)), pl.BlockSpec((8, LANES), lambda i: (0, 0))],
        out_specs=[pl.BlockSpec((8, TM), lambda i: (0, i)), pl.BlockSpec((TM, LANES), lambda i: (i, 0)),
                   pl.BlockSpec((None, 8, LANES), lambda i: (i, 0, 0))],
        out_shape=[jax.ShapeDtypeStruct((8, n), I32), jax.ShapeDtypeStruct((n, LANES), I32),
                   jax.ShapeDtypeStruct((nt, 8, LANES), I32)],
        scratch_shapes=[pltpu.VMEM((8, LANES), F32)],
        compiler_params=_cparams(("arbitrary",)),
        name="positions",
    )(idx, gs_rows)


def _run_copies(info_ref, tile, make_copy, act):
    base = tile * (3 * N_EXPERTS)

    def per_expert(e, carry):
        start = info_ref[base + e]
        length = info_ref[base + N_EXPERTS + e]
        off = info_ref[base + 2 * N_EXPERTS + e]
        for size in RUN_SIZES:
            @pl.when((length & size) != 0)
            def _():
                act(make_copy(pl.multiple_of(off, RUN_ALIGN), pl.multiple_of(start, RUN_ALIGN), size))
            taken = jnp.where((length & size) != 0, size, 0)
            off = off + taken
            start = start + taken
        return carry

    lax.fori_loop(0, N_EXPERTS, per_expert, 0)


PAD_SIZES = tuple((TG // 2) >> s for s in range(5))


def _dispatch_kernel(info_ref, ends_ref, cnt_ref, loc_ref, h2_ref, xs_ref, dump_ref, runs_ref, zero_ref, sem, zsem):
    i = pl.program_id(0)
    last_step = pl.num_programs(0) - 1
    slot = i % 2

    def zero_copies(act):
        def per_expert(e, carry):
            lo = jnp.where(e == 0, 0, ends_ref[jnp.maximum(e - 1, 0)])
            first = lo + cnt_ref[e]
            rest = ends_ref[e] - first
            done = rest * 0
            for size in PAD_SIZES:
                @pl.when((rest & size) != 0)
                def _():
                    act(pltpu.make_async_copy(zero_ref.at[pl.ds(0, size), :],
                                              xs_ref.at[pl.ds(pl.multiple_of(first + done, RUN_ALIGN), size), :], zsem))
                done = done + jnp.where((rest & size) != 0, size, 0)
            return carry

        lax.fori_loop(0, N_EXPERTS, per_expert, 0)

        def tail_tile(t, carry):
            act(pltpu.make_async_copy(zero_ref, xs_ref.at[pl.ds(pl.multiple_of(t * TG, TG), TG), :], zsem))
            return carry

        lax.fori_loop(ends_ref[N_EXPERTS - 1] // TG, xs_ref.shape[0] // TG, tail_tile, 0)

    def slot_done(s):
        return pltpu.make_async_copy(runs_ref.at[s], xs_ref.at[pl.ds(0, RUN_ROWS), :], sem.at[s])

    @pl.when(i == 0)
    def _():
        zero_ref[...] = jnp.zeros(zero_ref.shape, F32)
        for s in range(2):
            pltpu.make_async_copy(zero_ref.at[pl.ds(0, RUN_PAD), :], dump_ref.at[s], sem.at[s]).start()
        for s in range(2):
            pltpu.make_async_copy(zero_ref.at[pl.ds(0, RUN_PAD), :], dump_ref.at[s], sem.at[s]).wait()
        zero_copies(lambda cp: cp.start())

    @pl.when(i >= 2)
    def _():
        slot_done(slot).wait()

    loc = loc_ref[...]
    row = lax.broadcasted_iota(I32, (RUN_ROWS, TM), 0)
    sel = row == loc[0:1, :]
    for k in range(1, TOP_K):
        sel = jnp.logical_or(sel, row == loc[k:k + 1, :])
    runs_ref[slot] = jnp.dot(sel.astype(BF16), h2_ref[...].astype(BF16), preferred_element_type=F32)

    def piece(stacked_row, global_row, size):
        return pltpu.make_async_copy(runs_ref.at[slot, pl.ds(stacked_row, size), :],
                                     xs_ref.at[pl.ds(global_row, size), :], sem.at[slot])

    _run_copies(info_ref, i, piece, lambda cp: cp.start())
    last = i * (3 * N_EXPERTS) + N_EXPERTS - 1
    used = info_ref[last + 2 * N_EXPERTS] + info_ref[last + N_EXPERTS]
    rest = RUN_ROWS - used
    done = used * 0
    for size in FILL_SIZES:
        @pl.when((rest & size) != 0)
        def _():
            at = pl.multiple_of(used + done, RUN_ALIGN)
            to = pl.multiple_of(at - TM * TOP_K, RUN_ALIGN)
            pltpu.make_async_copy(runs_ref.at[slot, pl.ds(at, size), :], dump_ref.at[slot, pl.ds(to, size), :],
                                  sem.at[slot]).start()
        done = done + jnp.where((rest & size) != 0, size, 0)

    @pl.when(i == last_step)
    def _():
        slot_done(slot).wait()

        @pl.when(i >= 1)
        def _():
            slot_done(1 - slot).wait()

        zero_copies(lambda cp: cp.wait())


def _dispatch(info, ends, cnt, loc, h2, n_rows):
    n = h2.shape[0]
    grid_spec = pltpu.PrefetchScalarGridSpec(
        num_scalar_prefetch=3,
        grid=(n // TM,),
        in_specs=[pl.BlockSpec((8, TM), lambda i, *_: (0, i)),
                  pl.BlockSpec((TM, D_MODEL), lambda i, *_: (i, 0))],
        out_specs=[pl.BlockSpec(memory_space=pl.ANY), pl.BlockSpec(memory_space=pl.ANY)],
        scratch_shapes=[pltpu.VMEM((2, RUN_ROWS, D_MODEL), F32), pltpu.VMEM((TG, D_MODEL), F32),
                        pltpu.SemaphoreType.DMA((2,)), pltpu.SemaphoreType.DMA(())],
    )
    xs, _ = pl.pallas_call(
        _dispatch_kernel,
        grid_spec=grid_spec,
        out_shape=[jax.ShapeDtypeStruct((n_rows, D_MODEL), F32), jax.ShapeDtypeStruct((2, RUN_PAD, D_MODEL), F32)],
        compiler_params=_cparams(("arbitrary",)),
        name="dispatch",
    )(info, ends, cnt, loc, h2)
    return xs


def _ffn_kernel(te_ref, nv_ref, par_ref, nxt_ref, x_ref, bg_ref, bl_ref, bdn_ref, perm_ref, wgu_hbm, wdn_hbm, o_ref,
                wgu_buf, wdn_buf, wg_s, wl_s, wd_s, wsem):
    i = pl.program_id(0)
    e = te_ref[i]
    valid = i < nv_ref[0]
    fresh = jnp.logical_or(i == 0, e != te_ref[jnp.maximum(i - 1, 0)])
    wslot = par_ref[i]

    def weight_copies(expert, s):
        return (pltpu.make_async_copy(wgu_hbm.at[expert], wgu_buf.at[s], wsem.at[s]),
                pltpu.make_async_copy(wdn_hbm.at[expert], wdn_buf.at[s], wsem.at[s]))

    @pl.when(i == 0)
    def _():
        for cp in weight_copies(e, wslot):
            cp.start()

    @pl.when(jnp.logical_and(valid, fresh))
    def _():
        for cp in weight_copies(e, wslot):
            cp.wait()
        for j in range(2 * D_EXPERT // (2 * LANES)):
            blk = wgu_buf[wslot, :, j * 2 * LANES:(j + 1) * 2 * LANES].astype(BF16)
            d = jnp.dot(blk, perm_ref[...], preferred_element_type=F32)
            wg_s[:, j * LANES:(j + 1) * LANES] = d[:, :LANES].astype(BF16)
            wl_s[:, j * LANES:(j + 1) * LANES] = d[:, LANES:].astype(BF16)
        wd_s[...] = wdn_buf[wslot].astype(BF16)

        @pl.when(nxt_ref[i] >= 0)
        def _():
            for cp in weight_copies(nxt_ref[i], 1 - wslot):
                cp.start()

    @pl.when(valid)
    def _():
        x = x_ref[...].astype(BF16)
        g = jnp.dot(x, wg_s[...], preferred_element_type=F32) + bg_ref[...]
        l = jnp.dot(x, wl_s[...], preferred_element_type=F32) + bl_ref[...]
        g = jnp.minimum(g, SWIGLU_LIMIT)
        l = jnp.clip(l, -SWIGLU_LIMIT, SWIGLU_LIMIT)
        a = (l + 1.0) * (g * (1.0 / (1.0 + jnp.exp(-SWIGLU_ALPHA * g))))
        o_ref[...] = jnp.dot(a.astype(BF16), wd_s[...], preferred_element_type=F32) + bdn_ref[...]

    @pl.when(jnp.logical_not(valid))
    def _():
        o_ref[...] = jnp.zeros(o_ref.shape, F32)


def _expert_ffn(tile_expert, n_valid, parity, next_expert, xs, w_gu, b_g, b_l, w_dn, b_dn, perm):
    p = xs.shape[0]
    nt = p // TG
    tile = lambda i, te, nv, *_: (jnp.minimum(i, nv[0] - 1), 0)
    ex3 = lambda i, te, *_: (te[i], 0, 0)
    grid_spec = pltpu.PrefetchScalarGridSpec(
        num_scalar_prefetch=4,
        grid=(nt,),
        in_specs=[pl.BlockSpec((TG, D_MODEL), tile),
                  pl.BlockSpec((None, 1, D_EXPERT), ex3),
                  pl.BlockSpec((None, 1, D_EXPERT), ex3),
                  pl.BlockSpec((None, 1, D_MODEL), ex3),
                  pl.BlockSpec((2 * LANES, 2 * LANES), lambda i, *_: (0, 0)),
                  pl.BlockSpec(memory_space=pl.ANY),
                  pl.BlockSpec(memory_space=pl.ANY)],
        out_specs=pl.BlockSpec((TG, D_MODEL), lambda i, *_: (i, 0)),
        scratch_shapes=[pltpu.VMEM((2, D_MODEL, 2 * D_EXPERT), F32), pltpu.VMEM((2, D_EXPERT, D_MODEL), F32),
                        pltpu.VMEM((D_MODEL, D_EXPERT), BF16), pltpu.VMEM((D_MODEL, D_EXPERT), BF16),
                        pltpu.VMEM((D_EXPERT, D_MODEL), BF16), pltpu.SemaphoreType.DMA((2,))],
    )
    return pl.pallas_call(
        _ffn_kernel,
        grid_spec=grid_spec,
        out_shape=jax.ShapeDtypeStruct((p, D_MODEL), F32),
        compiler_params=_cparams(("arbitrary",)),
        name="expert_ffn",
    )(tile_expert, n_valid, parity, next_expert, xs, b_g, b_l, b_dn, perm, w_gu, w_dn)


def _combine_kernel(info_ref, loc_ref, w_ref, x1_ref, g2_ref, fg_ref, ys_ref, o_ref, runs_ref, sem, *, tile0, nc, tpb):
    j = pl.program_id(0)
    i = j + tile0
    r = _mod_row(i, nc, tpb)
    slot = j % 2

    def gather(tile, s, act):
        def piece(stacked_row, global_row, size):
            return pltpu.make_async_copy(ys_ref.at[pl.ds(global_row, size), :],
                                         runs_ref.at[s, pl.ds(stacked_row, size), :], sem.at[s])

        _run_copies(info_ref, tile, piece, act)
        last = tile * (3 * N_EXPERTS) + N_EXPERTS - 1
        used = info_ref[last + 2 * N_EXPERTS] + info_ref[last + N_EXPERTS]
        rest = RUN_ROWS - used
        done = used * 0
        for size in FILL_SIZES:
            @pl.when((rest & size) != 0)
            def _():
                act(piece(pl.multiple_of(used + done, RUN_ALIGN), pl.multiple_of(done, RUN_ALIGN), size))
            done = done + jnp.where((rest & size) != 0, size, 0)

    @pl.when(j == 0)
    def _():
        gather(i, 0, lambda cp: cp.start())

    @pl.when(j + 1 < pl.num_programs(0))
    def _():
        gather(i + 1, 1 - slot, lambda cp: cp.start())

    pltpu.make_async_copy(ys_ref.at[pl.ds(0, RUN_ROWS), :], runs_ref.at[slot], sem.at[slot]).wait()
    loc = loc_ref[...]
    w = w_ref[...]
    col = lax.broadcasted_iota(I32, (TM, RUN_ROWS), 1)
    mix = jnp.where(col == loc[:, 0:1], w[:, 0:1], 0.0)
    for k in range(1, TOP_K):
        mix = mix + jnp.where(col == loc[:, k:k + 1], w[:, k:k + 1], 0.0)
    moe = jnp.dot(mix.astype(BF16), runs_ref[slot].astype(BF16), preferred_element_type=F32)
    y = x1_ref[...] + g2_ref[pl.ds(r, 1), :] * moe
    o_ref[...] = _rms(y) * fg_ref[...]


def _combine(info, loc, w, x1, mod, final_g, ys, *, tile0, ntiles, nc, tpb):
    grid_spec = pltpu.PrefetchScalarGridSpec(
        num_scalar_prefetch=1,
        grid=(ntiles,),
        in_specs=[pl.BlockSpec((TM, LANES), lambda i, *_: (i + tile0, 0)),
                  pl.BlockSpec((TM, LANES), lambda i, *_: (i + tile0, 0)),
                  pl.BlockSpec((TM, D_MODEL), lambda i, *_: (i + tile0, 0)),
                  pl.BlockSpec((8, D_MODEL), lambda i, *_: (0, 5)),
                  pl.BlockSpec((1, D_MODEL), lambda i, *_: (0, 0)),
                  pl.BlockSpec(memory_space=pl.ANY)],
        out_specs=pl.BlockSpec((TM, D_MODEL), lambda i, *_: (i, 0)),
        scratch_shapes=[pltpu.VMEM((2, RUN_ROWS, D_MODEL), F32), pltpu.SemaphoreType.DMA((2,))],
    )
    return pl.pallas_call(
        functools.partial(_combine_kernel, tile0=tile0, nc=nc, tpb=tpb),
        grid_spec=grid_spec,
        out_shape=jax.ShapeDtypeStruct((ntiles * TM, D_MODEL), F32),
        compiler_params=_cparams(("arbitrary",)),
        name="combine",
    )(info, loc, w, x1, mod, final_g, ys)


def kernel(x_prompt, x_sample, cache_k, cache_v, c, c_ctx, norm1_g, w_mod, b_mod, w_in, q_norm_g, k_norm_g,
           w_fourier, attn_out_g, fourier_out_g, w_out, norm2_g, w_router, b_router, w_gate_up, b_gate_up,
           w_down, b_down, final_g):
    bc, tc, _ = x_prompt.shape
    bl, tl, _ = x_sample.shape
    depth = w_in.shape[0]
    past = cache_k.shape[2]
    assert depth == 1 and tl % TQ == 0 and tl % TM == 0 and (bc * tc) % TM == 0 and (bc * tc) % tl == 0
    n_ctx, n_lat = bc * tc, bl * tl
    n_tok = n_ctx + n_lat
    nc, nl, tpb = n_ctx // TM, n_lat // TM, tl // TM

    xc = x_prompt.reshape(n_ctx, D_MODEL)
    xl = x_sample.reshape(n_lat, D_MODEL)

    c_rows = jnp.concatenate([c_ctx[None, :], c, jnp.zeros((8 - 1 - bl, D_MODEL), F32)], axis=0)
    mod = _modulation(c_rows, w_mod[0], b_mod[0])

    cos_t, sin_t = _rope_tables(tl)
    q, k_heads, vt_ext, f, k_new, v_new = _pre(
        xc, xl, mod, norm1_g, w_in[0].astype(BF16), jnp.tile(q_norm_g, (1, N_HEADS)),
        jnp.tile(k_norm_g, (1, N_KV_HEADS)), _head_mean_matrix(), cos_t, sin_t, nc=nc, nl=nl, tpb=tpb)

    attn_c = _attention(q, [(k_heads, vt_ext, tc, 0)], batch=bc, t=tc, q_row0=0)
    ck, cvt = _cache_heads(cache_k[:, 0], cache_v[:, 0])
    attn_l = _attention(q, [(ck, cvt, past, 0), (k_heads, vt_ext, tl, n_ctx // tl)], batch=bl, t=tl, q_row0=n_ctx)

    wf = w_fourier[0].astype(BF16)
    zero = jnp.zeros_like(wf[0])
    wf_pairs = jnp.stack([jnp.block([[wf[0], zero], [zero, wf[1]]]), jnp.block([[wf[2], zero], [zero, wf[3]]])])
    four_c = _fourier(f, wf_pairs, batch=bc, t=tc, row0=0, n1=1)
    four_l = _fourier(f, wf_pairs, batch=bl, t=tl, row0=n_ctx, n1=FFT_N1)

    wr = jnp.pad(w_router[0], ((0, 0), (0, LANES - N_EXPERTS)))
    wr_hi = wr.astype(BF16)
    wr_lo = (wr - wr_hi.astype(F32)).astype(BF16)
    br = jnp.pad(b_router[0], (0, LANES - N_EXPERTS), constant_values=-1e30).reshape(1, LANES)
    x1, h2, idx, gate_w, counts = _mix(xc, xl, attn_c, attn_l, four_c, four_l, mod, attn_out_g, fourier_out_g,
                                       w_out[0].astype(BF16), norm2_g, wr_hi, wr_lo, br, nc=nc, nl=nl, tpb=tpb)
    cnt = counts[0, :N_EXPERTS].astype(I32)
    padded = ((cnt + TG - 1) // TG) * TG
    ends = jnp.cumsum(padded)
    starts = ends - padded
    n_rows = n_tok * TOP_K + (n_tok // TM) * N_EXPERTS * RUN_ALIGN + N_EXPERTS * TG
    n_tiles = n_rows // TG
    tile_ids = jnp.arange(n_tiles, dtype=I32)
    tile_expert = jnp.minimum(jnp.sum((ends[None, :] // TG <= tile_ids[:, None]).astype(I32), axis=1),
                              N_EXPERTS - 1)
    n_valid = (ends[N_EXPERTS - 1] // TG).reshape(1)
    gs_rows = jnp.broadcast_to(jnp.pad(starts.astype(F32), (0, LANES - N_EXPERTS))[None, :], (8, LANES))
    loc_t, loc, info = _positions(idx, gs_rows)
    info = info[:, :3, :N_EXPERTS].reshape(-1)

    xs = _dispatch(info, ends, cnt, loc_t, h2, n_rows)
    fresh_tile = jnp.concatenate([jnp.ones((1,), I32), (tile_expert[1:] != tile_expert[:-1]).astype(I32)])
    parity = (jnp.cumsum(fresh_tile) - 1) % 2
    experts = jnp.arange(N_EXPERTS, dtype=I32)
    later = jnp.logical_and(padded[None, :] > 0, experts[None, :] > experts[:, None])
    next_of = jnp.min(jnp.where(later, experts[None, :], N_EXPERTS), axis=1)
    next_of = jnp.where(next_of == N_EXPERTS, -1, next_of)
    next_expert = jnp.sum(jnp.where(tile_expert[:, None] == experts[None, :], next_of[None, :], 0), axis=1)
    b_gu = b_gate_up[0].reshape(N_EXPERTS, 1, D_EXPERT, 2)
    ys = _expert_ffn(tile_expert, n_valid, parity.astype(I32), next_expert.astype(I32), xs, w_gate_up[0],
                     b_gu[..., 0], b_gu[..., 1], w_down[0], b_down[0].reshape(N_EXPERTS, 1, D_MODEL),
                     _deinterleave_matrix())

    y_c = _combine(info, loc, gate_w, x1, mod, final_g.reshape(1, -1), ys, tile0=0, ntiles=nc, nc=nc, tpb=tpb)
    y_l = _combine(info, loc, gate_w, x1, mod, final_g.reshape(1, -1), ys, tile0=nc, ntiles=nl, nc=nc, tpb=tpb)

    new_k = k_new.reshape(bc, 1, tc, N_KV_HEADS, HEAD_DIM)
    new_v = v_new.reshape(bc, 1, tc, N_KV_HEADS, HEAD_DIM)
    return (y_c.reshape(bc, tc, D_MODEL), y_l.reshape(bl, tl, D_MODEL), new_k, new_v)
```

```python
import functools
import math

import numpy as np
import jax
import jax.numpy as jnp
from jax import lax
from jax.experimental import pallas as pl
from jax.experimental.pallas import tpu as pltpu

F32 = jnp.float32
BF16 = jnp.bfloat16
I32 = jnp.int32

D_MODEL = 1024
HEAD_DIM = 64
N_HEADS = 8
N_KV_HEADS = 2
HEADS_PER_KV = N_HEADS // N_KV_HEADS
ATTN_W = N_HEADS * HEAD_DIM
KV_W = N_KV_HEADS * HEAD_DIM
FOURIER_W = 512
FOURIER_GROUP_W = 128
D_IN = ATTN_W + 2 * KV_W + FOURIER_W
N_EXPERTS = 32
TOP_K = 4
D_EXPERT = 1024
SWIGLU_ALPHA = 1.702
SWIGLU_LIMIT = 7.0
ROPE_THETA = 10000.0
ROT_PAIRS = HEAD_DIM // 4
GRID_W = 64
EPS = 1e-6

LANES = 128
TM = 512
TQ = 512
TG = 256
FFT_N1 = 4
V_ROWS = 80
VMEM_LIMIT = 56 * 1024 * 1024


def _cparams(sem, vmem=VMEM_LIMIT):
    return pltpu.CompilerParams(dimension_semantics=sem, vmem_limit_bytes=vmem)


def _rope_tables(t_lat):
    pos = np.arange(t_lat)
    row = (pos // GRID_W).astype(np.float64)
    col = (pos % GRID_W).astype(np.float64)
    inv = ROPE_THETA ** (-np.arange(ROT_PAIRS, dtype=np.float64) / ROT_PAIRS)
    lane = np.arange(LANES)
    d = lane % HEAD_DIM
    axis = d // (2 * ROT_PAIRS)
    second = (d // ROT_PAIRS) % 2
    p = d % ROT_PAIRS
    ang = np.where(axis[None, :] == 0, row[:, None], col[:, None]) * inv[p][None, :]
    cos = np.cos(ang)
    sin = np.sin(ang) * np.where(second == 0, -1.0, 1.0)[None, :]
    cos = np.concatenate([np.ones((TM, LANES)), cos], axis=0)
    sin = np.concatenate([np.zeros((TM, LANES)), sin], axis=0)
    return jnp.asarray(cos, F32), jnp.asarray(sin, F32)


def _dft_tables(n1, n2):
    c = np.arange(FOURIER_GROUP_W)
    ang = 2.0 * np.pi * np.outer(c, c) / FOURIER_GROUP_W
    z = np.zeros_like(ang)
    cbd = np.block([[np.cos(ang), z], [z, np.cos(ang)]])
    sbd = np.block([[np.sin(ang), z], [z, np.sin(ang)]])
    cd = np.concatenate([cbd, -sbd], axis=1)
    u = np.arange(n2)
    ang2 = 2.0 * np.pi * np.outer(u, u) / n2
    ct, st = np.cos(ang2), np.sin(ang2)
    nt = max(n1 - 1, 1)
    tw_ang = 2.0 * np.pi * np.outer(np.arange(1, nt + 1), u) / (n1 * n2)
    twc = np.repeat(np.cos(tw_ang)[:, :, None], LANES, axis=2)
    tws = np.repeat(np.sin(tw_ang)[:, :, None], LANES, axis=2)
    return (jnp.asarray(cd, F32), jnp.asarray(ct, F32), jnp.asarray(st, F32),
            jnp.asarray(twc, F32), jnp.asarray(tws, F32))


def _head_mean_matrix():
    h = np.arange(ATTN_W) // HEAD_DIM
    return jnp.asarray((h[:, None] == h[None, :]) / HEAD_DIM, BF16)


def _deinterleave_matrix():
    p = np.zeros((2 * LANES, 2 * LANES))
    m = np.arange(LANES)
    p[2 * m, m] = 1.0
    p[2 * m + 1, LANES + m] = 1.0
    return jnp.asarray(p, BF16)


def _mod_row(i, nc, tpb):
    return jnp.where(i < nc, 0, 1 + (i - nc) // tpb)


def _rms(x):
    return x * lax.rsqrt(jnp.mean(x * x, axis=-1, keepdims=True) + EPS)


def _pack_cols(cols, lane):
    out = jnp.zeros(lane.shape, cols[0].dtype)
    for k, c in enumerate(cols):
        out = jnp.where(lane == k, c, out)
    return out


def _mod_kernel(c_ref, w_ref, b_ref, o_ref):
    c = c_ref[...]
    s = c * (1.0 / (1.0 + jnp.exp(-c)))
    o_ref[...] = jnp.dot(s.astype(BF16), w_ref[...].astype(BF16), preferred_element_type=F32) + b_ref[...]


def _modulation(c_rows, w_mod, b_mod):
    n = w_mod.shape[1] // D_MODEL
    return pl.pallas_call(
        _mod_kernel,
        grid=(n,),
        in_specs=[pl.BlockSpec((8, D_MODEL), lambda j: (0, 0)),
                  pl.BlockSpec((D_MODEL, D_MODEL), lambda j: (0, j)),
                  pl.BlockSpec((1, D_MODEL), lambda j: (0, j))],
        out_specs=pl.BlockSpec((8, D_MODEL), lambda j: (0, j)),
        out_shape=jax.ShapeDtypeStruct((8, w_mod.shape[1]), F32),
        compiler_params=_cparams(("parallel",)),
        name="modulation",
    )(c_rows, w_mod, b_mod.reshape(1, -1))


def _pre_kernel(xc_ref, xl_ref, sh_ref, sc_ref, g1_ref, win_ref, qg_ref, kg_ref, bd_ref, cos_ref, sin_ref,
                q_ref, k_ref, vt_ref, f_ref, kn_ref, vn_ref, *, nc, tpb):
    i = pl.program_id(0)
    r = _mod_row(i, nc, tpb)
    x = jnp.where(i < nc, xc_ref[...], xl_ref[...])
    h = _rms(x) * g1_ref[...]
    h = h * (1.0 + sc_ref[pl.ds(r, 1), :]) + sh_ref[pl.ds(r, 1), :]
    proj = jnp.dot(h.astype(BF16), win_ref[...], preferred_element_type=F32)
    q = proj[:, :ATTN_W]
    k = proj[:, ATTN_W:ATTN_W + KV_W]
    v = proj[:, ATTN_W + KV_W:ATTN_W + 2 * KV_W]
    f_ref[...] = proj[:, ATTN_W + 2 * KV_W:].astype(BF16)
    vn_ref[...] = v
    vt = v.T
    tail = (lax.broadcasted_iota(I32, (V_ROWS - HEAD_DIM, vt.shape[1]), 0) == 0).astype(BF16)
    for j in range(N_KV_HEADS):
        vt_ref[j, :HEAD_DIM, :] = vt[j * HEAD_DIM:(j + 1) * HEAD_DIM, :].astype(BF16)
        vt_ref[j, HEAD_DIM:, :] = tail
    q_ms = jnp.dot((q * q).astype(BF16), bd_ref[...], preferred_element_type=F32)
    k_ms = jnp.dot((k * k).astype(BF16), bd_ref[:KV_W, :KV_W], preferred_element_type=F32)
    qn = q * lax.rsqrt(q_ms + EPS) * qg_ref[...]
    kn = k * lax.rsqrt(k_ms + EPS) * kg_ref[...]
    kn_ref[...] = kn
    cos = cos_ref[...]
    sin = sin_ref[...]
    lane = lax.broadcasted_iota(I32, cos.shape, 1)
    first = (lane & ROT_PAIRS) == 0

    def rope(c):
        partner = jnp.where(first, pltpu.roll(c, LANES - ROT_PAIRS, 1), pltpu.roll(c, ROT_PAIRS, 1))
        return c * cos + partner * sin

    k_rot = rope(kn).astype(BF16)
    for j in range(N_KV_HEADS):
        k_ref[j] = k_rot[:, j * HEAD_DIM:(j + 1) * HEAD_DIM]
    scale = HEAD_DIM ** -0.5 * math.log2(math.e)
    q_rot = jnp.concatenate([rope(qn[:, j * LANES:(j + 1) * LANES]) for j in range(ATTN_W // LANES)], axis=1)
    q_ref[...] = (q_rot * scale).T.astype(BF16)


def _pre(xc, xl, mod, norm1_g, w_in, q_g, k_g, bd, cos_t, sin_t, *, nc, nl, tpb):
    n = (nc + nl) * TM
    row = lambda i: (i, 0)
    const = lambda i: (0, 0)
    tab = lambda i: (jnp.where(i < nc, 0, 1 + (i - nc) % tpb), 0)
    return pl.pallas_call(
        functools.partial(_pre_kernel, nc=nc, tpb=tpb),
        grid=(nc + nl,),
        in_specs=[pl.BlockSpec((TM, D_MODEL), lambda i: (jnp.minimum(i, nc - 1), 0)),
                  pl.BlockSpec((TM, D_MODEL), lambda i: (jnp.maximum(i - nc, 0), 0)),
                  pl.BlockSpec((8, D_MODEL), lambda i: (0, 0)),
                  pl.BlockSpec((8, D_MODEL), lambda i: (0, 1)),
                  pl.BlockSpec((1, D_MODEL), const),
                  pl.BlockSpec((D_MODEL, D_IN), const),
                  pl.BlockSpec((1, ATTN_W), const),
                  pl.BlockSpec((1, KV_W), const),
                  pl.BlockSpec((ATTN_W, ATTN_W), const),
                  pl.BlockSpec((TM, LANES), tab),
                  pl.BlockSpec((TM, LANES), tab)],
        out_specs=[pl.BlockSpec((ATTN_W, TM), lambda i: (0, i)),
                   pl.BlockSpec((N_KV_HEADS, TM, HEAD_DIM), lambda i: (0, i, 0)),
                   pl.BlockSpec((N_KV_HEADS, V_ROWS, TM), lambda i: (0, 0, i)),
                   pl.BlockSpec((TM, FOURIER_W), row), pl.BlockSpec((TM, KV_W), row), pl.BlockSpec((TM, KV_W), row)],
        out_shape=[jax.ShapeDtypeStruct((ATTN_W, n), BF16), jax.ShapeDtypeStruct((N_KV_HEADS, n, HEAD_DIM), BF16),
                   jax.ShapeDtypeStruct((N_KV_HEADS, V_ROWS, n), BF16), jax.ShapeDtypeStruct((n, FOURIER_W), BF16),
                   jax.ShapeDtypeStruct((n, KV_W), F32), jax.ShapeDtypeStruct((n, KV_W), F32)],
        compiler_params=_cparams(("parallel",)),
        name="pre_proj",
    )(xc, xl, mod, mod, norm1_g, w_in, q_g, k_g, bd, cos_t, sin_t)


def _attn_kernel(*refs, seg_lens):
    n_seg = len(seg_lens)
    qt_ref = refs[0]
    k_refs = refs[1:1 + n_seg]
    vt_refs = refs[1 + n_seg:1 + 2 * n_seg]
    o_ref, ot_ref, st_ref, pt_ref = refs[1 + 2 * n_seg:]
    offs = [sum(seg_lens[:m]) for m in range(n_seg)]

    def scores(h):
        qt = qt_ref[h * HEAD_DIM:(h + 1) * HEAD_DIM, :]
        for k_ref, off, ln in zip(k_refs, offs, seg_lens):
            st_ref[h % 2, off:off + ln] = jnp.dot(k_ref[h // HEADS_PER_KV], qt, preferred_element_type=F32)

    scores(0)
    for h in range(N_HEADS):
        if h + 1 < N_HEADS:
            scores(h + 1)
        st = st_ref[h % 2]
        pt_ref[h % 2] = jnp.exp2(st - jnp.max(st, axis=0, keepdims=True)).astype(BF16)
        ot = None
        for vt_ref, off, ln in zip(vt_refs, offs, seg_lens):
            part = jnp.dot(vt_ref[h // HEADS_PER_KV], pt_ref[h % 2, off:off + ln], preferred_element_type=F32)
            ot = part if ot is None else ot + part
        ot_ref[h * HEAD_DIM:(h + 1) * HEAD_DIM, :] = ot[:HEAD_DIM] * (1.0 / ot[HEAD_DIM:HEAD_DIM + 1])
    o_ref[...] = ot_ref[...].T


def _attention(qt, segments, *, batch, t, q_row0):
    tq = min(TQ, t)
    qpb = t // tq
    q0 = q_row0 // tq
    seg_lens = tuple(ln for _, _, ln, _ in segments)
    s_total = sum(seg_lens)
    k_specs = [pl.BlockSpec((N_KV_HEADS, ln, HEAD_DIM), functools.partial(lambda b, i, f: (0, f + b, 0), f=fb))
               for _, _, ln, fb in segments]
    v_specs = [pl.BlockSpec((N_KV_HEADS, V_ROWS, ln), functools.partial(lambda b, i, f: (0, 0, f + b), f=fb))
               for _, _, ln, fb in segments]
    return pl.pallas_call(
        functools.partial(_attn_kernel, seg_lens=seg_lens),
        grid=(batch, qpb),
        in_specs=[pl.BlockSpec((ATTN_W, tq), lambda b, i: (0, q0 + b * qpb + i))] + k_specs + v_specs,
        out_specs=pl.BlockSpec((tq, ATTN_W), lambda b, i: (b * qpb + i, 0)),
        out_shape=jax.ShapeDtypeStruct((batch * t, ATTN_W), F32),
        scratch_shapes=[pltpu.VMEM((ATTN_W, tq), F32), pltpu.VMEM((2, s_total, tq), F32),
                        pltpu.VMEM((2, s_total, tq), BF16)],
        compiler_params=_cparams(("parallel", "parallel")),
        name="attention",
    )(qt, *[k for k, _, _, _ in segments], *[v for _, v, _, _ in segments])


def _cache_heads(cache_k, cache_v):
    batch, past = cache_k.shape[:2]
    ck = cache_k.astype(BF16).transpose(2, 0, 1, 3).reshape(N_KV_HEADS, batch * past, HEAD_DIM)
    cv = cache_v.astype(BF16).transpose(2, 3, 0, 1).reshape(N_KV_HEADS, HEAD_DIM, batch * past)
    tail = jnp.concatenate([jnp.ones((N_KV_HEADS, 1, batch * past), BF16),
                            jnp.zeros((N_KV_HEADS, V_ROWS - HEAD_DIM - 1, batch * past), BF16)], axis=1)
    return ck, jnp.concatenate([cv, tail], axis=1)


def _fourier_kernel(x_ref, cd_ref, ct_ref, st_ref, twc_ref, tws_ref, wf_ref, o_ref, *, n1, n2):
    pw = 2 * FOURIER_GROUP_W
    scale = 1.0 / math.sqrt(n1 * n2 * FOURIER_GROUP_W)
    for gp in range(FOURIER_W // pw):
        z = jnp.dot(x_ref[:, gp * pw:(gp + 1) * pw], cd_ref[...], preferred_element_type=F32)
        zr = [z[t1 * n2:(t1 + 1) * n2, :pw] for t1 in range(n1)]
        zi = [z[t1 * n2:(t1 + 1) * n2, pw:] for t1 in range(n1)]
        if n1 == 1:
            a = [(zr[0], zi[0])]
        else:
            a = [(zr[0] + zr[1] + zr[2] + zr[3], zi[0] + zi[1] + zi[2] + zi[3]),
                 (zr[0] + zi[1] - zr[2] - zi[3], zi[0] - zr[1] - zi[2] + zr[3]),
                 (zr[0] - zr[1] + zr[2] - zr[3], zi[0] - zi[1] + zi[2] - zi[3]),
                 (zr[0] - zi[1] - zr[2] + zi[3], zi[0] + zr[1] - zi[2] - zr[3])]
        for u1 in range(n1):
            ar, ai = a[u1]
            if u1 > 0:
                c = jnp.concatenate([twc_ref[u1 - 1]] * 2, axis=1)
                s = jnp.concatenate([tws_ref[u1 - 1]] * 2, axis=1)
                ar, ai = ar * c + ai * s, ai * c - ar * s
            y = (jnp.dot(ct_ref[...], ar.astype(BF16), preferred_element_type=F32)
                 + jnp.dot(st_ref[...], ai.astype(BF16), preferred_element_type=F32)) * scale
            y = jnp.dot(y.astype(BF16), wf_ref[gp], preferred_element_type=F32)
            rows = pl.ds(u1, n2, stride=n1) if n1 > 1 else slice(None)
            for half in range(2):
                o_ref[2 * gp + half, rows, :] = y[:, half * FOURIER_GROUP_W:(half + 1) * FOURIER_GROUP_W]


def _fourier(f, wf_pairs, *, batch, t, row0, n1):
    n2 = t // n1
    cd, ct, st, twc, tws = _dft_tables(n1, n2)
    cd, ct, st = cd.astype(BF16), ct.astype(BF16), st.astype(BF16)
    nt = twc.shape[0]
    b0 = row0 // t
    c2 = lambda b: (0, 0)
    c3 = lambda b: (0, 0, 0)
    return pl.pallas_call(
        functools.partial(_fourier_kernel, n1=n1, n2=n2),
        grid=(batch,),
        in_specs=[pl.BlockSpec((t, FOURIER_W), lambda b: (b0 + b, 0)),
                  pl.BlockSpec(cd.shape, c2), pl.BlockSpec((n2, n2), c2), pl.BlockSpec((n2, n2), c2),
                  pl.BlockSpec((nt, n2, LANES), c3), pl.BlockSpec((nt, n2, LANES), c3),
                  pl.BlockSpec(wf_pairs.shape, c3)],
        out_specs=pl.BlockSpec((FOURIER_W // FOURIER_GROUP_W, t, FOURIER_GROUP_W), lambda b: (0, b, 0)),
        out_shape=jax.ShapeDtypeStruct((FOURIER_W // FOURIER_GROUP_W, batch * t, FOURIER_GROUP_W), F32),
        compiler_params=_cparams(("parallel",)),
        name="fourier",
    )(f, cd, ct, st, twc, tws, wf_pairs)


def _route(i, h, whi_ref, wlo_ref, br_ref, idx_ref, w_ref, cnt_ref):
    h_hi = h.astype(BF16)
    h_lo = (h - h_hi.astype(F32)).astype(BF16)
    logits = (jnp.dot(h_hi, whi_ref[...], preferred_element_type=F32)
              + jnp.dot(h_lo, whi_ref[...], preferred_element_type=F32)
              + jnp.dot(h_hi, wlo_ref[...], preferred_element_type=F32)) + br_ref[...]
    lane = lax.broadcasted_iota(I32, logits.shape, 1)
    l = logits
    vals, idxs = [], []
    hits = jnp.zeros(logits.shape, F32)
    for _ in range(TOP_K):
        m = jnp.max(l, axis=-1, keepdims=True)
        ix = jnp.min(jnp.where(l == m, lane, LANES), axis=-1, keepdims=True)
        sel = lane == ix
        hits = hits + sel.astype(F32)
        l = jnp.where(sel, -jnp.inf, l)
        vals.append(m)
        idxs.append(ix)
    es = [jnp.exp(v - vals[0]) for v in vals]
    den = es[0] + es[1] + es[2] + es[3]
    idx_ref[...] = _pack_cols(idxs, lane)
    w_ref[...] = _pack_cols([e / den for e in es], lane)

    @pl.when(i == 0)
    def _():
        cnt_ref[...] = jnp.zeros(cnt_ref.shape, F32)

    cnt_ref[...] += _round_up_runs(jnp.sum(hits, axis=0, keepdims=True))


def _mix_kernel(xc_ref, xl_ref, ac_ref, al_ref, fc_ref, fl_ref, g1_ref, sh2_ref, sc2_ref, ag_ref, fg_ref,
                wout_ref, n2g_ref, whi_ref, wlo_ref, br_ref, x1_ref, h2_ref, idx_ref, w_ref, cnt_ref, *, nc, tpb):
    i = pl.program_id(0)
    r = _mod_row(i, nc, tpb)
    is_ctx = i < nc
    x = jnp.where(is_ctx, xc_ref[...], xl_ref[...])
    a = _rms(jnp.where(is_ctx, ac_ref[...], al_ref[...])) * ag_ref[...]
    n_groups = FOURIER_W // FOURIER_GROUP_W
    f = jnp.concatenate([jnp.where(is_ctx, fc_ref[g], fl_ref[g]) for g in range(n_groups)], axis=1)
    f = _rms(f) * fg_ref[...]
    mixed = (jnp.dot(a.astype(BF16), wout_ref[:ATTN_W, :], preferred_element_type=F32)
             + jnp.dot(f.astype(BF16), wout_ref[ATTN_W:, :], preferred_element_type=F32))
    x1 = x + g1_ref[pl.ds(r, 1), :] * mixed
    x1_ref[...] = x1
    h2 = _rms(x1) * n2g_ref[...]
    h2 = h2 * (1.0 + sc2_ref[pl.ds(r, 1), :]) + sh2_ref[pl.ds(r, 1), :]
    h2_ref[...] = h2
    _route(i, h2, whi_ref, wlo_ref, br_ref, idx_ref, w_ref, cnt_ref)


def _mix(xc, xl, ac, al, fc, fl, mod, attn_g, four_g, w_out, norm2_g, wr_hi, wr_lo, br, *, nc, nl, tpb):
    n = (nc + nl) * TM
    lo = lambda i: (jnp.minimum(i, nc - 1), 0)
    hi = lambda i: (jnp.maximum(i - nc, 0), 0)
    const = lambda i: (0, 0)
    row = lambda i: (i, 0)
    return pl.pallas_call(
        functools.partial(_mix_kernel, nc=nc, tpb=tpb),
        grid=(nc + nl,),
        in_specs=[pl.BlockSpec((TM, D_MODEL), lo), pl.BlockSpec((TM, D_MODEL), hi),
                  pl.BlockSpec((TM, ATTN_W), lo), pl.BlockSpec((TM, ATTN_W), hi),
                  pl.BlockSpec((FOURIER_W // FOURIER_GROUP_W, TM, FOURIER_GROUP_W),
                               lambda i: (0, jnp.minimum(i, nc - 1), 0)),
                  pl.BlockSpec((FOURIER_W // FOURIER_GROUP_W, TM, FOURIER_GROUP_W),
                               lambda i: (0, jnp.maximum(i - nc, 0), 0)),
                  pl.BlockSpec((8, D_MODEL), lambda i: (0, 2)),
                  pl.BlockSpec((8, D_MODEL), lambda i: (0, 3)),
                  pl.BlockSpec((8, D_MODEL), lambda i: (0, 4)),
                  pl.BlockSpec((1, ATTN_W), const), pl.BlockSpec((1, FOURIER_W), const),
                  pl.BlockSpec((D_MODEL, D_MODEL), const), pl.BlockSpec((1, D_MODEL), const),
                  pl.BlockSpec((D_MODEL, LANES), const), pl.BlockSpec((D_MODEL, LANES), const),
                  pl.BlockSpec((1, LANES), const)],
        out_specs=[pl.BlockSpec((TM, D_MODEL), row), pl.BlockSpec((TM, D_MODEL), row),
                   pl.BlockSpec((TM, LANES), row), pl.BlockSpec((TM, LANES), row), pl.BlockSpec((8, LANES), const)],
        out_shape=[jax.ShapeDtypeStruct((n, D_MODEL), F32), jax.ShapeDtypeStruct((n, D_MODEL), F32),
                   jax.ShapeDtypeStruct((n, LANES), I32), jax.ShapeDtypeStruct((n, LANES), F32),
                   jax.ShapeDtypeStruct((8, LANES), F32)],
        compiler_params=_cparams(("arbitrary",)),
        name="mix_proj",
    )(xc, xl, ac, al, fc, fl, mod, mod, mod, attn_g, four_g, w_out, norm2_g, wr_hi, wr_lo, br)


RUN_ALIGN = 8
RUN_ROWS = TM * TOP_K + N_EXPERTS * RUN_ALIGN
RUN_SIZES = tuple(TM >> s for s in range(7))
RUN_PAD = RUN_ROWS - TM * TOP_K
FILL_SIZES = tuple(RUN_PAD >> s for s in range(6))


def _round_up_runs(x):
    return jnp.floor((x + (RUN_ALIGN - 1)) * (1.0 / RUN_ALIGN)) * RUN_ALIGN


def _pos_kernel(idx_ref, gs_ref, loct_ref, loc_ref, info_ref, carry_ref):
    i = pl.program_id(0)

    @pl.when(i == 0)
    def _():
        carry_ref[...] = jnp.zeros(carry_ref.shape, F32)

    idx = idx_ref[...]
    lane = lax.broadcasted_iota(I32, idx.shape, 1)
    sels = [lane == idx[:, k:k + 1] for k in range(TOP_K)]
    hits = jnp.zeros(idx.shape, F32)
    for s in sels:
        hits = hits + s.astype(F32)
    r = lax.broadcasted_iota(I32, (TM, TM), 0)
    c = lax.broadcasted_iota(I32, (TM, TM), 1)
    before = (c < r).astype(BF16)
    rank = jnp.dot(before, hits.astype(BF16), preferred_element_type=F32)
    run_len = _round_up_runs(jnp.sum(hits, axis=0, keepdims=True))
    a = lax.broadcasted_iota(I32, (LANES, LANES), 0)
    b = lax.broadcasted_iota(I32, (LANES, LANES), 1)
    units = jnp.broadcast_to(run_len * (1.0 / RUN_ALIGN), (8, LANES)).astype(BF16)
    run_off = jnp.dot(units, (a < b).astype(BF16), preferred_element_type=F32)[0:1, :] * RUN_ALIGN
    run_start = carry_ref[0:1, :] + gs_ref[0:1, :]
    loc_cols = [jnp.sum(jnp.where(s, rank + run_off, 0.0), axis=-1, keepdims=True) for s in sels]
    loc = _pack_cols(loc_cols, lane)
    loc_ref[...] = loc.astype(I32)
    loct_ref[...] = loc.T[:8, :].astype(I32)
    row = lax.broadcasted_iota(I32, (8, LANES), 0)
    info = jnp.where(row == 0, run_start, jnp.where(row == 1, run_len, jnp.where(row == 2, run_off, 0.0)))
    info_ref[...] = info.astype(I32)
    carry_ref[...] += run_len


def _positions(idx, gs_rows):
    n = idx.shape[0]
    nt = n // TM
    return pl.pallas_call(
        _pos_kernel,
        grid=(nt,),
        in_specs=[pl.BlockSpec((TM, LANES), lambda i: (i, 0)), pl.BlockSpec((8, LANES), lambda i: (0, 0))],
        out_specs=[pl.BlockSpec((8, TM), lambda i: (0, i)), pl.BlockSpec((TM, LANES), lambda i: (i, 0)),
                   pl.BlockSpec((None, 8, LANES), lambda i: (i, 0, 0))],
        out_shape=[jax.ShapeDtypeStruct((8, n), I32), jax.ShapeDtypeStruct((n, LANES), I32),
                   jax.ShapeDtypeStruct((nt, 8, LANES), I32)],
        scratch_shapes=[pltpu.VMEM((8, LANES), F32)],
        compiler_params=_cparams(("arbitrary",)),
        name="positions",
    )(idx, gs_rows)


def _run_copies(info_ref, tile, make_copy, act):
    base = tile * (3 * N_EXPERTS)

    def per_expert(e, carry):
        start = info_ref[base + e]
        length = info_ref[base + N_EXPERTS + e]
        off = info_ref[base + 2 * N_EXPERTS + e]
        for size in RUN_SIZES:
            @pl.when((length & size) != 0)
            def _():
                act(make_copy(pl.multiple_of(off, RUN_ALIGN), pl.multiple_of(start, RUN_ALIGN), size))
            taken = jnp.where((length & size) != 0, size, 0)
            off = off + taken
            start = start + taken
        return carry

    lax.fori_loop(0, N_EXPERTS, per_expert, 0)


PAD_SIZES = tuple((TG // 2) >> s for s in range(5))


def _dispatch_kernel(info_ref, ends_ref, cnt_ref, loc_ref, h2_ref, xs_ref, dump_ref, runs_ref, zero_ref, sem, zsem):
    i = pl.program_id(0)
    last_step = pl.num_programs(0) - 1
    slot = i % 2

    def zero_copies(act):
        def per_expert(e, carry):
            lo = jnp.where(e == 0, 0, ends_ref[jnp.maximum(e - 1, 0)])
            first = lo + cnt_ref[e]
            rest = ends_ref[e] - first
            done = rest * 0
            for size in PAD_SIZES:
                @pl.when((rest & size) != 0)
                def _():
                    act(pltpu.make_async_copy(zero_ref.at[pl.ds(0, size), :],
                                              xs_ref.at[pl.ds(pl.multiple_of(first + done, RUN_ALIGN), size), :], zsem))
                done = done + jnp.where((rest & size) != 0, size, 0)
            return carry

        lax.fori_loop(0, N_EXPERTS, per_expert, 0)

        def tail_tile(t, carry):
            act(pltpu.make_async_copy(zero_ref, xs_ref.at[pl.ds(pl.multiple_of(t * TG, TG), TG), :], zsem))
            return carry

        lax.fori_loop(ends_ref[N_EXPERTS - 1] // TG, xs_ref.shape[0] // TG, tail_tile, 0)

    def slot_done(s):
        return pltpu.make_async_copy(runs_ref.at[s], xs_ref.at[pl.ds(0, RUN_ROWS), :], sem.at[s])

    @pl.when(i == 0)
    def _():
        zero_ref[...] = jnp.zeros(zero_ref.shape, F32)
        for s in range(2):
            pltpu.make_async_copy(zero_ref.at[pl.ds(0, RUN_PAD), :], dump_ref.at[s], sem.at[s]).start()
        for s in range(2):
            pltpu.make_async_copy(zero_ref.at[pl.ds(0, RUN_PAD), :], dump_ref.at[s], sem.at[s]).wait()
        zero_copies(lambda cp: cp.start())

    @pl.when(i >= 2)
    def _():
        slot_done(slot).wait()

    loc = loc_ref[...]
    row = lax.broadcasted_iota(I32, (RUN_ROWS, TM), 0)
    sel = row == loc[0:1, :]
    for k in range(1, TOP_K):
        sel = jnp.logical_or(sel, row == loc[k:k + 1, :])
    runs_ref[slot] = jnp.dot(sel.astype(BF16), h2_ref[...].astype(BF16), preferred_element_type=F32)

    def piece(stacked_row, global_row, size):
        return pltpu.make_async_copy(runs_ref.at[slot, pl.ds(stacked_row, size), :],
                                     xs_ref.at[pl.ds(global_row, size), :], sem.at[slot])

    _run_copies(info_ref, i, piece, lambda cp: cp.start())
    last = i * (3 * N_EXPERTS) + N_EXPERTS - 1
    used = info_ref[last + 2 * N_EXPERTS] + info_ref[last + N_EXPERTS]
    rest = RUN_ROWS - used
    done = used * 0
    for size in FILL_SIZES:
        @pl.when((rest & size) != 0)
        def _():
            at = pl.multiple_of(used + done, RUN_ALIGN)
            to = pl.multiple_of(at - TM * TOP_K, RUN_ALIGN)
            pltpu.make_async_copy(runs_ref.at[slot, pl.ds(at, size), :], dump_ref.at[slot, pl.ds(to, size), :],
                                  sem.at[slot]).start()
        done = done + jnp.where((rest & size) != 0, size, 0)

    @pl.when(i == last_step)
    def _():
        slot_done(slot).wait()

        @pl.when(i >= 1)
        def _():
            slot_done(1 - slot).wait()

        zero_copies(lambda cp: cp.wait())


def _dispatch(info, ends, cnt, loc, h2, n_rows):
    n = h2.shape[0]
    grid_spec = pltpu.PrefetchScalarGridSpec(
        num_scalar_prefetch=3,
        grid=(n // TM,),
        in_specs=[pl.BlockSpec((8, TM), lambda i, *_: (0, i)),
                  pl.BlockSpec((TM, D_MODEL), lambda i, *_: (i, 0))],
        out_specs=[pl.BlockSpec(memory_space=pl.ANY), pl.BlockSpec(memory_space=pl.ANY)],
        scratch_shapes=[pltpu.VMEM((2, RUN_ROWS, D_MODEL), F32), pltpu.VMEM((TG, D_MODEL), F32),
                        pltpu.SemaphoreType.DMA((2,)), pltpu.SemaphoreType.DMA(())],
    )
    xs, _ = pl.pallas_call(
        _dispatch_kernel,
        grid_spec=grid_spec,
        out_shape=[jax.ShapeDtypeStruct((n_rows, D_MODEL), F32), jax.ShapeDtypeStruct((2, RUN_PAD, D_MODEL), F32)],
        compiler_params=_cparams(("arbitrary",)),
        name="dispatch",
    )(info, ends, cnt, loc, h2)
    return xs


def _ffn_kernel(te_ref, nv_ref, par_ref, nxt_ref, x_ref, bg_ref, bl_ref, bdn_ref, perm_ref, wgu_hbm, wdn_hbm, o_ref,
                wgu_buf, wdn_buf, wg_s, wl_s, wd_s, wsem):
    i = pl.program_id(0)
    e = te_ref[i]
    valid = i < nv_ref[0]
    fresh = jnp.logical_or(i == 0, e != te_ref[jnp.maximum(i - 1, 0)])
    wslot = par_ref[i]

    def weight_copies(expert, s):
        return (pltpu.make_async_copy(wgu_hbm.at[expert], wgu_buf.at[s], wsem.at[s]),
                pltpu.make_async_copy(wdn_hbm.at[expert], wdn_buf.at[s], wsem.at[s]))

    @pl.when(i == 0)
    def _():
        for cp in weight_copies(e, wslot):
            cp.start()

    @pl.when(jnp.logical_and(valid, fresh))
    def _():
        for cp in weight_copies(e, wslot):
            cp.wait()
        for j in range(2 * D_EXPERT // (2 * LANES)):
            blk = wgu_buf[wslot, :, j * 2 * LANES:(j + 1) * 2 * LANES].astype(BF16)
            d = jnp.dot(blk, perm_ref[...], preferred_element_type=F32)
            wg_s[:, j * LANES:(j + 1) * LANES] = d[:, :LANES].astype(BF16)
            wl_s[:, j * LANES:(j + 1) * LANES] = d[:, LANES:].astype(BF16)
        wd_s[...] = wdn_buf[wslot].astype(BF16)

        @pl.when(nxt_ref[i] >= 0)
        def _():
            for cp in weight_copies(nxt_ref[i], 1 - wslot):
                cp.start()

    @pl.when(valid)
    def _():
        x = x_ref[...].astype(BF16)
        g = jnp.dot(x, wg_s[...], preferred_element_type=F32) + bg_ref[...]
        l = jnp.dot(x, wl_s[...], preferred_element_type=F32) + bl_ref[...]
        g = jnp.minimum(g, SWIGLU_LIMIT)
        l = jnp.clip(l, -SWIGLU_LIMIT, SWIGLU_LIMIT)
        a = (l + 1.0) * (g * (1.0 / (1.0 + jnp.exp(-SWIGLU_ALPHA * g))))
        o_ref[...] = jnp.dot(a.astype(BF16), wd_s[...], preferred_element_type=F32) + bdn_ref[...]

    @pl.when(jnp.logical_not(valid))
    def _():
        o_ref[...] = jnp.zeros(o_ref.shape, F32)


def _expert_ffn(tile_expert, n_valid, parity, next_expert, xs, w_gu, b_g, b_l, w_dn, b_dn, perm):
    p = xs.shape[0]
    nt = p // TG
    tile = lambda i, te, nv, *_: (jnp.minimum(i, nv[0] - 1), 0)
    ex3 = lambda i, te, *_: (te[i], 0, 0)
    grid_spec = pltpu.PrefetchScalarGridSpec(
        num_scalar_prefetch=4,
        grid=(nt,),
        in_specs=[pl.BlockSpec((TG, D_MODEL), tile),
                  pl.BlockSpec((None, 1, D_EXPERT), ex3),
                  pl.BlockSpec((None, 1, D_EXPERT), ex3),
                  pl.BlockSpec((None, 1, D_MODEL), ex3),
                  pl.BlockSpec((2 * LANES, 2 * LANES), lambda i, *_: (0, 0)),
                  pl.BlockSpec(memory_space=pl.ANY),
                  pl.BlockSpec(memory_space=pl.ANY)],
        out_specs=pl.BlockSpec((TG, D_MODEL), lambda i, *_: (i, 0)),
        scratch_shapes=[pltpu.VMEM((2, D_MODEL, 2 * D_EXPERT), F32), pltpu.VMEM((2, D_EXPERT, D_MODEL), F32),
                        pltpu.VMEM((D_MODEL, D_EXPERT), BF16), pltpu.VMEM((D_MODEL, D_EXPERT), BF16),
                        pltpu.VMEM((D_EXPERT, D_MODEL), BF16), pltpu.SemaphoreType.DMA((2,))],
    )
    return pl.pallas_call(
        _ffn_kernel,
        grid_spec=grid_spec,
        out_shape=jax.ShapeDtypeStruct((p, D_MODEL), F32),
        compiler_params=_cparams(("arbitrary",)),
        name="expert_ffn",
    )(tile_expert, n_valid, parity, next_expert, xs, b_g, b_l, b_dn, perm, w_gu, w_dn)


def _combine_kernel(info_ref, loc_ref, w_ref, x1_ref, g2_ref, fg_ref, ys_ref, o_ref, runs_ref, sem, *, tile0, nc, tpb):
    j = pl.program_id(0)
    i = j + tile0
    r = _mod_row(i, nc, tpb)
    slot = j % 2

    def gather(tile, s, act):
        def piece(stacked_row, global_row, size):
            return pltpu.make_async_copy(ys_ref.at[pl.ds(global_row, size), :],
                                         runs_ref.at[s, pl.ds(stacked_row, size), :], sem.at[s])

        _run_copies(info_ref, tile, piece, act)
        last = tile * (3 * N_EXPERTS) + N_EXPERTS - 1
        used = info_ref[last + 2 * N_EXPERTS] + info_ref[last + N_EXPERTS]
        rest = RUN_ROWS - used
        done = used * 0
        for size in FILL_SIZES:
            @pl.when((rest & size) != 0)
            def _():
                act(piece(pl.multiple_of(used + done, RUN_ALIGN), pl.multiple_of(done, RUN_ALIGN), size))
            done = done + jnp.where((rest & size) != 0, size, 0)

    @pl.when(j == 0)
    def _():
        gather(i, 0, lambda cp: cp.start())

    @pl.when(j + 1 < pl.num_programs(0))
    def _():
        gather(i + 1, 1 - slot, lambda cp: cp.start())

    pltpu.make_async_copy(ys_ref.at[pl.ds(0, RUN_ROWS), :], runs_ref.at[slot], sem.at[slot]).wait()
    loc = loc_ref[...]
    w = w_ref[...]
    col = lax.broadcasted_iota(I32, (TM, RUN_ROWS), 1)
    mix = jnp.where(col == loc[:, 0:1], w[:, 0:1], 0.0)
    for k in range(1, TOP_K):
        mix = mix + jnp.where(col == loc[:, k:k + 1], w[:, k:k + 1], 0.0)
    moe = jnp.dot(mix.astype(BF16), runs_ref[slot].astype(BF16), preferred_element_type=F32)
    y = x1_ref[...] + g2_ref[pl.ds(r, 1), :] * moe
    o_ref[...] = _rms(y) * fg_ref[...]


def _combine(info, loc, w, x1, mod, final_g, ys, *, tile0, ntiles, nc, tpb):
    grid_spec = pltpu.PrefetchScalarGridSpec(
        num_scalar_prefetch=1,
        grid=(ntiles,),
        in_specs=[pl.BlockSpec((TM, LANES), lambda i, *_: (i + tile0, 0)),
                  pl.BlockSpec((TM, LANES), lambda i, *_: (i + tile0, 0)),
                  pl.BlockSpec((TM, D_MODEL), lambda i, *_: (i + tile0, 0)),
                  pl.BlockSpec((8, D_MODEL), lambda i, *_: (0, 5)),
                  pl.BlockSpec((1, D_MODEL), lambda i, *_: (0, 0)),
                  pl.BlockSpec(memory_space=pl.ANY)],
        out_specs=pl.BlockSpec((TM, D_MODEL), lambda i, *_: (i, 0)),
        scratch_shapes=[pltpu.VMEM((2, RUN_ROWS, D_MODEL), F32), pltpu.SemaphoreType.DMA((2,))],
    )
    return pl.pallas_call(
        functools.partial(_combine_kernel, tile0=tile0, nc=nc, tpb=tpb),
        grid_spec=grid_spec,
        out_shape=jax.ShapeDtypeStruct((ntiles * TM, D_MODEL), F32),
        compiler_params=_cparams(("arbitrary",)),
        name="combine",
    )(info, loc, w, x1, mod, final_g, ys)


def kernel(x_prompt, x_sample, cache_k, cache_v, c, c_ctx, norm1_g, w_mod, b_mod, w_in, q_norm_g, k_norm_g,
           w_fourier, attn_out_g, fourier_out_g, w_out, norm2_g, w_router, b_router, w_gate_up, b_gate_up,
           w_down, b_down, final_g):
    bc, tc, _ = x_prompt.shape
    bl, tl, _ = x_sample.shape
    depth = w_in.shape[0]
    past = cache_k.shape[2]
    assert depth == 1 and tl % TQ == 0 and tl % TM == 0 and (bc * tc) % TM == 0 and (bc * tc) % tl == 0
    n_ctx, n_lat = bc * tc, bl * tl
    n_tok = n_ctx + n_lat
    nc, nl, tpb = n_ctx // TM, n_lat // TM, tl // TM

    xc = x_prompt.reshape(n_ctx, D_MODEL)
    xl = x_sample.reshape(n_lat, D_MODEL)

    c_rows = jnp.concatenate([c_ctx[None, :], c, jnp.zeros((8 - 1 - bl, D_MODEL), F32)], axis=0)
    mod = _modulation(c_rows, w_mod[0], b_mod[0])

    cos_t, sin_t = _rope_tables(tl)
    q, k_heads, vt_ext, f, k_new, v_new = _pre(
        xc, xl, mod, norm1_g, w_in[0].astype(BF16), jnp.tile(q_norm_g, (1, N_HEADS)),
        jnp.tile(k_norm_g, (1, N_KV_HEADS)), _head_mean_matrix(), cos_t, sin_t, nc=nc, nl=nl, tpb=tpb)

    attn_c = _attention(q, [(k_heads, vt_ext, tc, 0)], batch=bc, t=tc, q_row0=0)
    ck, cvt = _cache_heads(cache_k[:, 0], cache_v[:, 0])
    attn_l = _attention(q, [(ck, cvt, past, 0), (k_heads, vt_ext, tl, n_ctx // tl)], batch=bl, t=tl, q_row0=n_ctx)

    wf = w_fourier[0].astype(BF16)
    zero = jnp.zeros_like(wf[0])
    wf_pairs = jnp.stack([jnp.block([[wf[0], zero], [zero, wf[1]]]), jnp.block([[wf[2], zero], [zero, wf[3]]])])
    four_c = _fourier(f, wf_pairs, batch=bc, t=tc, row0=0, n1=1)
    four_l = _fourier(f, wf_pairs, batch=bl, t=tl, row0=n_ctx, n1=FFT_N1)

    wr = jnp.pad(w_router[0], ((0, 0), (0, LANES - N_EXPERTS)))
    wr_hi = wr.astype(BF16)
    wr_lo = (wr - wr_hi.astype(F32)).astype(BF16)
    br = jnp.pad(b_router[0], (0, LANES - N_EXPERTS), constant_values=-1e30).reshape(1, LANES)
    x1, h2, idx, gate_w, counts = _mix(xc, xl, attn_c, attn_l, four_c, four_l, mod, attn_out_g, fourier_out_g,
                                       w_out[0].astype(BF16), norm2_g, wr_hi, wr_lo, br, nc=nc, nl=nl, tpb=tpb)
    cnt = counts[0, :N_EXPERTS].astype(I32)
    padded = ((cnt + TG - 1) // TG) * TG
    ends = jnp.cumsum(padded)
    starts = ends - padded
    n_rows = n_tok * TOP_K + (n_tok // TM) * N_EXPERTS * RUN_ALIGN + N_EXPERTS * TG
    n_tiles = n_rows // TG
    tile_ids = jnp.arange(n_tiles, dtype=I32)
    tile_expert = jnp.minimum(jnp.sum((ends[None, :] // TG <= tile_ids[:, None]).astype(I32), axis=1),
                              N_EXPERTS - 1)
    n_valid = (ends[N_EXPERTS - 1] // TG).reshape(1)
    gs_rows = jnp.broadcast_to(jnp.pad(starts.astype(F32), (0, LANES - N_EXPERTS))[None, :], (8, LANES))
    loc_t, loc, info = _positions(idx, gs_rows)
    info = info[:, :3, :N_EXPERTS].reshape(-1)

    xs = _dispatch(info, ends, cnt, loc_t, h2, n_rows)
    fresh_tile = jnp.concatenate([jnp.ones((1,), I32), (tile_expert[1:] != tile_expert[:-1]).astype(I32)])
    parity = (jnp.cumsum(fresh_tile) - 1) % 2
    experts = jnp.arange(N_EXPERTS, dtype=I32)
    later = jnp.logical_and(padded[None, :] > 0, experts[None, :] > experts[:, None])
    next_of = jnp.min(jnp.where(later, experts[None, :], N_EXPERTS), axis=1)
    next_of = jnp.where(next_of == N_EXPERTS, -1, next_of)
    next_expert = jnp.sum(jnp.where(tile_expert[:, None] == experts[None, :], next_of[None, :], 0), axis=1)
    b_gu = b_gate_up[0].reshape(N_EXPERTS, 1, D_EXPERT, 2)
    ys = _expert_ffn(tile_expert, n_valid, parity.astype(I32), next_expert.astype(I32), xs, w_gate_up[0],
                     b_gu[..., 0], b_gu[..., 1], w_down[0], b_down[0].reshape(N_EXPERTS, 1, D_MODEL),
                     _deinterleave_matrix())

    y_c = _combine(info, loc, gate_w, x1, mod, final_g.reshape(1, -1), ys, tile0=0, ntiles=nc, nc=nc, tpb=tpb)
    y_l = _combine(info, loc, gate_w, x1, mod, final_g.reshape(1, -1), ys, tile0=nc, ntiles=nl, nc=nc, tpb=tpb)

    new_k = k_new[:n_ctx].reshape(bc, 1, tc, N_KV_HEADS, HEAD_DIM)
    new_v = v_new[:n_ctx].reshape(bc, 1, tc, N_KV_HEADS, HEAD_DIM)
    return (y_c.reshape(bc, tc, D_MODEL), y_l.reshape(bl, tl, D_MODEL), new_k, new_v)
```

```python
import functools
import math

import numpy as np
import jax
import jax.numpy as jnp
from jax import lax
from jax.experimental import pallas as pl
from jax.experimental.pallas import tpu as pltpu

F32 = jnp.float32
BF16 = jnp.bfloat16
I32 = jnp.int32

D_MODEL = 1024
HEAD_DIM = 64
N_HEADS = 8
N_KV_HEADS = 2
HEADS_PER_KV = N_HEADS // N_KV_HEADS
ATTN_W = N_HEADS * HEAD_DIM
KV_W = N_KV_HEADS * HEAD_DIM
FOURIER_W = 512
FOURIER_GROUP_W = 128
D_IN = ATTN_W + 2 * KV_W + FOURIER_W
N_EXPERTS = 32
TOP_K = 4
D_EXPERT = 1024
SWIGLU_ALPHA = 1.702
SWIGLU_LIMIT = 7.0
ROPE_THETA = 10000.0
ROT_PAIRS = HEAD_DIM // 4
GRID_W = 64
EPS = 1e-6

LANES = 128
TM = 512
TQ = 512
TG = 512
FFT_N1 = 4
V_ROWS = 80
VMEM_LIMIT = 56 * 1024 * 1024


def _cparams(sem, vmem=VMEM_LIMIT):
    return pltpu.CompilerParams(dimension_semantics=sem, vmem_limit_bytes=vmem)


def _rope_tables(t_lat):
    pos = np.arange(t_lat)
    row = (pos // GRID_W).astype(np.float64)
    col = (pos % GRID_W).astype(np.float64)
    inv = ROPE_THETA ** (-np.arange(ROT_PAIRS, dtype=np.float64) / ROT_PAIRS)
    lane = np.arange(LANES)
    d = lane % HEAD_DIM
    axis = d // (2 * ROT_PAIRS)
    second = (d // ROT_PAIRS) % 2
    p = d % ROT_PAIRS
    ang = np.where(axis[None, :] == 0, row[:, None], col[:, None]) * inv[p][None, :]
    cos = np.cos(ang)
    sin = np.sin(ang) * np.where(second == 0, -1.0, 1.0)[None, :]
    cos = np.concatenate([np.ones((TM, LANES)), cos], axis=0)
    sin = np.concatenate([np.zeros((TM, LANES)), sin], axis=0)
    return jnp.asarray(cos, F32), jnp.asarray(sin, F32)


def _dft_tables(n1, n2):
    c = np.arange(FOURIER_GROUP_W)
    ang = 2.0 * np.pi * np.outer(c, c) / FOURIER_GROUP_W
    z = np.zeros_like(ang)
    cbd = np.block([[np.cos(ang), z], [z, np.cos(ang)]])
    sbd = np.block([[np.sin(ang), z], [z, np.sin(ang)]])
    cd = np.concatenate([cbd, -sbd], axis=1)
    u = np.arange(n2)
    ang2 = 2.0 * np.pi * np.outer(u, u) / n2
    ct, st = np.cos(ang2), np.sin(ang2)
    nt = max(n1 - 1, 1)
    tw_ang = 2.0 * np.pi * np.outer(np.arange(1, nt + 1), u) / (n1 * n2)
    twc = np.repeat(np.cos(tw_ang)[:, :, None], LANES, axis=2)
    tws = np.repeat(np.sin(tw_ang)[:, :, None], LANES, axis=2)
    return (jnp.asarray(cd, F32), jnp.asarray(ct, F32), jnp.asarray(st, F32),
            jnp.asarray(twc, F32), jnp.asarray(tws, F32))


def _head_mean_matrix():
    h = np.arange(ATTN_W) // HEAD_DIM
    return jnp.asarray((h[:, None] == h[None, :]) / HEAD_DIM, BF16)


def _deinterleave_matrix():
    p = np.zeros((2 * LANES, 2 * LANES))
    m = np.arange(LANES)
    p[2 * m, m] = 1.0
    p[2 * m + 1, LANES + m] = 1.0
    return jnp.asarray(p, BF16)


def _mod_row(i, nc, tpb):
    return jnp.where(i < nc, 0, 1 + (i - nc) // tpb)


def _rms(x):
    return x * lax.rsqrt(jnp.mean(x * x, axis=-1, keepdims=True) + EPS)


def _pack_cols(cols, lane):
    out = jnp.zeros(lane.shape, cols[0].dtype)
    for k, c in enumerate(cols):
        out = jnp.where(lane == k, c, out)
    return out


def _mod_kernel(c_ref, w_ref, b_ref, o_ref):
    c = c_ref[...]
    s = c * (1.0 / (1.0 + jnp.exp(-c)))
    o_ref[...] = jnp.dot(s.astype(BF16), w_ref[...].astype(BF16), preferred_element_type=F32) + b_ref[...]


def _modulation(c_rows, w_mod, b_mod):
    n = w_mod.shape[1] // D_MODEL
    return pl.pallas_call(
        _mod_kernel,
        grid=(n,),
        in_specs=[pl.BlockSpec((8, D_MODEL), lambda j: (0, 0)),
                  pl.BlockSpec((D_MODEL, D_MODEL), lambda j: (0, j)),
                  pl.BlockSpec((1, D_MODEL), lambda j: (0, j))],
        out_specs=pl.BlockSpec((8, D_MODEL), lambda j: (0, j)),
        out_shape=jax.ShapeDtypeStruct((8, w_mod.shape[1]), F32),
        compiler_params=_cparams(("parallel",)),
        name="modulation",
    )(c_rows, w_mod, b_mod.reshape(1, -1))


def _pre_kernel(xc_ref, xl_ref, sh_ref, sc_ref, g1_ref, win_ref, qg_ref, kg_ref, bd_ref, cos_ref, sin_ref,
                q_ref, k_ref, vt_ref, f_ref, kn_ref, vn_ref, *, nc, tpb):
    i = pl.program_id(0)
    r = _mod_row(i, nc, tpb)
    x = jnp.where(i < nc, xc_ref[...], xl_ref[...])
    h = _rms(x) * g1_ref[...]
    h = h * (1.0 + sc_ref[pl.ds(r, 1), :]) + sh_ref[pl.ds(r, 1), :]
    proj = jnp.dot(h.astype(BF16), win_ref[...], preferred_element_type=F32)
    q = proj[:, :ATTN_W]
    k = proj[:, ATTN_W:ATTN_W + KV_W]
    v = proj[:, ATTN_W + KV_W:ATTN_W + 2 * KV_W]
    f_ref[...] = proj[:, ATTN_W + 2 * KV_W:].astype(BF16)
    vn_ref[...] = v
    vt = v.T
    tail = (lax.broadcasted_iota(I32, (V_ROWS - HEAD_DIM, vt.shape[1]), 0) == 0).astype(BF16)
    for j in range(N_KV_HEADS):
        vt_ref[j, :HEAD_DIM, :] = vt[j * HEAD_DIM:(j + 1) * HEAD_DIM, :].astype(BF16)
        vt_ref[j, HEAD_DIM:, :] = tail
    q_ms = jnp.dot((q * q).astype(BF16), bd_ref[...], preferred_element_type=F32)
    k_ms = jnp.dot((k * k).astype(BF16), bd_ref[:KV_W, :KV_W], preferred_element_type=F32)
    qn = q * lax.rsqrt(q_ms + EPS) * qg_ref[...]
    kn = k * lax.rsqrt(k_ms + EPS) * kg_ref[...]
    kn_ref[...] = kn
    cos = cos_ref[...]
    sin = sin_ref[...]
    lane = lax.broadcasted_iota(I32, cos.shape, 1)
    first = (lane & ROT_PAIRS) == 0

    def rope(c):
        partner = jnp.where(first, pltpu.roll(c, LANES - ROT_PAIRS, 1), pltpu.roll(c, ROT_PAIRS, 1))
        return c * cos + partner * sin

    k_rot = rope(kn).astype(BF16)
    for j in range(N_KV_HEADS):
        k_ref[j] = k_rot[:, j * HEAD_DIM:(j + 1) * HEAD_DIM]
    scale = HEAD_DIM ** -0.5 * math.log2(math.e)
    q_rot = jnp.concatenate([rope(qn[:, j * LANES:(j + 1) * LANES]) for j in range(ATTN_W // LANES)], axis=1)
    q_ref[...] = (q_rot * scale).T.astype(BF16)


def _pre(xc, xl, mod, norm1_g, w_in, q_g, k_g, bd, cos_t, sin_t, *, nc, nl, tpb):
    n = (nc + nl) * TM
    row = lambda i: (i, 0)
    const = lambda i: (0, 0)
    tab = lambda i: (jnp.where(i < nc, 0, 1 + (i - nc) % tpb), 0)
    return pl.pallas_call(
        functools.partial(_pre_kernel, nc=nc, tpb=tpb),
        grid=(nc + nl,),
        in_specs=[pl.BlockSpec((TM, D_MODEL), lambda i: (jnp.minimum(i, nc - 1), 0)),
                  pl.BlockSpec((TM, D_MODEL), lambda i: (jnp.maximum(i - nc, 0), 0)),
                  pl.BlockSpec((8, D_MODEL), lambda i: (0, 0)),
                  pl.BlockSpec((8, D_MODEL), lambda i: (0, 1)),
                  pl.BlockSpec((1, D_MODEL), const),
                  pl.BlockSpec((D_MODEL, D_IN), const),
                  pl.BlockSpec((1, ATTN_W), const),
                  pl.BlockSpec((1, KV_W), const),
                  pl.BlockSpec((ATTN_W, ATTN_W), const),
                  pl.BlockSpec((TM, LANES), tab),
                  pl.BlockSpec((TM, LANES), tab)],
        out_specs=[pl.BlockSpec((ATTN_W, TM), lambda i: (0, i)),
                   pl.BlockSpec((N_KV_HEADS, TM, HEAD_DIM), lambda i: (0, i, 0)),
                   pl.BlockSpec((N_KV_HEADS, V_ROWS, TM), lambda i: (0, 0, i)),
                   pl.BlockSpec((TM, FOURIER_W), row), pl.BlockSpec((TM, KV_W), row), pl.BlockSpec((TM, KV_W), row)],
        out_shape=[jax.ShapeDtypeStruct((ATTN_W, n), BF16), jax.ShapeDtypeStruct((N_KV_HEADS, n, HEAD_DIM), BF16),
                   jax.ShapeDtypeStruct((N_KV_HEADS, V_ROWS, n), BF16), jax.ShapeDtypeStruct((n, FOURIER_W), BF16),
                   jax.ShapeDtypeStruct((n, KV_W), F32), jax.ShapeDtypeStruct((n, KV_W), F32)],
        compiler_params=_cparams(("parallel",)),
        name="pre_proj",
    )(xc, xl, mod, mod, norm1_g, w_in, q_g, k_g, bd, cos_t, sin_t)


def _attn_kernel(*refs, seg_lens):
    n_seg = len(seg_lens)
    qt_ref = refs[0]
    k_refs = refs[1:1 + n_seg]
    vt_refs = refs[1 + n_seg:1 + 2 * n_seg]
    o_ref, ot_ref, st_ref, pt_ref = refs[1 + 2 * n_seg:]
    offs = [sum(seg_lens[:m]) for m in range(n_seg)]

    def scores(h):
        qt = qt_ref[h * HEAD_DIM:(h + 1) * HEAD_DIM, :]
        for k_ref, off, ln in zip(k_refs, offs, seg_lens):
            st_ref[h % 2, off:off + ln] = jnp.dot(k_ref[h // HEADS_PER_KV], qt, preferred_element_type=F32)

    scores(0)
    for h in range(N_HEADS):
        if h + 1 < N_HEADS:
            scores(h + 1)
        st = st_ref[h % 2]
        pt_ref[h % 2] = jnp.exp2(st - jnp.max(st, axis=0, keepdims=True)).astype(BF16)
        ot = None
        for vt_ref, off, ln in zip(vt_refs, offs, seg_lens):
            part = jnp.dot(vt_ref[h // HEADS_PER_KV], pt_ref[h % 2, off:off + ln], preferred_element_type=F32)
            ot = part if ot is None else ot + part
        ot_ref[h * HEAD_DIM:(h + 1) * HEAD_DIM, :] = ot[:HEAD_DIM] * (1.0 / ot[HEAD_DIM:HEAD_DIM + 1])
    o_ref[...] = ot_ref[...].T


def _attention(qt, segments, *, batch, t, q_row0):
    tq = min(TQ, t)
    qpb = t // tq
    q0 = q_row0 // tq
    seg_lens = tuple(ln for _, _, ln, _ in segments)
    s_total = sum(seg_lens)
    k_specs = [pl.BlockSpec((N_KV_HEADS, ln, HEAD_DIM), functools.partial(lambda b, i, f: (0, f + b, 0), f=fb))
               for _, _, ln, fb in segments]
    v_specs = [pl.BlockSpec((N_KV_HEADS, V_ROWS, ln), functools.partial(lambda b, i, f: (0, 0, f + b), f=fb))
               for _, _, ln, fb in segments]
    return pl.pallas_call(
        functools.partial(_attn_kernel, seg_lens=seg_lens),
        grid=(batch, qpb),
        in_specs=[pl.BlockSpec((ATTN_W, tq), lambda b, i: (0, q0 + b * qpb + i))] + k_specs + v_specs,
        out_specs=pl.BlockSpec((tq, ATTN_W), lambda b, i: (b * qpb + i, 0)),
        out_shape=jax.ShapeDtypeStruct((batch * t, ATTN_W), F32),
        scratch_shapes=[pltpu.VMEM((ATTN_W, tq), F32), pltpu.VMEM((2, s_total, tq), F32),
                        pltpu.VMEM((2, s_total, tq), BF16)],
        compiler_params=_cparams(("parallel", "parallel")),
        name="attention",
    )(qt, *[k for k, _, _, _ in segments], *[v for _, v, _, _ in segments])


def _cache_heads(cache_k, cache_v):
    batch, past = cache_k.shape[:2]
    ck = cache_k.astype(BF16).transpose(2, 0, 1, 3).reshape(N_KV_HEADS, batch * past, HEAD_DIM)
    cv = cache_v.astype(BF16).transpose(2, 3, 0, 1).reshape(N_KV_HEADS, HEAD_DIM, batch * past)
    tail = jnp.concatenate([jnp.ones((N_KV_HEADS, 1, batch * past), BF16),
                            jnp.zeros((N_KV_HEADS, V_ROWS - HEAD_DIM - 1, batch * past), BF16)], axis=1)
    return ck, jnp.concatenate([cv, tail], axis=1)


def _fourier_kernel(x_ref, cd_ref, ct_ref, st_ref, twc_ref, tws_ref, wf_ref, o_ref, *, n1, n2):
    pw = 2 * FOURIER_GROUP_W
    scale = 1.0 / math.sqrt(n1 * n2 * FOURIER_GROUP_W)
    for gp in range(FOURIER_W // pw):
        z = jnp.dot(x_ref[:, gp * pw:(gp + 1) * pw], cd_ref[...], preferred_element_type=F32)
        zr = [z[t1 * n2:(t1 + 1) * n2, :pw] for t1 in range(n1)]
        zi = [z[t1 * n2:(t1 + 1) * n2, pw:] for t1 in range(n1)]
        if n1 == 1:
            a = [(zr[0], zi[0])]
        else:
            a = [(zr[0] + zr[1] + zr[2] + zr[3], zi[0] + zi[1] + zi[2] + zi[3]),
                 (zr[0] + zi[1] - zr[2] - zi[3], zi[0] - zr[1] - zi[2] + zr[3]),
                 (zr[0] - zr[1] + zr[2] - zr[3], zi[0] - zi[1] + zi[2] - zi[3]),
                 (zr[0] - zi[1] - zr[2] + zi[3], zi[0] + zr[1] - zi[2] - zr[3])]
        for u1 in range(n1):
            ar, ai = a[u1]
            if u1 > 0:
                c = jnp.concatenate([twc_ref[u1 - 1]] * 2, axis=1)
                s = jnp.concatenate([tws_ref[u1 - 1]] * 2, axis=1)
                ar, ai = ar * c + ai * s, ai * c - ar * s
            y = (jnp.dot(ct_ref[...], ar.astype(BF16), preferred_element_type=F32)
                 + jnp.dot(st_ref[...], ai.astype(BF16), preferred_element_type=F32)) * scale
            y = jnp.dot(y.astype(BF16), wf_ref[gp], preferred_element_type=F32)
            rows = pl.ds(u1, n2, stride=n1) if n1 > 1 else slice(None)
            for half in range(2):
                o_ref[2 * gp + half, rows, :] = y[:, half * FOURIER_GROUP_W:(half + 1) * FOURIER_GROUP_W]


def _fourier(f, wf_pairs, *, batch, t, row0, n1):
    n2 = t // n1
    cd, ct, st, twc, tws = _dft_tables(n1, n2)
    cd, ct, st = cd.astype(BF16), ct.astype(BF16), st.astype(BF16)
    nt = twc.shape[0]
    b0 = row0 // t
    c2 = lambda b: (0, 0)
    c3 = lambda b: (0, 0, 0)
    return pl.pallas_call(
        functools.partial(_fourier_kernel, n1=n1, n2=n2),
        grid=(batch,),
        in_specs=[pl.BlockSpec((t, FOURIER_W), lambda b: (b0 + b, 0)),
                  pl.BlockSpec(cd.shape, c2), pl.BlockSpec((n2, n2), c2), pl.BlockSpec((n2, n2), c2),
                  pl.BlockSpec((nt, n2, LANES), c3), pl.BlockSpec((nt, n2, LANES), c3),
                  pl.BlockSpec(wf_pairs.shape, c3)],
        out_specs=pl.BlockSpec((FOURIER_W // FOURIER_GROUP_W, t, FOURIER_GROUP_W), lambda b: (0, b, 0)),
        out_shape=jax.ShapeDtypeStruct((FOURIER_W // FOURIER_GROUP_W, batch * t, FOURIER_GROUP_W), F32),
        compiler_params=_cparams(("parallel",)),
        name="fourier",
    )(f, cd, ct, st, twc, tws, wf_pairs)


def _route(i, h, whi_ref, wlo_ref, br_ref, idx_ref, w_ref, cnt_ref):
    h_hi = h.astype(BF16)
    h_lo = (h - h_hi.astype(F32)).astype(BF16)
    logits = (jnp.dot(h_hi, whi_ref[...], preferred_element_type=F32)
              + jnp.dot(h_lo, whi_ref[...], preferred_element_type=F32)
              + jnp.dot(h_hi, wlo_ref[...], preferred_element_type=F32)) + br_ref[...]
    lane = lax.broadcasted_iota(I32, logits.shape, 1)
    l = logits
    vals, idxs = [], []
    hits = jnp.zeros(logits.shape, F32)
    for _ in range(TOP_K):
        m = jnp.max(l, axis=-1, keepdims=True)
        ix = jnp.min(jnp.where(l == m, lane, LANES), axis=-1, keepdims=True)
        sel = lane == ix
        hits = hits + sel.astype(F32)
        l = jnp.where(sel, -jnp.inf, l)
        vals.append(m)
        idxs.append(ix)
    es = [jnp.exp(v - vals[0]) for v in vals]
    den = es[0] + es[1] + es[2] + es[3]
    idx_ref[...] = _pack_cols(idxs, lane)
    w_ref[...] = _pack_cols([e / den for e in es], lane)

    @pl.when(i == 0)
    def _():
        cnt_ref[...] = jnp.zeros(cnt_ref.shape, F32)

    cnt_ref[...] += _round_up_runs(jnp.sum(hits, axis=0, keepdims=True))


def _mix_kernel(xc_ref, xl_ref, ac_ref, al_ref, fc_ref, fl_ref, g1_ref, sh2_ref, sc2_ref, ag_ref, fg_ref,
                wout_ref, n2g_ref, whi_ref, wlo_ref, br_ref, x1_ref, h2_ref, idx_ref, w_ref, cnt_ref, *, nc, tpb):
    i = pl.program_id(0)
    r = _mod_row(i, nc, tpb)
    is_ctx = i < nc
    x = jnp.where(is_ctx, xc_ref[...], xl_ref[...])
    a = _rms(jnp.where(is_ctx, ac_ref[...], al_ref[...])) * ag_ref[...]
    n_groups = FOURIER_W // FOURIER_GROUP_W
    f = jnp.concatenate([jnp.where(is_ctx, fc_ref[g], fl_ref[g]) for g in range(n_groups)], axis=1)
    f = _rms(f) * fg_ref[...]
    mixed = (jnp.dot(a.astype(BF16), wout_ref[:ATTN_W, :], preferred_element_type=F32)
             + jnp.dot(f.astype(BF16), wout_ref[ATTN_W:, :], preferred_element_type=F32))
    x1 = x + g1_ref[pl.ds(r, 1), :] * mixed
    x1_ref[...] = x1
    h2 = _rms(x1) * n2g_ref[...]
    h2 = h2 * (1.0 + sc2_ref[pl.ds(r, 1), :]) + sh2_ref[pl.ds(r, 1), :]
    h2_ref[...] = h2
    _route(i, h2, whi_ref, wlo_ref, br_ref, idx_ref, w_ref, cnt_ref)


def _mix(xc, xl, ac, al, fc, fl, mod, attn_g, four_g, w_out, norm2_g, wr_hi, wr_lo, br, *, nc, nl, tpb):
    n = (nc + nl) * TM
    lo = lambda i: (jnp.minimum(i, nc - 1), 0)
    hi = lambda i: (jnp.maximum(i - nc, 0), 0)
    const = lambda i: (0, 0)
    row = lambda i: (i, 0)
    return pl.pallas_call(
        functools.partial(_mix_kernel, nc=nc, tpb=tpb),
        grid=(nc + nl,),
        in_specs=[pl.BlockSpec((TM, D_MODEL), lo), pl.BlockSpec((TM, D_MODEL), hi),
                  pl.BlockSpec((TM, ATTN_W), lo), pl.BlockSpec((TM, ATTN_W), hi),
                  pl.BlockSpec((FOURIER_W // FOURIER_GROUP_W, TM, FOURIER_GROUP_W),
                               lambda i: (0, jnp.minimum(i, nc - 1), 0)),
                  pl.BlockSpec((FOURIER_W // FOURIER_GROUP_W, TM, FOURIER_GROUP_W),
                               lambda i: (0, jnp.maximum(i - nc, 0), 0)),
                  pl.BlockSpec((8, D_MODEL), lambda i: (0, 2)),
                  pl.BlockSpec((8, D_MODEL), lambda i: (0, 3)),
                  pl.BlockSpec((8, D_MODEL), lambda i: (0, 4)),
                  pl.BlockSpec((1, ATTN_W), const), pl.BlockSpec((1, FOURIER_W), const),
                  pl.BlockSpec((D_MODEL, D_MODEL), const), pl.BlockSpec((1, D_MODEL), const),
                  pl.BlockSpec((D_MODEL, LANES), const), pl.BlockSpec((D_MODEL, LANES), const),
                  pl.BlockSpec((1, LANES), const)],
        out_specs=[pl.BlockSpec((TM, D_MODEL), row), pl.BlockSpec((TM, D_MODEL), row),
                   pl.BlockSpec((TM, LANES), row), pl.BlockSpec((TM, LANES), row), pl.BlockSpec((8, LANES), const)],
        out_shape=[jax.ShapeDtypeStruct((n, D_MODEL), F32), jax.ShapeDtypeStruct((n, D_MODEL), F32),
                   jax.ShapeDtypeStruct((n, LANES), I32), jax.ShapeDtypeStruct((n, LANES), F32),
                   jax.ShapeDtypeStruct((8, LANES), F32)],
        compiler_params=_cparams(("arbitrary",)),
        name="mix_proj",
    )(xc, xl, ac, al, fc, fl, mod, mod, mod, attn_g, four_g, w_out, norm2_g, wr_hi, wr_lo, br)


RUN_ALIGN = 8
RUN_ROWS = TM * TOP_K + N_EXPERTS * RUN_ALIGN
RUN_SIZES = tuple(TM >> s for s in range(7))
RUN_PAD = RUN_ROWS - TM * TOP_K
FILL_SIZES = tuple(RUN_PAD >> s for s in range(6))


def _round_up_runs(x):
    return jnp.floor((x + (RUN_ALIGN - 1)) * (1.0 / RUN_ALIGN)) * RUN_ALIGN


def _pos_kernel(idx_ref, gs_ref, loct_ref, loc_ref, info_ref, carry_ref):
    i = pl.program_id(0)

    @pl.when(i == 0)
    def _():
        carry_ref[...] = jnp.zeros(carry_ref.shape, F32)

    idx = idx_ref[...]
    lane = lax.broadcasted_iota(I32, idx.shape, 1)
    sels = [lane == idx[:, k:k + 1] for k in range(TOP_K)]
    hits = jnp.zeros(idx.shape, F32)
    for s in sels:
        hits = hits + s.astype(F32)
    r = lax.broadcasted_iota(I32, (TM, TM), 0)
    c = lax.broadcasted_iota(I32, (TM, TM), 1)
    before = (c < r).astype(BF16)
    rank = jnp.dot(before, hits.astype(BF16), preferred_element_type=F32)
    run_len = _round_up_runs(jnp.sum(hits, axis=0, keepdims=True))
    a = lax.broadcasted_iota(I32, (LANES, LANES), 0)
    b = lax.broadcasted_iota(I32, (LANES, LANES), 1)
    units = jnp.broadcast_to(run_len * (1.0 / RUN_ALIGN), (8, LANES)).astype(BF16)
    run_off = jnp.dot(units, (a < b).astype(BF16), preferred_element_type=F32)[0:1, :] * RUN_ALIGN
    run_start = carry_ref[0:1, :] + gs_ref[0:1, :]
    loc_cols = [jnp.sum(jnp.where(s, rank + run_off, 0.0), axis=-1, keepdims=True) for s in sels]
    loc = _pack_cols(loc_cols, lane)
    loc_ref[...] = loc.astype(I32)
    loct_ref[...] = loc.T[:8, :].astype(I32)
    row = lax.broadcasted_iota(I32, (8, LANES), 0)
    info = jnp.where(row == 0, run_start, jnp.where(row == 1, run_len, jnp.where(row == 2, run_off, 0.0)))
    info_ref[...] = info.astype(I32)
    carry_ref[...] += run_len


def _positions(idx, gs_rows):
    n = idx.shape[0]
    nt = n // TM
    return pl.pallas_call(
        _pos_kernel,
        grid=(nt,),
        in_specs=[pl.BlockSpec((TM, LANES), lambda i: (i, 0)), pl.BlockSpec((8, LANES), lambda i: (0, 0))],
        out_specs=[pl.BlockSpec((8, TM), lambda i: (0, i)), pl.BlockSpec((TM, LANES), lambda i: (i, 0)),
                   pl.BlockSpec((None, 8, LANES), lambda i: (i, 0, 0))],
        out_shape=[jax.ShapeDtypeStruct((8, n), I32), jax.ShapeDtypeStruct((n, LANES), I32),
                   jax.ShapeDtypeStruct((nt, 8, LANES), I32)],
        scratch_shapes=[pltpu.VMEM((8, LANES), F32)],
        compiler_params=_cparams(("arbitrary",)),
        name="positions",
    )(idx, gs_rows)


def _run_copies(info_ref, tile, make_copy, act):
    base = tile * (3 * N_EXPERTS)

    def per_expert(e, carry):
        start = info_ref[base + e]
        length = info_ref[base + N_EXPERTS + e]
        off = info_ref[base + 2 * N_EXPERTS + e]
        for size in RUN_SIZES:
            @pl.when((length & size) != 0)
            def _():
                act(make_copy(pl.multiple_of(off, RUN_ALIGN), pl.multiple_of(start, RUN_ALIGN), size))
            taken = jnp.where((length & size) != 0, size, 0)
            off = off + taken
            start = start + taken
        return carry

    lax.fori_loop(0, N_EXPERTS, per_expert, 0)


PAD_SIZES = tuple(s for s in (256, 128, 64, 32, 16, 8) if s < TG)


def _dispatch_kernel(info_ref, ends_ref, cnt_ref, loc_ref, h2_ref, xs_ref, dump_ref, runs_ref, zero_ref, sem, zsem):
    i = pl.program_id(0)
    last_step = pl.num_programs(0) - 1
    slot = i % 2

    def zero_copies(act):
        def per_expert(e, carry):
            lo = jnp.where(e == 0, 0, ends_ref[jnp.maximum(e - 1, 0)])
            first = lo + cnt_ref[e]
            rest = ends_ref[e] - first
            done = rest * 0
            for size in PAD_SIZES:
                @pl.when((rest & size) != 0)
                def _():
                    act(pltpu.make_async_copy(zero_ref.at[pl.ds(0, size), :],
                                              xs_ref.at[pl.ds(pl.multiple_of(first + done, RUN_ALIGN), size), :], zsem))
                done = done + jnp.where((rest & size) != 0, size, 0)
            return carry

        lax.fori_loop(0, N_EXPERTS, per_expert, 0)

        def tail_tile(t, carry):
            act(pltpu.make_async_copy(zero_ref, xs_ref.at[pl.ds(pl.multiple_of(t * TG, TG), TG), :], zsem))
            return carry

        lax.fori_loop(ends_ref[N_EXPERTS - 1] // TG, xs_ref.shape[0] // TG, tail_tile, 0)

    def slot_done(s):
        return pltpu.make_async_copy(runs_ref.at[s], xs_ref.at[pl.ds(0, RUN_ROWS), :], sem.at[s])

    @pl.when(i == 0)
    def _():
        zero_ref[...] = jnp.zeros(zero_ref.shape, F32)
        for s in range(2):
            pltpu.make_async_copy(zero_ref.at[pl.ds(0, RUN_PAD), :], dump_ref.at[s], sem.at[s]).start()
        for s in range(2):
            pltpu.make_async_copy(zero_ref.at[pl.ds(0, RUN_PAD), :], dump_ref.at[s], sem.at[s]).wait()
        zero_copies(lambda cp: cp.start())

    @pl.when(i >= 2)
    def _():
        slot_done(slot).wait()

    loc = loc_ref[...]
    row = lax.broadcasted_iota(I32, (RUN_ROWS, TM), 0)
    sel = row == loc[0:1, :]
    for k in range(1, TOP_K):
        sel = jnp.logical_or(sel, row == loc[k:k + 1, :])
    runs_ref[slot] = jnp.dot(sel.astype(BF16), h2_ref[...].astype(BF16), preferred_element_type=F32)

    def piece(stacked_row, global_row, size):
        return pltpu.make_async_copy(runs_ref.at[slot, pl.ds(stacked_row, size), :],
                                     xs_ref.at[pl.ds(global_row, size), :], sem.at[slot])

    _run_copies(info_ref, i, piece, lambda cp: cp.start())
    last = i * (3 * N_EXPERTS) + N_EXPERTS - 1
    used = info_ref[last + 2 * N_EXPERTS] + info_ref[last + N_EXPERTS]
    rest = RUN_ROWS - used
    done = used * 0
    for size in FILL_SIZES:
        @pl.when((rest & size) != 0)
        def _():
            at = pl.multiple_of(used + done, RUN_ALIGN)
            to = pl.multiple_of(at - TM * TOP_K, RUN_ALIGN)
            pltpu.make_async_copy(runs_ref.at[slot, pl.ds(at, size), :], dump_ref.at[slot, pl.ds(to, size), :],
                                  sem.at[slot]).start()
        done = done + jnp.where((rest & size) != 0, size, 0)

    @pl.when(i == last_step)
    def _():
        slot_done(slot).wait()

        @pl.when(i >= 1)
        def _():
            slot_done(1 - slot).wait()

        zero_copies(lambda cp: cp.wait())


def _dispatch(info, ends, cnt, loc, h2, n_rows):
    n = h2.shape[0]
    grid_spec = pltpu.PrefetchScalarGridSpec(
        num_scalar_prefetch=3,
        grid=(n // TM,),
        in_specs=[pl.BlockSpec((8, TM), lambda i, *_: (0, i)),
                  pl.BlockSpec((TM, D_MODEL), lambda i, *_: (i, 0))],
        out_specs=[pl.BlockSpec(memory_space=pl.ANY), pl.BlockSpec(memory_space=pl.ANY)],
        scratch_shapes=[pltpu.VMEM((2, RUN_ROWS, D_MODEL), F32), pltpu.VMEM((TG, D_MODEL), F32),
                        pltpu.SemaphoreType.DMA((2,)), pltpu.SemaphoreType.DMA(())],
    )
    xs, _ = pl.pallas_call(
        _dispatch_kernel,
        grid_spec=grid_spec,
        out_shape=[jax.ShapeDtypeStruct((n_rows, D_MODEL), F32), jax.ShapeDtypeStruct((2, RUN_PAD, D_MODEL), F32)],
        compiler_params=_cparams(("arbitrary",)),
        name="dispatch",
    )(info, ends, cnt, loc, h2)
    return xs


def _ffn_kernel(te_ref, nv_ref, par_ref, nxt_ref, x_ref, bg_ref, bl_ref, bdn_ref, perm_ref, wgu_hbm, wdn_hbm, o_ref,
                wgu_buf, wdn_buf, wg_s, wl_s, wd_s, wsem):
    i = pl.program_id(0)
    e = te_ref[i]
    valid = i < nv_ref[0]
    fresh = jnp.logical_or(i == 0, e != te_ref[jnp.maximum(i - 1, 0)])
    wslot = par_ref[i]

    def weight_copies(expert, s):
        return (pltpu.make_async_copy(wgu_hbm.at[expert], wgu_buf.at[s], wsem.at[s]),
                pltpu.make_async_copy(wdn_hbm.at[expert], wdn_buf.at[s], wsem.at[s]))

    @pl.when(i == 0)
    def _():
        for cp in weight_copies(e, wslot):
            cp.start()

    @pl.when(jnp.logical_and(valid, fresh))
    def _():
        for cp in weight_copies(e, wslot):
            cp.wait()
        for j in range(2 * D_EXPERT // (2 * LANES)):
            blk = wgu_buf[wslot, :, j * 2 * LANES:(j + 1) * 2 * LANES].astype(BF16)
            d = jnp.dot(blk, perm_ref[...], preferred_element_type=F32)
            wg_s[:, j * LANES:(j + 1) * LANES] = d[:, :LANES].astype(BF16)
            wl_s[:, j * LANES:(j + 1) * LANES] = d[:, LANES:].astype(BF16)
        wd_s[...] = wdn_buf[wslot].astype(BF16)

        @pl.when(nxt_ref[i] >= 0)
        def _():
            for cp in weight_copies(nxt_ref[i], 1 - wslot):
                cp.start()

    @pl.when(valid)
    def _():
        x = x_ref[...].astype(BF16)
        g = jnp.dot(x, wg_s[...], preferred_element_type=F32) + bg_ref[...]
        l = jnp.dot(x, wl_s[...], preferred_element_type=F32) + bl_ref[...]
        g = jnp.minimum(g, SWIGLU_LIMIT)
        l = jnp.clip(l, -SWIGLU_LIMIT, SWIGLU_LIMIT)
        a = (l + 1.0) * (g * (1.0 / (1.0 + jnp.exp(-SWIGLU_ALPHA * g))))
        o_ref[...] = jnp.dot(a.astype(BF16), wd_s[...], preferred_element_type=F32) + bdn_ref[...]

    @pl.when(jnp.logical_not(valid))
    def _():
        o_ref[...] = jnp.zeros(o_ref.shape, F32)


def _expert_ffn(tile_expert, n_valid, parity, next_expert, xs, w_gu, b_g, b_l, w_dn, b_dn, perm):
    p = xs.shape[0]
    nt = p // TG
    tile = lambda i, te, nv, *_: (jnp.minimum(i, nv[0] - 1), 0)
    ex3 = lambda i, te, *_: (te[i], 0, 0)
    grid_spec = pltpu.PrefetchScalarGridSpec(
        num_scalar_prefetch=4,
        grid=(nt,),
        in_specs=[pl.BlockSpec((TG, D_MODEL), tile),
                  pl.BlockSpec((None, 1, D_EXPERT), ex3),
                  pl.BlockSpec((None, 1, D_EXPERT), ex3),
                  pl.BlockSpec((None, 1, D_MODEL), ex3),
                  pl.BlockSpec((2 * LANES, 2 * LANES), lambda i, *_: (0, 0)),
                  pl.BlockSpec(memory_space=pl.ANY),
                  pl.BlockSpec(memory_space=pl.ANY)],
        out_specs=pl.BlockSpec((TG, D_MODEL), lambda i, *_: (i, 0)),
        scratch_shapes=[pltpu.VMEM((2, D_MODEL, 2 * D_EXPERT), F32), pltpu.VMEM((2, D_EXPERT, D_MODEL), F32),
                        pltpu.VMEM((D_MODEL, D_EXPERT), BF16), pltpu.VMEM((D_MODEL, D_EXPERT), BF16),
                        pltpu.VMEM((D_EXPERT, D_MODEL), BF16), pltpu.SemaphoreType.DMA((2,))],
    )
    return pl.pallas_call(
        _ffn_kernel,
        grid_spec=grid_spec,
        out_shape=jax.ShapeDtypeStruct((p, D_MODEL), F32),
        compiler_params=_cparams(("arbitrary",)),
        name="expert_ffn",
    )(tile_expert, n_valid, parity, next_expert, xs, b_g, b_l, b_dn, perm, w_gu, w_dn)


def _combine_kernel(info_ref, loc_ref, w_ref, x1_ref, g2_ref, fg_ref, ys_ref, o_ref, runs_ref, sem, *, tile0, nc, tpb):
    j = pl.program_id(0)
    i = j + tile0
    r = _mod_row(i, nc, tpb)
    slot = j % 2

    def gather(tile, s, act):
        def piece(stacked_row, global_row, size):
            return pltpu.make_async_copy(ys_ref.at[pl.ds(global_row, size), :],
                                         runs_ref.at[s, pl.ds(stacked_row, size), :], sem.at[s])

        _run_copies(info_ref, tile, piece, act)
        last = tile * (3 * N_EXPERTS) + N_EXPERTS - 1
        used = info_ref[last + 2 * N_EXPERTS] + info_ref[last + N_EXPERTS]
        rest = RUN_ROWS - used
        done = used * 0
        for size in FILL_SIZES:
            @pl.when((rest & size) != 0)
            def _():
                act(piece(pl.multiple_of(used + done, RUN_ALIGN), pl.multiple_of(done, RUN_ALIGN), size))
            done = done + jnp.where((rest & size) != 0, size, 0)

    @pl.when(j == 0)
    def _():
        gather(i, 0, lambda cp: cp.start())

    @pl.when(j + 1 < pl.num_programs(0))
    def _():
        gather(i + 1, 1 - slot, lambda cp: cp.start())

    pltpu.make_async_copy(ys_ref.at[pl.ds(0, RUN_ROWS), :], runs_ref.at[slot], sem.at[slot]).wait()
    loc = loc_ref[...]
    w = w_ref[...]
    col = lax.broadcasted_iota(I32, (TM, RUN_ROWS), 1)
    mix = jnp.where(col == loc[:, 0:1], w[:, 0:1], 0.0)
    for k in range(1, TOP_K):
        mix = mix + jnp.where(col == loc[:, k:k + 1], w[:, k:k + 1], 0.0)
    moe = jnp.dot(mix.astype(BF16), runs_ref[slot].astype(BF16), preferred_element_type=F32)
    y = x1_ref[...] + g2_ref[pl.ds(r, 1), :] * moe
    o_ref[...] = _rms(y) * fg_ref[...]


def _combine(info, loc, w, x1, mod, final_g, ys, *, tile0, ntiles, nc, tpb):
    grid_spec = pltpu.PrefetchScalarGridSpec(
        num_scalar_prefetch=1,
        grid=(ntiles,),
        in_specs=[pl.BlockSpec((TM, LANES), lambda i, *_: (i + tile0, 0)),
                  pl.BlockSpec((TM, LANES), lambda i, *_: (i + tile0, 0)),
                  pl.BlockSpec((TM, D_MODEL), lambda i, *_: (i + tile0, 0)),
                  pl.BlockSpec((8, D_MODEL), lambda i, *_: (0, 5)),
                  pl.BlockSpec((1, D_MODEL), lambda i, *_: (0, 0)),
                  pl.BlockSpec(memory_space=pl.ANY)],
        out_specs=pl.BlockSpec((TM, D_MODEL), lambda i, *_: (i, 0)),
        scratch_shapes=[pltpu.VMEM((2, RUN_ROWS, D_MODEL), F32), pltpu.SemaphoreType.DMA((2,))],
    )
    return pl.pallas_call(
        functools.partial(_combine_kernel, tile0=tile0, nc=nc, tpb=tpb),
        grid_spec=grid_spec,
        out_shape=jax.ShapeDtypeStruct((ntiles * TM, D_MODEL), F32),
        compiler_params=_cparams(("arbitrary",)),
        name="combine",
    )(info, loc, w, x1, mod, final_g, ys)


def kernel(x_prompt, x_sample, cache_k, cache_v, c, c_ctx, norm1_g, w_mod, b_mod, w_in, q_norm_g, k_norm_g,
           w_fourier, attn_out_g, fourier_out_g, w_out, norm2_g, w_router, b_router, w_gate_up, b_gate_up,
           w_down, b_down, final_g):
    bc, tc, _ = x_prompt.shape
    bl, tl, _ = x_sample.shape
    depth = w_in.shape[0]
    past = cache_k.shape[2]
    assert depth == 1 and tl % TQ == 0 and tl % TM == 0 and (bc * tc) % TM == 0 and (bc * tc) % tl == 0
    n_ctx, n_lat = bc * tc, bl * tl
    n_tok = n_ctx + n_lat
    nc, nl, tpb = n_ctx // TM, n_lat // TM, tl // TM

    xc = x_prompt.reshape(n_ctx, D_MODEL)
    xl = x_sample.reshape(n_lat, D_MODEL)

    c_rows = jnp.concatenate([c_ctx[None, :], c, jnp.zeros((8 - 1 - bl, D_MODEL), F32)], axis=0)
    mod = _modulation(c_rows, w_mod[0], b_mod[0])

    cos_t, sin_t = _rope_tables(tl)
    q, k_heads, vt_ext, f, k_new, v_new = _pre(
        xc, xl, mod, norm1_g, w_in[0].astype(BF16), jnp.tile(q_norm_g, (1, N_HEADS)),
        jnp.tile(k_norm_g, (1, N_KV_HEADS)), _head_mean_matrix(), cos_t, sin_t, nc=nc, nl=nl, tpb=tpb)

    attn_c = _attention(q, [(k_heads, vt_ext, tc, 0)], batch=bc, t=tc, q_row0=0)
    ck, cvt = _cache_heads(cache_k[:, 0], cache_v[:, 0])
    attn_l = _attention(q, [(ck, cvt, past, 0), (k_heads, vt_ext, tl, n_ctx // tl)], batch=bl, t=tl, q_row0=n_ctx)

    wf = w_fourier[0].astype(BF16)
    zero = jnp.zeros_like(wf[0])
    wf_pairs = jnp.stack([jnp.block([[wf[0], zero], [zero, wf[1]]]), jnp.block([[wf[2], zero], [zero, wf[3]]])])
    four_c = _fourier(f, wf_pairs, batch=bc, t=tc, row0=0, n1=1)
    four_l = _fourier(f, wf_pairs, batch=bl, t=tl, row0=n_ctx, n1=FFT_N1)

    wr = jnp.pad(w_router[0], ((0, 0), (0, LANES - N_EXPERTS)))
    wr_hi = wr.astype(BF16)
    wr_lo = (wr - wr_hi.astype(F32)).astype(BF16)
    br = jnp.pad(b_router[0], (0, LANES - N_EXPERTS), constant_values=-1e30).reshape(1, LANES)
    x1, h2, idx, gate_w, counts = _mix(xc, xl, attn_c, attn_l, four_c, four_l, mod, attn_out_g, fourier_out_g,
                                       w_out[0].astype(BF16), norm2_g, wr_hi, wr_lo, br, nc=nc, nl=nl, tpb=tpb)
    cnt = counts[0, :N_EXPERTS].astype(I32)
    padded = ((cnt + TG - 1) // TG) * TG
    ends = jnp.cumsum(padded)
    starts = ends - padded
    n_rows = n_tok * TOP_K + (n_tok // TM) * N_EXPERTS * RUN_ALIGN + N_EXPERTS * TG
    n_tiles = n_rows // TG
    tile_ids = jnp.arange(n_tiles, dtype=I32)
    tile_expert = jnp.minimum(jnp.sum((ends[None, :] // TG <= tile_ids[:, None]).astype(I32), axis=1),
                              N_EXPERTS - 1)
    n_valid = (ends[N_EXPERTS - 1] // TG).reshape(1)
    gs_rows = jnp.broadcast_to(jnp.pad(starts.astype(F32), (0, LANES - N_EXPERTS))[None, :], (8, LANES))
    loc_t, loc, info = _positions(idx, gs_rows)
    info = info[:, :3, :N_EXPERTS].reshape(-1)

    xs = _dispatch(info, ends, cnt, loc_t, h2, n_rows)
    fresh_tile = jnp.concatenate([jnp.ones((1,), I32), (tile_expert[1:] != tile_expert[:-1]).astype(I32)])
    parity = (jnp.cumsum(fresh_tile) - 1) % 2
    experts = jnp.arange(N_EXPERTS, dtype=I32)
    later = jnp.logical_and(padded[None, :] > 0, experts[None, :] > experts[:, None])
    next_of = jnp.min(jnp.where(later, experts[None, :], N_EXPERTS), axis=1)
    next_of = jnp.where(next_of == N_EXPERTS, -1, next_of)
    next_expert = jnp.sum(jnp.where(tile_expert[:, None] == experts[None, :], next_of[None, :], 0), axis=1)
    b_gu = b_gate_up[0].reshape(N_EXPERTS, 1, D_EXPERT, 2)
    ys = _expert_ffn(tile_expert, n_valid, parity.astype(I32), next_expert.astype(I32), xs, w_gate_up[0],
                     b_gu[..., 0], b_gu[..., 1], w_down[0], b_down[0].reshape(N_EXPERTS, 1, D_MODEL),
                     _deinterleave_matrix())

    y_c = _combine(info, loc, gate_w, x1, mod, final_g.reshape(1, -1), ys, tile0=0, ntiles=nc, nc=nc, tpb=tpb)
    y_l = _combine(info, loc, gate_w, x1, mod, final_g.reshape(1, -1), ys, tile0=nc, ntiles=nl, nc=nc, tpb=tpb)

    new_k = k_new[:n_ctx].reshape(bc, 1, tc, N_KV_HEADS, HEAD_DIM)
    new_v = v_new[:n_ctx].reshape(bc, 1, tc, N_KV_HEADS, HEAD_DIM)
    return (y_c.reshape(bc, tc, D_MODEL), y_l.reshape(bl, tl, D_MODEL), new_k, new_v)
```

```python
import functools
import math

import numpy as np
import jax
import jax.numpy as jnp
from jax import lax
from jax.experimental import pallas as pl
from jax.experimental.pallas import tpu as pltpu

F32 = jnp.float32
BF16 = jnp.bfloat16
I32 = jnp.int32

D_MODEL = 1024
HEAD_DIM = 64
N_HEADS = 8
N_KV_HEADS = 2
HEADS_PER_KV = N_HEADS // N_KV_HEADS
ATTN_W = N_HEADS * HEAD_DIM
KV_W = N_KV_HEADS * HEAD_DIM
FOURIER_W = 512
FOURIER_GROUP_W = 128
D_IN = ATTN_W + 2 * KV_W + FOURIER_W
N_EXPERTS = 32
TOP_K = 4
D_EXPERT = 1024
SWIGLU_ALPHA = 1.702
SWIGLU_LIMIT = 7.0
ROPE_THETA = 10000.0
ROT_PAIRS = HEAD_DIM // 4
GRID_W = 64
EPS = 1e-6

LANES = 128
TM = 512
TQ = 512
TG = 512
FFT_N1 = 4
V_ROWS = 80
VMEM_LIMIT = 56 * 1024 * 1024


def _cparams(sem, vmem=VMEM_LIMIT):
    return pltpu.CompilerParams(dimension_semantics=sem, vmem_limit_bytes=vmem)


def _rope_tables(t_lat):
    pos = np.arange(t_lat)
    row = (pos // GRID_W).astype(np.float64)
    col = (pos % GRID_W).astype(np.float64)
    inv = ROPE_THETA ** (-np.arange(ROT_PAIRS, dtype=np.float64) / ROT_PAIRS)
    lane = np.arange(LANES)
    d = lane % HEAD_DIM
    axis = d // (2 * ROT_PAIRS)
    second = (d // ROT_PAIRS) % 2
    p = d % ROT_PAIRS
    ang = np.where(axis[None, :] == 0, row[:, None], col[:, None]) * inv[p][None, :]
    cos = np.cos(ang)
    sin = np.sin(ang) * np.where(second == 0, -1.0, 1.0)[None, :]
    cos = np.concatenate([np.ones((TM, LANES)), cos], axis=0)
    sin = np.concatenate([np.zeros((TM, LANES)), sin], axis=0)
    return jnp.asarray(cos, F32), jnp.asarray(sin, F32)


def _dft_tables(n1, n2):
    c = np.arange(FOURIER_GROUP_W)
    ang = 2.0 * np.pi * np.outer(c, c) / FOURIER_GROUP_W
    z = np.zeros_like(ang)
    cbd = np.block([[np.cos(ang), z], [z, np.cos(ang)]])
    sbd = np.block([[np.sin(ang), z], [z, np.sin(ang)]])
    cd = np.concatenate([cbd, -sbd], axis=1)
    u = np.arange(n2)
    ang2 = 2.0 * np.pi * np.outer(u, u) / n2
    ct, st = np.cos(ang2), np.sin(ang2)
    nt = max(n1 - 1, 1)
    tw_ang = 2.0 * np.pi * np.outer(np.arange(1, nt + 1), u) / (n1 * n2)
    twc = np.repeat(np.cos(tw_ang)[:, :, None], LANES, axis=2)
    tws = np.repeat(np.sin(tw_ang)[:, :, None], LANES, axis=2)
    return (jnp.asarray(cd, F32), jnp.asarray(ct, F32), jnp.asarray(st, F32),
            jnp.asarray(twc, F32), jnp.asarray(tws, F32))


def _head_mean_matrix():
    h = np.arange(ATTN_W) // HEAD_DIM
    return jnp.asarray((h[:, None] == h[None, :]) / HEAD_DIM, BF16)


def _deinterleave_matrix():
    p = np.zeros((2 * LANES, 2 * LANES))
    m = np.arange(LANES)
    p[2 * m, m] = 1.0
    p[2 * m + 1, LANES + m] = 1.0
    return jnp.asarray(p, BF16)


def _mod_row(i, nc, tpb):
    return jnp.where(i < nc, 0, 1 + (i - nc) // tpb)


def _rms(x):
    return x * lax.rsqrt(jnp.mean(x * x, axis=-1, keepdims=True) + EPS)


def _pack_cols(cols, lane):
    out = jnp.zeros(lane.shape, cols[0].dtype)
    for k, c in enumerate(cols):
        out = jnp.where(lane == k, c, out)
    return out


def _mod_kernel(c_ref, w_ref, b_ref, o_ref):
    c = c_ref[...]
    s = c * (1.0 / (1.0 + jnp.exp(-c)))
    o_ref[...] = jnp.dot(s.astype(BF16), w_ref[...].astype(BF16), preferred_element_type=F32) + b_ref[...]


def _modulation(c_rows, w_mod, b_mod):
    n = w_mod.shape[1] // D_MODEL
    return pl.pallas_call(
        _mod_kernel,
        grid=(n,),
        in_specs=[pl.BlockSpec((8, D_MODEL), lambda j: (0, 0)),
                  pl.BlockSpec((D_MODEL, D_MODEL), lambda j: (0, j)),
                  pl.BlockSpec((1, D_MODEL), lambda j: (0, j))],
        out_specs=pl.BlockSpec((8, D_MODEL), lambda j: (0, j)),
        out_shape=jax.ShapeDtypeStruct((8, w_mod.shape[1]), F32),
        compiler_params=_cparams(("parallel",)),
        name="modulation",
    )(c_rows, w_mod, b_mod.reshape(1, -1))


def _pre_kernel(xc_ref, xl_ref, sh_ref, sc_ref, g1_ref, win_ref, qg_ref, kg_ref, bd_ref, cos_ref, sin_ref,
                q_ref, k_ref, vt_ref, f_ref, kn_ref, vn_ref, *, nc, tpb):
    i = pl.program_id(0)
    r = _mod_row(i, nc, tpb)
    x = jnp.where(i < nc, xc_ref[...], xl_ref[...])
    h = _rms(x) * g1_ref[...]
    h = h * (1.0 + sc_ref[pl.ds(r, 1), :]) + sh_ref[pl.ds(r, 1), :]
    proj = jnp.dot(h.astype(BF16), win_ref[...], preferred_element_type=F32)
    q = proj[:, :ATTN_W]
    k = proj[:, ATTN_W:ATTN_W + KV_W]
    v = proj[:, ATTN_W + KV_W:ATTN_W + 2 * KV_W]
    f_ref[...] = proj[:, ATTN_W + 2 * KV_W:].astype(BF16)
    vn_ref[...] = v
    vt = v.T
    tail = (lax.broadcasted_iota(I32, (V_ROWS - HEAD_DIM, vt.shape[1]), 0) == 0).astype(BF16)
    for j in range(N_KV_HEADS):
        vt_ref[j, :HEAD_DIM, :] = vt[j * HEAD_DIM:(j + 1) * HEAD_DIM, :].astype(BF16)
        vt_ref[j, HEAD_DIM:, :] = tail
    q_ms = jnp.dot((q * q).astype(BF16), bd_ref[...], preferred_element_type=F32)
    k_ms = jnp.dot((k * k).astype(BF16), bd_ref[:KV_W, :KV_W], preferred_element_type=F32)
    qn = q * lax.rsqrt(q_ms + EPS) * qg_ref[...]
    kn = k * lax.rsqrt(k_ms + EPS) * kg_ref[...]
    kn_ref[...] = kn
    cos = cos_ref[...]
    sin = sin_ref[...]
    lane = lax.broadcasted_iota(I32, cos.shape, 1)
    first = (lane & ROT_PAIRS) == 0

    def rope(c):
        partner = jnp.where(first, pltpu.roll(c, LANES - ROT_PAIRS, 1), pltpu.roll(c, ROT_PAIRS, 1))
        return c * cos + partner * sin

    k_rot = rope(kn).astype(BF16)
    for j in range(N_KV_HEADS):
        k_ref[j] = k_rot[:, j * HEAD_DIM:(j + 1) * HEAD_DIM]
    scale = HEAD_DIM ** -0.5 * math.log2(math.e)
    q_rot = jnp.concatenate([rope(qn[:, j * LANES:(j + 1) * LANES]) for j in range(ATTN_W // LANES)], axis=1)
    q_ref[...] = (q_rot * scale).T.astype(BF16)


def _pre(xc, xl, mod, norm1_g, w_in, q_g, k_g, bd, cos_t, sin_t, *, nc, nl, tpb):
    n = (nc + nl) * TM
    row = lambda i: (i, 0)
    const = lambda i: (0, 0)
    tab = lambda i: (jnp.where(i < nc, 0, 1 + (i - nc) % tpb), 0)
    return pl.pallas_call(
        functools.partial(_pre_kernel, nc=nc, tpb=tpb),
        grid=(nc + nl,),
        in_specs=[pl.BlockSpec((TM, D_MODEL), lambda i: (jnp.minimum(i, nc - 1), 0)),
                  pl.BlockSpec((TM, D_MODEL), lambda i: (jnp.maximum(i - nc, 0), 0)),
                  pl.BlockSpec((8, D_MODEL), lambda i: (0, 0)),
                  pl.BlockSpec((8, D_MODEL), lambda i: (0, 1)),
                  pl.BlockSpec((1, D_MODEL), const),
                  pl.BlockSpec((D_MODEL, D_IN), const),
                  pl.BlockSpec((1, ATTN_W), const),
                  pl.BlockSpec((1, KV_W), const),
                  pl.BlockSpec((ATTN_W, ATTN_W), const),
                  pl.BlockSpec((TM, LANES), tab),
                  pl.BlockSpec((TM, LANES), tab)],
        out_specs=[pl.BlockSpec((ATTN_W, TM), lambda i: (0, i)),
                   pl.BlockSpec((N_KV_HEADS, TM, HEAD_DIM), lambda i: (0, i, 0)),
                   pl.BlockSpec((N_KV_HEADS, V_ROWS, TM), lambda i: (0, 0, i)),
                   pl.BlockSpec((TM, FOURIER_W), row), pl.BlockSpec((TM, KV_W), row), pl.BlockSpec((TM, KV_W), row)],
        out_shape=[jax.ShapeDtypeStruct((ATTN_W, n), BF16), jax.ShapeDtypeStruct((N_KV_HEADS, n, HEAD_DIM), BF16),
                   jax.ShapeDtypeStruct((N_KV_HEADS, V_ROWS, n), BF16), jax.ShapeDtypeStruct((n, FOURIER_W), BF16),
                   jax.ShapeDtypeStruct((n, KV_W), F32), jax.ShapeDtypeStruct((n, KV_W), F32)],
        compiler_params=_cparams(("parallel",)),
        name="pre_proj",
    )(xc, xl, mod, mod, norm1_g, w_in, q_g, k_g, bd, cos_t, sin_t)


def _attn_kernel(*refs, seg_lens):
    n_seg = len(seg_lens)
    qt_ref = refs[0]
    k_refs = refs[1:1 + n_seg]
    vt_refs = refs[1 + n_seg:1 + 2 * n_seg]
    o_ref, ot_ref, st_ref, pt_ref = refs[1 + 2 * n_seg:]
    offs = [sum(seg_lens[:m]) for m in range(n_seg)]

    def scores(h):
        qt = qt_ref[h * HEAD_DIM:(h + 1) * HEAD_DIM, :]
        for k_ref, off, ln in zip(k_refs, offs, seg_lens):
            st_ref[h % 2, off:off + ln] = jnp.dot(k_ref[h // HEADS_PER_KV], qt, preferred_element_type=F32)

    scores(0)
    for h in range(N_HEADS):
        if h + 1 < N_HEADS:
            scores(h + 1)
        st = st_ref[h % 2]
        pt_ref[h % 2] = jnp.exp2(st - jnp.max(st, axis=0, keepdims=True)).astype(BF16)
        ot = None
        for vt_ref, off, ln in zip(vt_refs, offs, seg_lens):
            part = jnp.dot(vt_ref[h // HEADS_PER_KV], pt_ref[h % 2, off:off + ln], preferred_element_type=F32)
            ot = part if ot is None else ot + part
        ot_ref[h * HEAD_DIM:(h + 1) * HEAD_DIM, :] = ot[:HEAD_DIM] * (1.0 / ot[HEAD_DIM:HEAD_DIM + 1])
    o_ref[...] = ot_ref[...].T


def _attention(qt, segments, *, batch, t, q_row0):
    tq = min(TQ, t)
    qpb = t // tq
    q0 = q_row0 // tq
    seg_lens = tuple(ln for _, _, ln, _ in segments)
    s_total = sum(seg_lens)
    k_specs = [pl.BlockSpec((N_KV_HEADS, ln, HEAD_DIM), functools.partial(lambda b, i, f: (0, f + b, 0), f=fb))
               for _, _, ln, fb in segments]
    v_specs = [pl.BlockSpec((N_KV_HEADS, V_ROWS, ln), functools.partial(lambda b, i, f: (0, 0, f + b), f=fb))
               for _, _, ln, fb in segments]
    return pl.pallas_call(
        functools.partial(_attn_kernel, seg_lens=seg_lens),
        grid=(batch, qpb),
        in_specs=[pl.BlockSpec((ATTN_W, tq), lambda b, i: (0, q0 + b * qpb + i))] + k_specs + v_specs,
        out_specs=pl.BlockSpec((tq, ATTN_W), lambda b, i: (b * qpb + i, 0)),
        out_shape=jax.ShapeDtypeStruct((batch * t, ATTN_W), F32),
        scratch_shapes=[pltpu.VMEM((ATTN_W, tq), F32), pltpu.VMEM((2, s_total, tq), F32),
                        pltpu.VMEM((2, s_total, tq), BF16)],
        compiler_params=_cparams(("parallel", "parallel")),
        name="attention",
    )(qt, *[k for k, _, _, _ in segments], *[v for _, v, _, _ in segments])


def _cache_heads(cache_k, cache_v):
    batch, past = cache_k.shape[:2]
    ck = cache_k.astype(BF16).transpose(2, 0, 1, 3).reshape(N_KV_HEADS, batch * past, HEAD_DIM)
    cv = cache_v.astype(BF16).transpose(2, 3, 0, 1).reshape(N_KV_HEADS, HEAD_DIM, batch * past)
    tail = jnp.concatenate([jnp.ones((N_KV_HEADS, 1, batch * past), BF16),
                            jnp.zeros((N_KV_HEADS, V_ROWS - HEAD_DIM - 1, batch * past), BF16)], axis=1)
    return ck, jnp.concatenate([cv, tail], axis=1)


def _fourier_kernel(x_ref, cd_ref, ct_ref, st_ref, twc_ref, tws_ref, wf_ref, o_ref, *, n1, n2):
    pw = 2 * FOURIER_GROUP_W
    scale = 1.0 / math.sqrt(n1 * n2 * FOURIER_GROUP_W)
    for gp in range(FOURIER_W // pw):
        z = jnp.dot(x_ref[:, gp * pw:(gp + 1) * pw], cd_ref[...], preferred_element_type=F32)
        zr = [z[t1 * n2:(t1 + 1) * n2, :pw] for t1 in range(n1)]
        zi = [z[t1 * n2:(t1 + 1) * n2, pw:] for t1 in range(n1)]
        if n1 == 1:
            a = [(zr[0], zi[0])]
        else:
            a = [(zr[0] + zr[1] + zr[2] + zr[3], zi[0] + zi[1] + zi[2] + zi[3]),
                 (zr[0] + zi[1] - zr[2] - zi[3], zi[0] - zr[1] - zi[2] + zr[3]),
                 (zr[0] - zr[1] + zr[2] - zr[3], zi[0] - zi[1] + zi[2] - zi[3]),
                 (zr[0] - zi[1] - zr[2] + zi[3], zi[0] + zr[1] - zi[2] - zr[3])]
        for u1 in range(n1):
            ar, ai = a[u1]
            if u1 > 0:
                c = jnp.concatenate([twc_ref[u1 - 1]] * 2, axis=1)
                s = jnp.concatenate([tws_ref[u1 - 1]] * 2, axis=1)
                ar, ai = ar * c + ai * s, ai * c - ar * s
            y = (jnp.dot(ct_ref[...], ar.astype(BF16), preferred_element_type=F32)
                 + jnp.dot(st_ref[...], ai.astype(BF16), preferred_element_type=F32)) * scale
            y = jnp.dot(y.astype(BF16), wf_ref[gp], preferred_element_type=F32)
            rows = pl.ds(u1, n2, stride=n1) if n1 > 1 else slice(None)
            for half in range(2):
                o_ref[2 * gp + half, rows, :] = y[:, half * FOURIER_GROUP_W:(half + 1) * FOURIER_GROUP_W]


def _fourier(f, wf_pairs, *, batch, t, row0, n1):
    n2 = t // n1
    cd, ct, st, twc, tws = _dft_tables(n1, n2)
    cd, ct, st = cd.astype(BF16), ct.astype(BF16), st.astype(BF16)
    nt = twc.shape[0]
    b0 = row0 // t
    c2 = lambda b: (0, 0)
    c3 = lambda b: (0, 0, 0)
    return pl.pallas_call(
        functools.partial(_fourier_kernel, n1=n1, n2=n2),
        grid=(batch,),
        in_specs=[pl.BlockSpec((t, FOURIER_W), lambda b: (b0 + b, 0)),
                  pl.BlockSpec(cd.shape, c2), pl.BlockSpec((n2, n2), c2), pl.BlockSpec((n2, n2), c2),
                  pl.BlockSpec((nt, n2, LANES), c3), pl.BlockSpec((nt, n2, LANES), c3),
                  pl.BlockSpec(wf_pairs.shape, c3)],
        out_specs=pl.BlockSpec((FOURIER_W // FOURIER_GROUP_W, t, FOURIER_GROUP_W), lambda b: (0, b, 0)),
        out_shape=jax.ShapeDtypeStruct((FOURIER_W // FOURIER_GROUP_W, batch * t, FOURIER_GROUP_W), F32),
        compiler_params=_cparams(("parallel",)),
        name="fourier",
    )(f, cd, ct, st, twc, tws, wf_pairs)


def _route(i, h, whi_ref, wlo_ref, br_ref, idx_ref, w_ref, cnt_ref):
    h_hi = h.astype(BF16)
    h_lo = (h - h_hi.astype(F32)).astype(BF16)
    both = jnp.dot(h_hi, jnp.concatenate([whi_ref[...], wlo_ref[...]], axis=1), preferred_element_type=F32)
    logits = (both[:, :LANES] + both[:, LANES:]
              + jnp.dot(h_lo, whi_ref[...], preferred_element_type=F32)) + br_ref[...]
    lane = lax.broadcasted_iota(I32, logits.shape, 1)
    l = logits
    vals, idxs = [], []
    hits = jnp.zeros(logits.shape, F32)
    for _ in range(TOP_K):
        m = jnp.max(l, axis=-1, keepdims=True)
        ix = jnp.min(jnp.where(l == m, lane, LANES), axis=-1, keepdims=True)
        sel = lane == ix
        hits = hits + sel.astype(F32)
        l = jnp.where(sel, -jnp.inf, l)
        vals.append(m)
        idxs.append(ix)
    es = [jnp.exp(v - vals[0]) for v in vals]
    den = es[0] + es[1] + es[2] + es[3]
    idx_ref[...] = _pack_cols(idxs, lane)
    w_ref[...] = _pack_cols([e / den for e in es], lane)

    @pl.when(i == 0)
    def _():
        cnt_ref[...] = jnp.zeros(cnt_ref.shape, F32)

    cnt_ref[...] += _round_up_runs(jnp.sum(hits, axis=0, keepdims=True))


def _mix_kernel(xc_ref, xl_ref, ac_ref, al_ref, fc_ref, fl_ref, g1_ref, sh2_ref, sc2_ref, ag_ref, fg_ref,
                wout_ref, n2g_ref, whi_ref, wlo_ref, br_ref, x1_ref, h2_ref, idx_ref, w_ref, cnt_ref, *, nc, tpb):
    i = pl.program_id(0)
    r = _mod_row(i, nc, tpb)
    is_ctx = i < nc
    x = jnp.where(is_ctx, xc_ref[...], xl_ref[...])
    a = _rms(jnp.where(is_ctx, ac_ref[...], al_ref[...])) * ag_ref[...]
    n_groups = FOURIER_W // FOURIER_GROUP_W
    f = jnp.concatenate([jnp.where(is_ctx, fc_ref[g], fl_ref[g]) for g in range(n_groups)], axis=1)
    f = _rms(f) * fg_ref[...]
    mixed = (jnp.dot(a.astype(BF16), wout_ref[:ATTN_W, :], preferred_element_type=F32)
             + jnp.dot(f.astype(BF16), wout_ref[ATTN_W:, :], preferred_element_type=F32))
    x1 = x + g1_ref[pl.ds(r, 1), :] * mixed
    x1_ref[...] = x1
    h2 = _rms(x1) * n2g_ref[...]
    h2 = h2 * (1.0 + sc2_ref[pl.ds(r, 1), :]) + sh2_ref[pl.ds(r, 1), :]
    h2_ref[...] = h2
    _route(i, h2, whi_ref, wlo_ref, br_ref, idx_ref, w_ref, cnt_ref)


def _mix(xc, xl, ac, al, fc, fl, mod, attn_g, four_g, w_out, norm2_g, wr_hi, wr_lo, br, *, nc, nl, tpb):
    n = (nc + nl) * TM
    lo = lambda i: (jnp.minimum(i, nc - 1), 0)
    hi = lambda i: (jnp.maximum(i - nc, 0), 0)
    const = lambda i: (0, 0)
    row = lambda i: (i, 0)
    return pl.pallas_call(
        functools.partial(_mix_kernel, nc=nc, tpb=tpb),
        grid=(nc + nl,),
        in_specs=[pl.BlockSpec((TM, D_MODEL), lo), pl.BlockSpec((TM, D_MODEL), hi),
                  pl.BlockSpec((TM, ATTN_W), lo), pl.BlockSpec((TM, ATTN_W), hi),
                  pl.BlockSpec((FOURIER_W // FOURIER_GROUP_W, TM, FOURIER_GROUP_W),
                               lambda i: (0, jnp.minimum(i, nc - 1), 0)),
                  pl.BlockSpec((FOURIER_W // FOURIER_GROUP_W, TM, FOURIER_GROUP_W),
                               lambda i: (0, jnp.maximum(i - nc, 0), 0)),
                  pl.BlockSpec((8, D_MODEL), lambda i: (0, 2)),
                  pl.BlockSpec((8, D_MODEL), lambda i: (0, 3)),
                  pl.BlockSpec((8, D_MODEL), lambda i: (0, 4)),
                  pl.BlockSpec((1, ATTN_W), const), pl.BlockSpec((1, FOURIER_W), const),
                  pl.BlockSpec((D_MODEL, D_MODEL), const), pl.BlockSpec((1, D_MODEL), const),
                  pl.BlockSpec((D_MODEL, LANES), const), pl.BlockSpec((D_MODEL, LANES), const),
                  pl.BlockSpec((1, LANES), const)],
        out_specs=[pl.BlockSpec((TM, D_MODEL), row), pl.BlockSpec((TM, D_MODEL), row),
                   pl.BlockSpec((TM, LANES), row), pl.BlockSpec((TM, LANES), row), pl.BlockSpec((8, LANES), const)],
        out_shape=[jax.ShapeDtypeStruct((n, D_MODEL), F32), jax.ShapeDtypeStruct((n, D_MODEL), F32),
                   jax.ShapeDtypeStruct((n, LANES), I32), jax.ShapeDtypeStruct((n, LANES), F32),
                   jax.ShapeDtypeStruct((8, LANES), F32)],
        compiler_params=_cparams(("arbitrary",)),
        name="mix_proj",
    )(xc, xl, ac, al, fc, fl, mod, mod, mod, attn_g, four_g, w_out, norm2_g, wr_hi, wr_lo, br)


RUN_ALIGN = 8
RUN_ROWS = TM * TOP_K + N_EXPERTS * RUN_ALIGN
RUN_SIZES = tuple(TM >> s for s in range(7))
RUN_PAD = RUN_ROWS - TM * TOP_K
FILL_SIZES = tuple(RUN_PAD >> s for s in range(6))


def _round_up_runs(x):
    return jnp.floor((x + (RUN_ALIGN - 1)) * (1.0 / RUN_ALIGN)) * RUN_ALIGN


def _pos_kernel(idx_ref, gs_ref, loct_ref, loc_ref, info_ref, carry_ref):
    i = pl.program_id(0)

    @pl.when(i == 0)
    def _():
        carry_ref[...] = jnp.zeros(carry_ref.shape, F32)

    idx = idx_ref[...]
    lane = lax.broadcasted_iota(I32, idx.shape, 1)
    sels = [lane == idx[:, k:k + 1] for k in range(TOP_K)]
    hits = jnp.zeros(idx.shape, F32)
    for s in sels:
        hits = hits + s.astype(F32)
    r = lax.broadcasted_iota(I32, (TM, TM), 0)
    c = lax.broadcasted_iota(I32, (TM, TM), 1)
    before = (c < r).astype(BF16)
    rank = jnp.dot(before, hits.astype(BF16), preferred_element_type=F32)
    run_len = _round_up_runs(jnp.sum(hits, axis=0, keepdims=True))
    a = lax.broadcasted_iota(I32, (LANES, LANES), 0)
    b = lax.broadcasted_iota(I32, (LANES, LANES), 1)
    units = jnp.broadcast_to(run_len * (1.0 / RUN_ALIGN), (8, LANES)).astype(BF16)
    run_off = jnp.dot(units, (a < b).astype(BF16), preferred_element_type=F32)[0:1, :] * RUN_ALIGN
    run_start = carry_ref[0:1, :] + gs_ref[0:1, :]
    loc_cols = [jnp.sum(jnp.where(s, rank + run_off, 0.0), axis=-1, keepdims=True) for s in sels]
    loc = _pack_cols(loc_cols, lane)
    loc_ref[...] = loc.astype(I32)
    loct_ref[...] = loc.T[:8, :].astype(I32)
    row = lax.broadcasted_iota(I32, (8, LANES), 0)
    info = jnp.where(row == 0, run_start, jnp.where(row == 1, run_len, jnp.where(row == 2, run_off, 0.0)))
    info_ref[...] = info.astype(I32)
    carry_ref[...] += run_len


def _positions(idx, gs_rows):
    n = idx.shape[0]
    nt = n // TM
    return pl.pallas_call(
        _pos_kernel,
        grid=(nt,),
        in_specs=[pl.BlockSpec((TM, LANES), lambda i: (i, 0)), pl.BlockSpec((8, LANES), lambda i: (0, 0))],
        out_specs=[pl.BlockSpec((8, TM), lambda i: (0, i)), pl.BlockSpec((TM, LANES), lambda i: (i, 0)),
                   pl.BlockSpec((None, 8, LANES), lambda i: (i, 0, 0))],
        out_shape=[jax.ShapeDtypeStruct((8, n), I32), jax.ShapeDtypeStruct((n, LANES), I32),
                   jax.ShapeDtypeStruct((nt, 8, LANES), I32)],
        scratch_shapes=[pltpu.VMEM((8, LANES), F32)],
        compiler_params=_cparams(("arbitrary",)),
        name="positions",
    )(idx, gs_rows)


def _run_copies(info_ref, tile, make_copy, act):
    base = tile * (3 * N_EXPERTS)

    def per_expert(e, carry):
        start = info_ref[base + e]
        length = info_ref[base + N_EXPERTS + e]
        off = info_ref[base + 2 * N_EXPERTS + e]
        for size in RUN_SIZES:
            @pl.when((length & size) != 0)
            def _():
                act(make_copy(pl.multiple_of(off, RUN_ALIGN), pl.multiple_of(start, RUN_ALIGN), size))
            taken = jnp.where((length & size) != 0, size, 0)
            off = off + taken
            start = start + taken
        return carry

    lax.fori_loop(0, N_EXPERTS, per_expert, 0)


PAD_SIZES = tuple(s for s in (256, 128, 64, 32, 16, 8) if s < TG)


def _dispatch_kernel(info_ref, ends_ref, cnt_ref, loc_ref, h2_ref, xs_ref, dump_ref, runs_ref, zero_ref, sem, zsem):
    i = pl.program_id(0)
    last_step = pl.num_programs(0) - 1
    slot = i % 2

    def zero_copies(act):
        def per_expert(e, carry):
            lo = jnp.where(e == 0, 0, ends_ref[jnp.maximum(e - 1, 0)])
            first = lo + cnt_ref[e]
            rest = ends_ref[e] - first
            done = rest * 0
            for size in PAD_SIZES:
                @pl.when((rest & size) != 0)
                def _():
                    act(pltpu.make_async_copy(zero_ref.at[pl.ds(0, size), :],
                                              xs_ref.at[pl.ds(pl.multiple_of(first + done, RUN_ALIGN), size), :], zsem))
                done = done + jnp.where((rest & size) != 0, size, 0)
            return carry

        lax.fori_loop(0, N_EXPERTS, per_expert, 0)

        def tail_tile(t, carry):
            act(pltpu.make_async_copy(zero_ref, xs_ref.at[pl.ds(pl.multiple_of(t * TG, TG), TG), :], zsem))
            return carry

        lax.fori_loop(ends_ref[N_EXPERTS - 1] // TG, xs_ref.shape[0] // TG, tail_tile, 0)

    def slot_done(s):
        return pltpu.make_async_copy(runs_ref.at[s], xs_ref.at[pl.ds(0, RUN_ROWS), :], sem.at[s])

    @pl.when(i == 0)
    def _():
        zero_ref[...] = jnp.zeros(zero_ref.shape, F32)
        for s in range(2):
            pltpu.make_async_copy(zero_ref.at[pl.ds(0, RUN_PAD), :], dump_ref.at[s], sem.at[s]).start()
        for s in range(2):
            pltpu.make_async_copy(zero_ref.at[pl.ds(0, RUN_PAD), :], dump_ref.at[s], sem.at[s]).wait()
        zero_copies(lambda cp: cp.start())

    @pl.when(i >= 2)
    def _():
        slot_done(slot).wait()

    loc = loc_ref[...]
    row = lax.broadcasted_iota(I32, (RUN_ROWS, TM), 0)
    sel = row == loc[0:1, :]
    for k in range(1, TOP_K):
        sel = jnp.logical_or(sel, row == loc[k:k + 1, :])
    runs_ref[slot] = jnp.dot(sel.astype(BF16), h2_ref[...].astype(BF16), preferred_element_type=F32)

    def piece(stacked_row, global_row, size):
        return pltpu.make_async_copy(runs_ref.at[slot, pl.ds(stacked_row, size), :],
                                     xs_ref.at[pl.ds(global_row, size), :], sem.at[slot])

    _run_copies(info_ref, i, piece, lambda cp: cp.start())
    last = i * (3 * N_EXPERTS) + N_EXPERTS - 1
    used = info_ref[last + 2 * N_EXPERTS] + info_ref[last + N_EXPERTS]
    rest = RUN_ROWS - used
    done = used * 0
    for size in FILL_SIZES:
        @pl.when((rest & size) != 0)
        def _():
            at = pl.multiple_of(used + done, RUN_ALIGN)
            to = pl.multiple_of(at - TM * TOP_K, RUN_ALIGN)
            pltpu.make_async_copy(runs_ref.at[slot, pl.ds(at, size), :], dump_ref.at[slot, pl.ds(to, size), :],
                                  sem.at[slot]).start()
        done = done + jnp.where((rest & size) != 0, size, 0)

    @pl.when(i == last_step)
    def _():
        slot_done(slot).wait()

        @pl.when(i >= 1)
        def _():
            slot_done(1 - slot).wait()

        zero_copies(lambda cp: cp.wait())


def _dispatch(info, ends, cnt, loc, h2, n_rows):
    n = h2.shape[0]
    grid_spec = pltpu.PrefetchScalarGridSpec(
        num_scalar_prefetch=3,
        grid=(n // TM,),
        in_specs=[pl.BlockSpec((8, TM), lambda i, *_: (0, i)),
                  pl.BlockSpec((TM, D_MODEL), lambda i, *_: (i, 0))],
        out_specs=[pl.BlockSpec(memory_space=pl.ANY), pl.BlockSpec(memory_space=pl.ANY)],
        scratch_shapes=[pltpu.VMEM((2, RUN_ROWS, D_MODEL), F32), pltpu.VMEM((TG, D_MODEL), F32),
                        pltpu.SemaphoreType.DMA((2,)), pltpu.SemaphoreType.DMA(())],
    )
    xs, _ = pl.pallas_call(
        _dispatch_kernel,
        grid_spec=grid_spec,
        out_shape=[jax.ShapeDtypeStruct((n_rows, D_MODEL), F32), jax.ShapeDtypeStruct((2, RUN_PAD, D_MODEL), F32)],
        compiler_params=_cparams(("arbitrary",)),
        name="dispatch",
    )(info, ends, cnt, loc, h2)
    return xs


def _ffn_kernel(te_ref, nv_ref, par_ref, nxt_ref, x_ref, bg_ref, bl_ref, bdn_ref, perm_ref, wgu_hbm, wdn_hbm, o_ref,
                wgu_buf, wdn_buf, wg_s, wl_s, wd_s, wsem):
    i = pl.program_id(0)
    e = te_ref[i]
    valid = i < nv_ref[0]
    fresh = jnp.logical_or(i == 0, e != te_ref[jnp.maximum(i - 1, 0)])
    wslot = par_ref[i]

    def weight_copies(expert, s):
        return (pltpu.make_async_copy(wgu_hbm.at[expert], wgu_buf.at[s], wsem.at[s]),
                pltpu.make_async_copy(wdn_hbm.at[expert], wdn_buf.at[s], wsem.at[s]))

    @pl.when(i == 0)
    def _():
        for cp in weight_copies(e, wslot):
            cp.start()

    @pl.when(jnp.logical_and(valid, fresh))
    def _():
        for cp in weight_copies(e, wslot):
            cp.wait()
        for j in range(2 * D_EXPERT // (2 * LANES)):
            blk = wgu_buf[wslot, :, j * 2 * LANES:(j + 1) * 2 * LANES].astype(BF16)
            d = jnp.dot(blk, perm_ref[...], preferred_element_type=F32)
            wg_s[:, j * LANES:(j + 1) * LANES] = d[:, :LANES].astype(BF16)
            wl_s[:, j * LANES:(j + 1) * LANES] = d[:, LANES:].astype(BF16)
        wd_s[...] = wdn_buf[wslot].astype(BF16)

        @pl.when(nxt_ref[i] >= 0)
        def _():
            for cp in weight_copies(nxt_ref[i], 1 - wslot):
                cp.start()

    @pl.when(valid)
    def _():
        x = x_ref[...].astype(BF16)
        g = jnp.dot(x, wg_s[...], preferred_element_type=F32) + bg_ref[...]
        l = jnp.dot(x, wl_s[...], preferred_element_type=F32) + bl_ref[...]
        g = jnp.minimum(g, SWIGLU_LIMIT)
        l = jnp.clip(l, -SWIGLU_LIMIT, SWIGLU_LIMIT)
        a = (l + 1.0) * (g * (1.0 / (1.0 + jnp.exp(-SWIGLU_ALPHA * g))))
        o_ref[...] = jnp.dot(a.astype(BF16), wd_s[...], preferred_element_type=F32) + bdn_ref[...]

    @pl.when(jnp.logical_not(valid))
    def _():
        o_ref[...] = jnp.zeros(o_ref.shape, F32)


def _expert_ffn(tile_expert, n_valid, parity, next_expert, xs, w_gu, b_g, b_l, w_dn, b_dn, perm):
    p = xs.shape[0]
    nt = p // TG
    tile = lambda i, te, nv, *_: (jnp.minimum(i, nv[0] - 1), 0)
    ex3 = lambda i, te, *_: (te[i], 0, 0)
    grid_spec = pltpu.PrefetchScalarGridSpec(
        num_scalar_prefetch=4,
        grid=(nt,),
        in_specs=[pl.BlockSpec((TG, D_MODEL), tile),
                  pl.BlockSpec((None, 1, D_EXPERT), ex3),
                  pl.BlockSpec((None, 1, D_EXPERT), ex3),
                  pl.BlockSpec((None, 1, D_MODEL), ex3),
                  pl.BlockSpec((2 * LANES, 2 * LANES), lambda i, *_: (0, 0)),
                  pl.BlockSpec(memory_space=pl.ANY),
                  pl.BlockSpec(memory_space=pl.ANY)],
        out_specs=pl.BlockSpec((TG, D_MODEL), lambda i, *_: (i, 0)),
        scratch_shapes=[pltpu.VMEM((2, D_MODEL, 2 * D_EXPERT), F32), pltpu.VMEM((2, D_EXPERT, D_MODEL), F32),
                        pltpu.VMEM((D_MODEL, D_EXPERT), BF16), pltpu.VMEM((D_MODEL, D_EXPERT), BF16),
                        pltpu.VMEM((D_EXPERT, D_MODEL), BF16), pltpu.SemaphoreType.DMA((2,))],
    )
    return pl.pallas_call(
        _ffn_kernel,
        grid_spec=grid_spec,
        out_shape=jax.ShapeDtypeStruct((p, D_MODEL), F32),
        compiler_params=_cparams(("arbitrary",)),
        name="expert_ffn",
    )(tile_expert, n_valid, parity, next_expert, xs, b_g, b_l, b_dn, perm, w_gu, w_dn)


def _combine_kernel(info_ref, loc_ref, w_ref, x1_ref, g2_ref, fg_ref, ys_ref, o_ref, runs_ref, sem, *, tile0, nc, tpb):
    j = pl.program_id(0)
    i = j + tile0
    r = _mod_row(i, nc, tpb)
    slot = j % 2

    def gather(tile, s, act):
        def piece(stacked_row, global_row, size):
            return pltpu.make_async_copy(ys_ref.at[pl.ds(global_row, size), :],
                                         runs_ref.at[s, pl.ds(stacked_row, size), :], sem.at[s])

        _run_copies(info_ref, tile, piece, act)
        last = tile * (3 * N_EXPERTS) + N_EXPERTS - 1
        used = info_ref[last + 2 * N_EXPERTS] + info_ref[last + N_EXPERTS]
        rest = RUN_ROWS - used
        done = used * 0
        for size in FILL_SIZES:
            @pl.when((rest & size) != 0)
            def _():
                act(piece(pl.multiple_of(used + done, RUN_ALIGN), pl.multiple_of(done, RUN_ALIGN), size))
            done = done + jnp.where((rest & size) != 0, size, 0)

    @pl.when(j == 0)
    def _():
        gather(i, 0, lambda cp: cp.start())

    @pl.when(j + 1 < pl.num_programs(0))
    def _():
        gather(i + 1, 1 - slot, lambda cp: cp.start())

    pltpu.make_async_copy(ys_ref.at[pl.ds(0, RUN_ROWS), :], runs_ref.at[slot], sem.at[slot]).wait()
    loc = loc_ref[...]
    w = w_ref[...]
    col = lax.broadcasted_iota(I32, (TM, RUN_ROWS), 1)
    mix = jnp.where(col == loc[:, 0:1], w[:, 0:1], 0.0)
    for k in range(1, TOP_K):
        mix = mix + jnp.where(col == loc[:, k:k + 1], w[:, k:k + 1], 0.0)
    moe = jnp.dot(mix.astype(BF16), runs_ref[slot].astype(BF16), preferred_element_type=F32)
    y = x1_ref[...] + g2_ref[pl.ds(r, 1), :] * moe
    o_ref[...] = _rms(y) * fg_ref[...]


def _combine(info, loc, w, x1, mod, final_g, ys, *, tile0, ntiles, nc, tpb):
    grid_spec = pltpu.PrefetchScalarGridSpec(
        num_scalar_prefetch=1,
        grid=(ntiles,),
        in_specs=[pl.BlockSpec((TM, LANES), lambda i, *_: (i + tile0, 0)),
                  pl.BlockSpec((TM, LANES), lambda i, *_: (i + tile0, 0)),
                  pl.BlockSpec((TM, D_MODEL), lambda i, *_: (i + tile0, 0)),
                  pl.BlockSpec((8, D_MODEL), lambda i, *_: (0, 5)),
                  pl.BlockSpec((1, D_MODEL), lambda i, *_: (0, 0)),
                  pl.BlockSpec(memory_space=pl.ANY)],
        out_specs=pl.BlockSpec((TM, D_MODEL), lambda i, *_: (i, 0)),
        scratch_shapes=[pltpu.VMEM((2, RUN_ROWS, D_MODEL), F32), pltpu.SemaphoreType.DMA((2,))],
    )
    return pl.pallas_call(
        functools.partial(_combine_kernel, tile0=tile0, nc=nc, tpb=tpb),
        grid_spec=grid_spec,
        out_shape=jax.ShapeDtypeStruct((ntiles * TM, D_MODEL), F32),
        compiler_params=_cparams(("arbitrary",)),
        name="combine",
    )(info, loc, w, x1, mod, final_g, ys)


def kernel(x_prompt, x_sample, cache_k, cache_v, c, c_ctx, norm1_g, w_mod, b_mod, w_in, q_norm_g, k_norm_g,
           w_fourier, attn_out_g, fourier_out_g, w_out, norm2_g, w_router, b_router, w_gate_up, b_gate_up,
           w_down, b_down, final_g):
    bc, tc, _ = x_prompt.shape
    bl, tl, _ = x_sample.shape
    depth = w_in.shape[0]
    past = cache_k.shape[2]
    assert depth == 1 and tl % TQ == 0 and tl % TM == 0 and (bc * tc) % TM == 0 and (bc * tc) % tl == 0
    n_ctx, n_lat = bc * tc, bl * tl
    n_tok = n_ctx + n_lat
    nc, nl, tpb = n_ctx // TM, n_lat // TM, tl // TM

    xc = x_prompt.reshape(n_ctx, D_MODEL)
    xl = x_sample.reshape(n_lat, D_MODEL)

    c_rows = jnp.concatenate([c_ctx[None, :], c, jnp.zeros((8 - 1 - bl, D_MODEL), F32)], axis=0)
    mod = _modulation(c_rows, w_mod[0], b_mod[0])

    cos_t, sin_t = _rope_tables(tl)
    q, k_heads, vt_ext, f, k_new, v_new = _pre(
        xc, xl, mod, norm1_g, w_in[0].astype(BF16), jnp.tile(q_norm_g, (1, N_HEADS)),
        jnp.tile(k_norm_g, (1, N_KV_HEADS)), _head_mean_matrix(), cos_t, sin_t, nc=nc, nl=nl, tpb=tpb)

    attn_c = _attention(q, [(k_heads, vt_ext, tc, 0)], batch=bc, t=tc, q_row0=0)
    ck, cvt = _cache_heads(cache_k[:, 0], cache_v[:, 0])
    attn_l = _attention(q, [(ck, cvt, past, 0), (k_heads, vt_ext, tl, n_ctx // tl)], batch=bl, t=tl, q_row0=n_ctx)

    wf = w_fourier[0].astype(BF16)
    zero = jnp.zeros_like(wf[0])
    wf_pairs = jnp.stack([jnp.block([[wf[0], zero], [zero, wf[1]]]), jnp.block([[wf[2], zero], [zero, wf[3]]])])
    four_c = _fourier(f, wf_pairs, batch=bc, t=tc, row0=0, n1=1)
    four_l = _fourier(f, wf_pairs, batch=bl, t=tl, row0=n_ctx, n1=FFT_N1)

    wr = jnp.pad(w_router[0], ((0, 0), (0, LANES - N_EXPERTS)))
    wr_hi = wr.astype(BF16)
    wr_lo = (wr - wr_hi.astype(F32)).astype(BF16)
    br = jnp.pad(b_router[0], (0, LANES - N_EXPERTS), constant_values=-jnp.inf).reshape(1, LANES)
    x1, h2, idx, gate_w, counts = _mix(xc, xl, attn_c, attn_l, four_c, four_l, mod, attn_out_g, fourier_out_g,
                                       w_out[0].astype(BF16), norm2_g, wr_hi, wr_lo, br, nc=nc, nl=nl, tpb=tpb)
    cnt = counts[0, :N_EXPERTS].astype(I32)
    padded = ((cnt + TG - 1) // TG) * TG
    ends = jnp.cumsum(padded)
    starts = ends - padded
    n_rows = n_tok * TOP_K + (n_tok // TM) * N_EXPERTS * RUN_ALIGN + N_EXPERTS * TG
    n_tiles = n_rows // TG
    tile_ids = jnp.arange(n_tiles, dtype=I32)
    tile_expert = jnp.minimum(jnp.sum((ends[None, :] // TG <= tile_ids[:, None]).astype(I32), axis=1),
                              N_EXPERTS - 1)
    n_valid = (ends[N_EXPERTS - 1] // TG).reshape(1)
    gs_rows = jnp.broadcast_to(jnp.pad(starts.astype(F32), (0, LANES - N_EXPERTS))[None, :], (8, LANES))
    loc_t, loc, info = _positions(idx, gs_rows)
    info = info[:, :3, :N_EXPERTS].reshape(-1)

    xs = _dispatch(info, ends, cnt, loc_t, h2, n_rows)
    fresh_tile = jnp.concatenate([jnp.ones((1,), I32), (tile_expert[1:] != tile_expert[:-1]).astype(I32)])
    parity = (jnp.cumsum(fresh_tile) - 1) % 2
    experts = jnp.arange(N_EXPERTS, dtype=I32)
    later = jnp.logical_and(padded[None, :] > 0, experts[None, :] > experts[:, None])
    next_of = jnp.min(jnp.where(later, experts[None, :], N_EXPERTS), axis=1)
    next_of = jnp.where(next_of == N_EXPERTS, -1, next_of)
    next_expert = jnp.sum(jnp.where(tile_expert[:, None] == experts[None, :], next_of[None, :], 0), axis=1)
    b_gu = b_gate_up[0].reshape(N_EXPERTS, 1, D_EXPERT, 2)
    ys = _expert_ffn(tile_expert, n_valid, parity.astype(I32), next_expert.astype(I32), xs, w_gate_up[0],
                     b_gu[..., 0], b_gu[..., 1], w_down[0], b_down[0].reshape(N_EXPERTS, 1, D_MODEL),
                     _deinterleave_matrix())

    y_c = _combine(info, loc, gate_w, x1, mod, final_g.reshape(1, -1), ys, tile0=0, ntiles=nc, nc=nc, tpb=tpb)
    y_l = _combine(info, loc, gate_w, x1, mod, final_g.reshape(1, -1), ys, tile0=nc, ntiles=nl, nc=nc, tpb=tpb)

    new_k = k_new[:n_ctx].reshape(bc, 1, tc, N_KV_HEADS, HEAD_DIM)
    new_v = v_new[:n_ctx].reshape(bc, 1, tc, N_KV_HEADS, HEAD_DIM)
    return (y_c.reshape(bc, tc, D_MODEL), y_l.reshape(bl, tl, D_MODEL), new_k, new_v)
```

```python
import functools
import math

import numpy as np
import jax
import jax.numpy as jnp
from jax import lax
from jax.experimental import pallas as pl
from jax.experimental.pallas import tpu as pltpu

F32 = jnp.float32
BF16 = jnp.bfloat16
I32 = jnp.int32

D_MODEL = 1024
HEAD_DIM = 64
N_HEADS = 8
N_KV_HEADS = 2
HEADS_PER_KV = N_HEADS // N_KV_HEADS
ATTN_W = N_HEADS * HEAD_DIM
KV_W = N_KV_HEADS * HEAD_DIM
FOURIER_W = 512
FOURIER_GROUP_W = 128
D_IN = ATTN_W + 2 * KV_W + FOURIER_W
N_EXPERTS = 32
TOP_K = 4
D_EXPERT = 1024
SWIGLU_ALPHA = 1.702
SWIGLU_LIMIT = 7.0
ROPE_THETA = 10000.0
ROT_PAIRS = HEAD_DIM // 4
GRID_W = 64
EPS = 1e-6

LANES = 128
TM = 512
TQ = 512
TG = 512
FFT_N1 = 4
V_ROWS = 80
VMEM_LIMIT = 56 * 1024 * 1024


def _cparams(sem, vmem=VMEM_LIMIT):
    return pltpu.CompilerParams(dimension_semantics=sem, vmem_limit_bytes=vmem)


def _rope_tables(t_lat):
    pos = np.arange(t_lat)
    row = (pos // GRID_W).astype(np.float64)
    col = (pos % GRID_W).astype(np.float64)
    inv = ROPE_THETA ** (-np.arange(ROT_PAIRS, dtype=np.float64) / ROT_PAIRS)
    lane = np.arange(LANES)
    d = lane % HEAD_DIM
    axis = d // (2 * ROT_PAIRS)
    second = (d // ROT_PAIRS) % 2
    p = d % ROT_PAIRS
    ang = np.where(axis[None, :] == 0, row[:, None], col[:, None]) * inv[p][None, :]
    cos = np.cos(ang)
    sin = np.sin(ang) * np.where(second == 0, -1.0, 1.0)[None, :]
    cos = np.concatenate([np.ones((TM, LANES)), cos], axis=0)
    sin = np.concatenate([np.zeros((TM, LANES)), sin], axis=0)
    return jnp.asarray(cos, F32), jnp.asarray(sin, F32)


def _dft_tables(n1, n2):
    c = np.arange(FOURIER_GROUP_W)
    ang = 2.0 * np.pi * np.outer(c, c) / FOURIER_GROUP_W
    z = np.zeros_like(ang)
    cbd = np.block([[np.cos(ang), z], [z, np.cos(ang)]])
    sbd = np.block([[np.sin(ang), z], [z, np.sin(ang)]])
    cd = np.concatenate([cbd, -sbd], axis=1)
    u = np.arange(n2)
    ang2 = 2.0 * np.pi * np.outer(u, u) / n2
    ct, st = np.cos(ang2), np.sin(ang2)
    nt = max(n1 - 1, 1)
    tw_ang = 2.0 * np.pi * np.outer(np.arange(1, nt + 1), u) / (n1 * n2)
    twc = np.repeat(np.cos(tw_ang)[:, :, None], LANES, axis=2)
    tws = np.repeat(np.sin(tw_ang)[:, :, None], LANES, axis=2)
    return (jnp.asarray(cd, F32), jnp.asarray(ct, F32), jnp.asarray(st, F32),
            jnp.asarray(twc, F32), jnp.asarray(tws, F32))


def _head_mean_matrix():
    h = np.arange(ATTN_W) // HEAD_DIM
    return jnp.asarray((h[:, None] == h[None, :]) / HEAD_DIM, BF16)


def _deinterleave_matrix():
    p = np.zeros((2 * LANES, 2 * LANES))
    m = np.arange(LANES)
    p[2 * m, m] = 1.0
    p[2 * m + 1, LANES + m] = 1.0
    return jnp.asarray(p, BF16)


def _mod_row(i, nc, tpb):
    return jnp.where(i < nc, 0, 1 + (i - nc) // tpb)


def _rms(x):
    return x * lax.rsqrt(jnp.mean(x * x, axis=-1, keepdims=True) + EPS)


def _pack_cols(cols, lane):
    out = jnp.zeros(lane.shape, cols[0].dtype)
    for k, c in enumerate(cols):
        out = jnp.where(lane == k, c, out)
    return out


def _mod_kernel(c_ref, w_ref, b_ref, o_ref):
    c = c_ref[...]
    s = c * (1.0 / (1.0 + jnp.exp(-c)))
    o_ref[...] = jnp.dot(s.astype(BF16), w_ref[...].astype(BF16), preferred_element_type=F32) + b_ref[...]


def _modulation(c_rows, w_mod, b_mod):
    n = w_mod.shape[1] // D_MODEL
    return pl.pallas_call(
        _mod_kernel,
        grid=(n,),
        in_specs=[pl.BlockSpec((8, D_MODEL), lambda j: (0, 0)),
                  pl.BlockSpec((D_MODEL, D_MODEL), lambda j: (0, j)),
                  pl.BlockSpec((1, D_MODEL), lambda j: (0, j))],
        out_specs=pl.BlockSpec((8, D_MODEL), lambda j: (0, j)),
        out_shape=jax.ShapeDtypeStruct((8, w_mod.shape[1]), F32),
        compiler_params=_cparams(("parallel",)),
        name="modulation",
    )(c_rows, w_mod, b_mod.reshape(1, -1))


def _pre_kernel(xc_ref, xl_ref, sh_ref, sc_ref, g1_ref, win_ref, qg_ref, kg_ref, bd_ref, cos_ref, sin_ref,
                q_ref, k_ref, vt_ref, f_ref, kn_ref, vn_ref, *, nc, tpb):
    i = pl.program_id(0)
    r = _mod_row(i, nc, tpb)
    x = jnp.where(i < nc, xc_ref[...], xl_ref[...])
    h = _rms(x) * g1_ref[...]
    h = h * (1.0 + sc_ref[pl.ds(r, 1), :]) + sh_ref[pl.ds(r, 1), :]
    proj = jnp.dot(h.astype(BF16), win_ref[...], preferred_element_type=F32)
    q = proj[:, :ATTN_W]
    k = proj[:, ATTN_W:ATTN_W + KV_W]
    v = proj[:, ATTN_W + KV_W:ATTN_W + 2 * KV_W]
    f_ref[...] = proj[:, ATTN_W + 2 * KV_W:].astype(BF16)
    vn_ref[...] = v
    vt = v.T
    tail = (lax.broadcasted_iota(I32, (V_ROWS - HEAD_DIM, vt.shape[1]), 0) == 0).astype(BF16)
    for j in range(N_KV_HEADS):
        vt_ref[j, :HEAD_DIM, :] = vt[j * HEAD_DIM:(j + 1) * HEAD_DIM, :].astype(BF16)
        vt_ref[j, HEAD_DIM:, :] = tail
    q_ms = jnp.dot((q * q).astype(BF16), bd_ref[...], preferred_element_type=F32)
    k_ms = jnp.dot((k * k).astype(BF16), bd_ref[:KV_W, :KV_W], preferred_element_type=F32)
    qn = q * lax.rsqrt(q_ms + EPS) * qg_ref[...]
    kn = k * lax.rsqrt(k_ms + EPS) * kg_ref[...]
    kn_ref[...] = kn
    cos = cos_ref[...]
    sin = sin_ref[...]
    lane = lax.broadcasted_iota(I32, cos.shape, 1)
    first = (lane & ROT_PAIRS) == 0

    def rope(c):
        partner = jnp.where(first, pltpu.roll(c, LANES - ROT_PAIRS, 1), pltpu.roll(c, ROT_PAIRS, 1))
        return c * cos + partner * sin

    k_rot = rope(kn).astype(BF16)
    for j in range(N_KV_HEADS):
        k_ref[j] = k_rot[:, j * HEAD_DIM:(j + 1) * HEAD_DIM]
    scale = HEAD_DIM ** -0.5 * math.log2(math.e)
    q_rot = jnp.concatenate([rope(qn[:, j * LANES:(j + 1) * LANES]) for j in range(ATTN_W // LANES)], axis=1)
    q_ref[...] = (q_rot * scale).T.astype(BF16)


def _pre(xc, xl, mod, norm1_g, w_in, q_g, k_g, bd, cos_t, sin_t, *, nc, nl, tpb):
    n = (nc + nl) * TM
    row = lambda i: (i, 0)
    const = lambda i: (0, 0)
    tab = lambda i: (jnp.where(i < nc, 0, 1 + (i - nc) % tpb), 0)
    return pl.pallas_call(
        functools.partial(_pre_kernel, nc=nc, tpb=tpb),
        grid=(nc + nl,),
        in_specs=[pl.BlockSpec((TM, D_MODEL), lambda i: (jnp.minimum(i, nc - 1), 0)),
                  pl.BlockSpec((TM, D_MODEL), lambda i: (jnp.maximum(i - nc, 0), 0)),
                  pl.BlockSpec((8, D_MODEL), lambda i: (0, 0)),
                  pl.BlockSpec((8, D_MODEL), lambda i: (0, 1)),
                  pl.BlockSpec((1, D_MODEL), const),
                  pl.BlockSpec((D_MODEL, D_IN), const),
                  pl.BlockSpec((1, ATTN_W), const),
                  pl.BlockSpec((1, KV_W), const),
                  pl.BlockSpec((ATTN_W, ATTN_W), const),
                  pl.BlockSpec((TM, LANES), tab),
                  pl.BlockSpec((TM, LANES), tab)],
        out_specs=[pl.BlockSpec((ATTN_W, TM), lambda i: (0, i)),
                   pl.BlockSpec((N_KV_HEADS, TM, HEAD_DIM), lambda i: (0, i, 0)),
                   pl.BlockSpec((N_KV_HEADS, V_ROWS, TM), lambda i: (0, 0, i)),
                   pl.BlockSpec((TM, FOURIER_W), row), pl.BlockSpec((TM, KV_W), row), pl.BlockSpec((TM, KV_W), row)],
        out_shape=[jax.ShapeDtypeStruct((ATTN_W, n), BF16), jax.ShapeDtypeStruct((N_KV_HEADS, n, HEAD_DIM), BF16),
                   jax.ShapeDtypeStruct((N_KV_HEADS, V_ROWS, n), BF16), jax.ShapeDtypeStruct((n, FOURIER_W), BF16),
                   jax.ShapeDtypeStruct((n, KV_W), F32), jax.ShapeDtypeStruct((n, KV_W), F32)],
        compiler_params=_cparams(("parallel",)),
        name="pre_proj",
    )(xc, xl, mod, mod, norm1_g, w_in, q_g, k_g, bd, cos_t, sin_t)


def _attn_kernel(*refs, seg_lens):
    n_seg = len(seg_lens)
    qt_ref = refs[0]
    k_refs = refs[1:1 + n_seg]
    vt_refs = refs[1 + n_seg:1 + 2 * n_seg]
    o_ref, ot_ref, st_ref, pt_ref = refs[1 + 2 * n_seg:]
    offs = [sum(seg_lens[:m]) for m in range(n_seg)]

    def scores(h):
        qt = qt_ref[h * HEAD_DIM:(h + 1) * HEAD_DIM, :]
        for k_ref, off, ln in zip(k_refs, offs, seg_lens):
            st_ref[h % 2, off:off + ln] = jnp.dot(k_ref[h // HEADS_PER_KV], qt, preferred_element_type=F32)

    scores(0)
    for h in range(N_HEADS):
        if h + 1 < N_HEADS:
            scores(h + 1)
        st = st_ref[h % 2]
        pt_ref[h % 2] = jnp.exp2(st - jnp.max(st, axis=0, keepdims=True)).astype(BF16)
        ot = None
        for vt_ref, off, ln in zip(vt_refs, offs, seg_lens):
            part = jnp.dot(vt_ref[h // HEADS_PER_KV], pt_ref[h % 2, off:off + ln], preferred_element_type=F32)
            ot = part if ot is None else ot + part
        ot_ref[h * HEAD_DIM:(h + 1) * HEAD_DIM, :] = ot[:HEAD_DIM] * (1.0 / ot[HEAD_DIM:HEAD_DIM + 1])
    o_ref[...] = ot_ref[...].T


def _attention(qt, segments, *, batch, t, q_row0):
    tq = min(TQ, t)
    qpb = t // tq
    q0 = q_row0 // tq
    seg_lens = tuple(ln for _, _, ln, _ in segments)
    s_total = sum(seg_lens)
    k_specs = [pl.BlockSpec((N_KV_HEADS, ln, HEAD_DIM), functools.partial(lambda b, i, f: (0, f + b, 0), f=fb))
               for _, _, ln, fb in segments]
    v_specs = [pl.BlockSpec((N_KV_HEADS, V_ROWS, ln), functools.partial(lambda b, i, f: (0, 0, f + b), f=fb))
               for _, _, ln, fb in segments]
    return pl.pallas_call(
        functools.partial(_attn_kernel, seg_lens=seg_lens),
        grid=(batch, qpb),
        in_specs=[pl.BlockSpec((ATTN_W, tq), lambda b, i: (0, q0 + b * qpb + i))] + k_specs + v_specs,
        out_specs=pl.BlockSpec((tq, ATTN_W), lambda b, i: (b * qpb + i, 0)),
        out_shape=jax.ShapeDtypeStruct((batch * t, ATTN_W), F32),
        scratch_shapes=[pltpu.VMEM((ATTN_W, tq), F32), pltpu.VMEM((2, s_total, tq), F32),
                        pltpu.VMEM((2, s_total, tq), BF16)],
        compiler_params=_cparams(("parallel", "parallel")),
        name="attention",
    )(qt, *[k for k, _, _, _ in segments], *[v for _, v, _, _ in segments])


def _cache_heads(cache_k, cache_v):
    batch, past = cache_k.shape[:2]
    ck = cache_k.astype(BF16).transpose(2, 0, 1, 3).reshape(N_KV_HEADS, batch * past, HEAD_DIM)
    cv = cache_v.astype(BF16).transpose(2, 3, 0, 1).reshape(N_KV_HEADS, HEAD_DIM, batch * past)
    tail = jnp.concatenate([jnp.ones((N_KV_HEADS, 1, batch * past), BF16),
                            jnp.zeros((N_KV_HEADS, V_ROWS - HEAD_DIM - 1, batch * past), BF16)], axis=1)
    return ck, jnp.concatenate([cv, tail], axis=1)


def _fourier_kernel(x_ref, cd_ref, ct_ref, st_ref, twc_ref, tws_ref, wf_ref, o_ref, *, n1, n2):
    pw = 2 * FOURIER_GROUP_W
    scale = 1.0 / math.sqrt(n1 * n2 * FOURIER_GROUP_W)
    for gp in range(FOURIER_W // pw):
        z = jnp.dot(x_ref[:, gp * pw:(gp + 1) * pw], cd_ref[...], preferred_element_type=F32)
        zr = [z[t1 * n2:(t1 + 1) * n2, :pw] for t1 in range(n1)]
        zi = [z[t1 * n2:(t1 + 1) * n2, pw:] for t1 in range(n1)]
        if n1 == 1:
            a = [(zr[0], zi[0])]
        else:
            a = [(zr[0] + zr[1] + zr[2] + zr[3], zi[0] + zi[1] + zi[2] + zi[3]),
                 (zr[0] + zi[1] - zr[2] - zi[3], zi[0] - zr[1] - zi[2] + zr[3]),
                 (zr[0] - zr[1] + zr[2] - zr[3], zi[0] - zi[1] + zi[2] - zi[3]),
                 (zr[0] - zi[1] - zr[2] + zi[3], zi[0] + zr[1] - zi[2] - zr[3])]
        for u1 in range(n1):
            ar, ai = a[u1]
            if u1 > 0:
                c = jnp.concatenate([twc_ref[u1 - 1]] * 2, axis=1)
                s = jnp.concatenate([tws_ref[u1 - 1]] * 2, axis=1)
                ar, ai = ar * c + ai * s, ai * c - ar * s
            y = (jnp.dot(ct_ref[...], ar.astype(BF16), preferred_element_type=F32)
                 + jnp.dot(st_ref[...], ai.astype(BF16), preferred_element_type=F32)) * scale
            y = jnp.dot(y.astype(BF16), wf_ref[gp], preferred_element_type=F32)
            rows = pl.ds(u1, n2, stride=n1) if n1 > 1 else slice(None)
            for half in range(2):
                o_ref[2 * gp + half, rows, :] = y[:, half * FOURIER_GROUP_W:(half + 1) * FOURIER_GROUP_W]


def _fourier(f, wf_pairs, *, batch, t, row0, n1):
    n2 = t // n1
    cd, ct, st, twc, tws = _dft_tables(n1, n2)
    cd, ct, st = cd.astype(BF16), ct.astype(BF16), st.astype(BF16)
    nt = twc.shape[0]
    b0 = row0 // t
    c2 = lambda b: (0, 0)
    c3 = lambda b: (0, 0, 0)
    return pl.pallas_call(
        functools.partial(_fourier_kernel, n1=n1, n2=n2),
        grid=(batch,),
        in_specs=[pl.BlockSpec((t, FOURIER_W), lambda b: (b0 + b, 0)),
                  pl.BlockSpec(cd.shape, c2), pl.BlockSpec((n2, n2), c2), pl.BlockSpec((n2, n2), c2),
                  pl.BlockSpec((nt, n2, LANES), c3), pl.BlockSpec((nt, n2, LANES), c3),
                  pl.BlockSpec(wf_pairs.shape, c3)],
        out_specs=pl.BlockSpec((FOURIER_W // FOURIER_GROUP_W, t, FOURIER_GROUP_W), lambda b: (0, b, 0)),
        out_shape=jax.ShapeDtypeStruct((FOURIER_W // FOURIER_GROUP_W, batch * t, FOURIER_GROUP_W), F32),
        compiler_params=_cparams(("parallel",)),
        name="fourier",
    )(f, cd, ct, st, twc, tws, wf_pairs)


def _route(i, h, whi_ref, wlo_ref, br_ref, idx_ref, w_ref, cnt_ref):
    h_hi = h.astype(BF16)
    h_lo = (h - h_hi.astype(F32)).astype(BF16)
    both = jnp.dot(h_hi, jnp.concatenate([whi_ref[...], wlo_ref[...]], axis=1), preferred_element_type=F32)
    logits = (both[:, :LANES] + both[:, LANES:]
              + jnp.dot(h_lo, whi_ref[...], preferred_element_type=F32)) + br_ref[...]
    lane = lax.broadcasted_iota(I32, logits.shape, 1)
    l = logits
    vals = []
    code = jnp.zeros(logits.shape, F32)
    a = lax.broadcasted_iota(I32, (LANES, LANES), 0)
    b = lax.broadcasted_iota(I32, (LANES, LANES), 1)
    upto = (a <= b).astype(BF16)
    for k in range(TOP_K):
        m = jnp.max(l, axis=-1, keepdims=True)
        eq = l == m
        first = jnp.dot(eq.astype(BF16), upto, preferred_element_type=F32) == 1.0
        sel = jnp.logical_and(eq, first)
        code = jnp.where(sel, float(k + 1), code)
        l = jnp.where(sel, -jnp.inf, l)
        vals.append(m)
    hits = (code > 0.0).astype(F32)
    es = [jnp.exp(v - vals[0]) for v in vals]
    den = es[0] + es[1] + es[2] + es[3]
    idx_ref[...] = code.astype(I32)
    w_ref[...] = _pack_cols([e / den for e in es], lane)

    @pl.when(i == 0)
    def _():
        cnt_ref[...] = jnp.zeros(cnt_ref.shape, F32)

    cnt_ref[...] += _round_up_runs(jnp.sum(hits, axis=0, keepdims=True))


def _mix_kernel(xc_ref, xl_ref, ac_ref, al_ref, fc_ref, fl_ref, g1_ref, sh2_ref, sc2_ref, ag_ref, fg_ref,
                wout_ref, n2g_ref, whi_ref, wlo_ref, br_ref, x1_ref, h2_ref, idx_ref, w_ref, cnt_ref, *, nc, tpb):
    i = pl.program_id(0)
    r = _mod_row(i, nc, tpb)
    is_ctx = i < nc
    x = jnp.where(is_ctx, xc_ref[...], xl_ref[...])
    a = _rms(jnp.where(is_ctx, ac_ref[...], al_ref[...])) * ag_ref[...]
    n_groups = FOURIER_W // FOURIER_GROUP_W
    f = jnp.concatenate([jnp.where(is_ctx, fc_ref[g], fl_ref[g]) for g in range(n_groups)], axis=1)
    f = _rms(f) * fg_ref[...]
    mixed = (jnp.dot(a.astype(BF16), wout_ref[:ATTN_W, :], preferred_element_type=F32)
             + jnp.dot(f.astype(BF16), wout_ref[ATTN_W:, :], preferred_element_type=F32))
    x1 = x + g1_ref[pl.ds(r, 1), :] * mixed
    x1_ref[...] = x1
    h2 = _rms(x1) * n2g_ref[...]
    h2 = h2 * (1.0 + sc2_ref[pl.ds(r, 1), :]) + sh2_ref[pl.ds(r, 1), :]
    h2_ref[...] = h2
    _route(i, h2, whi_ref, wlo_ref, br_ref, idx_ref, w_ref, cnt_ref)


def _mix(xc, xl, ac, al, fc, fl, mod, attn_g, four_g, w_out, norm2_g, wr_hi, wr_lo, br, *, nc, nl, tpb):
    n = (nc + nl) * TM
    lo = lambda i: (jnp.minimum(i, nc - 1), 0)
    hi = lambda i: (jnp.maximum(i - nc, 0), 0)
    const = lambda i: (0, 0)
    row = lambda i: (i, 0)
    return pl.pallas_call(
        functools.partial(_mix_kernel, nc=nc, tpb=tpb),
        grid=(nc + nl,),
        in_specs=[pl.BlockSpec((TM, D_MODEL), lo), pl.BlockSpec((TM, D_MODEL), hi),
                  pl.BlockSpec((TM, ATTN_W), lo), pl.BlockSpec((TM, ATTN_W), hi),
                  pl.BlockSpec((FOURIER_W // FOURIER_GROUP_W, TM, FOURIER_GROUP_W),
                               lambda i: (0, jnp.minimum(i, nc - 1), 0)),
                  pl.BlockSpec((FOURIER_W // FOURIER_GROUP_W, TM, FOURIER_GROUP_W),
                               lambda i: (0, jnp.maximum(i - nc, 0), 0)),
                  pl.BlockSpec((8, D_MODEL), lambda i: (0, 2)),
                  pl.BlockSpec((8, D_MODEL), lambda i: (0, 3)),
                  pl.BlockSpec((8, D_MODEL), lambda i: (0, 4)),
                  pl.BlockSpec((1, ATTN_W), const), pl.BlockSpec((1, FOURIER_W), const),
                  pl.BlockSpec((D_MODEL, D_MODEL), const), pl.BlockSpec((1, D_MODEL), const),
                  pl.BlockSpec((D_MODEL, LANES), const), pl.BlockSpec((D_MODEL, LANES), const),
                  pl.BlockSpec((1, LANES), const)],
        out_specs=[pl.BlockSpec((TM, D_MODEL), row), pl.BlockSpec((TM, D_MODEL), row),
                   pl.BlockSpec((TM, LANES), row), pl.BlockSpec((TM, LANES), row), pl.BlockSpec((8, LANES), const)],
        out_shape=[jax.ShapeDtypeStruct((n, D_MODEL), F32), jax.ShapeDtypeStruct((n, D_MODEL), F32),
                   jax.ShapeDtypeStruct((n, LANES), I32), jax.ShapeDtypeStruct((n, LANES), F32),
                   jax.ShapeDtypeStruct((8, LANES), F32)],
        compiler_params=_cparams(("arbitrary",)),
        name="mix_proj",
    )(xc, xl, ac, al, fc, fl, mod, mod, mod, attn_g, four_g, w_out, norm2_g, wr_hi, wr_lo, br)


RUN_ALIGN = 8
RUN_ROWS = TM * TOP_K + N_EXPERTS * RUN_ALIGN
RUN_SIZES = tuple(TM >> s for s in range(7))
RUN_PAD = RUN_ROWS - TM * TOP_K
FILL_SIZES = tuple(RUN_PAD >> s for s in range(6))


def _round_up_runs(x):
    return jnp.floor((x + (RUN_ALIGN - 1)) * (1.0 / RUN_ALIGN)) * RUN_ALIGN


def _pos_kernel(idx_ref, gs_ref, loct_ref, loc_ref, info_ref, carry_ref):
    i = pl.program_id(0)

    @pl.when(i == 0)
    def _():
        carry_ref[...] = jnp.zeros(carry_ref.shape, F32)

    idx = idx_ref[...]
    lane = lax.broadcasted_iota(I32, idx.shape, 1)
    sels = [idx == k + 1 for k in range(TOP_K)]
    hits = jnp.zeros(idx.shape, F32)
    for s in sels:
        hits = hits + s.astype(F32)
    r = lax.broadcasted_iota(I32, (TM, TM), 0)
    c = lax.broadcasted_iota(I32, (TM, TM), 1)
    before = (c < r).astype(BF16)
    rank = jnp.dot(before, hits.astype(BF16), preferred_element_type=F32)
    run_len = _round_up_runs(jnp.sum(hits, axis=0, keepdims=True))
    a = lax.broadcasted_iota(I32, (LANES, LANES), 0)
    b = lax.broadcasted_iota(I32, (LANES, LANES), 1)
    units = jnp.broadcast_to(run_len * (1.0 / RUN_ALIGN), (8, LANES)).astype(BF16)
    run_off = jnp.dot(units, (a < b).astype(BF16), preferred_element_type=F32)[0:1, :] * RUN_ALIGN
    run_start = carry_ref[0:1, :] + gs_ref[0:1, :]
    loc_cols = [jnp.sum(jnp.where(s, rank + run_off, 0.0), axis=-1, keepdims=True) for s in sels]
    loc = _pack_cols(loc_cols, lane)
    loc_ref[...] = loc.astype(I32)
    loct_ref[...] = loc.T[:8, :].astype(I32)
    row = lax.broadcasted_iota(I32, (8, LANES), 0)
    info = jnp.where(row == 0, run_start, jnp.where(row == 1, run_len, jnp.where(row == 2, run_off, 0.0)))
    info_ref[...] = info.astype(I32)
    carry_ref[...] += run_len


def _positions(idx, gs_rows):
    n = idx.shape[0]
    nt = n // TM
    return pl.pallas_call(
        _pos_kernel,
        grid=(nt,),
        in_specs=[pl.BlockSpec((TM, LANES), lambda i: (i, 0)), pl.BlockSpec((8, LANES), lambda i: (0, 0))],
        out_specs=[pl.BlockSpec((8, TM), lambda i: (0, i)), pl.BlockSpec((TM, LANES), lambda i: (i, 0)),
                   pl.BlockSpec((None, 8, LANES), lambda i: (i, 0, 0))],
        out_shape=[jax.ShapeDtypeStruct((8, n), I32), jax.ShapeDtypeStruct((n, LANES), I32),
                   jax.ShapeDtypeStruct((nt, 8, LANES), I32)],
        scratch_shapes=[pltpu.VMEM((8, LANES), F32)],
        compiler_params=_cparams(("arbitrary",)),
        name="positions",
    )(idx, gs_rows)


def _run_copies(info_ref, tile, make_copy, act):
    base = tile * (3 * N_EXPERTS)

    def per_expert(e, carry):
        start = info_ref[base + e]
        length = info_ref[base + N_EXPERTS + e]
        off = info_ref[base + 2 * N_EXPERTS + e]
        for size in RUN_SIZES:
            @pl.when((length & size) != 0)
            def _():
                act(make_copy(pl.multiple_of(off, RUN_ALIGN), pl.multiple_of(start, RUN_ALIGN), size))
            taken = jnp.where((length & size) != 0, size, 0)
            off = off + taken
            start = start + taken
        return carry

    lax.fori_loop(0, N_EXPERTS, per_expert, 0)


PAD_SIZES = tuple(s for s in (256, 128, 64, 32, 16, 8) if s < TG)


def _dispatch_kernel(info_ref, ends_ref, cnt_ref, loc_ref, h2_ref, xs_ref, dump_ref, runs_ref, zero_ref, sem, zsem):
    i = pl.program_id(0)
    last_step = pl.num_programs(0) - 1
    slot = i % 2

    def zero_copies(act):
        def per_expert(e, carry):
            lo = jnp.where(e == 0, 0, ends_ref[jnp.maximum(e - 1, 0)])
            first = lo + cnt_ref[e]
            rest = ends_ref[e] - first
            done = rest * 0
            for size in PAD_SIZES:
                @pl.when((rest & size) != 0)
                def _():
                    act(pltpu.make_async_copy(zero_ref.at[pl.ds(0, size), :],
                                              xs_ref.at[pl.ds(pl.multiple_of(first + done, RUN_ALIGN), size), :], zsem))
                done = done + jnp.where((rest & size) != 0, size, 0)
            return carry

        lax.fori_loop(0, N_EXPERTS, per_expert, 0)

        def tail_tile(t, carry):
            act(pltpu.make_async_copy(zero_ref, xs_ref.at[pl.ds(pl.multiple_of(t * TG, TG), TG), :], zsem))
            return carry

        lax.fori_loop(ends_ref[N_EXPERTS - 1] // TG, xs_ref.shape[0] // TG, tail_tile, 0)

    def slot_done(s):
        return pltpu.make_async_copy(runs_ref.at[s], xs_ref.at[pl.ds(0, RUN_ROWS), :], sem.at[s])

    @pl.when(i == 0)
    def _():
        zero_ref[...] = jnp.zeros(zero_ref.shape, F32)
        for s in range(2):
            pltpu.make_async_copy(zero_ref.at[pl.ds(0, RUN_PAD), :], dump_ref.at[s], sem.at[s]).start()
        for s in range(2):
            pltpu.make_async_copy(zero_ref.at[pl.ds(0, RUN_PAD), :], dump_ref.at[s], sem.at[s]).wait()
        zero_copies(lambda cp: cp.start())

    @pl.when(i >= 2)
    def _():
        slot_done(slot).wait()

    loc = loc_ref[...]
    row = lax.broadcasted_iota(I32, (RUN_ROWS, TM), 0)
    sel = row == loc[0:1, :]
    for k in range(1, TOP_K):
        sel = jnp.logical_or(sel, row == loc[k:k + 1, :])
    runs_ref[slot] = jnp.dot(sel.astype(BF16), h2_ref[...].astype(BF16), preferred_element_type=F32)

    def piece(stacked_row, global_row, size):
        return pltpu.make_async_copy(runs_ref.at[slot, pl.ds(stacked_row, size), :],
                                     xs_ref.at[pl.ds(global_row, size), :], sem.at[slot])

    _run_copies(info_ref, i, piece, lambda cp: cp.start())
    last = i * (3 * N_EXPERTS) + N_EXPERTS - 1
    used = info_ref[last + 2 * N_EXPERTS] + info_ref[last + N_EXPERTS]
    rest = RUN_ROWS - used
    done = used * 0
    for size in FILL_SIZES:
        @pl.when((rest & size) != 0)
        def _():
            at = pl.multiple_of(used + done, RUN_ALIGN)
            to = pl.multiple_of(at - TM * TOP_K, RUN_ALIGN)
            pltpu.make_async_copy(runs_ref.at[slot, pl.ds(at, size), :], dump_ref.at[slot, pl.ds(to, size), :],
                                  sem.at[slot]).start()
        done = done + jnp.where((rest & size) != 0, size, 0)

    @pl.when(i == last_step)
    def _():
        slot_done(slot).wait()

        @pl.when(i >= 1)
        def _():
            slot_done(1 - slot).wait()

        zero_copies(lambda cp: cp.wait())


def _dispatch(info, ends, cnt, loc, h2, n_rows):
    n = h2.shape[0]
    grid_spec = pltpu.PrefetchScalarGridSpec(
        num_scalar_prefetch=3,
        grid=(n // TM,),
        in_specs=[pl.BlockSpec((8, TM), lambda i, *_: (0, i)),
                  pl.BlockSpec((TM, D_MODEL), lambda i, *_: (i, 0))],
        out_specs=[pl.BlockSpec(memory_space=pl.ANY), pl.BlockSpec(memory_space=pl.ANY)],
        scratch_shapes=[pltpu.VMEM((2, RUN_ROWS, D_MODEL), F32), pltpu.VMEM((TG, D_MODEL), F32),
                        pltpu.SemaphoreType.DMA((2,)), pltpu.SemaphoreType.DMA(())],
    )
    xs, _ = pl.pallas_call(
        _dispatch_kernel,
        grid_spec=grid_spec,
        out_shape=[jax.ShapeDtypeStruct((n_rows, D_MODEL), F32), jax.ShapeDtypeStruct((2, RUN_PAD, D_MODEL), F32)],
        compiler_params=_cparams(("arbitrary",)),
        name="dispatch",
    )(info, ends, cnt, loc, h2)
    return xs


def _ffn_kernel(te_ref, nv_ref, par_ref, nxt_ref, x_ref, bg_ref, bl_ref, bdn_ref, perm_ref, wgu_hbm, wdn_hbm, o_ref,
                wgu_buf, wdn_buf, wg_s, wl_s, wd_s, wsem):
    i = pl.program_id(0)
    e = te_ref[i]
    valid = i < nv_ref[0]
    fresh = jnp.logical_or(i == 0, e != te_ref[jnp.maximum(i - 1, 0)])
    wslot = par_ref[i]

    def weight_copies(expert, s):
        return (pltpu.make_async_copy(wgu_hbm.at[expert], wgu_buf.at[s], wsem.at[s]),
                pltpu.make_async_copy(wdn_hbm.at[expert], wdn_buf.at[s], wsem.at[s]))

    @pl.when(i == 0)
    def _():
        for cp in weight_copies(e, wslot):
            cp.start()

    @pl.when(jnp.logical_and(valid, fresh))
    def _():
        for cp in weight_copies(e, wslot):
            cp.wait()
        for j in range(2 * D_EXPERT // (2 * LANES)):
            blk = wgu_buf[wslot, :, j * 2 * LANES:(j + 1) * 2 * LANES].astype(BF16)
            d = jnp.dot(blk, perm_ref[...], preferred_element_type=F32)
            wg_s[:, j * LANES:(j + 1) * LANES] = d[:, :LANES].astype(BF16)
            wl_s[:, j * LANES:(j + 1) * LANES] = d[:, LANES:].astype(BF16)
        wd_s[...] = wdn_buf[wslot].astype(BF16)

        @pl.when(nxt_ref[i] >= 0)
        def _():
            for cp in weight_copies(nxt_ref[i], 1 - wslot):
                cp.start()

    @pl.when(valid)
    def _():
        x = x_ref[...].astype(BF16)
        g = jnp.dot(x, wg_s[...], preferred_element_type=F32) + bg_ref[...]
        l = jnp.dot(x, wl_s[...], preferred_element_type=F32) + bl_ref[...]
        g = jnp.minimum(g, SWIGLU_LIMIT)
        l = jnp.clip(l, -SWIGLU_LIMIT, SWIGLU_LIMIT)
        a = (l + 1.0) * (g * (1.0 / (1.0 + jnp.exp(-SWIGLU_ALPHA * g))))
        o_ref[...] = jnp.dot(a.astype(BF16), wd_s[...], preferred_element_type=F32) + bdn_ref[...]

    @pl.when(jnp.logical_not(valid))
    def _():
        o_ref[...] = jnp.zeros(o_ref.shape, F32)


def _expert_ffn(tile_expert, n_valid, parity, next_expert, xs, w_gu, b_g, b_l, w_dn, b_dn, perm):
    p = xs.shape[0]
    nt = p // TG
    tile = lambda i, te, nv, *_: (jnp.minimum(i, nv[0] - 1), 0)
    ex3 = lambda i, te, *_: (te[i], 0, 0)
    grid_spec = pltpu.PrefetchScalarGridSpec(
        num_scalar_prefetch=4,
        grid=(nt,),
        in_specs=[pl.BlockSpec((TG, D_MODEL), tile),
                  pl.BlockSpec((None, 1, D_EXPERT), ex3),
                  pl.BlockSpec((None, 1, D_EXPERT), ex3),
                  pl.BlockSpec((None, 1, D_MODEL), ex3),
                  pl.BlockSpec((2 * LANES, 2 * LANES), lambda i, *_: (0, 0)),
                  pl.BlockSpec(memory_space=pl.ANY),
                  pl.BlockSpec(memory_space=pl.ANY)],
        out_specs=pl.BlockSpec((TG, D_MODEL), lambda i, *_: (i, 0)),
        scratch_shapes=[pltpu.VMEM((2, D_MODEL, 2 * D_EXPERT), F32), pltpu.VMEM((2, D_EXPERT, D_MODEL), F32),
                        pltpu.VMEM((D_MODEL, D_EXPERT), BF16), pltpu.VMEM((D_MODEL, D_EXPERT), BF16),
                        pltpu.VMEM((D_EXPERT, D_MODEL), BF16), pltpu.SemaphoreType.DMA((2,))],
    )
    return pl.pallas_call(
        _ffn_kernel,
        grid_spec=grid_spec,
        out_shape=jax.ShapeDtypeStruct((p, D_MODEL), F32),
        compiler_params=_cparams(("arbitrary",)),
        name="expert_ffn",
    )(tile_expert, n_valid, parity, next_expert, xs, b_g, b_l, b_dn, perm, w_gu, w_dn)


def _combine_kernel(info_ref, loc_ref, w_ref, x1_ref, g2_ref, fg_ref, ys_ref, o_ref, runs_ref, sem, *, tile0, nc, tpb):
    j = pl.program_id(0)
    i = j + tile0
    r = _mod_row(i, nc, tpb)
    slot = j % 2

    def gather(tile, s, act):
        def piece(stacked_row, global_row, size):
            return pltpu.make_async_copy(ys_ref.at[pl.ds(global_row, size), :],
                                         runs_ref.at[s, pl.ds(stacked_row, size), :], sem.at[s])

        _run_copies(info_ref, tile, piece, act)
        last = tile * (3 * N_EXPERTS) + N_EXPERTS - 1
        used = info_ref[last + 2 * N_EXPERTS] + info_ref[last + N_EXPERTS]
        rest = RUN_ROWS - used
        done = used * 0
        for size in FILL_SIZES:
            @pl.when((rest & size) != 0)
            def _():
                act(piece(pl.multiple_of(used + done, RUN_ALIGN), pl.multiple_of(done, RUN_ALIGN), size))
            done = done + jnp.where((rest & size) != 0, size, 0)

    @pl.when(j == 0)
    def _():
        gather(i, 0, lambda cp: cp.start())

    @pl.when(j + 1 < pl.num_programs(0))
    def _():
        gather(i + 1, 1 - slot, lambda cp: cp.start())

    pltpu.make_async_copy(ys_ref.at[pl.ds(0, RUN_ROWS), :], runs_ref.at[slot], sem.at[slot]).wait()
    loc = loc_ref[...]
    w = w_ref[...]
    col = lax.broadcasted_iota(I32, (TM, RUN_ROWS), 1)
    mix = jnp.where(col == loc[:, 0:1], w[:, 0:1], 0.0)
    for k in range(1, TOP_K):
        mix = mix + jnp.where(col == loc[:, k:k + 1], w[:, k:k + 1], 0.0)
    moe = jnp.dot(mix.astype(BF16), runs_ref[slot].astype(BF16), preferred_element_type=F32)
    y = x1_ref[...] + g2_ref[pl.ds(r, 1), :] * moe
    o_ref[...] = _rms(y) * fg_ref[...]


def _combine(info, loc, w, x1, mod, final_g, ys, *, tile0, ntiles, nc, tpb):
    grid_spec = pltpu.PrefetchScalarGridSpec(
        num_scalar_prefetch=1,
        grid=(ntiles,),
        in_specs=[pl.BlockSpec((TM, LANES), lambda i, *_: (i + tile0, 0)),
                  pl.BlockSpec((TM, LANES), lambda i, *_: (i + tile0, 0)),
                  pl.BlockSpec((TM, D_MODEL), lambda i, *_: (i + tile0, 0)),
                  pl.BlockSpec((8, D_MODEL), lambda i, *_: (0, 5)),
                  pl.BlockSpec((1, D_MODEL), lambda i, *_: (0, 0)),
                  pl.BlockSpec(memory_space=pl.ANY)],
        out_specs=pl.BlockSpec((TM, D_MODEL), lambda i, *_: (i, 0)),
        scratch_shapes=[pltpu.VMEM((2, RUN_ROWS, D_MODEL), F32), pltpu.SemaphoreType.DMA((2,))],
    )
    return pl.pallas_call(
        functools.partial(_combine_kernel, tile0=tile0, nc=nc, tpb=tpb),
        grid_spec=grid_spec,
        out_shape=jax.ShapeDtypeStruct((ntiles * TM, D_MODEL), F32),
        compiler_params=_cparams(("arbitrary",)),
        name="combine",
    )(info, loc, w, x1, mod, final_g, ys)


def kernel(x_prompt, x_sample, cache_k, cache_v, c, c_ctx, norm1_g, w_mod, b_mod, w_in, q_norm_g, k_norm_g,
           w_fourier, attn_out_g, fourier_out_g, w_out, norm2_g, w_router, b_router, w_gate_up, b_gate_up,
           w_down, b_down, final_g):
    bc, tc, _ = x_prompt.shape
    bl, tl, _ = x_sample.shape
    depth = w_in.shape[0]
    past = cache_k.shape[2]
    assert depth == 1 and tl % TQ == 0 and tl % TM == 0 and (bc * tc) % TM == 0 and (bc * tc) % tl == 0
    n_ctx, n_lat = bc * tc, bl * tl
    n_tok = n_ctx + n_lat
    nc, nl, tpb = n_ctx // TM, n_lat // TM, tl // TM

    xc = x_prompt.reshape(n_ctx, D_MODEL)
    xl = x_sample.reshape(n_lat, D_MODEL)

    c_rows = jnp.concatenate([c_ctx[None, :], c, jnp.zeros((8 - 1 - bl, D_MODEL), F32)], axis=0)
    mod = _modulation(c_rows, w_mod[0], b_mod[0])

    cos_t, sin_t = _rope_tables(tl)
    q, k_heads, vt_ext, f, k_new, v_new = _pre(
        xc, xl, mod, norm1_g, w_in[0].astype(BF16), jnp.tile(q_norm_g, (1, N_HEADS)),
        jnp.tile(k_norm_g, (1, N_KV_HEADS)), _head_mean_matrix(), cos_t, sin_t, nc=nc, nl=nl, tpb=tpb)

    attn_c = _attention(q, [(k_heads, vt_ext, tc, 0)], batch=bc, t=tc, q_row0=0)
    ck, cvt = _cache_heads(cache_k[:, 0], cache_v[:, 0])
    attn_l = _attention(q, [(ck, cvt, past, 0), (k_heads, vt_ext, tl, n_ctx // tl)], batch=bl, t=tl, q_row0=n_ctx)

    wf = w_fourier[0].astype(BF16)
    zero = jnp.zeros_like(wf[0])
    wf_pairs = jnp.stack([jnp.block([[wf[0], zero], [zero, wf[1]]]), jnp.block([[wf[2], zero], [zero, wf[3]]])])
    four_c = _fourier(f, wf_pairs, batch=bc, t=tc, row0=0, n1=1)
    four_l = _fourier(f, wf_pairs, batch=bl, t=tl, row0=n_ctx, n1=FFT_N1)

    wr = jnp.pad(w_router[0], ((0, 0), (0, LANES - N_EXPERTS)))
    wr_hi = wr.astype(BF16)
    wr_lo = (wr - wr_hi.astype(F32)).astype(BF16)
    br = jnp.pad(b_router[0], (0, LANES - N_EXPERTS), constant_values=-jnp.inf).reshape(1, LANES)
    x1, h2, idx, gate_w, counts = _mix(xc, xl, attn_c, attn_l, four_c, four_l, mod, attn_out_g, fourier_out_g,
                                       w_out[0].astype(BF16), norm2_g, wr_hi, wr_lo, br, nc=nc, nl=nl, tpb=tpb)
    cnt = counts[0, :N_EXPERTS].astype(I32)
    padded = ((cnt + TG - 1) // TG) * TG
    ends = jnp.cumsum(padded)
    starts = ends - padded
    n_rows = n_tok * TOP_K + (n_tok // TM) * N_EXPERTS * RUN_ALIGN + N_EXPERTS * TG
    n_tiles = n_rows // TG
    tile_ids = jnp.arange(n_tiles, dtype=I32)
    tile_expert = jnp.minimum(jnp.sum((ends[None, :] // TG <= tile_ids[:, None]).astype(I32), axis=1),
                              N_EXPERTS - 1)
    n_valid = (ends[N_EXPERTS - 1] // TG).reshape(1)
    gs_rows = jnp.broadcast_to(jnp.pad(starts.astype(F32), (0, LANES - N_EXPERTS))[None, :], (8, LANES))
    loc_t, loc, info = _positions(idx, gs_rows)
    info = info[:, :3, :N_EXPERTS].reshape(-1)

    xs = _dispatch(info, ends, cnt, loc_t, h2, n_rows)
    fresh_tile = jnp.concatenate([jnp.ones((1,), I32), (tile_expert[1:] != tile_expert[:-1]).astype(I32)])
    parity = (jnp.cumsum(fresh_tile) - 1) % 2
    experts = jnp.arange(N_EXPERTS, dtype=I32)
    later = jnp.logical_and(padded[None, :] > 0, experts[None, :] > experts[:, None])
    next_of = jnp.min(jnp.where(later, experts[None, :], N_EXPERTS), axis=1)
    next_of = jnp.where(next_of == N_EXPERTS, -1, next_of)
    next_expert = jnp.sum(jnp.where(tile_expert[:, None] == experts[None, :], next_of[None, :], 0), axis=1)
    b_gu = b_gate_up[0].reshape(N_EXPERTS, 1, D_EXPERT, 2)
    ys = _expert_ffn(tile_expert, n_valid, parity.astype(I32), next_expert.astype(I32), xs, w_gate_up[0],
                     b_gu[..., 0], b_gu[..., 1], w_down[0], b_down[0].reshape(N_EXPERTS, 1, D_MODEL),
                     _deinterleave_matrix())

    y_c = _combine(info, loc, gate_w, x1, mod, final_g.reshape(1, -1), ys, tile0=0, ntiles=nc, nc=nc, tpb=tpb)
    y_l = _combine(info, loc, gate_w, x1, mod, final_g.reshape(1, -1), ys, tile0=nc, ntiles=nl, nc=nc, tpb=tpb)

    new_k = k_new[:n_ctx].reshape(bc, 1, tc, N_KV_HEADS, HEAD_DIM)
    new_v = v_new[:n_ctx].reshape(bc, 1, tc, N_KV_HEADS, HEAD_DIM)
    return (y_c.reshape(bc, tc, D_MODEL), y_l.reshape(bl, tl, D_MODEL), new_k, new_v)
```

```python
import functools
import math

import numpy as np
import jax
import jax.numpy as jnp
from jax import lax
from jax.experimental import pallas as pl
from jax.experimental.pallas import tpu as pltpu

F32 = jnp.float32
BF16 = jnp.bfloat16
I32 = jnp.int32

D_MODEL = 1024
HEAD_DIM = 64
N_HEADS = 8
N_KV_HEADS = 2
HEADS_PER_KV = N_HEADS // N_KV_HEADS
ATTN_W = N_HEADS * HEAD_DIM
KV_W = N_KV_HEADS * HEAD_DIM
FOURIER_W = 512
FOURIER_GROUP_W = 128
D_IN = ATTN_W + 2 * KV_W + FOURIER_W
N_EXPERTS = 32
TOP_K = 4
D_EXPERT = 1024
SWIGLU_ALPHA = 1.702
SWIGLU_LIMIT = 7.0
ROPE_THETA = 10000.0
ROT_PAIRS = HEAD_DIM // 4
GRID_W = 64
EPS = 1e-6

LANES = 128
TM = 512
TQ = 512
TG = 512
FFT_N1 = 4
V_ROWS = 80
VMEM_LIMIT = 56 * 1024 * 1024


def _cparams(sem, vmem=VMEM_LIMIT):
    return pltpu.CompilerParams(dimension_semantics=sem, vmem_limit_bytes=vmem)


def _rope_tables(t_lat):
    pos = np.arange(t_lat)
    row = (pos // GRID_W).astype(np.float64)
    col = (pos % GRID_W).astype(np.float64)
    inv = ROPE_THETA ** (-np.arange(ROT_PAIRS, dtype=np.float64) / ROT_PAIRS)
    lane = np.arange(LANES)
    d = lane % HEAD_DIM
    axis = d // (2 * ROT_PAIRS)
    second = (d // ROT_PAIRS) % 2
    p = d % ROT_PAIRS
    ang = np.where(axis[None, :] == 0, row[:, None], col[:, None]) * inv[p][None, :]
    cos = np.cos(ang)
    sin = np.sin(ang) * np.where(second == 0, -1.0, 1.0)[None, :]
    cos = np.concatenate([np.ones((TM, LANES)), cos], axis=0)
    sin = np.concatenate([np.zeros((TM, LANES)), sin], axis=0)
    return jnp.asarray(cos, F32), jnp.asarray(sin, F32)


def _dft_tables(n1, n2):
    c = np.arange(FOURIER_GROUP_W)
    ang = 2.0 * np.pi * np.outer(c, c) / FOURIER_GROUP_W
    z = np.zeros_like(ang)
    cbd = np.block([[np.cos(ang), z], [z, np.cos(ang)]])
    sbd = np.block([[np.sin(ang), z], [z, np.sin(ang)]])
    cd = np.concatenate([cbd, -sbd], axis=1)
    u = np.arange(n2)
    ang2 = 2.0 * np.pi * np.outer(u, u) / n2
    ct, st = np.cos(ang2), np.sin(ang2)
    nt = max(n1 - 1, 1)
    tw_ang = 2.0 * np.pi * np.outer(np.arange(1, nt + 1), u) / (n1 * n2)
    twc = np.repeat(np.cos(tw_ang)[:, :, None], LANES, axis=2)
    tws = np.repeat(np.sin(tw_ang)[:, :, None], LANES, axis=2)
    return (jnp.asarray(cd, F32), jnp.asarray(ct, F32), jnp.asarray(st, F32),
            jnp.asarray(twc, F32), jnp.asarray(tws, F32))


def _head_mean_matrix():
    h = np.arange(ATTN_W) // HEAD_DIM
    return jnp.asarray((h[:, None] == h[None, :]) / HEAD_DIM, BF16)


def _deinterleave_matrix():
    p = np.zeros((2 * LANES, 2 * LANES))
    m = np.arange(LANES)
    p[2 * m, m] = 1.0
    p[2 * m + 1, LANES + m] = 1.0
    return jnp.asarray(p, BF16)


def _mod_row(i, nc, tpb):
    return jnp.where(i < nc, 0, 1 + (i - nc) // tpb)


def _rms(x):
    return x * lax.rsqrt(jnp.mean(x * x, axis=-1, keepdims=True) + EPS)


def _pack_cols(cols, lane):
    out = jnp.zeros(lane.shape, cols[0].dtype)
    for k, c in enumerate(cols):
        out = jnp.where(lane == k, c, out)
    return out


def _mod_kernel(c_ref, w_ref, b_ref, o_ref):
    c = c_ref[...]
    s = c * (1.0 / (1.0 + jnp.exp(-c)))
    o_ref[...] = jnp.dot(s.astype(BF16), w_ref[...].astype(BF16), preferred_element_type=F32) + b_ref[...]


def _modulation(c_rows, w_mod, b_mod):
    n = w_mod.shape[1] // D_MODEL
    return pl.pallas_call(
        _mod_kernel,
        grid=(n,),
        in_specs=[pl.BlockSpec((8, D_MODEL), lambda j: (0, 0)),
                  pl.BlockSpec((D_MODEL, D_MODEL), lambda j: (0, j)),
                  pl.BlockSpec((1, D_MODEL), lambda j: (0, j))],
        out_specs=pl.BlockSpec((8, D_MODEL), lambda j: (0, j)),
        out_shape=jax.ShapeDtypeStruct((8, w_mod.shape[1]), F32),
        compiler_params=_cparams(("parallel",)),
        name="modulation",
    )(c_rows, w_mod, b_mod.reshape(1, -1))


def _pre_kernel(xc_ref, xl_ref, sh_ref, sc_ref, g1_ref, win_ref, qg_ref, kg_ref, bd_ref, cos_ref, sin_ref,
                q_ref, k_ref, vt_ref, f_ref, kn_ref, vn_ref, *, nc, tpb):
    i = pl.program_id(0)
    r = _mod_row(i, nc, tpb)
    x = jnp.where(i < nc, xc_ref[...], xl_ref[...])
    h = _rms(x) * g1_ref[...]
    h = h * (1.0 + sc_ref[pl.ds(r, 1), :]) + sh_ref[pl.ds(r, 1), :]
    proj = jnp.dot(h.astype(BF16), win_ref[...], preferred_element_type=F32)
    q = proj[:, :ATTN_W]
    k = proj[:, ATTN_W:ATTN_W + KV_W]
    v = proj[:, ATTN_W + KV_W:ATTN_W + 2 * KV_W]
    f_ref[...] = proj[:, ATTN_W + 2 * KV_W:].astype(BF16)
    vn_ref[...] = v
    vt = v.T
    tail = (lax.broadcasted_iota(I32, (V_ROWS - HEAD_DIM, vt.shape[1]), 0) == 0).astype(BF16)
    for j in range(N_KV_HEADS):
        vt_ref[j, :HEAD_DIM, :] = vt[j * HEAD_DIM:(j + 1) * HEAD_DIM, :].astype(BF16)
        vt_ref[j, HEAD_DIM:, :] = tail
    q_ms = jnp.dot((q * q).astype(BF16), bd_ref[...], preferred_element_type=F32)
    k_ms = jnp.dot((k * k).astype(BF16), bd_ref[:KV_W, :KV_W], preferred_element_type=F32)
    qn = q * lax.rsqrt(q_ms + EPS) * qg_ref[...]
    kn = k * lax.rsqrt(k_ms + EPS) * kg_ref[...]
    kn_ref[...] = kn
    cos = cos_ref[...]
    sin = sin_ref[...]
    lane = lax.broadcasted_iota(I32, cos.shape, 1)
    first = (lane & ROT_PAIRS) == 0

    def rope(c):
        partner = jnp.where(first, pltpu.roll(c, LANES - ROT_PAIRS, 1), pltpu.roll(c, ROT_PAIRS, 1))
        return c * cos + partner * sin

    k_rot = rope(kn).astype(BF16)
    for j in range(N_KV_HEADS):
        k_ref[j] = k_rot[:, j * HEAD_DIM:(j + 1) * HEAD_DIM]
    scale = HEAD_DIM ** -0.5 * math.log2(math.e)
    q_rot = jnp.concatenate([rope(qn[:, j * LANES:(j + 1) * LANES]) for j in range(ATTN_W // LANES)], axis=1)
    q_ref[...] = (q_rot * scale).T.astype(BF16)


def _pre(xc, xl, mod, norm1_g, w_in, q_g, k_g, bd, cos_t, sin_t, *, nc, nl, tpb):
    n = (nc + nl) * TM
    row = lambda i: (i, 0)
    const = lambda i: (0, 0)
    tab = lambda i: (jnp.where(i < nc, 0, 1 + (i - nc) % tpb), 0)
    return pl.pallas_call(
        functools.partial(_pre_kernel, nc=nc, tpb=tpb),
        grid=(nc + nl,),
        in_specs=[pl.BlockSpec((TM, D_MODEL), lambda i: (jnp.minimum(i, nc - 1), 0)),
                  pl.BlockSpec((TM, D_MODEL), lambda i: (jnp.maximum(i - nc, 0), 0)),
                  pl.BlockSpec((8, D_MODEL), lambda i: (0, 0)),
                  pl.BlockSpec((8, D_MODEL), lambda i: (0, 1)),
                  pl.BlockSpec((1, D_MODEL), const),
                  pl.BlockSpec((D_MODEL, D_IN), const),
                  pl.BlockSpec((1, ATTN_W), const),
                  pl.BlockSpec((1, KV_W), const),
                  pl.BlockSpec((ATTN_W, ATTN_W), const),
                  pl.BlockSpec((TM, LANES), tab),
                  pl.BlockSpec((TM, LANES), tab)],
        out_specs=[pl.BlockSpec((ATTN_W, TM), lambda i: (0, i)),
                   pl.BlockSpec((N_KV_HEADS, TM, HEAD_DIM), lambda i: (0, i, 0)),
                   pl.BlockSpec((N_KV_HEADS, V_ROWS, TM), lambda i: (0, 0, i)),
                   pl.BlockSpec((TM, FOURIER_W), row), pl.BlockSpec((TM, KV_W), row), pl.BlockSpec((TM, KV_W), row)],
        out_shape=[jax.ShapeDtypeStruct((ATTN_W, n), BF16), jax.ShapeDtypeStruct((N_KV_HEADS, n, HEAD_DIM), BF16),
                   jax.ShapeDtypeStruct((N_KV_HEADS, V_ROWS, n), BF16), jax.ShapeDtypeStruct((n, FOURIER_W), BF16),
                   jax.ShapeDtypeStruct((n, KV_W), F32), jax.ShapeDtypeStruct((n, KV_W), F32)],
        compiler_params=_cparams(("parallel",)),
        name="pre_proj",
    )(xc, xl, mod, mod, norm1_g, w_in, q_g, k_g, bd, cos_t, sin_t)


def _attn_kernel(*refs, seg_lens, seqs):
    n_seg = len(seg_lens)
    qt_ref = refs[0]
    k_refs = refs[1:1 + n_seg]
    vt_refs = refs[1 + n_seg:1 + 2 * n_seg]
    o_ref, ot_ref, st_ref, pt_ref = refs[1 + 2 * n_seg:]
    offs = [sum(seg_lens[:m]) for m in range(n_seg)]
    tq = st_ref.shape[2]

    for sq in range(seqs):
        def scores(h):
            qt = qt_ref[h * HEAD_DIM:(h + 1) * HEAD_DIM, sq * tq:(sq + 1) * tq]
            for k_ref, off, ln in zip(k_refs, offs, seg_lens):
                st_ref[h % 2, off:off + ln] = jnp.dot(k_ref[h // HEADS_PER_KV, sq * ln:(sq + 1) * ln], qt,
                                                      preferred_element_type=F32)

        scores(0)
        for h in range(N_HEADS):
            if h + 1 < N_HEADS:
                scores(h + 1)
            st = st_ref[h % 2]
            pt_ref[h % 2] = jnp.exp2(st - jnp.max(st, axis=0, keepdims=True)).astype(BF16)
            ot = None
            for vt_ref, off, ln in zip(vt_refs, offs, seg_lens):
                part = jnp.dot(vt_ref[h // HEADS_PER_KV, :, sq * ln:(sq + 1) * ln], pt_ref[h % 2, off:off + ln],
                               preferred_element_type=F32)
                ot = part if ot is None else ot + part
            ot_ref[h * HEAD_DIM:(h + 1) * HEAD_DIM, :] = ot[:HEAD_DIM] * (1.0 / ot[HEAD_DIM:HEAD_DIM + 1])
        o_ref[sq * tq:(sq + 1) * tq, :] = ot_ref[...].T


def _attention(qt, segments, *, batch, t, q_row0, seqs=1):
    tq = min(TQ, t)
    qpb = t // tq
    assert seqs == 1 or (qpb == 1 and batch % seqs == 0 and q_row0 % (seqs * tq) == 0
                         and all(fb % seqs == 0 for _, _, _, fb in segments))
    q0 = q_row0 // (seqs * tq)
    seg_lens = tuple(ln for _, _, ln, _ in segments)
    s_total = sum(seg_lens)
    k_specs = [pl.BlockSpec((N_KV_HEADS, seqs * ln, HEAD_DIM),
                            functools.partial(lambda b, i, f: (0, f + b, 0), f=fb // seqs))
               for _, _, ln, fb in segments]
    v_specs = [pl.BlockSpec((N_KV_HEADS, V_ROWS, seqs * ln),
                            functools.partial(lambda b, i, f: (0, 0, f + b), f=fb // seqs))
               for _, _, ln, fb in segments]
    return pl.pallas_call(
        functools.partial(_attn_kernel, seg_lens=seg_lens, seqs=seqs),
        grid=(batch // seqs, qpb),
        in_specs=[pl.BlockSpec((ATTN_W, seqs * tq), lambda b, i: (0, q0 + b * qpb + i))] + k_specs + v_specs,
        out_specs=pl.BlockSpec((seqs * tq, ATTN_W), lambda b, i: (b * qpb + i, 0)),
        out_shape=jax.ShapeDtypeStruct((batch * t, ATTN_W), F32),
        scratch_shapes=[pltpu.VMEM((ATTN_W, tq), F32), pltpu.VMEM((2, s_total, tq), F32),
                        pltpu.VMEM((2, s_total, tq), BF16)],
        compiler_params=_cparams(("parallel", "parallel")),
        name="attention",
    )(qt, *[k for k, _, _, _ in segments], *[v for _, v, _, _ in segments])


def _cache_heads(cache_k, cache_v):
    batch, past = cache_k.shape[:2]
    ck = cache_k.astype(BF16).transpose(2, 0, 1, 3).reshape(N_KV_HEADS, batch * past, HEAD_DIM)
    cv = cache_v.astype(BF16).transpose(2, 3, 0, 1).reshape(N_KV_HEADS, HEAD_DIM, batch * past)
    tail = jnp.concatenate([jnp.ones((N_KV_HEADS, 1, batch * past), BF16),
                            jnp.zeros((N_KV_HEADS, V_ROWS - HEAD_DIM - 1, batch * past), BF16)], axis=1)
    return ck, jnp.concatenate([cv, tail], axis=1)


def _fourier_kernel(x_ref, cd_ref, ct_ref, st_ref, twc_ref, tws_ref, wf_ref, o_ref, *, n1, n2, seqs):
    pw = 2 * FOURIER_GROUP_W
    t = n1 * n2
    scale = 1.0 / math.sqrt(t * FOURIER_GROUP_W)
    for sq in range(seqs):
        for gp in range(FOURIER_W // pw):
            z = jnp.dot(x_ref[sq * t:(sq + 1) * t, gp * pw:(gp + 1) * pw], cd_ref[...],
                        preferred_element_type=F32)
            zr = [z[t1 * n2:(t1 + 1) * n2, :pw] for t1 in range(n1)]
            zi = [z[t1 * n2:(t1 + 1) * n2, pw:] for t1 in range(n1)]
            if n1 == 1:
                a = [(zr[0], zi[0])]
            else:
                a = [(zr[0] + zr[1] + zr[2] + zr[3], zi[0] + zi[1] + zi[2] + zi[3]),
                     (zr[0] + zi[1] - zr[2] - zi[3], zi[0] - zr[1] - zi[2] + zr[3]),
                     (zr[0] - zr[1] + zr[2] - zr[3], zi[0] - zi[1] + zi[2] - zi[3]),
                     (zr[0] - zi[1] - zr[2] + zi[3], zi[0] + zr[1] - zi[2] - zr[3])]
            for u1 in range(n1):
                ar, ai = a[u1]
                if u1 > 0:
                    c = jnp.concatenate([twc_ref[u1 - 1]] * 2, axis=1)
                    s = jnp.concatenate([tws_ref[u1 - 1]] * 2, axis=1)
                    ar, ai = ar * c + ai * s, ai * c - ar * s
                y = (jnp.dot(ct_ref[...], ar.astype(BF16), preferred_element_type=F32)
                     + jnp.dot(st_ref[...], ai.astype(BF16), preferred_element_type=F32)) * scale
                y = jnp.dot(y.astype(BF16), wf_ref[gp], preferred_element_type=F32)
                rows = pl.ds(sq * t + u1, n2, stride=n1) if n1 > 1 else slice(sq * t, (sq + 1) * t)
                for half in range(2):
                    o_ref[2 * gp + half, rows, :] = y[:, half * FOURIER_GROUP_W:(half + 1) * FOURIER_GROUP_W]


def _fourier(f, wf_pairs, *, batch, t, row0, n1, seqs=1):
    n2 = t // n1
    assert batch % seqs == 0 and row0 % (seqs * t) == 0
    cd, ct, st, twc, tws = _dft_tables(n1, n2)
    cd, ct, st = cd.astype(BF16), ct.astype(BF16), st.astype(BF16)
    nt = twc.shape[0]
    b0 = row0 // (seqs * t)
    c2 = lambda b: (0, 0)
    c3 = lambda b: (0, 0, 0)
    return pl.pallas_call(
        functools.partial(_fourier_kernel, n1=n1, n2=n2, seqs=seqs),
        grid=(batch // seqs,),
        in_specs=[pl.BlockSpec((seqs * t, FOURIER_W), lambda b: (b0 + b, 0)),
                  pl.BlockSpec(cd.shape, c2), pl.BlockSpec((n2, n2), c2), pl.BlockSpec((n2, n2), c2),
                  pl.BlockSpec((nt, n2, LANES), c3), pl.BlockSpec((nt, n2, LANES), c3),
                  pl.BlockSpec(wf_pairs.shape, c3)],
        out_specs=pl.BlockSpec((FOURIER_W // FOURIER_GROUP_W, seqs * t, FOURIER_GROUP_W), lambda b: (0, b, 0)),
        out_shape=jax.ShapeDtypeStruct((FOURIER_W // FOURIER_GROUP_W, batch * t, FOURIER_GROUP_W), F32),
        compiler_params=_cparams(("parallel",)),
        name="fourier",
    )(f, cd, ct, st, twc, tws, wf_pairs)


def _route(i, h, whi_ref, wlo_ref, br_ref, idx_ref, w_ref, cnt_ref):
    h_hi = h.astype(BF16)
    h_lo = (h - h_hi.astype(F32)).astype(BF16)
    both = jnp.dot(h_hi, jnp.concatenate([whi_ref[...], wlo_ref[...]], axis=1), preferred_element_type=F32)
    logits = (both[:, :LANES] + both[:, LANES:]
              + jnp.dot(h_lo, whi_ref[...], preferred_element_type=F32)) + br_ref[...]
    lane = lax.broadcasted_iota(I32, logits.shape, 1)
    l = logits
    vals = []
    code = jnp.zeros(logits.shape, F32)
    a = lax.broadcasted_iota(I32, (LANES, LANES), 0)
    b = lax.broadcasted_iota(I32, (LANES, LANES), 1)
    upto = (a <= b).astype(BF16)
    for k in range(TOP_K):
        m = jnp.max(l, axis=-1, keepdims=True)
        eq = l == m
        first = jnp.dot(eq.astype(BF16), upto, preferred_element_type=F32) == 1.0
        sel = jnp.logical_and(eq, first)
        code = jnp.where(sel, float(k + 1), code)
        l = jnp.where(sel, -jnp.inf, l)
        vals.append(m)
    hits = (code > 0.0).astype(F32)
    es = [jnp.exp(v - vals[0]) for v in vals]
    den = es[0] + es[1] + es[2] + es[3]
    idx_ref[...] = code.astype(I32)
    w_ref[...] = _pack_cols([e / den for e in es], lane)

    @pl.when(i == 0)
    def _():
        cnt_ref[...] = jnp.zeros(cnt_ref.shape, F32)

    cnt_ref[...] += _round_up_runs(jnp.sum(hits, axis=0, keepdims=True))


def _mix_kernel(xc_ref, xl_ref, ac_ref, al_ref, fc_ref, fl_ref, g1_ref, sh2_ref, sc2_ref, ag_ref, fg_ref,
                wout_ref, n2g_ref, whi_ref, wlo_ref, br_ref, x1_ref, h2_ref, idx_ref, w_ref, cnt_ref, *, nc, tpb):
    i = pl.program_id(0)
    r = _mod_row(i, nc, tpb)
    is_ctx = i < nc
    x = jnp.where(is_ctx, xc_ref[...], xl_ref[...])
    a = _rms(jnp.where(is_ctx, ac_ref[...], al_ref[...])) * ag_ref[...]
    n_groups = FOURIER_W // FOURIER_GROUP_W
    f = jnp.concatenate([jnp.where(is_ctx, fc_ref[g], fl_ref[g]) for g in range(n_groups)], axis=1)
    f = _rms(f) * fg_ref[...]
    mixed = (jnp.dot(a.astype(BF16), wout_ref[:ATTN_W, :], preferred_element_type=F32)
             + jnp.dot(f.astype(BF16), wout_ref[ATTN_W:, :], preferred_element_type=F32))
    x1 = x + g1_ref[pl.ds(r, 1), :] * mixed
    x1_ref[...] = x1
    h2 = _rms(x1) * n2g_ref[...]
    h2 = h2 * (1.0 + sc2_ref[pl.ds(r, 1), :]) + sh2_ref[pl.ds(r, 1), :]
    h2_ref[...] = h2
    _route(i, h2, whi_ref, wlo_ref, br_ref, idx_ref, w_ref, cnt_ref)


def _mix(xc, xl, ac, al, fc, fl, mod, attn_g, four_g, w_out, norm2_g, wr_hi, wr_lo, br, *, nc, nl, tpb):
    n = (nc + nl) * TM
    lo = lambda i: (jnp.minimum(i, nc - 1), 0)
    hi = lambda i: (jnp.maximum(i - nc, 0), 0)
    const = lambda i: (0, 0)
    row = lambda i: (i, 0)
    return pl.pallas_call(
        functools.partial(_mix_kernel, nc=nc, tpb=tpb),
        grid=(nc + nl,),
        in_specs=[pl.BlockSpec((TM, D_MODEL), lo), pl.BlockSpec((TM, D_MODEL), hi),
                  pl.BlockSpec((TM, ATTN_W), lo), pl.BlockSpec((TM, ATTN_W), hi),
                  pl.BlockSpec((FOURIER_W // FOURIER_GROUP_W, TM, FOURIER_GROUP_W),
                               lambda i: (0, jnp.minimum(i, nc - 1), 0)),
                  pl.BlockSpec((FOURIER_W // FOURIER_GROUP_W, TM, FOURIER_GROUP_W),
                               lambda i: (0, jnp.maximum(i - nc, 0), 0)),
                  pl.BlockSpec((8, D_MODEL), lambda i: (0, 2)),
                  pl.BlockSpec((8, D_MODEL), lambda i: (0, 3)),
                  pl.BlockSpec((8, D_MODEL), lambda i: (0, 4)),
                  pl.BlockSpec((1, ATTN_W), const), pl.BlockSpec((1, FOURIER_W), const),
                  pl.BlockSpec((D_MODEL, D_MODEL), const), pl.BlockSpec((1, D_MODEL), const),
                  pl.BlockSpec((D_MODEL, LANES), const), pl.BlockSpec((D_MODEL, LANES), const),
                  pl.BlockSpec((1, LANES), const)],
        out_specs=[pl.BlockSpec((TM, D_MODEL), row), pl.BlockSpec((TM, D_MODEL), row),
                   pl.BlockSpec((TM, LANES), row), pl.BlockSpec((TM, LANES), row), pl.BlockSpec((8, LANES), const)],
        out_shape=[jax.ShapeDtypeStruct((n, D_MODEL), F32), jax.ShapeDtypeStruct((n, D_MODEL), F32),
                   jax.ShapeDtypeStruct((n, LANES), I32), jax.ShapeDtypeStruct((n, LANES), F32),
                   jax.ShapeDtypeStruct((8, LANES), F32)],
        compiler_params=_cparams(("arbitrary",)),
        name="mix_proj",
    )(xc, xl, ac, al, fc, fl, mod, mod, mod, attn_g, four_g, w_out, norm2_g, wr_hi, wr_lo, br)


RUN_ALIGN = 8
RUN_ROWS = TM * TOP_K + N_EXPERTS * RUN_ALIGN
RUN_SIZES = tuple(TM >> s for s in range(7))
RUN_PAD = RUN_ROWS - TM * TOP_K
FILL_SIZES = tuple(RUN_PAD >> s for s in range(6))


def _round_up_runs(x):
    return jnp.floor((x + (RUN_ALIGN - 1)) * (1.0 / RUN_ALIGN)) * RUN_ALIGN


def _pos_kernel(idx_ref, gs_ref, loct_ref, loc_ref, info_ref, carry_ref):
    i = pl.program_id(0)

    @pl.when(i == 0)
    def _():
        carry_ref[...] = jnp.zeros(carry_ref.shape, F32)

    idx = idx_ref[...]
    lane = lax.broadcasted_iota(I32, idx.shape, 1)
    sels = [idx == k + 1 for k in range(TOP_K)]
    hits = jnp.zeros(idx.shape, F32)
    for s in sels:
        hits = hits + s.astype(F32)
    r = lax.broadcasted_iota(I32, (TM, TM), 0)
    c = lax.broadcasted_iota(I32, (TM, TM), 1)
    before = (c < r).astype(BF16)
    rank = jnp.dot(before, hits.astype(BF16), preferred_element_type=F32)
    run_len = _round_up_runs(jnp.sum(hits, axis=0, keepdims=True))
    a = lax.broadcasted_iota(I32, (LANES, LANES), 0)
    b = lax.broadcasted_iota(I32, (LANES, LANES), 1)
    units = jnp.broadcast_to(run_len * (1.0 / RUN_ALIGN), (8, LANES)).astype(BF16)
    run_off = jnp.dot(units, (a < b).astype(BF16), preferred_element_type=F32)[0:1, :] * RUN_ALIGN
    run_start = carry_ref[0:1, :] + gs_ref[0:1, :]
    loc_cols = [jnp.sum(jnp.where(s, rank + run_off, 0.0), axis=-1, keepdims=True) for s in sels]
    loc = _pack_cols(loc_cols, lane)
    loc_ref[...] = loc.astype(I32)
    loct_ref[...] = loc.T[:8, :].astype(I32)
    row = lax.broadcasted_iota(I32, (8, LANES), 0)
    info = jnp.where(row == 0, run_start, jnp.where(row == 1, run_len, jnp.where(row == 2, run_off, 0.0)))
    info_ref[...] = info.astype(I32)
    carry_ref[...] += run_len


def _positions(idx, gs_rows):
    n = idx.shape[0]
    nt = n // TM
    return pl.pallas_call(
        _pos_kernel,
        grid=(nt,),
        in_specs=[pl.BlockSpec((TM, LANES), lambda i: (i, 0)), pl.BlockSpec((8, LANES), lambda i: (0, 0))],
        out_specs=[pl.BlockSpec((8, TM), lambda i: (0, i)), pl.BlockSpec((TM, LANES), lambda i: (i, 0)),
                   pl.BlockSpec((None, 8, LANES), lambda i: (i, 0, 0))],
        out_shape=[jax.ShapeDtypeStruct((8, n), I32), jax.ShapeDtypeStruct((n, LANES), I32),
                   jax.ShapeDtypeStruct((nt, 8, LANES), I32)],
        scratch_shapes=[pltpu.VMEM((8, LANES), F32)],
        compiler_params=_cparams(("arbitrary",)),
        name="positions",
    )(idx, gs_rows)


def _run_copies(info_ref, tile, make_copy, act):
    base = tile * (3 * N_EXPERTS)

    def per_expert(e, carry):
        start = info_ref[base + e]
        length = info_ref[base + N_EXPERTS + e]
        off = info_ref[base + 2 * N_EXPERTS + e]
        for size in RUN_SIZES:
            @pl.when((length & size) != 0)
            def _():
                act(make_copy(pl.multiple_of(off, RUN_ALIGN), pl.multiple_of(start, RUN_ALIGN), size))
            taken = jnp.where((length & size) != 0, size, 0)
            off = off + taken
            start = start + taken
        return carry

    lax.fori_loop(0, N_EXPERTS, per_expert, 0)


PAD_SIZES = tuple(s for s in (256, 128, 64, 32, 16, 8) if s < TG)


def _dispatch_kernel(info_ref, ends_ref, cnt_ref, loc_ref, h2_ref, xs_ref, dump_ref, runs_ref, zero_ref, sem, zsem):
    i = pl.program_id(0)
    last_step = pl.num_programs(0) - 1
    slot = i % 2

    def zero_copies(act):
        def per_expert(e, carry):
            lo = jnp.where(e == 0, 0, ends_ref[jnp.maximum(e - 1, 0)])
            first = lo + cnt_ref[e]
            rest = ends_ref[e] - first
            done = rest * 0
            for size in PAD_SIZES:
                @pl.when((rest & size) != 0)
                def _():
                    act(pltpu.make_async_copy(zero_ref.at[pl.ds(0, size), :],
                                              xs_ref.at[pl.ds(pl.multiple_of(first + done, RUN_ALIGN), size), :], zsem))
                done = done + jnp.where((rest & size) != 0, size, 0)
            return carry

        lax.fori_loop(0, N_EXPERTS, per_expert, 0)

        def tail_tile(t, carry):
            act(pltpu.make_async_copy(zero_ref, xs_ref.at[pl.ds(pl.multiple_of(t * TG, TG), TG), :], zsem))
            return carry

        lax.fori_loop(ends_ref[N_EXPERTS - 1] // TG, xs_ref.shape[0] // TG, tail_tile, 0)

    def slot_done(s):
        return pltpu.make_async_copy(runs_ref.at[s], xs_ref.at[pl.ds(0, RUN_ROWS), :], sem.at[s])

    @pl.when(i == 0)
    def _():
        zero_ref[...] = jnp.zeros(zero_ref.shape, F32)
        for s in range(2):
            pltpu.make_async_copy(zero_ref.at[pl.ds(0, RUN_PAD), :], dump_ref.at[s], sem.at[s]).start()
        for s in range(2):
            pltpu.make_async_copy(zero_ref.at[pl.ds(0, RUN_PAD), :], dump_ref.at[s], sem.at[s]).wait()
        zero_copies(lambda cp: cp.start())

    @pl.when(i >= 2)
    def _():
        slot_done(slot).wait()

    loc = loc_ref[...]
    row = lax.broadcasted_iota(I32, (RUN_ROWS, TM), 0)
    sel = row == loc[0:1, :]
    for k in range(1, TOP_K):
        sel = jnp.logical_or(sel, row == loc[k:k + 1, :])
    runs_ref[slot] = jnp.dot(sel.astype(BF16), h2_ref[...].astype(BF16), preferred_element_type=F32)

    def piece(stacked_row, global_row, size):
        return pltpu.make_async_copy(runs_ref.at[slot, pl.ds(stacked_row, size), :],
                                     xs_ref.at[pl.ds(global_row, size), :], sem.at[slot])

    _run_copies(info_ref, i, piece, lambda cp: cp.start())
    last = i * (3 * N_EXPERTS) + N_EXPERTS - 1
    used = info_ref[last + 2 * N_EXPERTS] + info_ref[last + N_EXPERTS]
    rest = RUN_ROWS - used
    done = used * 0
    for size in FILL_SIZES:
        @pl.when((rest & size) != 0)
        def _():
            at = pl.multiple_of(used + done, RUN_ALIGN)
            to = pl.multiple_of(at - TM * TOP_K, RUN_ALIGN)
            pltpu.make_async_copy(runs_ref.at[slot, pl.ds(at, size), :], dump_ref.at[slot, pl.ds(to, size), :],
                                  sem.at[slot]).start()
        done = done + jnp.where((rest & size) != 0, size, 0)

    @pl.when(i == last_step)
    def _():
        slot_done(slot).wait()

        @pl.when(i >= 1)
        def _():
            slot_done(1 - slot).wait()

        zero_copies(lambda cp: cp.wait())


def _dispatch(info, ends, cnt, loc, h2, n_rows):
    n = h2.shape[0]
    grid_spec = pltpu.PrefetchScalarGridSpec(
        num_scalar_prefetch=3,
        grid=(n // TM,),
        in_specs=[pl.BlockSpec((8, TM), lambda i, *_: (0, i)),
                  pl.BlockSpec((TM, D_MODEL), lambda i, *_: (i, 0))],
        out_specs=[pl.BlockSpec(memory_space=pl.ANY), pl.BlockSpec(memory_space=pl.ANY)],
        scratch_shapes=[pltpu.VMEM((2, RUN_ROWS, D_MODEL), F32), pltpu.VMEM((TG, D_MODEL), F32),
                        pltpu.SemaphoreType.DMA((2,)), pltpu.SemaphoreType.DMA(())],
    )
    xs, _ = pl.pallas_call(
        _dispatch_kernel,
        grid_spec=grid_spec,
        out_shape=[jax.ShapeDtypeStruct((n_rows, D_MODEL), F32), jax.ShapeDtypeStruct((2, RUN_PAD, D_MODEL), F32)],
        compiler_params=_cparams(("arbitrary",)),
        name="dispatch",
    )(info, ends, cnt, loc, h2)
    return xs


def _ffn_kernel(te_ref, nv_ref, par_ref, nxt_ref, x_ref, bg_ref, bl_ref, bdn_ref, perm_ref, wgu_hbm, wdn_hbm, o_ref,
                wgu_buf, wdn_buf, wg_s, wl_s, wd_s, wsem):
    i = pl.program_id(0)
    e = te_ref[i]
    valid = i < nv_ref[0]
    fresh = jnp.logical_or(i == 0, e != te_ref[jnp.maximum(i - 1, 0)])
    wslot = par_ref[i]

    def weight_copies(expert, s):
        return (pltpu.make_async_copy(wgu_hbm.at[expert], wgu_buf.at[s], wsem.at[s]),
                pltpu.make_async_copy(wdn_hbm.at[expert], wdn_buf.at[s], wsem.at[s]))

    @pl.when(i == 0)
    def _():
        for cp in weight_copies(e, wslot):
            cp.start()

    @pl.when(jnp.logical_and(valid, fresh))
    def _():
        for cp in weight_copies(e, wslot):
            cp.wait()
        for j in range(2 * D_EXPERT // (2 * LANES)):
            blk = wgu_buf[wslot, :, j * 2 * LANES:(j + 1) * 2 * LANES].astype(BF16)
            d = jnp.dot(blk, perm_ref[...], preferred_element_type=F32)
            wg_s[:, j * LANES:(j + 1) * LANES] = d[:, :LANES].astype(BF16)
            wl_s[:, j * LANES:(j + 1) * LANES] = d[:, LANES:].astype(BF16)
        wd_s[...] = wdn_buf[wslot].astype(BF16)

        @pl.when(nxt_ref[i] >= 0)
        def _():
            for cp in weight_copies(nxt_ref[i], 1 - wslot):
                cp.start()

    @pl.when(valid)
    def _():
        x = x_ref[...].astype(BF16)
        g = jnp.dot(x, wg_s[...], preferred_element_type=F32) + bg_ref[...]
        l = jnp.dot(x, wl_s[...], preferred_element_type=F32) + bl_ref[...]
        g = jnp.minimum(g, SWIGLU_LIMIT)
        l = jnp.clip(l, -SWIGLU_LIMIT, SWIGLU_LIMIT)
        a = (l + 1.0) * (g * (1.0 / (1.0 + jnp.exp(-SWIGLU_ALPHA * g))))
        o_ref[...] = jnp.dot(a.astype(BF16), wd_s[...], preferred_element_type=F32) + bdn_ref[...]

    @pl.when(jnp.logical_not(valid))
    def _():
        o_ref[...] = jnp.zeros(o_ref.shape, F32)


def _expert_ffn(tile_expert, n_valid, parity, next_expert, xs, w_gu, b_g, b_l, w_dn, b_dn, perm):
    p = xs.shape[0]
    nt = p // TG
    tile = lambda i, te, nv, *_: (jnp.minimum(i, nv[0] - 1), 0)
    ex3 = lambda i, te, *_: (te[i], 0, 0)
    grid_spec = pltpu.PrefetchScalarGridSpec(
        num_scalar_prefetch=4,
        grid=(nt,),
        in_specs=[pl.BlockSpec((TG, D_MODEL), tile),
                  pl.BlockSpec((None, 1, D_EXPERT), ex3),
                  pl.BlockSpec((None, 1, D_EXPERT), ex3),
                  pl.BlockSpec((None, 1, D_MODEL), ex3),
                  pl.BlockSpec((2 * LANES, 2 * LANES), lambda i, *_: (0, 0)),
                  pl.BlockSpec(memory_space=pl.ANY),
                  pl.BlockSpec(memory_space=pl.ANY)],
        out_specs=pl.BlockSpec((TG, D_MODEL), lambda i, *_: (i, 0)),
        scratch_shapes=[pltpu.VMEM((2, D_MODEL, 2 * D_EXPERT), F32), pltpu.VMEM((2, D_EXPERT, D_MODEL), F32),
                        pltpu.VMEM((D_MODEL, D_EXPERT), BF16), pltpu.VMEM((D_MODEL, D_EXPERT), BF16),
                        pltpu.VMEM((D_EXPERT, D_MODEL), BF16), pltpu.SemaphoreType.DMA((2,))],
    )
    return pl.pallas_call(
        _ffn_kernel,
        grid_spec=grid_spec,
        out_shape=jax.ShapeDtypeStruct((p, D_MODEL), F32),
        compiler_params=_cparams(("arbitrary",)),
        name="expert_ffn",
    )(tile_expert, n_valid, parity, next_expert, xs, b_g, b_l, b_dn, perm, w_gu, w_dn)


def _combine_kernel(info_ref, loc_ref, w_ref, x1_ref, g2_ref, fg_ref, ys_ref, o_ref, runs_ref, sem, *, tile0, nc, tpb):
    j = pl.program_id(0)
    i = j + tile0
    r = _mod_row(i, nc, tpb)
    slot = j % 2

    def gather(tile, s, act):
        def piece(stacked_row, global_row, size):
            return pltpu.make_async_copy(ys_ref.at[pl.ds(global_row, size), :],
                                         runs_ref.at[s, pl.ds(stacked_row, size), :], sem.at[s])

        _run_copies(info_ref, tile, piece, act)
        last = tile * (3 * N_EXPERTS) + N_EXPERTS - 1
        used = info_ref[last + 2 * N_EXPERTS] + info_ref[last + N_EXPERTS]
        rest = RUN_ROWS - used
        done = used * 0
        for size in FILL_SIZES:
            @pl.when((rest & size) != 0)
            def _():
                act(piece(pl.multiple_of(used + done, RUN_ALIGN), pl.multiple_of(done, RUN_ALIGN), size))
            done = done + jnp.where((rest & size) != 0, size, 0)

    @pl.when(j == 0)
    def _():
        gather(i, 0, lambda cp: cp.start())

    @pl.when(j + 1 < pl.num_programs(0))
    def _():
        gather(i + 1, 1 - slot, lambda cp: cp.start())

    pltpu.make_async_copy(ys_ref.at[pl.ds(0, RUN_ROWS), :], runs_ref.at[slot], sem.at[slot]).wait()
    loc = loc_ref[...]
    w = w_ref[...]
    col = lax.broadcasted_iota(I32, (TM, RUN_ROWS), 1)
    mix = jnp.where(col == loc[:, 0:1], w[:, 0:1], 0.0)
    for k in range(1, TOP_K):
        mix = mix + jnp.where(col == loc[:, k:k + 1], w[:, k:k + 1], 0.0)
    moe = jnp.dot(mix.astype(BF16), runs_ref[slot].astype(BF16), preferred_element_type=F32)
    y = x1_ref[...] + g2_ref[pl.ds(r, 1), :] * moe
    o_ref[...] = _rms(y) * fg_ref[...]


def _combine(info, loc, w, x1, mod, final_g, ys, *, tile0, ntiles, nc, tpb):
    grid_spec = pltpu.PrefetchScalarGridSpec(
        num_scalar_prefetch=1,
        grid=(ntiles,),
        in_specs=[pl.BlockSpec((TM, LANES), lambda i, *_: (i + tile0, 0)),
                  pl.BlockSpec((TM, LANES), lambda i, *_: (i + tile0, 0)),
                  pl.BlockSpec((TM, D_MODEL), lambda i, *_: (i + tile0, 0)),
                  pl.BlockSpec((8, D_MODEL), lambda i, *_: (0, 5)),
                  pl.BlockSpec((1, D_MODEL), lambda i, *_: (0, 0)),
                  pl.BlockSpec(memory_space=pl.ANY)],
        out_specs=pl.BlockSpec((TM, D_MODEL), lambda i, *_: (i, 0)),
        scratch_shapes=[pltpu.VMEM((2, RUN_ROWS, D_MODEL), F32), pltpu.SemaphoreType.DMA((2,))],
    )
    return pl.pallas_call(
        functools.partial(_combine_kernel, tile0=tile0, nc=nc, tpb=tpb),
        grid_spec=grid_spec,
        out_shape=jax.ShapeDtypeStruct((ntiles * TM, D_MODEL), F32),
        compiler_params=_cparams(("arbitrary",)),
        name="combine",
    )(info, loc, w, x1, mod, final_g, ys)


def kernel(x_prompt, x_sample, cache_k, cache_v, c, c_ctx, norm1_g, w_mod, b_mod, w_in, q_norm_g, k_norm_g,
           w_fourier, attn_out_g, fourier_out_g, w_out, norm2_g, w_router, b_router, w_gate_up, b_gate_up,
           w_down, b_down, final_g):
    bc, tc, _ = x_prompt.shape
    bl, tl, _ = x_sample.shape
    depth = w_in.shape[0]
    past = cache_k.shape[2]
    assert depth == 1 and tl % TQ == 0 and tl % TM == 0 and (bc * tc) % TM == 0 and (bc * tc) % tl == 0
    n_ctx, n_lat = bc * tc, bl * tl
    n_tok = n_ctx + n_lat
    nc, nl, tpb = n_ctx // TM, n_lat // TM, tl // TM

    xc = x_prompt.reshape(n_ctx, D_MODEL)
    xl = x_sample.reshape(n_lat, D_MODEL)

    c_rows = jnp.concatenate([c_ctx[None, :], c, jnp.zeros((8 - 1 - bl, D_MODEL), F32)], axis=0)
    mod = _modulation(c_rows, w_mod[0], b_mod[0])

    cos_t, sin_t = _rope_tables(tl)
    q, k_heads, vt_ext, f, k_new, v_new = _pre(
        xc, xl, mod, norm1_g, w_in[0].astype(BF16), jnp.tile(q_norm_g, (1, N_HEADS)),
        jnp.tile(k_norm_g, (1, N_KV_HEADS)), _head_mean_matrix(), cos_t, sin_t, nc=nc, nl=nl, tpb=tpb)

    attn_c = _attention(q, [(k_heads, vt_ext, tc, 0)], batch=bc, t=tc, q_row0=0, seqs=4)
    ck, cvt = _cache_heads(cache_k[:, 0], cache_v[:, 0])
    attn_l = _attention(q, [(ck, cvt, past, 0), (k_heads, vt_ext, tl, n_ctx // tl)], batch=bl, t=tl, q_row0=n_ctx)

    wf = w_fourier[0].astype(BF16)
    zero = jnp.zeros_like(wf[0])
    wf_pairs = jnp.stack([jnp.block([[wf[0], zero], [zero, wf[1]]]), jnp.block([[wf[2], zero], [zero, wf[3]]])])
    four_c = _fourier(f, wf_pairs, batch=bc, t=tc, row0=0, n1=1, seqs=4)
    four_l = _fourier(f, wf_pairs, batch=bl, t=tl, row0=n_ctx, n1=FFT_N1)

    wr = jnp.pad(w_router[0], ((0, 0), (0, LANES - N_EXPERTS)))
    wr_hi = wr.astype(BF16)
    wr_lo = (wr - wr_hi.astype(F32)).astype(BF16)
    br = jnp.pad(b_router[0], (0, LANES - N_EXPERTS), constant_values=-jnp.inf).reshape(1, LANES)
    x1, h2, idx, gate_w, counts = _mix(xc, xl, attn_c, attn_l, four_c, four_l, mod, attn_out_g, fourier_out_g,
                                       w_out[0].astype(BF16), norm2_g, wr_hi, wr_lo, br, nc=nc, nl=nl, tpb=tpb)
    cnt = counts[0, :N_EXPERTS].astype(I32)
    padded = ((cnt + TG - 1) // TG) * TG
    ends = jnp.cumsum(padded)
    starts = ends - padded
    n_rows = n_tok * TOP_K + (n_tok // TM) * N_EXPERTS * RUN_ALIGN + N_EXPERTS * TG
    n_tiles = n_rows // TG
    tile_ids = jnp.arange(n_tiles, dtype=I32)
    tile_expert = jnp.minimum(jnp.sum((ends[None, :] // TG <= tile_ids[:, None]).astype(I32), axis=1),
                              N_EXPERTS - 1)
    n_valid = (ends[N_EXPERTS - 1] // TG).reshape(1)
    gs_rows = jnp.broadcast_to(jnp.pad(starts.astype(F32), (0, LANES - N_EXPERTS))[None, :], (8, LANES))
    loc_t, loc, info = _positions(idx, gs_rows)
    info = info[:, :3, :N_EXPERTS].reshape(-1)

    xs = _dispatch(info, ends, cnt, loc_t, h2, n_rows)
    fresh_tile = jnp.concatenate([jnp.ones((1,), I32), (tile_expert[1:] != tile_expert[:-1]).astype(I32)])
    parity = (jnp.cumsum(fresh_tile) - 1) % 2
    experts = jnp.arange(N_EXPERTS, dtype=I32)
    later = jnp.logical_and(padded[None, :] > 0, experts[None, :] > experts[:, None])
    next_of = jnp.min(jnp.where(later, experts[None, :], N_EXPERTS), axis=1)
    next_of = jnp.where(next_of == N_EXPERTS, -1, next_of)
    next_expert = jnp.sum(jnp.where(tile_expert[:, None] == experts[None, :], next_of[None, :], 0), axis=1)
    b_gu = b_gate_up[0].reshape(N_EXPERTS, 1, D_EXPERT, 2)
    ys = _expert_ffn(tile_expert, n_valid, parity.astype(I32), next_expert.astype(I32), xs, w_gate_up[0],
                     b_gu[..., 0], b_gu[..., 1], w_down[0], b_down[0].reshape(N_EXPERTS, 1, D_MODEL),
                     _deinterleave_matrix())

    y_c = _combine(info, loc, gate_w, x1, mod, final_g.reshape(1, -1), ys, tile0=0, ntiles=nc, nc=nc, tpb=tpb)
    y_l = _combine(info, loc, gate_w, x1, mod, final_g.reshape(1, -1), ys, tile0=nc, ntiles=nl, nc=nc, tpb=tpb)

    new_k = k_new[:n_ctx].reshape(bc, 1, tc, N_KV_HEADS, HEAD_DIM)
    new_v = v_new[:n_ctx].reshape(bc, 1, tc, N_KV_HEADS, HEAD_DIM)
    return (y_c.reshape(bc, tc, D_MODEL), y_l.reshape(bl, tl, D_MODEL), new_k, new_v)
```
